```python
import jax, jax.numpy as jnp
from jax import lax
import numpy as np

D_MODEL = 1024
BATCH = 8
SEQ = 2048
DEPTH = 4

HEAD_DIM = 64
N_HEADS = D_MODEL // HEAD_DIM
N_SB_HEADS = N_HEADS // 2
N_DIL_HEADS = N_HEADS - N_SB_HEADS
N_FOX_HEADS = N_HEADS
D_ATTN = N_HEADS * HEAD_DIM
D_FF = -(-8 * D_MODEL // (3 * 256)) * 256
ROPE_THETA = 500000.0
ROT_DIM = HEAD_DIM // 4
Q_BLOCK = 128
DIL_PATTERNS = ((128, 1), (512, 4), (2048, 16))
RMS_EPS = 1e-5
N_EVEN = (DEPTH + 1) // 2
N_ODD = DEPTH // 2

kernel_name = "hybrid_stickbreak_dilated_fox_trunk"


def rms_norm(x, g):
    xf = x.astype(jnp.float32)
    y = xf * lax.rsqrt(jnp.mean(xf * xf, axis=-1, keepdims=True) + RMS_EPS)
    return (y * g.astype(jnp.float32)).astype(x.dtype)


def partial_rotary(x, pos):
    half = ROT_DIM // 2
    inv_freq = ROPE_THETA ** (-jnp.arange(half, dtype=jnp.float32) * 2.0 / ROT_DIM)
    ang = pos[:, None] * inv_freq[None, :]
    cos = jnp.cos(ang)[:, None, :].astype(x.dtype)
    sin = jnp.sin(ang)[:, None, :].astype(x.dtype)
    x1, x2, x_pass = x[..., :half], x[..., half:ROT_DIM], x[..., ROT_DIM:]
    return jnp.concatenate([x1 * cos - x2 * sin, x2 * cos + x1 * sin, x_pass], axis=-1)


def stick_breaking_attention(q, k, v):
    B, H, S, Dh = q.shape
    nb = S // Q_BLOCK
    qb = q.reshape(B, H, nb, Q_BLOCK, Dh).transpose(2, 0, 1, 3, 4)
    kpos = jnp.arange(S)

    def block(args):
        qblk, n = args
        z = jnp.einsum('bhqd,bhkd->bhqk', qblk, k).astype(jnp.float32)
        qpos = n * Q_BLOCK + jnp.arange(Q_BLOCK)
        strict = kpos[None, :] < qpos[:, None]
        log_1m_beta = jnp.where(strict, jax.nn.log_sigmoid(-z), 0.0)
        between = lax.cumsum(log_1m_beta, axis=3, reverse=True) - log_1m_beta
        a = jnp.where(strict, jnp.exp(jax.nn.log_sigmoid(z) + between), 0.0)
        return jnp.einsum('bhqk,bhkd->bhqd', a.astype(v.dtype), v)

    out = lax.map(block, (qb, jnp.arange(nb)))
    return out.transpose(1, 2, 0, 3, 4).reshape(B, H, S, Dh)


def dilated_window_attention(q, k, v, window, dilation):
    B, S, H, Dh = q.shape
    span = window // dilation
    L = S // dilation
    nb = -(-L // span)
    Lp = nb * span

    def to_blocks(t):
        t = t.reshape(B, L, dilation, H, Dh)
        t = jnp.pad(t, ((0, 0), (0, Lp - L), (0, 0), (0, 0), (0, 0)))
        return t.reshape(B, nb, span, dilation, H, Dh)

    def with_prev(t):
        prev = jnp.pad(t[:, :-1], ((0, 0), (1, 0), (0, 0), (0, 0), (0, 0), (0, 0)))
        return jnp.concatenate([prev, t], axis=2)

    qb = to_blocks(q)
    kw = with_prev(to_blocks(k))
    vw = with_prev(to_blocks(v))
    s = jnp.einsum('bnqrhd,bnkrhd->bnrhqk', qb, kw).astype(jnp.float32)
    a = jnp.arange(span)[None, :, None]
    kk = jnp.arange(2 * span)[None, None, :]
    blk = jnp.arange(nb)[:, None, None]
    valid = (kk >= a) & (kk <= a + span) & (blk * span - span + kk >= 0)
    s = jnp.where(valid[None, :, None, None], s, -jnp.inf)
    m = jnp.max(s, axis=-1, keepdims=True)
    p = jnp.exp(s - m)
    l = jnp.sum(p, axis=-1, keepdims=True)
    o = jnp.einsum('bnrhqk,bnkrhd->bnqrhd', (p / l).astype(v.dtype), vw)
    lse = (m + jnp.log(l))[..., 0]
    o = o.reshape(B, Lp, dilation, H, Dh)[:, :L].reshape(B, S, H, Dh)
    lse = lse.transpose(0, 1, 4, 2, 3).reshape(B, Lp, dilation, H)[:, :L].reshape(B, S, H)
    return o, lse


def forgetting_attention(q, k, v, log_f):
    B, H, S, Dh = q.shape
    F = lax.cumsum(log_f, axis=2)
    nb = S // Q_BLOCK
    qb = q.reshape(B, H, nb, Q_BLOCK, Dh).transpose(2, 0, 1, 3, 4)
    Fb = F.reshape(B, H, nb, Q_BLOCK).transpose(2, 0, 1, 3)
    kpos = jnp.arange(S)

    def block(args):
        qblk, Fq, n = args
        z = jnp.einsum('bhqd,bhkd->bhqk', qblk, k).astype(jnp.float32)
        z = z + Fq[..., None] - F[:, :, None, :]
        qpos = n * Q_BLOCK + jnp.arange(Q_BLOCK)
        z = jnp.where(kpos[None, :] <= qpos[:, None], z, -jnp.inf)
        p = jax.nn.softmax(z, axis=-1)
        return jnp.einsum('bhqk,bhkd->bhqd', p.astype(v.dtype), v)

    out = lax.map(block, (qb, Fb, jnp.arange(nb)))
    return out.transpose(1, 2, 0, 3, 4).reshape(B, H, S, Dh)


def even_mixer(h, w_qkv, w_o):
    B, S, _ = h.shape
    scale = HEAD_DIM ** -0.5
    qkv = (h @ w_qkv).reshape(B, S, 3, N_HEADS, HEAD_DIM)
    q, k, v = qkv[:, :, 0], qkv[:, :, 1], qkv[:, :, 2]
    qa = (q[:, :, :N_SB_HEADS] * scale).transpose(0, 2, 1, 3)
    ka = k[:, :, :N_SB_HEADS].transpose(0, 2, 1, 3)
    va = v[:, :, :N_SB_HEADS].transpose(0, 2, 1, 3)
    out_a = stick_breaking_attention(qa, ka, va).transpose(0, 2, 1, 3)
    pos = jnp.arange(S, dtype=jnp.float32)
    qd = partial_rotary(q[:, :, N_SB_HEADS:], pos) * scale
    kd = partial_rotary(k[:, :, N_SB_HEADS:], pos)
    vd = v[:, :, N_SB_HEADS:]
    outs, lses = [], []
    for window, dilation in DIL_PATTERNS:
        o_p, lse_p = dilated_window_attention(qd, kd, vd, window, dilation)
        outs.append(o_p)
        lses.append(lse_p)
    mix = jax.nn.softmax(jnp.stack(lses), axis=0)
    out_b = jnp.einsum('pbsh,pbshd->bshd', mix.astype(vd.dtype), jnp.stack(outs))
    o = jnp.concatenate([out_a, out_b], axis=2).reshape(B, S, D_ATTN)
    return o @ w_o


def odd_mixer(h, w_qkvf, b_forget, w_o):
    B, S, _ = h.shape
    scale = HEAD_DIM ** -0.5
    proj = h @ w_qkvf
    qkv = proj[..., :3 * D_ATTN].reshape(B, S, 3, N_FOX_HEADS, HEAD_DIM)
    f_logit = proj[..., 3 * D_ATTN:] + b_forget
    log_f = jax.nn.log_sigmoid(f_logit.astype(jnp.float32)).transpose(0, 2, 1)
    q = (qkv[:, :, 0] * scale).transpose(0, 2, 1, 3)
    k = qkv[:, :, 1].transpose(0, 2, 1, 3)
    v = qkv[:, :, 2].transpose(0, 2, 1, 3)
    o = forgetting_attention(q, k, v, log_f).transpose(0, 2, 1, 3).reshape(B, S, D_ATTN)
    return o @ w_o


def swiglu(h, w_in, w_out):
    g, u = jnp.split(h @ w_in, 2, axis=-1)
    return (jax.nn.silu(g) * u) @ w_out


def _fwd_setup_inputs(seed: int = 0) -> dict:
    key = jax.random.key(seed)
    ks = jax.random.split(key, 12)
    f32 = jnp.float32
    res_scale = (2.0 * DEPTH) ** -0.5
    x = jax.random.normal(ks[0], (BATCH, SEQ, D_MODEL), f32)
    norm_mix = 1.0 + 0.02 * jax.random.normal(ks[1], (DEPTH, D_MODEL), f32)
    w_qkv_even = jax.random.normal(ks[2], (N_EVEN, D_MODEL, 3 * D_ATTN), f32) * D_MODEL ** -0.5
    w_o_even = jax.random.normal(ks[3], (N_EVEN, D_ATTN, D_MODEL), f32) * (D_ATTN ** -0.5 * res_scale)
    w_qkvf_odd = jax.random.normal(ks[4], (N_ODD, D_MODEL, 3 * D_ATTN + N_FOX_HEADS), f32) * D_MODEL ** -0.5
    b_forget = jax.random.uniform(ks[5], (N_ODD, N_FOX_HEADS), f32, minval=1.0, maxval=4.0)
    w_o_odd = jax.random.normal(ks[6], (N_ODD, D_ATTN, D_MODEL), f32) * (D_ATTN ** -0.5 * res_scale)
    norm_ffn = 1.0 + 0.02 * jax.random.normal(ks[7], (DEPTH, D_MODEL), f32)
    w_ffn_in = jax.random.normal(ks[8], (DEPTH, D_MODEL, 2 * D_FF), f32) * D_MODEL ** -0.5
    w_ffn_out = jax.random.normal(ks[9], (DEPTH, D_FF, D_MODEL), f32) * (D_FF ** -0.5 * res_scale)
    norm_final = 1.0 + 0.02 * jax.random.normal(ks[10], (D_MODEL,), f32)
    return {"x": x, "norm_mix": norm_mix, "w_qkv_even": w_qkv_even, "w_o_even": w_o_even,
            "w_qkvf_odd": w_qkvf_odd, "b_forget": b_forget, "w_o_odd": w_o_odd,
            "norm_ffn": norm_ffn, "w_ffn_in": w_ffn_in, "w_ffn_out": w_ffn_out,
            "norm_final": norm_final}


def _fwd_reference(x, norm_mix, w_qkv_even, w_o_even, w_qkvf_odd, b_forget, w_o_odd,
              norm_ffn, w_ffn_in, w_ffn_out, norm_final):
    for layer in range(DEPTH):
        h = rms_norm(x, norm_mix[layer])
        if layer % 2 == 0:
            x = x + even_mixer(h, w_qkv_even[layer // 2], w_o_even[layer // 2])
        else:
            x = x + odd_mixer(h, w_qkvf_odd[layer // 2], b_forget[layer // 2], w_o_odd[layer // 2])
        h = rms_norm(x, norm_ffn[layer])
        x = x + swiglu(h, w_ffn_in[layer], w_ffn_out[layer])
    return rms_norm(x, norm_final)


import jax as _jax
import jax.numpy as _jnp

TWIN_FORMAT = 'train_step'
FWD_PARAMS = ['x', 'norm_mix', 'w_qkv_even', 'w_o_even', 'w_qkvf_odd', 'b_forget', 'w_o_odd', 'norm_ffn', 'w_ffn_in', 'w_ffn_out', 'norm_final']
TWIN_WEIGHTS = ['norm_mix', 'w_qkv_even', 'w_o_even', 'w_qkvf_odd', 'b_forget', 'w_o_odd', 'norm_ffn', 'w_ffn_in', 'w_ffn_out', 'norm_final']
TWIN_DIFF_INPUT = 'x'
TWIN_INPUTS = ['x', 'norm_mix', 'w_qkv_even', 'w_o_even', 'w_qkvf_odd', 'b_forget', 'w_o_odd', 'norm_ffn', 'w_ffn_in', 'w_ffn_out', 'norm_final', 'loss_target', 'm_norm_mix', 'm_w_qkv_even', 'm_w_o_even', 'm_w_qkvf_odd', 'm_b_forget', 'm_w_o_odd', 'm_norm_ffn', 'm_w_ffn_in', 'm_w_ffn_out', 'm_norm_final', 'v_norm_mix', 'v_w_qkv_even', 'v_w_o_even', 'v_w_qkvf_odd', 'v_b_forget', 'v_w_o_odd', 'v_norm_ffn', 'v_w_ffn_in', 'v_w_ffn_out', 'v_norm_final']
TWIN_OUTPUTS = ['loss', 'grad_x', 'grad_norm_mix', 'grad_w_qkv_even', 'grad_w_o_even', 'grad_w_qkvf_odd', 'grad_b_forget', 'grad_w_o_odd', 'grad_norm_ffn', 'grad_w_ffn_in', 'grad_w_ffn_out', 'grad_norm_final', 'delta_norm_mix', 'delta_w_qkv_even', 'delta_w_o_even', 'delta_w_qkvf_odd', 'delta_b_forget', 'delta_w_o_odd', 'delta_norm_ffn', 'delta_w_ffn_in', 'delta_w_ffn_out', 'delta_norm_final', 'new_m_norm_mix', 'new_m_w_qkv_even', 'new_m_w_o_even', 'new_m_w_qkvf_odd', 'new_m_b_forget', 'new_m_w_o_odd', 'new_m_norm_ffn', 'new_m_w_ffn_in', 'new_m_w_ffn_out', 'new_m_norm_final', 'new_v_norm_mix', 'new_v_w_qkv_even', 'new_v_w_o_even', 'new_v_w_qkvf_odd', 'new_v_b_forget', 'new_v_w_o_odd', 'new_v_norm_ffn', 'new_v_w_ffn_in', 'new_v_w_ffn_out', 'new_v_norm_final']
TWIN_LEAF_KINDS = {'loss': 'loss', 'grad_x': 'grad_x', 'grad_norm_mix': 'grad_w', 'grad_w_qkv_even': 'grad_w', 'grad_w_o_even': 'grad_w', 'grad_w_qkvf_odd': 'grad_w', 'grad_b_forget': 'grad_w', 'grad_w_o_odd': 'grad_w', 'grad_norm_ffn': 'grad_w', 'grad_w_ffn_in': 'grad_w', 'grad_w_ffn_out': 'grad_w', 'grad_norm_final': 'grad_w', 'delta_norm_mix': 'delta_w', 'delta_w_qkv_even': 'delta_w', 'delta_w_o_even': 'delta_w', 'delta_w_qkvf_odd': 'delta_w', 'delta_b_forget': 'delta_w', 'delta_w_o_odd': 'delta_w', 'delta_norm_ffn': 'delta_w', 'delta_w_ffn_in': 'delta_w', 'delta_w_ffn_out': 'delta_w', 'delta_norm_final': 'delta_w', 'new_m_norm_mix': 'new_m', 'new_m_w_qkv_even': 'new_m', 'new_m_w_o_even': 'new_m', 'new_m_w_qkvf_odd': 'new_m', 'new_m_b_forget': 'new_m', 'new_m_w_o_odd': 'new_m', 'new_m_norm_ffn': 'new_m', 'new_m_w_ffn_in': 'new_m', 'new_m_w_ffn_out': 'new_m', 'new_m_norm_final': 'new_m', 'new_v_norm_mix': 'new_v', 'new_v_w_qkv_even': 'new_v', 'new_v_w_o_even': 'new_v', 'new_v_w_qkvf_odd': 'new_v', 'new_v_b_forget': 'new_v', 'new_v_w_o_odd': 'new_v', 'new_v_norm_ffn': 'new_v', 'new_v_w_ffn_in': 'new_v', 'new_v_w_ffn_out': 'new_v', 'new_v_norm_final': 'new_v'}


def _forward(args):
    return _fwd_reference(*[args[k] for k in FWD_PARAMS])


def _output_shape():
    out = _jax.eval_shape(lambda: _forward(_fwd_setup_inputs(0)))
    return out.shape, out.dtype

N_MICROBATCH = 1
ADAM_LR = 0.001
ADAM_B1 = 0.9
ADAM_B2 = 0.999
ADAM_EPS = 1e-08
ADAM_WD = 0.01
ADAM_STEP = 10
PER_EXAMPLE_BATCH_AXIS = {'x': 0, 'loss_target': 0}
SHARED_INPUTS = []
_WEIGHT_DTYPES = {'norm_mix': _jnp.float32, 'w_qkv_even': _jnp.float32, 'w_o_even': _jnp.float32, 'w_qkvf_odd': _jnp.float32, 'b_forget': _jnp.float32, 'w_o_odd': _jnp.float32, 'norm_ffn': _jnp.float32, 'w_ffn_in': _jnp.float32, 'w_ffn_out': _jnp.float32, 'norm_final': _jnp.float32}
MOMENT_SCALE = {'norm_mix': 2.545299e-02, 'w_qkv_even': 1.594103e-02, 'w_o_even': 6.359945e-02, 'w_qkvf_odd': 1.344047e-02, 'b_forget': 7.546541e-02, 'w_o_odd': 4.511967e-02, 'norm_ffn': 3.690331e-02, 'w_ffn_in': 1.567617e-02, 'w_ffn_out': 7.234128e-02, 'norm_final': 1.600926e+01}


def _to_microbatches(a, axis):
    t = _jnp.moveaxis(a, axis, 0)
    t = t.reshape((N_MICROBATCH, t.shape[0] // N_MICROBATCH) + t.shape[1:])
    return _jnp.moveaxis(t, 1, axis + 1)


def setup_inputs(seed: int = 0) -> dict:
    inp = _fwd_setup_inputs(seed)
    key = _jax.random.fold_in(_jax.random.key(seed), 7919)
    shape, _ = _output_shape()
    out = dict(inp)
    out["loss_target"] = _jax.random.normal(_jax.random.fold_in(key, 0), shape, _jnp.float32)
    for i, name in enumerate(TWIN_WEIGHTS):
        w = inp[name].astype(_jnp.float32)
        if MOMENT_SCALE is None:
            s = _jnp.sqrt(_jnp.mean(_jnp.square(w)) + 1e-30)
        else:
            s = MOMENT_SCALE[name]
        km, kv = _jax.random.split(_jax.random.fold_in(key, i + 1))
        out[name] = w
        out["m_" + name] = s * _jax.random.normal(km, w.shape, _jnp.float32)
        out["v_" + name] = (s * s) * _jax.random.uniform(kv, w.shape, _jnp.float32, 0.5, 1.5)
    if N_MICROBATCH > 1:
        for name, axis in PER_EXAMPLE_BATCH_AXIS.items():
            out[name] = _to_microbatches(out[name], axis)
    return {'x': out['x'], 'norm_mix': out['norm_mix'], 'w_qkv_even': out['w_qkv_even'], 'w_o_even': out['w_o_even'], 'w_qkvf_odd': out['w_qkvf_odd'], 'b_forget': out['b_forget'], 'w_o_odd': out['w_o_odd'], 'norm_ffn': out['norm_ffn'], 'w_ffn_in': out['w_ffn_in'], 'w_ffn_out': out['w_ffn_out'], 'norm_final': out['norm_final'], 'loss_target': out['loss_target'], 'm_norm_mix': out['m_norm_mix'], 'm_w_qkv_even': out['m_w_qkv_even'], 'm_w_o_even': out['m_w_o_even'], 'm_w_qkvf_odd': out['m_w_qkvf_odd'], 'm_b_forget': out['m_b_forget'], 'm_w_o_odd': out['m_w_o_odd'], 'm_norm_ffn': out['m_norm_ffn'], 'm_w_ffn_in': out['m_w_ffn_in'], 'm_w_ffn_out': out['m_w_ffn_out'], 'm_norm_final': out['m_norm_final'], 'v_norm_mix': out['v_norm_mix'], 'v_w_qkv_even': out['v_w_qkv_even'], 'v_w_o_even': out['v_w_o_even'], 'v_w_qkvf_odd': out['v_w_qkvf_odd'], 'v_b_forget': out['v_b_forget'], 'v_w_o_odd': out['v_w_o_odd'], 'v_norm_ffn': out['v_norm_ffn'], 'v_w_ffn_in': out['v_w_ffn_in'], 'v_w_ffn_out': out['v_w_ffn_out'], 'v_norm_final': out['v_norm_final']}


def _loss(weights, diff, rest, loss_target):
    with _jax.named_scope("forward"):
        args = {**rest, TWIN_DIFF_INPUT: diff, **{k: w.astype(_WEIGHT_DTYPES[k]) for k, w in weights.items()}}
        y = _forward(args)
    with _jax.named_scope("loss_head"):
        err = _jnp.square(y.astype(_jnp.float32) - loss_target)
        return 0.5 * _jnp.sum(_jnp.mean(err, axis=-1)) if err.ndim else 0.5 * err


def _adamw(w, g, m, v):
    m = ADAM_B1 * m + (1.0 - ADAM_B1) * g
    v = ADAM_B2 * v + (1.0 - ADAM_B2) * _jnp.square(g)
    m_hat = m / (1.0 - ADAM_B1 ** ADAM_STEP)
    v_hat = v / (1.0 - ADAM_B2 ** ADAM_STEP)
    delta = -ADAM_LR * (m_hat / (_jnp.sqrt(v_hat) + ADAM_EPS) + ADAM_WD * w)
    return delta, m, v


def reference(x, norm_mix, w_qkv_even, w_o_even, w_qkvf_odd, b_forget, w_o_odd, norm_ffn, w_ffn_in, w_ffn_out, norm_final, loss_target, m_norm_mix, m_w_qkv_even, m_w_o_even, m_w_qkvf_odd, m_b_forget, m_w_o_odd, m_norm_ffn, m_w_ffn_in, m_w_ffn_out, m_norm_final, v_norm_mix, v_w_qkv_even, v_w_o_even, v_w_qkvf_odd, v_b_forget, v_w_o_odd, v_norm_ffn, v_w_ffn_in, v_w_ffn_out, v_norm_final):
    given = dict(x=x, norm_mix=norm_mix, w_qkv_even=w_qkv_even, w_o_even=w_o_even, w_qkvf_odd=w_qkvf_odd, b_forget=b_forget, w_o_odd=w_o_odd, norm_ffn=norm_ffn, w_ffn_in=w_ffn_in, w_ffn_out=w_ffn_out, norm_final=norm_final, loss_target=loss_target, m_norm_mix=m_norm_mix, m_w_qkv_even=m_w_qkv_even, m_w_o_even=m_w_o_even, m_w_qkvf_odd=m_w_qkvf_odd, m_b_forget=m_b_forget, m_w_o_odd=m_w_o_odd, m_norm_ffn=m_norm_ffn, m_w_ffn_in=m_w_ffn_in, m_w_ffn_out=m_w_ffn_out, m_norm_final=m_norm_final, v_norm_mix=v_norm_mix, v_w_qkv_even=v_w_qkv_even, v_w_o_even=v_w_o_even, v_w_qkvf_odd=v_w_qkvf_odd, v_b_forget=v_b_forget, v_w_o_odd=v_w_o_odd, v_norm_ffn=v_norm_ffn, v_w_ffn_in=v_w_ffn_in, v_w_ffn_out=v_w_ffn_out, v_norm_final=v_norm_final)
    weights = {n: given[n] for n in TWIN_WEIGHTS}
    shared = {n: given[n] for n in SHARED_INPUTS}
    per_example = {n: given[n] for n in ['x']}
    grad_fn = _jax.value_and_grad(_loss, argnums=(0, 1))

    def one_microbatch(ex, loss_target):
        ex = dict(ex)
        diff = ex.pop(TWIN_DIFF_INPUT)
        return grad_fn(weights, diff, {**shared, **ex}, loss_target)

    if N_MICROBATCH == 1:
        loss, (grad_w, grad_x) = one_microbatch(per_example, given["loss_target"])
    else:
        def body(carry, xs):
            loss_sum, grad_sum = carry
            l_k, (gw_k, gx_k) = one_microbatch(xs[0], xs[1])
            with _jax.named_scope("update"):
                return (loss_sum + l_k, _jax.tree.map(_jnp.add, grad_sum, gw_k)), gx_k

        init = (_jnp.zeros((), _jnp.float32), _jax.tree.map(_jnp.zeros_like, weights))
        (loss, grad_w), grad_x = _jax.lax.scan(body, init, (per_example, given["loss_target"]))
    with _jax.named_scope("update"):
        delta_w, new_m, new_v = {}, {}, {}
        for n in TWIN_WEIGHTS:
            delta_w[n], new_m[n], new_v[n] = _adamw(weights[n], grad_w[n], given["m_" + n], given["v_" + n])
    return (loss, grad_x, *[grad_w[n] for n in TWIN_WEIGHTS], *[delta_w[n] for n in TWIN_WEIGHTS],
            *[new_m[n] for n in TWIN_WEIGHTS], *[new_v[n] for n in TWIN_WEIGHTS])
```

```python
import functools
import math

import jax
import jax.numpy as jnp
from jax import lax
from jax.experimental import pallas as pl
from jax.experimental.pallas import tpu as pltpu

F32 = jnp.float32
BF16 = jnp.bfloat16

D_MODEL = 1024
HEAD_DIM = 64
N_HEADS = 16
D_ATTN = N_HEADS * HEAD_DIM
D_FF = 2816
DEPTH = 4
ROPE_THETA = 500000.0
ROT_DIM = HEAD_DIM // 4
RMS_EPS = 1e-5
SCALE = HEAD_DIM ** -0.5
DIL_PATTERNS = ((128, 1), (512, 4), (2048, 16))
N_DEV = 8
QKVF_PAD = 3200

ADAM_LR = 0.001
ADAM_B1 = 0.9
ADAM_B2 = 0.999
ADAM_EPS = 1e-08
ADAM_WD = 0.01
ADAM_STEP = 10

LANES = 128
BLK = 128
NEG = -1e30
VMEM_LIMIT = 48 * 1024 * 1024

MESH = pl.DeviceIdType.MESH


def _params(n_grid=0, **kw):
    sem = ("arbitrary",) * n_grid if n_grid else None
    return pltpu.CompilerParams(dimension_semantics=sem, vmem_limit_bytes=VMEM_LIMIT, **kw)


def _mm(a, b, *, name, ta=False, tb=False, add=None, out_dtype=F32, tm=512, tn=512, tk=None):
    m = a.shape[1] if ta else a.shape[0]
    k = a.shape[0] if ta else a.shape[1]
    n = b.shape[0] if tb else b.shape[1]
    assert (b.shape[1] if tb else b.shape[0]) == k
    tm, tn = min(tm, m), min(tn, n)
    tk = k if tk is None else min(tk, k)
    assert m % tm == 0 and n % tn == 0 and k % tk == 0, (name, m, n, k, tm, tn, tk)
    nk = k // tk
    dn = (((0 if ta else 1,), (1 if tb else 0,)), ((), ()))

    def body(*refs):
        a_ref, b_ref = refs[0], refs[1]
        add_ref = refs[2] if add is not None else None
        o_ref = refs[3] if add is not None else refs[2]
        part = lax.dot_general(a_ref[...], b_ref[...], dn, preferred_element_type=F32)
        if nk == 1:
            if add_ref is not None:
                part = part + add_ref[...]
            o_ref[...] = part.astype(out_dtype)
            return
        acc_ref = refs[-1]
        kk = pl.program_id(2)

        @pl.when(kk == 0)
        def _():
            acc_ref[...] = part

        @pl.when(kk > 0)
        def _():
            acc_ref[...] += part

        @pl.when(kk == nk - 1)
        def _():
            res = acc_ref[...]
            if add_ref is not None:
                res = res + add_ref[...]
            o_ref[...] = res.astype(out_dtype)

    a_spec = (pl.BlockSpec((tk, tm), lambda i, j, kk: (kk, i)) if ta
              else pl.BlockSpec((tm, tk), lambda i, j, kk: (i, kk)))
    b_spec = (pl.BlockSpec((tn, tk), lambda i, j, kk: (j, kk)) if tb
              else pl.BlockSpec((tk, tn), lambda i, j, kk: (kk, j)))
    o_spec = pl.BlockSpec((tm, tn), lambda i, j, kk: (i, j))
    in_specs = [a_spec, b_spec] + ([o_spec] if add is not None else [])
    args = (a, b) + ((add,) if add is not None else ())
    return pl.pallas_call(
        body, name=name, grid=(m // tm, n // tn, nk),
        in_specs=in_specs, out_specs=o_spec,
        out_shape=jax.ShapeDtypeStruct((m, n), out_dtype),
        scratch_shapes=[pltpu.VMEM((tm, tn), F32)] if nk > 1 else [],
        compiler_params=_params(3),
    )(*args)


def _rms_fwd(x, g, *, name, tr=256):
    s, d = x.shape

    def body(x_ref, g_ref, h_ref):
        xv = x_ref[...]
        r = lax.rsqrt(jnp.mean(xv * xv, axis=-1, keepdims=True) + RMS_EPS)
        h_ref[...] = (xv * r * g_ref[...]).astype(BF16)

    return pl.pallas_call(
        body, name=name, grid=(s // tr,),
        in_specs=[pl.BlockSpec((tr, d), lambda i: (i, 0)), pl.BlockSpec((1, d), lambda i: (0, 0))],
        out_specs=pl.BlockSpec((tr, d), lambda i: (i, 0)),
        out_shape=jax.ShapeDtypeStruct((s, d), BF16),
        compiler_params=_params(1),
    )(x, g)


def _rms_bwd(x, g, dh, dres, *, name, tr=256):
    s, d = x.shape

    def body(x_ref, g_ref, dh_ref, dres_ref, dx_ref, dxb_ref, dg_ref):
        xv = x_ref[...]
        r = lax.rsqrt(jnp.mean(xv * xv, axis=-1, keepdims=True) + RMS_EPS)
        y = xv * r
        dhv = dh_ref[...]
        dy = dhv * g_ref[...]
        dx = dres_ref[...] + r * (dy - y * jnp.mean(dy * y, axis=-1, keepdims=True))
        dx_ref[...] = dx
        dxb_ref[...] = dx.astype(BF16)
        part = jnp.sum(dhv * y, axis=0, keepdims=True)

        @pl.when(pl.program_id(0) == 0)
        def _():
            dg_ref[...] = part

        @pl.when(pl.program_id(0) > 0)
        def _():
            dg_ref[...] += part

    row = pl.BlockSpec((tr, d), lambda i: (i, 0))
    vec = pl.BlockSpec((1, d), lambda i: (0, 0))
    return pl.pallas_call(
        body, name=name, grid=(s // tr,),
        in_specs=[row, vec, row, row], out_specs=[row, row, vec],
        out_shape=[jax.ShapeDtypeStruct((s, d), F32), jax.ShapeDtypeStruct((s, d), BF16),
                   jax.ShapeDtypeStruct((1, d), F32)],
        compiler_params=_params(1),
    )(x, g, dh, dres)


def _final_loss(x, g, tgt, *, name, tr=256):
    s, d = x.shape

    def body(x_ref, g_ref, t_ref, loss_ref, dx_ref, dxb_ref, dg_ref):
        xv = x_ref[...]
        gv = g_ref[...]
        r = lax.rsqrt(jnp.mean(xv * xv, axis=-1, keepdims=True) + RMS_EPS)
        y = xv * r
        err = y * gv - t_ref[...]
        lpart = 0.5 * jnp.sum(jnp.mean(err * err, axis=-1, keepdims=True), axis=0, keepdims=True)
        dh = err * (1.0 / d)
        dy = dh * gv
        dx = r * (dy - y * jnp.mean(dy * y, axis=-1, keepdims=True))
        dx_ref[...] = dx
        dxb_ref[...] = dx.astype(BF16)
        gpart = jnp.sum(dh * y, axis=0, keepdims=True)
        lrow = jnp.broadcast_to(lpart, (1, LANES))

        @pl.when(pl.program_id(0) == 0)
        def _():
            dg_ref[...] = gpart
            loss_ref[...] = lrow

        @pl.when(pl.program_id(0) > 0)
        def _():
            dg_ref[...] += gpart
            loss_ref[...] += lrow

    row = pl.BlockSpec((tr, d), lambda i: (i, 0))
    vec = pl.BlockSpec((1, d), lambda i: (0, 0))
    lsp = pl.BlockSpec((1, LANES), lambda i: (0, 0))
    return pl.pallas_call(
        body, name=name, grid=(s // tr,),
        in_specs=[row, vec, row], out_specs=[lsp, row, row, vec],
        out_shape=[jax.ShapeDtypeStruct((1, LANES), F32), jax.ShapeDtypeStruct((s, d), F32),
                   jax.ShapeDtypeStruct((s, d), BF16), jax.ShapeDtypeStruct((1, d), F32)],
        compiler_params=_params(1),
    )(x, g, tgt)


FF_BLK = D_FF // 2


def _swiglu_fwd(gu, *, name, tr=256):
    s = gu.shape[0]

    def body(g_ref, u_ref, a_ref):
        gv = g_ref[...]
        a_ref[...] = (gv * jax.nn.sigmoid(gv) * u_ref[...]).astype(BF16)

    return pl.pallas_call(
        body, name=name, grid=(s // tr, 2),
        in_specs=[pl.BlockSpec((tr, FF_BLK), lambda i, j: (i, j)),
                  pl.BlockSpec((tr, FF_BLK), lambda i, j: (i, j + 2))],
        out_specs=pl.BlockSpec((tr, FF_BLK), lambda i, j: (i, j)),
        out_shape=jax.ShapeDtypeStruct((s, D_FF), BF16),
        compiler_params=_params(2),
    )(gu, gu)


def _swiglu_bwd(da, gu, *, name, tr=256):
    s = gu.shape[0]

    def body(da_ref, g_ref, u_ref, o_ref):
        gv = g_ref[...]
        dav = da_ref[...]
        sg = jax.nn.sigmoid(gv)
        j = pl.program_id(1)

        @pl.when(j < 2)
        def _():
            o_ref[...] = (dav * u_ref[...] * (sg * (1.0 + gv * (1.0 - sg)))).astype(BF16)

        @pl.when(j >= 2)
        def _():
            o_ref[...] = (dav * gv * sg).astype(BF16)

    return pl.pallas_call(
        body, name=name, grid=(s // tr, 4),
        in_specs=[pl.BlockSpec((tr, FF_BLK), lambda i, j: (i, j % 2)),
                  pl.BlockSpec((tr, FF_BLK), lambda i, j: (i, j % 2)),
                  pl.BlockSpec((tr, FF_BLK), lambda i, j: (i, 2 + j % 2))],
        out_specs=pl.BlockSpec((tr, FF_BLK), lambda i, j: (i, j)),
        out_shape=jax.ShapeDtypeStruct((s, 2 * D_FF), BF16),
        compiler_params=_params(2),
    )(da, gu, gu)


def _split3(x):
    hi = x.astype(BF16)
    r1 = x - hi.astype(F32)
    mid = r1.astype(BF16)
    lo = (r1 - mid.astype(F32)).astype(BF16)
    return hi, mid, lo


def _dot3(x, m_bf):
    hi, mid, lo = _split3(x)
    return (jnp.dot(hi, m_bf, preferred_element_type=F32)
            + jnp.dot(mid, m_bf, preferred_element_type=F32)
            + jnp.dot(lo, m_bf, preferred_element_type=F32))


def _nt(a, b):
    return lax.dot_general(a, b, (((1,), (1,)), ((), ())), preferred_element_type=F32)


def _tn(a, b):
    return lax.dot_general(a, b, (((0,), (0,)), ((), ())), preferred_element_type=F32)


def _iota2(shape, dim):
    return lax.broadcasted_iota(jnp.int32, shape, dim)


def _rope_tables(s):
    half = ROT_DIM // 2
    pos = jnp.arange(s, dtype=F32)
    inv_freq = ROPE_THETA ** (-jnp.arange(half, dtype=F32) * 2.0 / ROT_DIM)
    ang = pos[:, None] * inv_freq[None, :]
    cos, sin = jnp.cos(ang), jnp.sin(ang)
    ones = jnp.ones((s, HEAD_DIM - ROT_DIM), F32)
    cos_t = jnp.concatenate([cos, cos, ones], axis=1)
    sin_t = jnp.concatenate([-sin, sin, 0.0 * ones], axis=1)
    idx = jnp.arange(HEAD_DIM)
    partner = jnp.where(idx < half, idx + half, idx - half)
    swap = ((idx[:, None] == partner[None, :]) & (idx[None, :] < ROT_DIM)).astype(BF16)
    return cos_t, sin_t, swap


def _rope(x, cos_t, sin_t, swap):
    return x * cos_t + _dot3(x, swap) * sin_t


def _rope_t(g, cos_t, sin_t, swap):
    return g * cos_t + _dot3(g * sin_t, swap)


def _rope_qk(q_ref, k_ref, qb, kb, cos_ref, sin_ref, swap):
    def step(i, _):
        r0 = pl.multiple_of(i * BLK, BLK)
        c, sn = cos_ref[pl.ds(r0, BLK), :], sin_ref[pl.ds(r0, BLK), :]
        for hh in range(2):
            lo = hh * HEAD_DIM
            qb[pl.ds(r0, BLK), lo:lo + HEAD_DIM] = _rope(q_ref[pl.ds(r0, BLK), lo:lo + HEAD_DIM], c, sn, swap).astype(BF16)
            kb[pl.ds(r0, BLK), lo:lo + HEAD_DIM] = _rope(k_ref[pl.ds(r0, BLK), lo:lo + HEAD_DIM], c, sn, swap).astype(BF16)
        return 0

    lax.fori_loop(0, q_ref.shape[0] // BLK, step, 0)


def _dil_weight(off, rmc):
    dlt = off * BLK + rmc
    nonneg = dlt >= 0
    w = jnp.zeros(rmc.shape, F32)
    for window, dil in DIL_PATTERNS:
        ok = nonneg & (dlt <= window) & ((dlt & (dil - 1)) == 0)
        w = w + ok.astype(F32)
    return w


def _head_spec(s, col0):
    return pl.BlockSpec((s, LANES), lambda p: (0, col0 + p))


def _stat_spec(s):
    return pl.BlockSpec((2, s, 1), lambda p: (p, 0, 0))


def _rowstat_spec(s):
    return pl.BlockSpec((2, 1, s), lambda p: (p, 0, 0))


def _full_spec(shape):
    nd = len(shape)
    return pl.BlockSpec(shape, lambda p: (0,) * nd)


def _log_sig_pair(z):
    sp = jnp.log(1.0 + jnp.exp(-jnp.abs(z)))
    return jnp.minimum(z, 0.0) - sp, -jnp.maximum(z, 0.0) - sp


def _sb_fwd(qkv, n_pairs, *, name):
    s = qkv.shape[0]
    nq = s // BLK
    kcol, vcol = D_ATTN // LANES, 2 * D_ATTN // LANES

    def body(q_ref, k_ref, v_ref, o_ref, ct_ref, qb, kb, vb):
        qb[...] = q_ref[...].astype(BF16)
        kb[...] = k_ref[...].astype(BF16)
        vb[...] = v_ref[...].astype(BF16)
        row, col = _iota2((BLK, BLK), 0), _iota2((BLK, BLK), 1)
        strict = col < row
        u_gt = (row > col).astype(BF16)
        for hh in range(2):
            lo = hh * HEAD_DIM

            def qloop(i, _):
                r0 = pl.multiple_of(i * BLK, BLK)
                q = qb[pl.ds(r0, BLK), lo:lo + HEAD_DIM]

                def kloop(t, carry):
                    c, acc = carry
                    k0 = pl.multiple_of((i - t) * BLK, BLK)
                    k = kb[pl.ds(k0, BLK), lo:lo + HEAD_DIM]
                    v = vb[pl.ds(k0, BLK), lo:lo + HEAD_DIM]
                    z = _nt(q, k) * SCALE
                    mask = jnp.logical_or(t > 0, strict)
                    lb, lm = _log_sig_pair(z)
                    lm = jnp.where(mask, lm, 0.0)
                    rc = _dot3(lm, u_gt)
                    a = jnp.where(mask, jnp.exp(lb + rc + c), 0.0)
                    acc = acc + jnp.dot(a.astype(BF16), v, preferred_element_type=F32)
                    c = c + jnp.sum(lm, axis=1, keepdims=True)
                    return c, acc

                c, acc = lax.fori_loop(0, i + 1, kloop,
                                       (jnp.zeros((BLK, 1), F32), jnp.zeros((BLK, HEAD_DIM), F32)))
                o_ref[pl.ds(r0, BLK), lo:lo + HEAD_DIM] = acc
                ct_ref[hh, pl.ds(r0, BLK), :] = c
                return 0

            lax.fori_loop(0, nq, qloop, 0)

    return pl.pallas_call(
        body, name=name, grid=(n_pairs,),
        in_specs=[_head_spec(s, 0), _head_spec(s, kcol), _head_spec(s, vcol)],
        out_specs=[_head_spec(s, 0), _stat_spec(s)],
        out_shape=[jax.ShapeDtypeStruct((s, LANES * n_pairs), F32),
                   jax.ShapeDtypeStruct((2 * n_pairs, s, 1), F32)],
        scratch_shapes=[pltpu.VMEM((s, LANES), BF16)] * 3,
        compiler_params=_params(1),
    )(qkv, qkv, qkv)


def _sb_bwd(qkv, do, ctot, n_pairs, do_col0, *, name):
    s = qkv.shape[0]
    nq = s // BLK
    kcol, vcol = D_ATTN // LANES, 2 * D_ATTN // LANES

    def body(q_ref, k_ref, v_ref, do_ref, ct_ref, dq_ref, dk_ref, dv_ref, qb, kb, vb, dks, dvs):
        qb[...] = q_ref[...].astype(BF16)
        kb[...] = k_ref[...].astype(BF16)
        vb[...] = v_ref[...].astype(BF16)
        row, col = _iota2((BLK, BLK), 0), _iota2((BLK, BLK), 1)
        strict = col < row
        u_le = (row <= col).astype(BF16)
        u_lt = (row < col).astype(BF16)
        for hh in range(2):
            lo = hh * HEAD_DIM
            dks[...] = jnp.zeros_like(dks)
            dvs[...] = jnp.zeros_like(dvs)

            def qloop(i, _):
                r0 = pl.multiple_of(i * BLK, BLK)
                q = qb[pl.ds(r0, BLK), lo:lo + HEAD_DIM]
                dob = do_ref[pl.ds(r0, BLK), lo:lo + HEAD_DIM].astype(BF16)
                ct = ct_ref[hh, pl.ds(r0, BLK), :]

                def kloop(kbi, carry):
                    pl_, hl, dq = carry
                    k0 = pl.multiple_of(kbi * BLK, BLK)
                    k = kb[pl.ds(k0, BLK), lo:lo + HEAD_DIM]
                    v = vb[pl.ds(k0, BLK), lo:lo + HEAD_DIM]
                    z = _nt(q, k) * SCALE
                    mask = jnp.logical_or(kbi < i, strict)
                    lb, lm = _log_sig_pair(z)
                    lm = jnp.where(mask, lm, 0.0)
                    between = ct - pl_ - _dot3(lm, u_le)
                    a = jnp.where(mask, jnp.exp(lb + between), 0.0)
                    g = a * _nt(dob, v)
                    h = hl + _dot3(g, u_lt)
                    beta = jnp.exp(lb)
                    dz = jnp.where(mask, g * (1.0 - beta) - h * beta, 0.0)
                    dzb = (dz * SCALE).astype(BF16)
                    dq = dq + jnp.dot(dzb, k, preferred_element_type=F32)
                    dks[pl.ds(k0, BLK), :] += _tn(dzb, q)
                    dvs[pl.ds(k0, BLK), :] += _tn(a.astype(BF16), dob)
                    pl_ = pl_ + jnp.sum(lm, axis=1, keepdims=True)
                    hl = hl + jnp.sum(g, axis=1, keepdims=True)
                    return pl_, hl, dq

                z1 = jnp.zeros((BLK, 1), F32)
                _, _, dq = lax.fori_loop(0, i + 1, kloop, (z1, z1, jnp.zeros((BLK, HEAD_DIM), F32)))
                dq_ref[pl.ds(r0, BLK), lo:lo + HEAD_DIM] = dq.astype(BF16)
                return 0

            lax.fori_loop(0, nq, qloop, 0)

            def wloop(i, _):
                r0 = pl.multiple_of(i * BLK, BLK)
                dk_ref[pl.ds(r0, BLK), lo:lo + HEAD_DIM] = dks[pl.ds(r0, BLK), :].astype(BF16)
                dv_ref[pl.ds(r0, BLK), lo:lo + HEAD_DIM] = dvs[pl.ds(r0, BLK), :].astype(BF16)
                return 0

            lax.fori_loop(0, nq, wloop, 0)

    out = jax.ShapeDtypeStruct((s, LANES * n_pairs), BF16)
    return pl.pallas_call(
        body, name=name, grid=(n_pairs,),
        in_specs=[_head_spec(s, 0), _head_spec(s, kcol), _head_spec(s, vcol),
                  _head_spec(s, do_col0), _stat_spec(s)],
        out_specs=[_head_spec(s, 0)] * 3,
        out_shape=[out, out, out],
        scratch_shapes=[pltpu.VMEM((s, LANES), BF16)] * 3 + [pltpu.VMEM((s, HEAD_DIM), F32)] * 2,
        compiler_params=_params(1),
    )(qkv, qkv, qkv, do, ctot)


def _bias_fwd(mode, qkv, head0_col, n_pairs, extra, *, name):
    s = qkv.shape[0]
    nq = s // BLK
    kcol, vcol = D_ATTN // LANES, 2 * D_ATTN // LANES
    fox = mode == "fox"

    def body(q_ref, k_ref, v_ref, e0, e1, *rest):
        if fox:
            o_ref, lse_ref, qb, kb, vb = rest
        else:
            e2, o_ref, lse_ref, qb, kb, vb = rest
        vb[...] = v_ref[...].astype(BF16)
        if fox:
            qb[...] = q_ref[...].astype(BF16)
            kb[...] = k_ref[...].astype(BF16)
        else:
            _rope_qk(q_ref, k_ref, qb, kb, e0, e1, e2[...])
        row, col = _iota2((BLK, BLK), 0), _iota2((BLK, BLK), 1)
        causal = col <= row
        rmc = row - col
        for hh in range(2):
            lo = hh * HEAD_DIM

            def qloop(i, _):
                r0 = pl.multiple_of(i * BLK, BLK)
                q = qb[pl.ds(r0, BLK), lo:lo + HEAD_DIM]
                if fox:
                    fq = e0[hh, pl.ds(r0, BLK), :]

                def kloop(kbi, carry):
                    m, l, acc = carry
                    k0 = pl.multiple_of(kbi * BLK, BLK)
                    k = kb[pl.ds(k0, BLK), lo:lo + HEAD_DIM]
                    v = vb[pl.ds(k0, BLK), lo:lo + HEAD_DIM]
                    sc = _nt(q, k) * SCALE
                    if fox:
                        sc = sc + (fq - e1[hh, :, pl.ds(k0, BLK)])
                        valid = jnp.logical_or(kbi < i, causal)
                        sc = jnp.where(valid, sc, NEG)
                    else:
                        w = _dil_weight(i - kbi, rmc)
                        sc = jnp.where(w > 0.0, sc, NEG)
                    m_new = jnp.maximum(m, jnp.max(sc, axis=1, keepdims=True))
                    alpha = jnp.exp(m - m_new)
                    p = jnp.exp(sc - m_new)
                    p = jnp.where(valid, p, 0.0) if fox else p * w
                    l = alpha * l + jnp.sum(p, axis=1, keepdims=True)
                    acc = alpha * acc + jnp.dot(p.astype(BF16), v, preferred_element_type=F32)
                    return m_new, l, acc

                m, l, acc = lax.fori_loop(
                    0, i + 1, kloop,
                    (jnp.full((BLK, 1), NEG, F32), jnp.zeros((BLK, 1), F32), jnp.zeros((BLK, HEAD_DIM), F32)))
                o_ref[pl.ds(r0, BLK), lo:lo + HEAD_DIM] = acc / l
                lse_ref[hh, pl.ds(r0, BLK), :] = m + jnp.log(l)
                return 0

            lax.fori_loop(0, nq, qloop, 0)

    hp0 = head0_col
    if fox:
        e_specs = [_stat_spec(s), _rowstat_spec(s)]
    else:
        e_specs = [_full_spec((s, HEAD_DIM)), _full_spec((s, HEAD_DIM)), _full_spec((HEAD_DIM, HEAD_DIM))]
    return pl.pallas_call(
        body, name=name, grid=(n_pairs,),
        in_specs=[_head_spec(s, hp0), _head_spec(s, kcol + hp0), _head_spec(s, vcol + hp0)] + e_specs,
        out_specs=[_head_spec(s, 0), _stat_spec(s)],
        out_shape=[jax.ShapeDtypeStruct((s, LANES * n_pairs), F32),
                   jax.ShapeDtypeStruct((2 * n_pairs, s, 1), F32)],
        scratch_shapes=[pltpu.VMEM((s, LANES), BF16)] * 3,
        compiler_params=_params(1),
    )(qkv, qkv, qkv, *extra)


def _bias_bwd(mode, qkv, head0_col, n_pairs, extra, o, do, do_col0, lse, *, name):
    s = qkv.shape[0]
    nq = s // BLK
    kcol, vcol = D_ATTN // LANES, 2 * D_ATTN // LANES
    fox = mode == "fox"

    def body(q_ref, k_ref, v_ref, o_ref, do_ref, lse_ref, e0, e1, *rest):
        if fox:
            dq_ref, dk_ref, dv_ref, dfr_ref, dfc_ref, qb, kb, vb, dks, dvs = rest
        else:
            e2, dq_ref, dk_ref, dv_ref, qb, kb, vb, dks, dvs = rest
        vb[...] = v_ref[...].astype(BF16)
        if fox:
            qb[...] = q_ref[...].astype(BF16)
            kb[...] = k_ref[...].astype(BF16)
            dfr_ref[...] = jnp.zeros_like(dfr_ref)
        else:
            swap = e2[...]
            _rope_qk(q_ref, k_ref, qb, kb, e0, e1, swap)
        row, col = _iota2((BLK, BLK), 0), _iota2((BLK, BLK), 1)
        causal = col <= row
        rmc = row - col
        for hh in range(2):
            lo = hh * HEAD_DIM
            dks[...] = jnp.zeros_like(dks)
            dvs[...] = jnp.zeros_like(dvs)

            def qloop(i, _):
                r0 = pl.multiple_of(i * BLK, BLK)
                q = qb[pl.ds(r0, BLK), lo:lo + HEAD_DIM]
                dov = do_ref[pl.ds(r0, BLK), lo:lo + HEAD_DIM]
                dob = dov.astype(BF16)
                dsum = jnp.sum(dov * o_ref[pl.ds(r0, BLK), lo:lo + HEAD_DIM], axis=1, keepdims=True)
                lse_i = lse_ref[hh, pl.ds(r0, BLK), :]
                if fox:
                    fq = e0[hh, pl.ds(r0, BLK), :]

                def kloop(kbi, carry):
                    dq, rsum = carry
                    k0 = pl.multiple_of(kbi * BLK, BLK)
                    k = kb[pl.ds(k0, BLK), lo:lo + HEAD_DIM]
                    v = vb[pl.ds(k0, BLK), lo:lo + HEAD_DIM]
                    sc = _nt(q, k) * SCALE
                    if fox:
                        sc = sc + (fq - e1[hh, :, pl.ds(k0, BLK)])
                        valid = jnp.logical_or(kbi < i, causal)
                        p = jnp.where(valid, jnp.exp(jnp.where(valid, sc, NEG) - lse_i), 0.0)
                    else:
                        w = _dil_weight(i - kbi, rmc)
                        p = w * jnp.exp(jnp.where(w > 0.0, sc, NEG) - lse_i)
                    ds = p * (_nt(dob, v) - dsum)
                    dsb = (ds * SCALE).astype(BF16)
                    dq = dq + jnp.dot(dsb, k, preferred_element_type=F32)
                    dks[pl.ds(k0, BLK), :] += _tn(dsb, q)
                    dvs[pl.ds(k0, BLK), :] += _tn(p.astype(BF16), dob)
                    if fox:
                        dfr_ref[hh, :, pl.ds(k0, BLK)] -= jnp.sum(ds, axis=0, keepdims=True)
                        rsum = rsum + jnp.sum(ds, axis=1, keepdims=True)
                    return dq, rsum

                dq, rsum = lax.fori_loop(0, i + 1, kloop,
                                         (jnp.zeros((BLK, HEAD_DIM), F32), jnp.zeros((BLK, 1), F32)))
                if fox:
                    dfc_ref[hh, pl.ds(r0, BLK), :] = rsum
                if not fox:
                    dq = _rope_t(dq, e0[pl.ds(r0, BLK), :], e1[pl.ds(r0, BLK), :], swap)
                dq_ref[pl.ds(r0, BLK), lo:lo + HEAD_DIM] = dq.astype(BF16)
                return 0

            lax.fori_loop(0, nq, qloop, 0)

            def wloop(i, _):
                r0 = pl.multiple_of(i * BLK, BLK)
                dk = dks[pl.ds(r0, BLK), :]
                if not fox:
                    dk = _rope_t(dk, e0[pl.ds(r0, BLK), :], e1[pl.ds(r0, BLK), :], swap)
                dk_ref[pl.ds(r0, BLK), lo:lo + HEAD_DIM] = dk.astype(BF16)
                dv_ref[pl.ds(r0, BLK), lo:lo + HEAD_DIM] = dvs[pl.ds(r0, BLK), :].astype(BF16)
                return 0

            lax.fori_loop(0, nq, wloop, 0)

    hp0 = head0_col
    out = jax.ShapeDtypeStruct((s, LANES * n_pairs), BF16)
    out_specs = [_head_spec(s, 0)] * 3
    out_shape = [out, out, out]
    if fox:
        e_specs = [_stat_spec(s), _rowstat_spec(s)]
        out_specs += [_rowstat_spec(s), _stat_spec(s)]
        out_shape += [jax.ShapeDtypeStruct((2 * n_pairs, 1, s), F32), jax.ShapeDtypeStruct((2 * n_pairs, s, 1), F32)]
    else:
        e_specs = [_full_spec((s, HEAD_DIM)), _full_spec((s, HEAD_DIM)), _full_spec((HEAD_DIM, HEAD_DIM))]
    return pl.pallas_call(
        body, name=name, grid=(n_pairs,),
        in_specs=[_head_spec(s, hp0), _head_spec(s, kcol + hp0), _head_spec(s, vcol + hp0),
                  _head_spec(s, 0), _head_spec(s, do_col0), _stat_spec(s)] + e_specs,
        out_specs=out_specs, out_shape=out_shape,
        scratch_shapes=[pltpu.VMEM((s, LANES), BF16)] * 3 + [pltpu.VMEM((s, HEAD_DIM), F32)] * 2,
        compiler_params=_params(1),
    )(qkv, qkv, qkv, o, do, lse, *extra)


F_COL = 3 * D_ATTN // LANES


def _dot3_left(m_bf, x):
    hi, mid, lo = _split3(x)
    return (jnp.dot(m_bf, hi, preferred_element_type=F32)
            + jnp.dot(m_bf, mid, preferred_element_type=F32)
            + jnp.dot(m_bf, lo, preferred_element_type=F32))


def _fgate_fwd(qkvf, brow, *, name):
    s = qkvf.shape[0]
    nb = s // BLK

    def body(f_ref, b_ref, fc_ref, fr_ref, fs):
        row, col = _iota2((BLK, BLK), 0), _iota2((BLK, BLK), 1)
        l_incl = (col <= row).astype(BF16)

        def step(i, carry):
            r0 = pl.multiple_of(i * BLK, BLK)
            lf, _ = _log_sig_pair(f_ref[pl.ds(r0, BLK), :] + b_ref[...])
            fblk = carry + _dot3_left(l_incl, lf)
            fs[pl.ds(r0, BLK), :] = fblk
            return fblk[BLK - 1:BLK, :]

        lax.fori_loop(0, nb, step, jnp.zeros((1, LANES), F32))
        ft = fs[...].T
        for h in range(N_HEADS):
            fc_ref[h, :, :] = fs[:, h:h + 1]
            fr_ref[h, :, :] = ft[h:h + 1, :]

    return pl.pallas_call(
        body, name=name, grid=(1,),
        in_specs=[pl.BlockSpec((s, LANES), lambda i: (0, F_COL)), pl.BlockSpec((1, LANES), lambda i: (0, 0))],
        out_specs=[pl.BlockSpec((N_HEADS, s, 1), lambda i: (0, 0, 0)),
                   pl.BlockSpec((N_HEADS, 1, s), lambda i: (0, 0, 0))],
        out_shape=[jax.ShapeDtypeStruct((N_HEADS, s, 1), F32), jax.ShapeDtypeStruct((N_HEADS, 1, s), F32)],
        scratch_shapes=[pltpu.VMEM((s, LANES), F32)],
        compiler_params=_params(1),
    )(qkvf, brow)


def _fgate_bwd(dfr, dfc, qkvf, brow, *, name):
    s = qkvf.shape[0]
    nb = s // BLK

    def body(dfr_ref, dfc_ref, f_ref, b_ref, dfl_ref, db_ref, ts, fs):
        ts[...] = jnp.zeros_like(ts)
        for h in range(N_HEADS):
            ts[h:h + 1, :] = dfr_ref[h]
        fs[...] = ts[...].T
        for h in range(N_HEADS):
            fs[:, h:h + 1] += dfc_ref[h]
        row, col = _iota2((BLK, BLK), 0), _iota2((BLK, BLK), 1)
        u_incl = (col >= row).astype(BF16)
        head_lane = _iota2((BLK, LANES), 1) < N_HEADS

        def step(ii, carry):
            tail, db = carry
            r0 = pl.multiple_of((nb - 1 - ii) * BLK, BLK)
            rblk = tail + _dot3_left(u_incl, fs[pl.ds(r0, BLK), :])
            _, lsn = _log_sig_pair(f_ref[pl.ds(r0, BLK), :] + b_ref[...])
            dfl = jnp.where(head_lane, rblk * jnp.exp(lsn), 0.0)
            dfl_ref[pl.ds(r0, BLK), :] = dfl.astype(BF16)
            return rblk[0:1, :], db + jnp.sum(dfl, axis=0, keepdims=True)

        z = jnp.zeros((1, LANES), F32)
        _, db = lax.fori_loop(0, nb, step, (z, z))
        db_ref[...] = db

    return pl.pallas_call(
        body, name=name, grid=(1,),
        in_specs=[pl.BlockSpec((N_HEADS, 1, s), lambda i: (0, 0, 0)), pl.BlockSpec((N_HEADS, s, 1), lambda i: (0, 0, 0)),
                  pl.BlockSpec((s, LANES), lambda i: (0, F_COL)), pl.BlockSpec((1, LANES), lambda i: (0, 0))],
        out_specs=[pl.BlockSpec((s, LANES), lambda i: (0, 0)), pl.BlockSpec((1, LANES), lambda i: (0, 0))],
        out_shape=[jax.ShapeDtypeStruct((s, LANES), BF16), jax.ShapeDtypeStruct((1, LANES), F32)],
        scratch_shapes=[pltpu.VMEM((LANES, s), F32), pltpu.VMEM((s, LANES), F32)],
        compiler_params=_params(1),
    )(dfr, dfc, qkvf, brow)


def _adamw_math(w, g, m, v):
    m2 = ADAM_B1 * m + (1.0 - ADAM_B1) * g
    v2 = ADAM_B2 * v + (1.0 - ADAM_B2) * (g * g)
    m_hat = m2 / (1.0 - ADAM_B1 ** ADAM_STEP)
    v_hat = v2 / (1.0 - ADAM_B2 ** ADAM_STEP)
    delta = -ADAM_LR * (m_hat / (jnp.sqrt(v_hat) + ADAM_EPS) + ADAM_WD * w)
    return delta, m2, v2


def _row_tile(r, cap=256, mult=16):
    best = None
    for t in range(mult, min(r, cap) + 1, mult):
        if r % t == 0:
            best = t
    assert best is not None, r
    return best


def _adamw_shard(w, m, v, lidx, g_all, r1, r2, sc, prev, *, name):
    nl, r, c = w.shape
    tr = _row_tile(r)

    def body(sc_ref, w_ref, m_ref, v_ref, g_ref, r1_ref, r2_ref, *rest):
        go_ref, d_ref, mo_ref, vo_ref = rest[-4:]
        g = g_ref[...] + r1_ref[...]
        g = g + r2_ref[0].astype(F32)
        g = g + r2_ref[1].astype(F32)
        g = g + r2_ref[2].astype(F32)
        delta, m2, v2 = _adamw_math(w_ref[...], g, m_ref[...], v_ref[...])
        go_ref[...] = g
        d_ref[...] = delta
        mo_ref[...] = m2
        vo_ref[...] = v2

    lay = pl.BlockSpec((None, tr, c), lambda i, s_: (lidx, i, 0))
    in_specs = [lay, lay, lay,
                pl.BlockSpec((None, tr, c), lambda i, s_: (s_[0], i, 0)),
                pl.BlockSpec((None, tr, c), lambda i, s_: (s_[1], i, 0)),
                pl.BlockSpec((3, tr, c), lambda i, s_: (0, i, 0))]
    args = [sc, w, m, v, g_all, r1, r2]
    aliases = {}
    if prev is not None:
        in_specs += [pl.BlockSpec(memory_space=pl.ANY)] * 4
        aliases = {7 + t: t for t in range(4)}
        args += list(prev)
    shp = jax.ShapeDtypeStruct((nl, r, c), F32)
    return pl.pallas_call(
        body, name=name,
        grid_spec=pltpu.PrefetchScalarGridSpec(
            num_scalar_prefetch=1, grid=(r // tr,), in_specs=in_specs, out_specs=[lay] * 4),
        out_shape=[shp] * 4, input_output_aliases=aliases,
        compiler_params=_params(1),
    )(*args)


def _adamw_small(w, g, m, v, *, name):
    def body(w_ref, g_ref, m_ref, v_ref, d_ref, mo_ref, vo_ref):
        delta, m2, v2 = _adamw_math(w_ref[...], g_ref[...], m_ref[...], v_ref[...])
        d_ref[...] = delta
        mo_ref[...] = m2
        vo_ref[...] = v2

    shp = jax.ShapeDtypeStruct(w.shape, F32)
    return pl.pallas_call(body, name=name, out_shape=[shp] * 3, compiler_params=_params())(w, g, m, v)


def _pos():
    return lax.axis_index("x"), lax.axis_index("y"), lax.axis_index("c")


def _other_chips(x, y):
    return [(1 - x, y), (x, 1 - y), (1 - x, 1 - y)]


def _dev_index(x, y, c):
    return 4 * x + 2 * y + c


HBM_SPEC = pl.BlockSpec(memory_space=pltpu.HBM)


def _all_gather(shards, *, name):
    n = len(shards)

    def body(*refs):
        xs, outs = refs[:n], refs[n:2 * n]
        send, recv, loc = refs[2 * n:]
        x, y, c = _pos()
        me, sib = (x, y, c), (x, y, 1 - c)
        chips = _other_chips(x, y)

        def copy(a, k, block, to, src=None):
            dst = outs[a].at[_dev_index(*block)]
            return pltpu.make_async_remote_copy(
                src_ref=dst if src is None else src, dst_ref=dst,
                send_sem=send.at[a, k], recv_sem=recv.at[a, k], device_id=to, device_id_type=MESH)

        mine = [pltpu.make_async_copy(xs[a], outs[a].at[_dev_index(*me)], loc.at[a]) for a in range(n)]
        for cp in mine:
            cp.start()
        first = []
        for a in range(n):
            first.append(copy(a, 0, me, sib, src=xs[a]))
            first += [copy(a, 1 + j, me, (*chip, c), src=xs[a]) for j, chip in enumerate(chips)]
        for cp in first:
            cp.start()
        passed = []
        for j, chip in enumerate(chips):
            for a in range(n):
                copy(a, 1 + j, (*chip, c), me).wait_recv()
                fwd = copy(a, 4 + j, (*chip, c), sib)
                fwd.start()
                passed.append(fwd)
        for a in range(n):
            copy(a, 0, sib, me).wait_recv()
            for j, chip in enumerate(chips):
                copy(a, 4 + j, (*chip, 1 - c), me).wait_recv()
        for cp in first + passed:
            cp.wait_send()
        for cp in mine:
            cp.wait()

    return pl.pallas_call(
        body, name=name,
        in_specs=[HBM_SPEC] * n, out_specs=[HBM_SPEC] * n,
        out_shape=[jax.ShapeDtypeStruct((N_DEV,) + a.shape, a.dtype) for a in shards],
        scratch_shapes=[pltpu.SemaphoreType.DMA((n, 7)), pltpu.SemaphoreType.DMA((n, 7)),
                        pltpu.SemaphoreType.DMA((n,))],
    )(*shards)


def _rs_sibling(gs, *, name):
    n = len(gs)

    def body(*refs):
        g_refs, r_refs = refs[:n], refs[n:2 * n]
        send, recv = refs[2 * n:]
        x, y, c = _pos()
        copies = []
        for a in range(n):
            for k in range(4):
                copies.append(pltpu.make_async_remote_copy(
                    src_ref=g_refs[a].at[_dev_index(k // 2, k % 2, 1 - c)], dst_ref=r_refs[a].at[k],
                    send_sem=send.at[a, k], recv_sem=recv.at[a, k],
                    device_id=(x, y, 1 - c), device_id_type=MESH))
        for cp in copies:
            cp.start()
        for cp in copies:
            cp.wait()

    return pl.pallas_call(
        body, name=name,
        in_specs=[HBM_SPEC] * n, out_specs=[HBM_SPEC] * n,
        out_shape=[jax.ShapeDtypeStruct((4,) + g.shape[1:], g.dtype) for g in gs],
        scratch_shapes=[pltpu.SemaphoreType.DMA((n, 4)), pltpu.SemaphoreType.DMA((n, 4))],
    )(*gs)


def _rs_partial(g_all, r1, sc, *, name):
    _, r, c = g_all.shape
    tr = _row_tile(r)

    def body(sc_ref, g_ref, r_ref, o_ref):
        o_ref[...] = (g_ref[...] + r_ref[...]).astype(BF16)

    return pl.pallas_call(
        body, name=name,
        grid_spec=pltpu.PrefetchScalarGridSpec(
            num_scalar_prefetch=1, grid=(3, r // tr),
            in_specs=[pl.BlockSpec((None, tr, c), lambda j, i, s_: (s_[2 + j], i, 0)),
                      pl.BlockSpec((None, tr, c), lambda j, i, s_: (s_[5 + j], i, 0))],
            out_specs=pl.BlockSpec((None, tr, c), lambda j, i, s_: (j, i, 0))),
        out_shape=jax.ShapeDtypeStruct((3, r, c), BF16),
        compiler_params=_params(2),
    )(sc, g_all, r1)


def _rs_cross(ps, *, name):
    n = len(ps)

    def body(*refs):
        p_refs, r_refs = refs[:n], refs[n:2 * n]
        send, recv = refs[2 * n:]
        x, y, c = _pos()
        copies = []
        for j, chip in enumerate(_other_chips(x, y)):
            for a in range(n):
                copies.append(pltpu.make_async_remote_copy(
                    src_ref=p_refs[a].at[j], dst_ref=r_refs[a].at[j],
                    send_sem=send.at[a, j], recv_sem=recv.at[a, j],
                    device_id=(*chip, c), device_id_type=MESH))
        for cp in copies:
            cp.start()
        for cp in copies:
            cp.wait()

    return pl.pallas_call(
        body, name=name,
        in_specs=[HBM_SPEC] * n, out_specs=[HBM_SPEC] * n,
        out_shape=[jax.ShapeDtypeStruct(p.shape, p.dtype) for p in ps],
        scratch_shapes=[pltpu.SemaphoreType.DMA((n, 3)), pltpu.SemaphoreType.DMA((n, 3))],
    )(*ps)


SMALL_ROWS = 16


def _all_reduce_small(pack, *, name):
    def body(x_ref, o_ref, buf, send, recv):
        x, y, c = _pos()
        me = _dev_index(x, y, c)
        buf[me] = x_ref[...]
        copies = []
        for k in range(1, N_DEV):
            fx, fy, fc = (k >> 2) & 1, (k >> 1) & 1, k & 1
            peer = (1 - x if fx else x, 1 - y if fy else y, 1 - c if fc else c)
            copies.append(pltpu.make_async_remote_copy(
                src_ref=x_ref, dst_ref=buf.at[me], send_sem=send.at[k - 1], recv_sem=recv.at[k - 1],
                device_id=peer, device_id_type=MESH))
        for cp in copies:
            cp.start()
        for cp in copies:
            cp.wait()
        acc = buf[0]
        for d in range(1, N_DEV):
            acc = acc + buf[d]
        o_ref[...] = acc

    return pl.pallas_call(
        body, name=name,
        in_specs=[pl.BlockSpec(memory_space=pltpu.VMEM)], out_specs=pl.BlockSpec(memory_space=pltpu.VMEM),
        out_shape=jax.ShapeDtypeStruct(pack.shape, F32),
        scratch_shapes=[pltpu.VMEM((N_DEV,) + pack.shape, F32),
                        pltpu.SemaphoreType.DMA((N_DEV - 1,)), pltpu.SemaphoreType.DMA((N_DEV - 1,))],
    )(pack)


def _unshard_cols(g):
    return jnp.transpose(g, (1, 0, 2)).reshape(g.shape[1], N_DEV * g.shape[2])


def _shard_cols(w):
    k, n8 = w.shape
    return jnp.transpose(w.reshape(k, N_DEV, n8 // N_DEV), (1, 0, 2))


def _pad_row(v, width=D_MODEL):
    v = v.reshape(1, -1)
    return jnp.pad(v, ((0, 0), (0, width - v.shape[1])))


def _local_step(xs, tgt, norm_mix, norm_ffn, norm_final, b_forget, weights):
    s = xs.shape[0]
    rope = _rope_tables(s)
    brow = [_pad_row(b_forget[i], LANES) for i in range(DEPTH // 2)]
    saved = []
    xc = xs
    for l in range(DEPTH):
        wq, wo, win, wout = weights[l]
        even = l % 2 == 0
        h1 = _rms_fwd(xc, norm_mix[l:l + 1], name=f"norm_mix_fwd{l}")
        qkv = _mm(h1, wq, name=f"qkv_fwd{l}", tm=1024, tn=768 if even else 640)
        if even:
            o_a, st_a = _sb_fwd(qkv, N_HEADS // 4, name=f"sb_fwd{l}")
            o_b, st_b = _bias_fwd("dil", qkv, N_HEADS // 4, N_HEADS // 4, rope, name=f"dil_fwd{l}")
            o = jnp.concatenate([o_a, o_b], axis=1)
            att = (o_b, st_a, st_b)
        else:
            fcol, frow = _fgate_fwd(qkv, brow[l // 2], name=f"fgate_fwd{l}")
            o, lse = _bias_fwd("fox", qkv, 0, N_HEADS // 2, (fcol, frow), name=f"fox_fwd{l}")
            att = (o, lse, fcol, frow)
        o_bf = o.astype(BF16)
        xm = _mm(o_bf, wo, add=xc, name=f"wo_fwd{l}", tm=512, tn=1024)
        h2 = _rms_fwd(xm, norm_ffn[l:l + 1], name=f"norm_ffn_fwd{l}")
        gu = _mm(h2, win, name=f"ffn_in_fwd{l}", tm=1024, tn=512)
        a = _swiglu_fwd(gu, name=f"swiglu_fwd{l}")
        xo = _mm(a, wout, add=xm, name=f"ffn_out_fwd{l}", tm=512, tn=1024)
        saved.append((xc, h1, qkv, att, o_bf, xm, h2, gu, a))
        xc = xo

    loss_row, dx, dxb, dg_final = _final_loss(xc, norm_final.reshape(1, -1), tgt, name="final_loss")

    wgrads = [None] * DEPTH
    dg_mix, dg_ffn, db_f = [None] * DEPTH, [None] * DEPTH, [None] * (DEPTH // 2)
    for l in reversed(range(DEPTH)):
        wq, wo, win, wout = weights[l]
        xin, h1, qkv, att, o_bf, xm, h2, gu, a = saved[l]
        even = l % 2 == 0
        da = _mm(dxb, wout, tb=True, name=f"ffn_out_dx{l}", tm=1024, tn=FF_BLK)
        d_wout = _mm(a, dxb, ta=True, name=f"ffn_out_dw{l}", tm=FF_BLK, tn=512)
        dgu = _swiglu_bwd(da, gu, name=f"swiglu_bwd{l}")
        d_win = _mm(h2, dgu, ta=True, name=f"ffn_in_dw{l}", tm=1024, tn=512)
        dh2 = _mm(dgu, win, tb=True, name=f"ffn_in_dx{l}", tm=512, tn=1024, tk=FF_BLK)
        dxm, dxmb, dg_ffn[l] = _rms_bwd(xm, norm_ffn[l:l + 1], dh2, dx, name=f"norm_ffn_bwd{l}")
        d_wo = _mm(o_bf, dxmb, ta=True, name=f"wo_dw{l}", tm=512, tn=1024)
        do = _mm(dxmb, wo, tb=True, name=f"wo_dx{l}", tm=1024, tn=1024)
        if even:
            o_b, st_a, st_b = att
            dqa, dka, dva = _sb_bwd(qkv, do, st_a, N_HEADS // 4, 0, name=f"sb_bwd{l}")
            dqb, dkb, dvb = _bias_bwd("dil", qkv, N_HEADS // 4, N_HEADS // 4, rope, o_b, do,
                                      N_HEADS // 4, st_b, name=f"dil_bwd{l}")
            dqkv = jnp.concatenate([dqa, dqb, dka, dkb, dva, dvb], axis=1)
        else:
            o, lse, fcol, frow = att
            dq, dk, dv, dfr, dfc = _bias_bwd("fox", qkv, 0, N_HEADS // 2, (fcol, frow), o, do, 0, lse,
                                             name=f"fox_bwd{l}")
            dfl, db_f[l // 2] = _fgate_bwd(dfr, dfc, qkv, brow[l // 2], name=f"fgate_bwd{l}")
            dqkv = jnp.concatenate([dq, dk, dv, dfl], axis=1)
        d_wq = _mm(h1, dqkv, ta=True, name=f"qkv_dw{l}", tm=1024, tn=768 if even else 640)
        dh1 = _mm(dqkv, wq, tb=True, name=f"qkv_dx{l}", tm=512, tn=1024, tk=1024 if even else 640)
        dx, dxb, dg_mix[l] = _rms_bwd(xin, norm_mix[l:l + 1], dh1, dxm, name=f"norm_mix_bwd{l}")
        wgrads[l] = (d_wq, d_wo, d_win, d_wout)
    return loss_row, dx, wgrads, dg_mix, dg_ffn, dg_final, db_f


def kernel(x, norm_mix, w_qkv_even, w_o_even, w_qkvf_odd, b_forget, w_o_odd, norm_ffn, w_ffn_in, w_ffn_out, norm_final, loss_target, m_norm_mix, m_w_qkv_even, m_w_o_even, m_w_qkvf_odd, m_b_forget, m_w_o_odd, m_norm_ffn, m_w_ffn_in, m_w_ffn_out, m_norm_final, v_norm_mix, v_w_qkv_even, v_w_o_even, v_w_qkvf_odd, v_b_forget, v_w_o_odd, v_norm_ffn, v_w_ffn_in, v_w_ffn_out, v_norm_final):
    xi, yi, ci = _pos()
    others = _other_chips(xi, yi)
    sc = jnp.stack([_dev_index(xi, yi, ci), 2 * xi + yi]
                   + [_dev_index(px, py, ci) for px, py in others]
                   + [2 * px + py for px, py in others]).astype(jnp.int32)
    n_odd_cols = w_qkvf_odd.shape[2] * N_DEV

    weights = []
    for l in range(DEPTH):
        even = l % 2 == 0
        wq_s = (w_qkv_even if even else w_qkvf_odd)[l // 2]
        wo_s = (w_o_even if even else w_o_odd)[l // 2]
        gq, go, gi, gout = _all_gather(
            [wq_s.astype(BF16), wo_s.astype(BF16), w_ffn_in[l].astype(BF16), w_ffn_out[l].astype(BF16)],
            name=f"gather_weights{l}")
        wq = _unshard_cols(gq)
        if not even:
            wq = jnp.pad(wq, ((0, 0), (0, QKVF_PAD - n_odd_cols)))
        weights.append((wq, go.reshape(D_ATTN, D_MODEL), _unshard_cols(gi), gout.reshape(D_FF, D_MODEL)))

    loss_row, dx, wgrads, dg_mix, dg_ffn, dg_final, db_f = _local_step(
        x[0], loss_target[0], norm_mix, norm_ffn, norm_final, b_forget, weights)

    sharded = {
        "qkv_even": (w_qkv_even, m_w_qkv_even, v_w_qkv_even), "o_even": (w_o_even, m_w_o_even, v_w_o_even),
        "qkvf_odd": (w_qkvf_odd, m_w_qkvf_odd, v_w_qkvf_odd), "o_odd": (w_o_odd, m_w_o_odd, v_w_o_odd),
        "ffn_in": (w_ffn_in, m_w_ffn_in, v_w_ffn_in), "ffn_out": (w_ffn_out, m_w_ffn_out, v_w_ffn_out),
    }
    results = {k: None for k in sharded}
    for l in reversed(range(DEPTH)):
        even = l % 2 == 0
        d_wq, d_wo, d_win, d_wout = wgrads[l]
        if not even:
            d_wq = d_wq[:, :n_odd_cols]
        gs = [_shard_cols(d_wq), d_wo.reshape(N_DEV, D_ATTN // N_DEV, D_MODEL),
              _shard_cols(d_win), d_wout.reshape(N_DEV, D_FF // N_DEV, D_MODEL)]
        r1s = _rs_sibling(gs, name=f"grads_to_sibling{l}")
        ps = [_rs_partial(g, r1, sc, name=f"grads_chip_sum{l}_{a}") for a, (g, r1) in enumerate(zip(gs, r1s))]
        r2s = _rs_cross(ps, name=f"grads_to_chips{l}")
        keys = [("qkv_even" if even else "qkvf_odd", l // 2), ("o_even" if even else "o_odd", l // 2),
                ("ffn_in", l), ("ffn_out", l)]
        for a, (key, lidx) in enumerate(keys):
            w, m, v = sharded[key]
            results[key] = _adamw_shard(w, m, v, lidx, gs[a], r1s[a], r2s[a], sc, results[key],
                                        name=f"adamw_{key}{l}")

    zeros = jnp.zeros((SMALL_ROWS - 11, D_MODEL), F32)
    db_row = _pad_row(jnp.concatenate([d[:, :N_HEADS] for d in db_f], axis=1))
    pack_g = jnp.concatenate(dg_mix + dg_ffn + [dg_final, db_row, _pad_row(loss_row[:, :1]), zeros], axis=0)
    tot = _all_reduce_small(pack_g, name="small_all_reduce")

    def pack(nm, nf, nfin, bf):
        return jnp.concatenate([nm, nf, nfin.reshape(1, -1), _pad_row(bf),
                                jnp.zeros((SMALL_ROWS - 10, D_MODEL), F32)], axis=0)

    d_s, m_s, v_s = _adamw_small(
        pack(norm_mix, norm_ffn, norm_final, b_forget), tot,
        pack(m_norm_mix, m_norm_ffn, m_norm_final, m_b_forget),
        pack(v_norm_mix, v_norm_ffn, v_norm_final, v_b_forget), name="adamw_small")

    def unpack(p):
        nb = b_forget.size
        return {"norm_mix": p[0:DEPTH], "norm_ffn": p[DEPTH:2 * DEPTH], "norm_final": p[2 * DEPTH],
                "b_forget": p[2 * DEPTH + 1, :nb].reshape(b_forget.shape)}

    small = [unpack(tot), unpack(d_s), unpack(m_s), unpack(v_s)]
    loss = tot[2 * DEPTH + 2, 0]

    order = ["norm_mix", "qkv_even", "o_even", "qkvf_odd", "b_forget", "o_odd", "norm_ffn", "ffn_in", "ffn_out",
             "norm_final"]
    outs = [loss, dx[None]]
    for t in range(4):
        for key in order:
            outs.append(small[t][key] if key in small[t] else results[key][t])
    return tuple(outs)
```

```python
import jax
import jax.numpy as jnp
from jax import lax
from jax.experimental import pallas as pl
from jax.experimental.pallas import tpu as pltpu

F32 = jnp.float32
BF16 = jnp.bfloat16

D_MODEL = 1024
HEAD_DIM = 64
N_HEADS = 16
D_ATTN = N_HEADS * HEAD_DIM
D_FF = 2816
DEPTH = 4
ROPE_THETA = 500000.0
ROT_DIM = HEAD_DIM // 4
RMS_EPS = 1e-5
SCALE = HEAD_DIM ** -0.5
DIL_PATTERNS = ((128, 1), (512, 4), (2048, 16))
N_DEV = 8
QKVF_PAD = 3200

ADAM_LR = 0.001
ADAM_B1 = 0.9
ADAM_B2 = 0.999
ADAM_EPS = 1e-08
ADAM_WD = 0.01
ADAM_STEP = 10

LANES = 128
BLK = 128
TB = 256
NEG = -1e30
VMEM_LIMIT = 48 * 1024 * 1024

MESH = pl.DeviceIdType.MESH


def _params(n_grid=0, **kw):
    sem = ("arbitrary",) * n_grid if n_grid else None
    return pltpu.CompilerParams(dimension_semantics=sem, vmem_limit_bytes=VMEM_LIMIT, **kw)


def _mm(a, b, *, name, ta=False, tb=False, add=None, out_dtype=F32, tm=512, tn=512, tk=None):
    m = a.shape[1] if ta else a.shape[0]
    k = a.shape[0] if ta else a.shape[1]
    n = b.shape[0] if tb else b.shape[1]
    assert (b.shape[1] if tb else b.shape[0]) == k
    tm, tn = min(tm, m), min(tn, n)
    tk = k if tk is None else min(tk, k)
    assert m % tm == 0 and n % tn == 0 and k % tk == 0, (name, m, n, k, tm, tn, tk)
    nk = k // tk
    dn = (((0 if ta else 1,), (1 if tb else 0,)), ((), ()))

    def body(*refs):
        a_ref, b_ref = refs[0], refs[1]
        add_ref = refs[2] if add is not None else None
        o_ref = refs[3] if add is not None else refs[2]
        part = lax.dot_general(a_ref[...], b_ref[...], dn, preferred_element_type=F32)
        if nk == 1:
            if add_ref is not None:
                part = part + add_ref[...]
            o_ref[...] = part.astype(out_dtype)
            return
        acc_ref = refs[-1]
        kk = pl.program_id(2)

        @pl.when(kk == 0)
        def _():
            acc_ref[...] = part

        @pl.when(kk > 0)
        def _():
            acc_ref[...] += part

        @pl.when(kk == nk - 1)
        def _():
            res = acc_ref[...]
            if add_ref is not None:
                res = res + add_ref[...]
            o_ref[...] = res.astype(out_dtype)

    a_spec = (pl.BlockSpec((tk, tm), lambda i, j, kk: (kk, i)) if ta
              else pl.BlockSpec((tm, tk), lambda i, j, kk: (i, kk)))
    b_spec = (pl.BlockSpec((tn, tk), lambda i, j, kk: (j, kk)) if tb
              else pl.BlockSpec((tk, tn), lambda i, j, kk: (kk, j)))
    o_spec = pl.BlockSpec((tm, tn), lambda i, j, kk: (i, j))
    in_specs = [a_spec, b_spec] + ([o_spec] if add is not None else [])
    args = (a, b) + ((add,) if add is not None else ())
    return pl.pallas_call(
        body, name=name, grid=(m // tm, n // tn, nk),
        in_specs=in_specs, out_specs=o_spec,
        out_shape=jax.ShapeDtypeStruct((m, n), out_dtype),
        scratch_shapes=[pltpu.VMEM((tm, tn), F32)] if nk > 1 else [],
        compiler_params=_params(3),
    )(*args)


def _rms_fwd(x, g, *, name, tr=256):
    s, d = x.shape

    def body(x_ref, g_ref, h_ref):
        xv = x_ref[...]
        r = lax.rsqrt(jnp.mean(xv * xv, axis=-1, keepdims=True) + RMS_EPS)
        h_ref[...] = (xv * r * g_ref[...]).astype(BF16)

    return pl.pallas_call(
        body, name=name, grid=(s // tr,),
        in_specs=[pl.BlockSpec((tr, d), lambda i: (i, 0)), pl.BlockSpec((1, d), lambda i: (0, 0))],
        out_specs=pl.BlockSpec((tr, d), lambda i: (i, 0)),
        out_shape=jax.ShapeDtypeStruct((s, d), BF16),
        compiler_params=_params(1),
    )(x, g)


def _rms_bwd(x, g, dh, dres, *, name, tr=256):
    s, d = x.shape

    def body(x_ref, g_ref, dh_ref, dres_ref, dx_ref, dxb_ref, dg_ref):
        xv = x_ref[...]
        r = lax.rsqrt(jnp.mean(xv * xv, axis=-1, keepdims=True) + RMS_EPS)
        y = xv * r
        dhv = dh_ref[...]
        dy = dhv * g_ref[...]
        dx = dres_ref[...] + r * (dy - y * jnp.mean(dy * y, axis=-1, keepdims=True))
        dx_ref[...] = dx
        dxb_ref[...] = dx.astype(BF16)
        part = jnp.sum(dhv * y, axis=0, keepdims=True)

        @pl.when(pl.program_id(0) == 0)
        def _():
            dg_ref[...] = part

        @pl.when(pl.program_id(0) > 0)
        def _():
            dg_ref[...] += part

    row = pl.BlockSpec((tr, d), lambda i: (i, 0))
    vec = pl.BlockSpec((1, d), lambda i: (0, 0))
    return pl.pallas_call(
        body, name=name, grid=(s // tr,),
        in_specs=[row, vec, row, row], out_specs=[row, row, vec],
        out_shape=[jax.ShapeDtypeStruct((s, d), F32), jax.ShapeDtypeStruct((s, d), BF16),
                   jax.ShapeDtypeStruct((1, d), F32)],
        compiler_params=_params(1),
    )(x, g, dh, dres)


def _final_loss(x, g, tgt, *, name, tr=256):
    s, d = x.shape

    def body(x_ref, g_ref, t_ref, loss_ref, dx_ref, dxb_ref, dg_ref):
        xv = x_ref[...]
        gv = g_ref[...]
        r = lax.rsqrt(jnp.mean(xv * xv, axis=-1, keepdims=True) + RMS_EPS)
        y = xv * r
        err = y * gv - t_ref[...]
        lpart = 0.5 * jnp.sum(jnp.mean(err * err, axis=-1, keepdims=True), axis=0, keepdims=True)
        dh = err * (1.0 / d)
        dy = dh * gv
        dx = r * (dy - y * jnp.mean(dy * y, axis=-1, keepdims=True))
        dx_ref[...] = dx
        dxb_ref[...] = dx.astype(BF16)
        gpart = jnp.sum(dh * y, axis=0, keepdims=True)
        lrow = jnp.broadcast_to(lpart, (1, LANES))

        @pl.when(pl.program_id(0) == 0)
        def _():
            dg_ref[...] = gpart
            loss_ref[...] = lrow

        @pl.when(pl.program_id(0) > 0)
        def _():
            dg_ref[...] += gpart
            loss_ref[...] += lrow

    row = pl.BlockSpec((tr, d), lambda i: (i, 0))
    vec = pl.BlockSpec((1, d), lambda i: (0, 0))
    lsp = pl.BlockSpec((1, LANES), lambda i: (0, 0))
    return pl.pallas_call(
        body, name=name, grid=(s // tr,),
        in_specs=[row, vec, row], out_specs=[lsp, row, row, vec],
        out_shape=[jax.ShapeDtypeStruct((1, LANES), F32), jax.ShapeDtypeStruct((s, d), F32),
                   jax.ShapeDtypeStruct((s, d), BF16), jax.ShapeDtypeStruct((1, d), F32)],
        compiler_params=_params(1),
    )(x, g, tgt)


FF_BLK = D_FF // 2


def _swiglu_fwd(gu, *, name, tr=256):
    s = gu.shape[0]

    def body(g_ref, u_ref, a_ref):
        gv = g_ref[...]
        a_ref[...] = (gv * jax.nn.sigmoid(gv) * u_ref[...]).astype(BF16)

    return pl.pallas_call(
        body, name=name, grid=(s // tr, 2),
        in_specs=[pl.BlockSpec((tr, FF_BLK), lambda i, j: (i, j)),
                  pl.BlockSpec((tr, FF_BLK), lambda i, j: (i, j + 2))],
        out_specs=pl.BlockSpec((tr, FF_BLK), lambda i, j: (i, j)),
        out_shape=jax.ShapeDtypeStruct((s, D_FF), BF16),
        compiler_params=_params(2),
    )(gu, gu)


def _swiglu_bwd(da, gu, *, name, tr=256):
    s = gu.shape[0]

    def body(da_ref, g_ref, u_ref, o_ref):
        gv = g_ref[...]
        dav = da_ref[...]
        sg = jax.nn.sigmoid(gv)
        j = pl.program_id(1)

        @pl.when(j < 2)
        def _():
            o_ref[...] = (dav * u_ref[...] * (sg * (1.0 + gv * (1.0 - sg)))).astype(BF16)

        @pl.when(j >= 2)
        def _():
            o_ref[...] = (dav * gv * sg).astype(BF16)

    return pl.pallas_call(
        body, name=name, grid=(s // tr, 4),
        in_specs=[pl.BlockSpec((tr, FF_BLK), lambda i, j: (i, j % 2)),
                  pl.BlockSpec((tr, FF_BLK), lambda i, j: (i, j % 2)),
                  pl.BlockSpec((tr, FF_BLK), lambda i, j: (i, 2 + j % 2))],
        out_specs=pl.BlockSpec((tr, FF_BLK), lambda i, j: (i, j)),
        out_shape=jax.ShapeDtypeStruct((s, 2 * D_FF), BF16),
        compiler_params=_params(2),
    )(da, gu, gu)


def _split3(x):
    hi = x.astype(BF16)
    r1 = x - hi.astype(F32)
    mid = r1.astype(BF16)
    lo = (r1 - mid.astype(F32)).astype(BF16)
    return hi, mid, lo


def _dot3(x, m_bf):
    hi, mid, lo = _split3(x)
    return (jnp.dot(hi, m_bf, preferred_element_type=F32)
            + jnp.dot(mid, m_bf, preferred_element_type=F32)
            + jnp.dot(lo, m_bf, preferred_element_type=F32))


def _dot3_left(m_bf, x):
    hi, mid, lo = _split3(x)
    return (jnp.dot(m_bf, hi, preferred_element_type=F32)
            + jnp.dot(m_bf, mid, preferred_element_type=F32)
            + jnp.dot(m_bf, lo, preferred_element_type=F32))


def _dot2(x, m_bf):
    hi = x.astype(BF16)
    lo = (x - hi.astype(F32)).astype(BF16)
    return jnp.dot(hi, m_bf, preferred_element_type=F32) + jnp.dot(lo, m_bf, preferred_element_type=F32)


def _nt(a, b):
    return lax.dot_general(a, b, (((1,), (1,)), ((), ())), preferred_element_type=F32)


def _mm32(a, b):
    return jnp.dot(a, b, preferred_element_type=F32)


def _iota2(shape, dim):
    return lax.broadcasted_iota(jnp.int32, shape, dim)


def _rope_tables(s):
    half = ROT_DIM // 2
    pos = jnp.arange(s, dtype=F32)
    inv_freq = ROPE_THETA ** (-jnp.arange(half, dtype=F32) * 2.0 / ROT_DIM)
    ang = pos[:, None] * inv_freq[None, :]
    cos, sin = jnp.cos(ang), jnp.sin(ang)
    ones = jnp.ones((s, HEAD_DIM - ROT_DIM), F32)
    cos_t = jnp.concatenate([cos, cos, ones], axis=1)
    sin_t = jnp.concatenate([-sin, sin, 0.0 * ones], axis=1)
    idx = jnp.arange(HEAD_DIM)
    partner = jnp.where(idx < half, idx + half, idx - half)
    swap = ((idx[:, None] == partner[None, :]) & (idx[None, :] < ROT_DIM)).astype(F32)
    swap2 = jnp.kron(jnp.eye(2, dtype=F32), swap).astype(BF16)
    return jnp.tile(cos_t, (1, 2)), jnp.tile(sin_t, (1, 2)), swap2


def _rope(x, cos_t, sin_t, swap):
    return x * cos_t + _dot3(x, swap) * sin_t


def _rope_t(g, cos_t, sin_t, swap):
    return g * cos_t + _dot3(g * sin_t, swap)


def _dil_weight(dlt):
    nonneg = dlt >= 0
    w = jnp.zeros(dlt.shape, F32)
    for window, dil in DIL_PATTERNS:
        ok = nonneg & (dlt <= window) & ((dlt & (dil - 1)) == 0)
        w = w + ok.astype(F32)
    return w


FAR_TILES = 3
assert (FAR_TILES - 1) * TB + 1 > DIL_PATTERNS[1][0] and DIL_PATTERNS[2][0] >= 2048


def _log_sig_pair(z):
    sp = jnp.log(1.0 + jnp.exp(-jnp.abs(z)))
    return jnp.minimum(z, 0.0) - sp, -jnp.maximum(z, 0.0) - sp


def _pair_masks(x, lane_lo):
    z = jnp.zeros_like(x)
    return jnp.where(lane_lo, x, z).astype(BF16), jnp.where(lane_lo, z, x).astype(BF16)


def _rows(i):
    return pl.ds(pl.multiple_of(i * TB, TB), TB)


def _head_spec(s, col0):
    return pl.BlockSpec((s, LANES), lambda p: (0, col0 + p))


def _stat_spec(s):
    return pl.BlockSpec((2, s, 1), lambda p: (p, 0, 0))


def _rowstat_spec(s):
    return pl.BlockSpec((2, 1, s), lambda p: (p, 0, 0))


def _full_spec(shape):
    nd = len(shape)
    return pl.BlockSpec(shape, lambda p: (0,) * nd)


K_COL, V_COL = D_ATTN // LANES, 2 * D_ATTN // LANES


def _bwd_scratch(s):
    return ([pltpu.VMEM((s, LANES), BF16)] * 8 + [pltpu.VMEM((LANES, s), BF16)] * 4
            + [pltpu.VMEM((LANES, s), F32)] * 2)


def _bwd_prep(i, q, k, v, dov, scr, lane_lo, sub_lo):
    qlo, qhi, klo, khi, kbf, vbf, dolo, dohi, qtlo, qthi, dotlo, dothi = scr[:12]
    rows = _rows(i)
    qlo[rows, :], qhi[rows, :] = _pair_masks(q, lane_lo)
    klo[rows, :], khi[rows, :] = _pair_masks(k, lane_lo)
    kbf[rows, :] = k.astype(BF16)
    vbf[rows, :] = v.astype(BF16)
    dolo[rows, :], dohi[rows, :] = _pair_masks(dov, lane_lo)
    qtlo[:, rows], qthi[:, rows] = _pair_masks(q.T, sub_lo)
    dotlo[:, rows], dothi[:, rows] = _pair_masks(dov.T, sub_lo)


def _sb_fwd(qkv, n_pairs, *, name):
    s = qkv.shape[0]
    assert s % TB == 0
    nq = s // TB

    def body(q_ref, k_ref, v_ref, o_ref, ct_ref, qlo, qhi, kbf, vlo, vhi):
        lane_lo = _iota2((TB, LANES), 1) < HEAD_DIM

        def prep(i, _):
            rows = _rows(i)
            qlo[rows, :], qhi[rows, :] = _pair_masks(q_ref[rows, :], lane_lo)
            kbf[rows, :] = k_ref[rows, :].astype(BF16)
            vlo[rows, :], vhi[rows, :] = _pair_masks(v_ref[rows, :], lane_lo)
            return 0

        lax.fori_loop(0, nq, prep, 0)
        rmc = _iota2((TB, TB), 0) - _iota2((TB, TB), 1)
        strict = rmc > 0
        u_ge = (rmc >= 0).astype(BF16)
        qm, vm = (qlo, qhi), (vlo, vhi)

        def qloop(i, _):
            rows = _rows(i)

            def tile(kb, carry, diag):
                c, acc = list(carry[:2]), carry[2]
                keys = _rows(kb)
                k = kbf[keys, :]
                for h in range(2):
                    z = _nt(qm[h][rows, :], k) * SCALE
                    lb, lm = _log_sig_pair(z)
                    if diag:
                        lm = jnp.where(strict, lm, 0.0)
                    r_in = _dot2(lm, u_ge)
                    a = jnp.exp(lb + (r_in - lm) + c[h])
                    if diag:
                        a = jnp.where(strict, a, 0.0)
                    acc = acc + _mm32(a.astype(BF16), vm[h][keys, :])
                    c[h] = c[h] + r_in[:, 0:1]
                return c[0], c[1], acc

            z1 = jnp.zeros((TB, 1), F32)
            carry = tile(i, (z1, z1, jnp.zeros((TB, LANES), F32)), True)
            c0, c1, acc = lax.fori_loop(0, i, lambda t, cr: tile(i - 1 - t, cr, False), carry)
            o_ref[rows, :] = acc
            ct_ref[0, rows, :] = c0
            ct_ref[1, rows, :] = c1
            return 0

        lax.fori_loop(0, nq, qloop, 0)

    return pl.pallas_call(
        body, name=name, grid=(n_pairs,),
        in_specs=[_head_spec(s, 0), _head_spec(s, K_COL), _head_spec(s, V_COL)],
        out_specs=[_head_spec(s, 0), _stat_spec(s)],
        out_shape=[jax.ShapeDtypeStruct((s, LANES * n_pairs), F32),
                   jax.ShapeDtypeStruct((2 * n_pairs, s, 1), F32)],
        scratch_shapes=[pltpu.VMEM((s, LANES), BF16)] * 5,
        compiler_params=_params(1),
    )(qkv, qkv, qkv)


def _sb_bwd(qkv, do, ctot, n_pairs, do_col0, *, name):
    s = qkv.shape[0]
    assert s % TB == 0
    nq = s // TB

    def body(q_ref, k_ref, v_ref, do_ref, ct_ref, dq_ref, dk_ref, dv_ref, *scr):
        qlo, qhi, klo, khi, kbf, vbf, dolo, dohi, qtlo, qthi, dotlo, dothi, dkt, dvt = scr
        lane_lo = _iota2((TB, LANES), 1) < HEAD_DIM
        sub_lo = _iota2((LANES, TB), 0) < HEAD_DIM

        def prep(i, _):
            rows = _rows(i)
            _bwd_prep(i, q_ref[rows, :], k_ref[rows, :], v_ref[rows, :], do_ref[rows, :], scr, lane_lo, sub_lo)
            return 0

        lax.fori_loop(0, nq, prep, 0)
        dkt[...] = jnp.zeros_like(dkt)
        dvt[...] = jnp.zeros_like(dvt)
        rmc = _iota2((TB, TB), 0) - _iota2((TB, TB), 1)
        strict = rmc > 0
        u_le = (rmc <= 0).astype(BF16)
        qm, km, dom, qtm, dotm = (qlo, qhi), (klo, khi), (dolo, dohi), (qtlo, qthi), (dotlo, dothi)

        def qloop(i, _):
            rows = _rows(i)
            ct = (ct_ref[0, rows, :], ct_ref[1, rows, :])

            def tile(kb, carry, diag):
                pre, hl, dq = list(carry[0:2]), list(carry[2:4]), carry[4]
                keys = _rows(kb)
                k, v = kbf[keys, :], vbf[keys, :]
                dk_t, dv_t = dkt[:, keys], dvt[:, keys]
                for h in range(2):
                    z = _nt(qm[h][rows, :], k) * SCALE
                    lb, lm = _log_sig_pair(z)
                    if diag:
                        lm = jnp.where(strict, lm, 0.0)
                    pin = _dot2(lm, u_le)
                    a = jnp.exp(lb + (ct[h] - pre[h] - pin))
                    if diag:
                        a = jnp.where(strict, a, 0.0)
                    g = a * _nt(dom[h][rows, :], v)
                    hin = _dot2(g, u_le)
                    beta = jnp.exp(lb)
                    dz = g * (1.0 - beta) - (hl[h] + hin - g) * beta
                    if diag:
                        dz = jnp.where(strict, dz, 0.0)
                    dzb = (dz * SCALE).astype(BF16)
                    dq = dq + _mm32(dzb, km[h][keys, :])
                    dk_t = dk_t + _mm32(qtm[h][:, rows], dzb)
                    dv_t = dv_t + _mm32(dotm[h][:, rows], a.astype(BF16))
                    pre[h] = pre[h] + pin[:, TB - 1:TB]
                    hl[h] = hl[h] + hin[:, TB - 1:TB]
                dkt[:, keys] = dk_t
                dvt[:, keys] = dv_t
                return pre[0], pre[1], hl[0], hl[1], dq

            z1 = jnp.zeros((TB, 1), F32)
            carry = lax.fori_loop(0, i, lambda kb, cr: tile(kb, cr, False),
                                  (z1, z1, z1, z1, jnp.zeros((TB, LANES), F32)))
            dq = tile(i, carry, True)[4]
            dq_ref[rows, :] = dq.astype(BF16)
            return 0

        lax.fori_loop(0, nq, qloop, 0)

        def wloop(i, _):
            rows = _rows(i)
            dk_ref[rows, :] = dkt[:, rows].T.astype(BF16)
            dv_ref[rows, :] = dvt[:, rows].T.astype(BF16)
            return 0

        lax.fori_loop(0, nq, wloop, 0)

    out = jax.ShapeDtypeStruct((s, LANES * n_pairs), BF16)
    return pl.pallas_call(
        body, name=name, grid=(n_pairs,),
        in_specs=[_head_spec(s, 0), _head_spec(s, K_COL), _head_spec(s, V_COL),
                  _head_spec(s, do_col0), _stat_spec(s)],
        out_specs=[_head_spec(s, 0)] * 3,
        out_shape=[out, out, out],
        scratch_shapes=_bwd_scratch(s),
        compiler_params=_params(1),
    )(qkv, qkv, qkv, do, ctot)


def _bias_fwd(mode, qkv, head0_col, n_pairs, extra, *, name):
    s = qkv.shape[0]
    assert s % TB == 0 and s <= DIL_PATTERNS[2][0]
    nq = s // TB
    fox = mode == "fox"

    def body(q_ref, k_ref, v_ref, e0, e1, *rest):
        if fox:
            o_ref, lse_ref, qlo, qhi, kbf, vx0, vx1 = rest
        else:
            e2, o_ref, lse_ref, qlo, qhi, kbf, vx0, vx1 = rest
        lane_lo = _iota2((TB, LANES), 1) < HEAD_DIM

        def prep(i, _):
            rows = _rows(i)
            q, k, v = q_ref[rows, :], k_ref[rows, :], v_ref[rows, :]
            if not fox:
                c, sn, sw = e0[rows, :], e1[rows, :], e2[...]
                q, k = _rope(q, c, sn, sw), _rope(k, c, sn, sw)
            qlo[rows, :], qhi[rows, :] = _pair_masks(q, lane_lo)
            kbf[rows, :] = k.astype(BF16)
            one = jnp.ones_like(v)
            vx0[rows, :] = jnp.where(lane_lo, v, one).astype(BF16)
            vx1[rows, :] = jnp.where(lane_lo, one, v).astype(BF16)
            return 0

        lax.fori_loop(0, nq, prep, 0)
        rmc = _iota2((TB, TB), 0) - _iota2((TB, TB), 1)
        far_w = ((rmc & (DIL_PATTERNS[2][1] - 1)) == 0).astype(F32)
        qm, vx = (qlo, qhi), (vx0, vx1)

        def qloop(i, _):
            rows = _rows(i)
            if fox:
                fq = (e0[0, rows, :], e0[1, rows, :])

            def tile(kb, carry, near):
                keys = _rows(kb)
                k = kbf[keys, :]
                if not fox:
                    w = _dil_weight((i - kb) * TB + rmc) if near else far_w
                out = []
                for h in range(2):
                    m, acc = carry[2 * h], carry[2 * h + 1]
                    sc = _nt(qm[h][rows, :], k) * SCALE
                    if fox:
                        sc = sc + (fq[h] - e1[h, :, keys])
                        if near:
                            sc = jnp.where(rmc >= 0, sc, NEG)
                    else:
                        sc = jnp.where(w > 0.0, sc, NEG)
                    m_new = jnp.maximum(m, jnp.max(sc, axis=1, keepdims=True))
                    p = jnp.exp(sc - m_new)
                    if not fox:
                        p = p * w
                    acc = jnp.exp(m - m_new) * acc + _mm32(p.astype(BF16), vx[h][keys, :])
                    out += [m_new, acc]
                return tuple(out)

            m0 = jnp.full((TB, 1), NEG, F32)
            a0 = jnp.zeros((TB, LANES), F32)
            n_far = i if fox else jnp.maximum(i + 1 - FAR_TILES, 0)
            carry = lax.fori_loop(0, n_far, lambda kb, cr: tile(kb, cr, False), (m0, a0, m0, a0))
            if fox:
                carry = tile(i, carry, True)
            else:
                carry = lax.fori_loop(n_far, i + 1, lambda kb, cr: tile(kb, cr, True), carry)
            m_0, acc0, m_1, acc1 = carry
            l0, l1 = acc0[:, HEAD_DIM:HEAD_DIM + 1], acc1[:, 0:1]
            o_ref[rows, :] = jnp.where(lane_lo, acc0 / l0, acc1 / l1)
            lse_ref[0, rows, :] = m_0 + jnp.log(l0)
            lse_ref[1, rows, :] = m_1 + jnp.log(l1)
            return 0

        lax.fori_loop(0, nq, qloop, 0)

    hp0 = head0_col
    if fox:
        e_specs = [_stat_spec(s), _rowstat_spec(s)]
    else:
        e_specs = [_full_spec((s, LANES)), _full_spec((s, LANES)), _full_spec((LANES, LANES))]
    return pl.pallas_call(
        body, name=name, grid=(n_pairs,),
        in_specs=[_head_spec(s, hp0), _head_spec(s, K_COL + hp0), _head_spec(s, V_COL + hp0)] + e_specs,
        out_specs=[_head_spec(s, 0), _stat_spec(s)],
        out_shape=[jax.ShapeDtypeStruct((s, LANES * n_pairs), F32),
                   jax.ShapeDtypeStruct((2 * n_pairs, s, 1), F32)],
        scratch_shapes=[pltpu.VMEM((s, LANES), BF16)] * 5,
        compiler_params=_params(1),
    )(qkv, qkv, qkv, *extra)


def _bias_bwd(mode, qkv, head0_col, n_pairs, extra, o, do, do_col0, lse, *, name):
    s = qkv.shape[0]
    assert s % TB == 0 and s <= DIL_PATTERNS[2][0]
    nq = s // TB
    fox = mode == "fox"

    def body(q_ref, k_ref, v_ref, o_ref, do_ref, lse_ref, e0, e1, *rest):
        if fox:
            dq_ref, dk_ref, dv_ref, dfr_ref, dfc_ref = rest[:5]
            scr = rest[5:]
        else:
            e2, dq_ref, dk_ref, dv_ref = rest[:4]
            scr = rest[4:]
        qlo, qhi, klo, khi, kbf, vbf, dolo, dohi, qtlo, qthi, dotlo, dothi, dkt, dvt = scr
        lane_lo = _iota2((TB, LANES), 1) < HEAD_DIM
        sub_lo = _iota2((LANES, TB), 0) < HEAD_DIM

        def prep(i, _):
            rows = _rows(i)
            q, k = q_ref[rows, :], k_ref[rows, :]
            if not fox:
                c, sn, sw = e0[rows, :], e1[rows, :], e2[...]
                q, k = _rope(q, c, sn, sw), _rope(k, c, sn, sw)
            _bwd_prep(i, q, k, v_ref[rows, :], do_ref[rows, :], scr, lane_lo, sub_lo)
            return 0

        lax.fori_loop(0, nq, prep, 0)
        dkt[...] = jnp.zeros_like(dkt)
        dvt[...] = jnp.zeros_like(dvt)
        if fox:
            dfr_ref[...] = jnp.zeros_like(dfr_ref)
        rmc = _iota2((TB, TB), 0) - _iota2((TB, TB), 1)
        far_w = ((rmc & (DIL_PATTERNS[2][1] - 1)) == 0).astype(F32)
        qm, km, dom, qtm, dotm = (qlo, qhi), (klo, khi), (dolo, dohi), (qtlo, qthi), (dotlo, dothi)

        def qloop(i, _):
            rows = _rows(i)
            prod = do_ref[rows, :] * o_ref[rows, :]
            dsum = (jnp.sum(jnp.where(lane_lo, prod, 0.0), axis=1, keepdims=True),
                    jnp.sum(jnp.where(lane_lo, 0.0, prod), axis=1, keepdims=True))
            lse_i = (lse_ref[0, rows, :], lse_ref[1, rows, :])
            if fox:
                fq = (e0[0, rows, :], e0[1, rows, :])

            def tile(kb, carry, near):
                dq, rs = carry[0], list(carry[1:])
                keys = _rows(kb)
                k, v = kbf[keys, :], vbf[keys, :]
                dk_t, dv_t = dkt[:, keys], dvt[:, keys]
                if not fox:
                    w = _dil_weight((i - kb) * TB + rmc) if near else far_w
                for h in range(2):
                    sc = _nt(qm[h][rows, :], k) * SCALE
                    if fox:
                        sc = sc + (fq[h] - e1[h, :, keys])
                        if near:
                            sc = jnp.where(rmc >= 0, sc, NEG)
                        p = jnp.exp(sc - lse_i[h])
                    else:
                        p = w * jnp.exp(jnp.where(w > 0.0, sc, NEG) - lse_i[h])
                    ds = p * (_nt(dom[h][rows, :], v) - dsum[h])
                    dsb = (ds * SCALE).astype(BF16)
                    dq = dq + _mm32(dsb, km[h][keys, :])
                    dk_t = dk_t + _mm32(qtm[h][:, rows], dsb)
                    dv_t = dv_t + _mm32(dotm[h][:, rows], p.astype(BF16))
                    if fox:
                        dfr_ref[h, :, keys] -= jnp.sum(ds, axis=0, keepdims=True)
                        rs[h] = rs[h] + jnp.sum(ds, axis=1, keepdims=True)
                dkt[:, keys] = dk_t
                dvt[:, keys] = dv_t
                return (dq, *rs)

            z1 = jnp.zeros((TB, 1), F32)
            init = (jnp.zeros((TB, LANES), F32), z1, z1) if fox else (jnp.zeros((TB, LANES), F32),)
            n_far = i if fox else jnp.maximum(i + 1 - FAR_TILES, 0)
            carry = lax.fori_loop(0, n_far, lambda kb, cr: tile(kb, cr, False), init)
            if fox:
                carry = tile(i, carry, True)
                dfc_ref[0, rows, :] = carry[1]
                dfc_ref[1, rows, :] = carry[2]
            else:
                carry = lax.fori_loop(n_far, i + 1, lambda kb, cr: tile(kb, cr, True), carry)
            dq = carry[0]
            if not fox:
                dq = _rope_t(dq, e0[rows, :], e1[rows, :], e2[...])
            dq_ref[rows, :] = dq.astype(BF16)
            return 0

        lax.fori_loop(0, nq, qloop, 0)

        def wloop(i, _):
            rows = _rows(i)
            dk = dkt[:, rows].T
            if not fox:
                dk = _rope_t(dk, e0[rows, :], e1[rows, :], e2[...])
            dk_ref[rows, :] = dk.astype(BF16)
            dv_ref[rows, :] = dvt[:, rows].T.astype(BF16)
            return 0

        lax.fori_loop(0, nq, wloop, 0)

    hp0 = head0_col
    out = jax.ShapeDtypeStruct((s, LANES * n_pairs), BF16)
    out_specs = [_head_spec(s, 0)] * 3
    out_shape = [out, out, out]
    if fox:
        e_specs = [_stat_spec(s), _rowstat_spec(s)]
        out_specs += [_rowstat_spec(s), _stat_spec(s)]
        out_shape += [jax.ShapeDtypeStruct((2 * n_pairs, 1, s), F32), jax.ShapeDtypeStruct((2 * n_pairs, s, 1), F32)]
    else:
        e_specs = [_full_spec((s, LANES)), _full_spec((s, LANES)), _full_spec((LANES, LANES))]
    return pl.pallas_call(
        body, name=name, grid=(n_pairs,),
        in_specs=[_head_spec(s, hp0), _head_spec(s, K_COL + hp0), _head_spec(s, V_COL + hp0),
                  _head_spec(s, 0), _head_spec(s, do_col0), _stat_spec(s)] + e_specs,
        out_specs=out_specs, out_shape=out_shape,
        scratch_shapes=_bwd_scratch(s),
        compiler_params=_params(1),
    )(qkv, qkv, qkv, o, do, lse, *extra)


F_COL = 3 * D_ATTN // LANES


def _fgate_fwd(qkvf, brow, *, name):
    s = qkvf.shape[0]
    nb = s // BLK

    def body(f_ref, b_ref, fc_ref, fr_ref, fs):
        row, col = _iota2((BLK, BLK), 0), _iota2((BLK, BLK), 1)
        l_incl = (col <= row).astype(BF16)

        def step(i, carry):
            r0 = pl.multiple_of(i * BLK, BLK)
            lf, _ = _log_sig_pair(f_ref[pl.ds(r0, BLK), :] + b_ref[...])
            fblk = carry + _dot3_left(l_incl, lf)
            fs[pl.ds(r0, BLK), :] = fblk
            return fblk[BLK - 1:BLK, :]

        lax.fori_loop(0, nb, step, jnp.zeros((1, LANES), F32))
        ft = fs[...].T
        for h in range(N_HEADS):
            fc_ref[h, :, :] = fs[:, h:h + 1]
            fr_ref[h, :, :] = ft[h:h + 1, :]

    return pl.pallas_call(
        body, name=name, grid=(1,),
        in_specs=[pl.BlockSpec((s, LANES), lambda i: (0, F_COL)), pl.BlockSpec((1, LANES), lambda i: (0, 0))],
        out_specs=[pl.BlockSpec((N_HEADS, s, 1), lambda i: (0, 0, 0)),
                   pl.BlockSpec((N_HEADS, 1, s), lambda i: (0, 0, 0))],
        out_shape=[jax.ShapeDtypeStruct((N_HEADS, s, 1), F32), jax.ShapeDtypeStruct((N_HEADS, 1, s), F32)],
        scratch_shapes=[pltpu.VMEM((s, LANES), F32)],
        compiler_params=_params(1),
    )(qkvf, brow)


def _fgate_bwd(dfr, dfc, qkvf, brow, *, name):
    s = qkvf.shape[0]
    nb = s // BLK

    def body(dfr_ref, dfc_ref, f_ref, b_ref, dfl_ref, db_ref, ts, fs):
        ts[...] = jnp.zeros_like(ts)
        for h in range(N_HEADS):
            ts[h:h + 1, :] = dfr_ref[h]
        fs[...] = ts[...].T
        for h in range(N_HEADS):
            fs[:, h:h + 1] += dfc_ref[h]
        row, col = _iota2((BLK, BLK), 0), _iota2((BLK, BLK), 1)
        u_incl = (col >= row).astype(BF16)
        head_lane = _iota2((BLK, LANES), 1) < N_HEADS

        def step(ii, carry):
            tail, db = carry
            r0 = pl.multiple_of((nb - 1 - ii) * BLK, BLK)
            rblk = tail + _dot3_left(u_incl, fs[pl.ds(r0, BLK), :])
            _, lsn = _log_sig_pair(f_ref[pl.ds(r0, BLK), :] + b_ref[...])
            dfl = jnp.where(head_lane, rblk * jnp.exp(lsn), 0.0)
            dfl_ref[pl.ds(r0, BLK), :] = dfl.astype(BF16)
            return rblk[0:1, :], db + jnp.sum(dfl, axis=0, keepdims=True)

        z = jnp.zeros((1, LANES), F32)
        _, db = lax.fori_loop(0, nb, step, (z, z))
        db_ref[...] = db

    return pl.pallas_call(
        body, name=name, grid=(1,),
        in_specs=[pl.BlockSpec((N_HEADS, 1, s), lambda i: (0, 0, 0)), pl.BlockSpec((N_HEADS, s, 1), lambda i: (0, 0, 0)),
                  pl.BlockSpec((s, LANES), lambda i: (0, F_COL)), pl.BlockSpec((1, LANES), lambda i: (0, 0))],
        out_specs=[pl.BlockSpec((s, LANES), lambda i: (0, 0)), pl.BlockSpec((1, LANES), lambda i: (0, 0))],
        out_shape=[jax.ShapeDtypeStruct((s, LANES), BF16), jax.ShapeDtypeStruct((1, LANES), F32)],
        scratch_shapes=[pltpu.VMEM((LANES, s), F32), pltpu.VMEM((s, LANES), F32)],
        compiler_params=_params(1),
    )(dfr, dfc, qkvf, brow)


def _adamw_math(w, g, m, v):
    m2 = ADAM_B1 * m + (1.0 - ADAM_B1) * g
    v2 = ADAM_B2 * v + (1.0 - ADAM_B2) * (g * g)
    m_hat = m2 / (1.0 - ADAM_B1 ** ADAM_STEP)
    v_hat = v2 / (1.0 - ADAM_B2 ** ADAM_STEP)
    delta = -ADAM_LR * (m_hat / (jnp.sqrt(v_hat) + ADAM_EPS) + ADAM_WD * w)
    return delta, m2, v2


def _row_tile(r, cap=256, mult=16):
    best = None
    for t in range(mult, min(r, cap) + 1, mult):
        if r % t == 0:
            best = t
    assert best is not None, r
    return best


def _adamw_shard(w, m, v, lidx, g_all, r1, r2, sc, prev, *, name):
    nl, r, c = w.shape
    tr = _row_tile(r)

    def body(sc_ref, w_ref, m_ref, v_ref, g_ref, r1_ref, r2_ref, *rest):
        go_ref, d_ref, mo_ref, vo_ref = rest[-4:]
        g = g_ref[...] + r1_ref[...]
        g = g + r2_ref[0].astype(F32)
        g = g + r2_ref[1].astype(F32)
        g = g + r2_ref[2].astype(F32)
        delta, m2, v2 = _adamw_math(w_ref[...], g, m_ref[...], v_ref[...])
        go_ref[...] = g
        d_ref[...] = delta
        mo_ref[...] = m2
        vo_ref[...] = v2

    lay = pl.BlockSpec((None, tr, c), lambda i, s_: (lidx, i, 0))
    in_specs = [lay, lay, lay,
                pl.BlockSpec((None, tr, c), lambda i, s_: (s_[0], i, 0)),
                pl.BlockSpec((None, tr, c), lambda i, s_: (s_[1], i, 0)),
                pl.BlockSpec((3, tr, c), lambda i, s_: (0, i, 0))]
    args = [sc, w, m, v, g_all, r1, r2]
    aliases = {}
    if prev is not None:
        in_specs += [pl.BlockSpec(memory_space=pl.ANY)] * 4
        aliases = {7 + t: t for t in range(4)}
        args += list(prev)
    shp = jax.ShapeDtypeStruct((nl, r, c), F32)
    return pl.pallas_call(
        body, name=name,
        grid_spec=pltpu.PrefetchScalarGridSpec(
            num_scalar_prefetch=1, grid=(r // tr,), in_specs=in_specs, out_specs=[lay] * 4),
        out_shape=[shp] * 4, input_output_aliases=aliases,
        compiler_params=_params(1),
    )(*args)


def _adamw_small(w, g, m, v, *, name):
    def body(w_ref, g_ref, m_ref, v_ref, d_ref, mo_ref, vo_ref):
        delta, m2, v2 = _adamw_math(w_ref[...], g_ref[...], m_ref[...], v_ref[...])
        d_ref[...] = delta
        mo_ref[...] = m2
        vo_ref[...] = v2

    shp = jax.ShapeDtypeStruct(w.shape, F32)
    return pl.pallas_call(body, name=name, out_shape=[shp] * 3, compiler_params=_params())(w, g, m, v)


def _pos():
    return lax.axis_index("x"), lax.axis_index("y"), lax.axis_index("c")


def _other_chips(x, y):
    return [(1 - x, y), (x, 1 - y), (1 - x, 1 - y)]


def _dev_index(x, y, c):
    return 4 * x + 2 * y + c


HBM_SPEC = pl.BlockSpec(memory_space=pltpu.HBM)


def _all_gather(shards, *, name):
    n = len(shards)

    def body(*refs):
        xs, outs = refs[:n], refs[n:2 * n]
        send, recv, loc = refs[2 * n:]
        x, y, c = _pos()
        me, sib = (x, y, c), (x, y, 1 - c)
        chips = _other_chips(x, y)

        def copy(a, k, block, to, src=None):
            dst = outs[a].at[_dev_index(*block)]
            return pltpu.make_async_remote_copy(
                src_ref=dst if src is None else src, dst_ref=dst,
                send_sem=send.at[a, k], recv_sem=recv.at[a, k], device_id=to, device_id_type=MESH)

        mine = [pltpu.make_async_copy(xs[a], outs[a].at[_dev_index(*me)], loc.at[a]) for a in range(n)]
        for cp in mine:
            cp.start()
        first = []
        for a in range(n):
            first.append(copy(a, 0, me, sib, src=xs[a]))
            first += [copy(a, 1 + j, me, (*chip, c), src=xs[a]) for j, chip in enumerate(chips)]
        for cp in first:
            cp.start()
        passed = []
        for j, chip in enumerate(chips):
            for a in range(n):
                copy(a, 1 + j, (*chip, c), me).wait_recv()
                fwd = copy(a, 4 + j, (*chip, c), sib)
                fwd.start()
                passed.append(fwd)
        for a in range(n):
            copy(a, 0, sib, me).wait_recv()
            for j, chip in enumerate(chips):
                copy(a, 4 + j, (*chip, 1 - c), me).wait_recv()
        for cp in first + passed:
            cp.wait_send()
        for cp in mine:
            cp.wait()

    return pl.pallas_call(
        body, name=name,
        in_specs=[HBM_SPEC] * n, out_specs=[HBM_SPEC] * n,
        out_shape=[jax.ShapeDtypeStruct((N_DEV,) + a.shape, a.dtype) for a in shards],
        scratch_shapes=[pltpu.SemaphoreType.DMA((n, 7)), pltpu.SemaphoreType.DMA((n, 7)),
                        pltpu.SemaphoreType.DMA((n,))],
    )(*shards)


def _rs_sibling(gs, *, name):
    n = len(gs)

    def body(*refs):
        g_refs, r_refs = refs[:n], refs[n:2 * n]
        send, recv = refs[2 * n:]
        x, y, c = _pos()
        copies = []
        for a in range(n):
            for k in range(4):
                copies.append(pltpu.make_async_remote_copy(
                    src_ref=g_refs[a].at[_dev_index(k // 2, k % 2, 1 - c)], dst_ref=r_refs[a].at[k],
                    send_sem=send.at[a, k], recv_sem=recv.at[a, k],
                    device_id=(x, y, 1 - c), device_id_type=MESH))
        for cp in copies:
            cp.start()
        for cp in copies:
            cp.wait()

    return pl.pallas_call(
        body, name=name,
        in_specs=[HBM_SPEC] * n, out_specs=[HBM_SPEC] * n,
        out_shape=[jax.ShapeDtypeStruct((4,) + g.shape[1:], g.dtype) for g in gs],
        scratch_shapes=[pltpu.SemaphoreType.DMA((n, 4)), pltpu.SemaphoreType.DMA((n, 4))],
    )(*gs)


def _rs_partial(g_all, r1, sc, *, name):
    _, r, c = g_all.shape
    tr = _row_tile(r)

    def body(sc_ref, g_ref, r_ref, o_ref):
        o_ref[...] = (g_ref[...] + r_ref[...]).astype(BF16)

    return pl.pallas_call(
        body, name=name,
        grid_spec=pltpu.PrefetchScalarGridSpec(
            num_scalar_prefetch=1, grid=(3, r // tr),
            in_specs=[pl.BlockSpec((None, tr, c), lambda j, i, s_: (s_[2 + j], i, 0)),
                      pl.BlockSpec((None, tr, c), lambda j, i, s_: (s_[5 + j], i, 0))],
            out_specs=pl.BlockSpec((None, tr, c), lambda j, i, s_: (j, i, 0))),
        out_shape=jax.ShapeDtypeStruct((3, r, c), BF16),
        compiler_params=_params(2),
    )(sc, g_all, r1)


def _rs_cross(ps, *, name):
    n = len(ps)

    def body(*refs):
        p_refs, r_refs = refs[:n], refs[n:2 * n]
        send, recv = refs[2 * n:]
        x, y, c = _pos()
        copies = []
        for j, chip in enumerate(_other_chips(x, y)):
            for a in range(n):
                copies.append(pltpu.make_async_remote_copy(
                    src_ref=p_refs[a].at[j], dst_ref=r_refs[a].at[j],
                    send_sem=send.at[a, j], recv_sem=recv.at[a, j],
                    device_id=(*chip, c), device_id_type=MESH))
        for cp in copies:
            cp.start()
        for cp in copies:
            cp.wait()

    return pl.pallas_call(
        body, name=name,
        in_specs=[HBM_SPEC] * n, out_specs=[HBM_SPEC] * n,
        out_shape=[jax.ShapeDtypeStruct(p.shape, p.dtype) for p in ps],
        scratch_shapes=[pltpu.SemaphoreType.DMA((n, 3)), pltpu.SemaphoreType.DMA((n, 3))],
    )(*ps)


SMALL_ROWS = 16


def _all_reduce_small(pack, *, name):
    def body(x_ref, o_ref, buf, send, recv):
        x, y, c = _pos()
        me = _dev_index(x, y, c)
        buf[me] = x_ref[...]
        copies = []
        for k in range(1, N_DEV):
            fx, fy, fc = (k >> 2) & 1, (k >> 1) & 1, k & 1
            peer = (1 - x if fx else x, 1 - y if fy else y, 1 - c if fc else c)
            copies.append(pltpu.make_async_remote_copy(
                src_ref=x_ref, dst_ref=buf.at[me], send_sem=send.at[k - 1], recv_sem=recv.at[k - 1],
                device_id=peer, device_id_type=MESH))
        for cp in copies:
            cp.start()
        for cp in copies:
            cp.wait()
        acc = buf[0]
        for d in range(1, N_DEV):
            acc = acc + buf[d]
        o_ref[...] = acc

    return pl.pallas_call(
        body, name=name,
        in_specs=[pl.BlockSpec(memory_space=pltpu.VMEM)], out_specs=pl.BlockSpec(memory_space=pltpu.VMEM),
        out_shape=jax.ShapeDtypeStruct(pack.shape, F32),
        scratch_shapes=[pltpu.VMEM((N_DEV,) + pack.shape, F32),
                        pltpu.SemaphoreType.DMA((N_DEV - 1,)), pltpu.SemaphoreType.DMA((N_DEV - 1,))],
    )(pack)


def _unshard_cols(g):
    return jnp.transpose(g, (1, 0, 2)).reshape(g.shape[1], N_DEV * g.shape[2])


def _shard_cols(w):
    k, n8 = w.shape
    return jnp.transpose(w.reshape(k, N_DEV, n8 // N_DEV), (1, 0, 2))


def _pad_row(v, width=D_MODEL):
    v = v.reshape(1, -1)
    return jnp.pad(v, ((0, 0), (0, width - v.shape[1])))


def _local_step(xs, tgt, norm_mix, norm_ffn, norm_final, b_forget, weights):
    s = xs.shape[0]
    rope = _rope_tables(s)
    brow = [_pad_row(b_forget[i], LANES) for i in range(DEPTH // 2)]
    saved = []
    xc = xs
    for l in range(DEPTH):
        wq, wo, win, wout = weights[l]
        even = l % 2 == 0
        h1 = _rms_fwd(xc, norm_mix[l:l + 1], name=f"norm_mix_fwd{l}")
        qkv = _mm(h1, wq, name=f"qkv_fwd{l}", tm=1024, tn=768 if even else 640)
        if even:
            o_a, st_a = _sb_fwd(qkv, N_HEADS // 4, name=f"sb_fwd{l}")
            o_b, st_b = _bias_fwd("dil", qkv, N_HEADS // 4, N_HEADS // 4, rope, name=f"dil_fwd{l}")
            o = jnp.concatenate([o_a, o_b], axis=1)
            att = (o_b, st_a, st_b)
        else:
            fcol, frow = _fgate_fwd(qkv, brow[l // 2], name=f"fgate_fwd{l}")
            o, lse = _bias_fwd("fox", qkv, 0, N_HEADS // 2, (fcol, frow), name=f"fox_fwd{l}")
            att = (o, lse, fcol, frow)
        o_bf = o.astype(BF16)
        xm = _mm(o_bf, wo, add=xc, name=f"wo_fwd{l}", tm=512, tn=1024)
        h2 = _rms_fwd(xm, norm_ffn[l:l + 1], name=f"norm_ffn_fwd{l}")
        gu = _mm(h2, win, name=f"ffn_in_fwd{l}", tm=1024, tn=512)
        a = _swiglu_fwd(gu, name=f"swiglu_fwd{l}")
        xo = _mm(a, wout, add=xm, name=f"ffn_out_fwd{l}", tm=512, tn=1024)
        saved.append((xc, h1, qkv, att, o_bf, xm, h2, gu, a))
        xc = xo

    loss_row, dx, dxb, dg_final = _final_loss(xc, norm_final.reshape(1, -1), tgt, name="final_loss")

    wgrads = [None] * DEPTH
    dg_mix, dg_ffn, db_f = [None] * DEPTH, [None] * DEPTH, [None] * (DEPTH // 2)
    for l in reversed(range(DEPTH)):
        wq, wo, win, wout = weights[l]
        xin, h1, qkv, att, o_bf, xm, h2, gu, a = saved[l]
        even = l % 2 == 0
        da = _mm(dxb, wout, tb=True, name=f"ffn_out_dx{l}", tm=1024, tn=FF_BLK)
        d_wout = _mm(a, dxb, ta=True, name=f"ffn_out_dw{l}", tm=FF_BLK, tn=512)
        dgu = _swiglu_bwd(da, gu, name=f"swiglu_bwd{l}")
        d_win = _mm(h2, dgu, ta=True, name=f"ffn_in_dw{l}", tm=1024, tn=512)
        dh2 = _mm(dgu, win, tb=True, name=f"ffn_in_dx{l}", tm=512, tn=1024, tk=FF_BLK)
        dxm, dxmb, dg_ffn[l] = _rms_bwd(xm, norm_ffn[l:l + 1], dh2, dx, name=f"norm_ffn_bwd{l}")
        d_wo = _mm(o_bf, dxmb, ta=True, name=f"wo_dw{l}", tm=512, tn=1024)
        do = _mm(dxmb, wo, tb=True, name=f"wo_dx{l}", tm=1024, tn=1024)
        if even:
            o_b, st_a, st_b = att
            dqa, dka, dva = _sb_bwd(qkv, do, st_a, N_HEADS // 4, 0, name=f"sb_bwd{l}")
            dqb, dkb, dvb = _bias_bwd("dil", qkv, N_HEADS // 4, N_HEADS // 4, rope, o_b, do,
                                      N_HEADS // 4, st_b, name=f"dil_bwd{l}")
            dqkv = jnp.concatenate([dqa, dqb, dka, dkb, dva, dvb], axis=1)
        else:
            o, lse, fcol, frow = att
            dq, dk, dv, dfr, dfc = _bias_bwd("fox", qkv, 0, N_HEADS // 2, (fcol, frow), o, do, 0, lse,
                                             name=f"fox_bwd{l}")
            dfl, db_f[l // 2] = _fgate_bwd(dfr, dfc, qkv, brow[l // 2], name=f"fgate_bwd{l}")
            dqkv = jnp.concatenate([dq, dk, dv, dfl], axis=1)
        d_wq = _mm(h1, dqkv, ta=True, name=f"qkv_dw{l}", tm=1024, tn=768 if even else 640)
        dh1 = _mm(dqkv, wq, tb=True, name=f"qkv_dx{l}", tm=512, tn=1024, tk=1024 if even else 640)
        dx, dxb, dg_mix[l] = _rms_bwd(xin, norm_mix[l:l + 1], dh1, dxm, name=f"norm_mix_bwd{l}")
        wgrads[l] = (d_wq, d_wo, d_win, d_wout)
    return loss_row, dx, wgrads, dg_mix, dg_ffn, dg_final, db_f


def kernel(x, norm_mix, w_qkv_even, w_o_even, w_qkvf_odd, b_forget, w_o_odd, norm_ffn, w_ffn_in, w_ffn_out, norm_final, loss_target, m_norm_mix, m_w_qkv_even, m_w_o_even, m_w_qkvf_odd, m_b_forget, m_w_o_odd, m_norm_ffn, m_w_ffn_in, m_w_ffn_out, m_norm_final, v_norm_mix, v_w_qkv_even, v_w_o_even, v_w_qkvf_odd, v_b_forget, v_w_o_odd, v_norm_ffn, v_w_ffn_in, v_w_ffn_out, v_norm_final):
    xi, yi, ci = _pos()
    others = _other_chips(xi, yi)
    sc = jnp.stack([_dev_index(xi, yi, ci), 2 * xi + yi]
                   + [_dev_index(px, py, ci) for px, py in others]
                   + [2 * px + py for px, py in others]).astype(jnp.int32)
    n_odd_cols = w_qkvf_odd.shape[2] * N_DEV

    weights = []
    for l in range(DEPTH):
        even = l % 2 == 0
        wq_s = (w_qkv_even if even else w_qkvf_odd)[l // 2]
        wo_s = (w_o_even if even else w_o_odd)[l // 2]
        gq, go, gi, gout = _all_gather(
            [wq_s.astype(BF16), wo_s.astype(BF16), w_ffn_in[l].astype(BF16), w_ffn_out[l].astype(BF16)],
            name=f"gather_weights{l}")
        wq = _unshard_cols(gq)
        if not even:
            wq = jnp.pad(wq, ((0, 0), (0, QKVF_PAD - n_odd_cols)))
        weights.append((wq, go.reshape(D_ATTN, D_MODEL), _unshard_cols(gi), gout.reshape(D_FF, D_MODEL)))

    loss_row, dx, wgrads, dg_mix, dg_ffn, dg_final, db_f = _local_step(
        x[0], loss_target[0], norm_mix, norm_ffn, norm_final, b_forget, weights)

    sharded = {
        "qkv_even": (w_qkv_even, m_w_qkv_even, v_w_qkv_even), "o_even": (w_o_even, m_w_o_even, v_w_o_even),
        "qkvf_odd": (w_qkvf_odd, m_w_qkvf_odd, v_w_qkvf_odd), "o_odd": (w_o_odd, m_w_o_odd, v_w_o_odd),
        "ffn_in": (w_ffn_in, m_w_ffn_in, v_w_ffn_in), "ffn_out": (w_ffn_out, m_w_ffn_out, v_w_ffn_out),
    }
    results = {k: None for k in sharded}
    for l in reversed(range(DEPTH)):
        even = l % 2 == 0
        d_wq, d_wo, d_win, d_wout = wgrads[l]
        if not even:
            d_wq = d_wq[:, :n_odd_cols]
        gs = [_shard_cols(d_wq), d_wo.reshape(N_DEV, D_ATTN // N_DEV, D_MODEL),
              _shard_cols(d_win), d_wout.reshape(N_DEV, D_FF // N_DEV, D_MODEL)]
        r1s = _rs_sibling(gs, name=f"grads_to_sibling{l}")
        ps = [_rs_partial(g, r1, sc, name=f"grads_chip_sum{l}_{a}") for a, (g, r1) in enumerate(zip(gs, r1s))]
        r2s = _rs_cross(ps, name=f"grads_to_chips{l}")
        keys = [("qkv_even" if even else "qkvf_odd", l // 2), ("o_even" if even else "o_odd", l // 2),
                ("ffn_in", l), ("ffn_out", l)]
        for a, (key, lidx) in enumerate(keys):
            w, m, v = sharded[key]
            results[key] = _adamw_shard(w, m, v, lidx, gs[a], r1s[a], r2s[a], sc, results[key],
                                        name=f"adamw_{key}{l}")

    zeros = jnp.zeros((SMALL_ROWS - 11, D_MODEL), F32)
    db_row = _pad_row(jnp.concatenate([d[:, :N_HEADS] for d in db_f], axis=1))
    pack_g = jnp.concatenate(dg_mix + dg_ffn + [dg_final, db_row, _pad_row(loss_row[:, :1]), zeros], axis=0)
    tot = _all_reduce_small(pack_g, name="small_all_reduce")

    def pack(nm, nf, nfin, bf):
        return jnp.concatenate([nm, nf, nfin.reshape(1, -1), _pad_row(bf),
                                jnp.zeros((SMALL_ROWS - 10, D_MODEL), F32)], axis=0)

    d_s, m_s, v_s = _adamw_small(
        pack(norm_mix, norm_ffn, norm_final, b_forget), tot,
        pack(m_norm_mix, m_norm_ffn, m_norm_final, m_b_forget),
        pack(v_norm_mix, v_norm_ffn, v_norm_final, v_b_forget), name="adamw_small")

    def unpack(p):
        nb = b_forget.size
        return {"norm_mix": p[0:DEPTH], "norm_ffn": p[DEPTH:2 * DEPTH], "norm_final": p[2 * DEPTH],
                "b_forget": p[2 * DEPTH + 1, :nb].reshape(b_forget.shape)}

    small = [unpack(tot), unpack(d_s), unpack(m_s), unpack(v_s)]
    loss = tot[2 * DEPTH + 2, 0]

    order = ["norm_mix", "qkv_even", "o_even", "qkvf_odd", "b_forget", "o_odd", "norm_ffn", "ffn_in", "ffn_out",
             "norm_final"]
    outs = [loss, dx[None]]
    for t in range(4):
        for key in order:
            outs.append(small[t][key] if key in small[t] else results[key][t])
    return tuple(outs)
```

```python
import jax
import jax.numpy as jnp
from jax import lax
from jax.experimental import pallas as pl
from jax.experimental.pallas import tpu as pltpu

F32 = jnp.float32
BF16 = jnp.bfloat16

D_MODEL = 1024
HEAD_DIM = 64
N_HEADS = 16
D_ATTN = N_HEADS * HEAD_DIM
D_FF = 2816
DEPTH = 4
ROPE_THETA = 500000.0
ROT_DIM = HEAD_DIM // 4
RMS_EPS = 1e-5
SCALE = HEAD_DIM ** -0.5
DIL_PATTERNS = ((128, 1), (512, 4), (2048, 16))
N_DEV = 8
QKVF_PAD = 3200

ADAM_LR = 0.001
ADAM_B1 = 0.9
ADAM_B2 = 0.999
ADAM_EPS = 1e-08
ADAM_WD = 0.01
ADAM_STEP = 10

LANES = 128
BLK = 128
TB = 256
NEG = -1e30
VMEM_LIMIT = 48 * 1024 * 1024

MESH = pl.DeviceIdType.MESH


def _params(n_grid=0, **kw):
    sem = ("arbitrary",) * n_grid if n_grid else None
    return pltpu.CompilerParams(dimension_semantics=sem, vmem_limit_bytes=VMEM_LIMIT, **kw)


def _mm(a, b, *, name, ta=False, tb=False, add=None, out_dtype=F32, tm=512, tn=512, tk=None):
    m = a.shape[1] if ta else a.shape[0]
    k = a.shape[0] if ta else a.shape[1]
    n = b.shape[0] if tb else b.shape[1]
    assert (b.shape[1] if tb else b.shape[0]) == k
    tm, tn = min(tm, m), min(tn, n)
    tk = k if tk is None else min(tk, k)
    assert m % tm == 0 and n % tn == 0 and k % tk == 0, (name, m, n, k, tm, tn, tk)
    nk = k // tk
    dn = (((0 if ta else 1,), (1 if tb else 0,)), ((), ()))

    def body(*refs):
        a_ref, b_ref = refs[0], refs[1]
        add_ref = refs[2] if add is not None else None
        o_ref = refs[3] if add is not None else refs[2]
        part = lax.dot_general(a_ref[...], b_ref[...], dn, preferred_element_type=F32)
        if nk == 1:
            if add_ref is not None:
                part = part + add_ref[...]
            o_ref[...] = part.astype(out_dtype)
            return
        acc_ref = refs[-1]
        kk = pl.program_id(2)

        @pl.when(kk == 0)
        def _():
            acc_ref[...] = part

        @pl.when(kk > 0)
        def _():
            acc_ref[...] += part

        @pl.when(kk == nk - 1)
        def _():
            res = acc_ref[...]
            if add_ref is not None:
                res = res + add_ref[...]
            o_ref[...] = res.astype(out_dtype)

    a_spec = (pl.BlockSpec((tk, tm), lambda i, j, kk: (kk, i)) if ta
              else pl.BlockSpec((tm, tk), lambda i, j, kk: (i, kk)))
    b_spec = (pl.BlockSpec((tn, tk), lambda i, j, kk: (j, kk)) if tb
              else pl.BlockSpec((tk, tn), lambda i, j, kk: (kk, j)))
    o_spec = pl.BlockSpec((tm, tn), lambda i, j, kk: (i, j))
    in_specs = [a_spec, b_spec] + ([o_spec] if add is not None else [])
    args = (a, b) + ((add,) if add is not None else ())
    return pl.pallas_call(
        body, name=name, grid=(m // tm, n // tn, nk),
        in_specs=in_specs, out_specs=o_spec,
        out_shape=jax.ShapeDtypeStruct((m, n), out_dtype),
        scratch_shapes=[pltpu.VMEM((tm, tn), F32)] if nk > 1 else [],
        compiler_params=_params(3),
    )(*args)


def _rms_fwd(x, g, *, name, tr=256):
    s, d = x.shape

    def body(x_ref, g_ref, h_ref):
        xv = x_ref[...]
        r = lax.rsqrt(jnp.mean(xv * xv, axis=-1, keepdims=True) + RMS_EPS)
        h_ref[...] = (xv * r * g_ref[...]).astype(BF16)

    return pl.pallas_call(
        body, name=name, grid=(s // tr,),
        in_specs=[pl.BlockSpec((tr, d), lambda i: (i, 0)), pl.BlockSpec((1, d), lambda i: (0, 0))],
        out_specs=pl.BlockSpec((tr, d), lambda i: (i, 0)),
        out_shape=jax.ShapeDtypeStruct((s, d), BF16),
        compiler_params=_params(1),
    )(x, g)


def _rms_bwd(x, g, dh, dres, *, name, tr=256):
    s, d = x.shape

    def body(x_ref, g_ref, dh_ref, dres_ref, dx_ref, dxb_ref, dg_ref):
        xv = x_ref[...]
        r = lax.rsqrt(jnp.mean(xv * xv, axis=-1, keepdims=True) + RMS_EPS)
        y = xv * r
        dhv = dh_ref[...]
        dy = dhv * g_ref[...]
        dx = dres_ref[...] + r * (dy - y * jnp.mean(dy * y, axis=-1, keepdims=True))
        dx_ref[...] = dx
        dxb_ref[...] = dx.astype(BF16)
        part = jnp.sum(dhv * y, axis=0, keepdims=True)

        @pl.when(pl.program_id(0) == 0)
        def _():
            dg_ref[...] = part

        @pl.when(pl.program_id(0) > 0)
        def _():
            dg_ref[...] += part

    row = pl.BlockSpec((tr, d), lambda i: (i, 0))
    vec = pl.BlockSpec((1, d), lambda i: (0, 0))
    return pl.pallas_call(
        body, name=name, grid=(s // tr,),
        in_specs=[row, vec, row, row], out_specs=[row, row, vec],
        out_shape=[jax.ShapeDtypeStruct((s, d), F32), jax.ShapeDtypeStruct((s, d), BF16),
                   jax.ShapeDtypeStruct((1, d), F32)],
        compiler_params=_params(1),
    )(x, g, dh, dres)


def _final_loss(x, g, tgt, *, name, tr=256):
    s, d = x.shape

    def body(x_ref, g_ref, t_ref, loss_ref, dx_ref, dxb_ref, dg_ref):
        xv = x_ref[...]
        gv = g_ref[...]
        r = lax.rsqrt(jnp.mean(xv * xv, axis=-1, keepdims=True) + RMS_EPS)
        y = xv * r
        err = y * gv - t_ref[...]
        lpart = 0.5 * jnp.sum(jnp.mean(err * err, axis=-1, keepdims=True), axis=0, keepdims=True)
        dh = err * (1.0 / d)
        dy = dh * gv
        dx = r * (dy - y * jnp.mean(dy * y, axis=-1, keepdims=True))
        dx_ref[...] = dx
        dxb_ref[...] = dx.astype(BF16)
        gpart = jnp.sum(dh * y, axis=0, keepdims=True)
        lrow = jnp.broadcast_to(lpart, (1, LANES))

        @pl.when(pl.program_id(0) == 0)
        def _():
            dg_ref[...] = gpart
            loss_ref[...] = lrow

        @pl.when(pl.program_id(0) > 0)
        def _():
            dg_ref[...] += gpart
            loss_ref[...] += lrow

    row = pl.BlockSpec((tr, d), lambda i: (i, 0))
    vec = pl.BlockSpec((1, d), lambda i: (0, 0))
    lsp = pl.BlockSpec((1, LANES), lambda i: (0, 0))
    return pl.pallas_call(
        body, name=name, grid=(s // tr,),
        in_specs=[row, vec, row], out_specs=[lsp, row, row, vec],
        out_shape=[jax.ShapeDtypeStruct((1, LANES), F32), jax.ShapeDtypeStruct((s, d), F32),
                   jax.ShapeDtypeStruct((s, d), BF16), jax.ShapeDtypeStruct((1, d), F32)],
        compiler_params=_params(1),
    )(x, g, tgt)


FF_BLK = D_FF // 2


def _swiglu_fwd(gu, *, name, tr=256):
    s = gu.shape[0]

    def body(g_ref, u_ref, a_ref):
        gv = g_ref[...]
        a_ref[...] = (gv * jax.nn.sigmoid(gv) * u_ref[...]).astype(BF16)

    return pl.pallas_call(
        body, name=name, grid=(s // tr, 2),
        in_specs=[pl.BlockSpec((tr, FF_BLK), lambda i, j: (i, j)),
                  pl.BlockSpec((tr, FF_BLK), lambda i, j: (i, j + 2))],
        out_specs=pl.BlockSpec((tr, FF_BLK), lambda i, j: (i, j)),
        out_shape=jax.ShapeDtypeStruct((s, D_FF), BF16),
        compiler_params=_params(2),
    )(gu, gu)


def _swiglu_bwd(da, gu, *, name, tr=256):
    s = gu.shape[0]

    def body(da_ref, g_ref, u_ref, o_ref):
        gv = g_ref[...]
        dav = da_ref[...]
        sg = jax.nn.sigmoid(gv)
        j = pl.program_id(1)

        @pl.when(j < 2)
        def _():
            o_ref[...] = (dav * u_ref[...] * (sg * (1.0 + gv * (1.0 - sg)))).astype(BF16)

        @pl.when(j >= 2)
        def _():
            o_ref[...] = (dav * gv * sg).astype(BF16)

    return pl.pallas_call(
        body, name=name, grid=(s // tr, 4),
        in_specs=[pl.BlockSpec((tr, FF_BLK), lambda i, j: (i, j % 2)),
                  pl.BlockSpec((tr, FF_BLK), lambda i, j: (i, j % 2)),
                  pl.BlockSpec((tr, FF_BLK), lambda i, j: (i, 2 + j % 2))],
        out_specs=pl.BlockSpec((tr, FF_BLK), lambda i, j: (i, j)),
        out_shape=jax.ShapeDtypeStruct((s, 2 * D_FF), BF16),
        compiler_params=_params(2),
    )(da, gu, gu)


def _split3(x):
    hi = x.astype(BF16)
    r1 = x - hi.astype(F32)
    mid = r1.astype(BF16)
    lo = (r1 - mid.astype(F32)).astype(BF16)
    return hi, mid, lo


def _dot3(x, m_bf):
    hi, mid, lo = _split3(x)
    return (jnp.dot(hi, m_bf, preferred_element_type=F32)
            + jnp.dot(mid, m_bf, preferred_element_type=F32)
            + jnp.dot(lo, m_bf, preferred_element_type=F32))


def _dot3_left(m_bf, x):
    hi, mid, lo = _split3(x)
    return (jnp.dot(m_bf, hi, preferred_element_type=F32)
            + jnp.dot(m_bf, mid, preferred_element_type=F32)
            + jnp.dot(m_bf, lo, preferred_element_type=F32))


def _dot2(x, m_bf):
    hi = x.astype(BF16)
    lo = (x - hi.astype(F32)).astype(BF16)
    return jnp.dot(hi, m_bf, preferred_element_type=F32) + jnp.dot(lo, m_bf, preferred_element_type=F32)


def _nt(a, b):
    return lax.dot_general(a, b, (((1,), (1,)), ((), ())), preferred_element_type=F32)


def _mm32(a, b):
    return jnp.dot(a, b, preferred_element_type=F32)


def _iota2(shape, dim):
    return lax.broadcasted_iota(jnp.int32, shape, dim)


def _rope_tables(s):
    half = ROT_DIM // 2
    pos = jnp.arange(s, dtype=F32)
    inv_freq = ROPE_THETA ** (-jnp.arange(half, dtype=F32) * 2.0 / ROT_DIM)
    ang = pos[:, None] * inv_freq[None, :]
    cos, sin = jnp.cos(ang), jnp.sin(ang)
    ones = jnp.ones((s, HEAD_DIM - ROT_DIM), F32)
    cos_t = jnp.concatenate([cos, cos, ones], axis=1)
    sin_t = jnp.concatenate([-sin, sin, 0.0 * ones], axis=1)
    idx = jnp.arange(HEAD_DIM)
    partner = jnp.where(idx < half, idx + half, idx - half)
    swap = ((idx[:, None] == partner[None, :]) & (idx[None, :] < ROT_DIM)).astype(F32)
    swap2 = jnp.kron(jnp.eye(2, dtype=F32), swap).astype(BF16)
    return jnp.tile(cos_t, (1, 2)), jnp.tile(sin_t, (1, 2)), swap2


def _rope(x, cos_t, sin_t, swap):
    return x * cos_t + _dot3(x, swap) * sin_t


def _rope_t(g, cos_t, sin_t, swap):
    return g * cos_t + _dot3(g * sin_t, swap)


def _dil_weight(dlt):
    nonneg = dlt >= 0
    w = jnp.zeros(dlt.shape, F32)
    for window, dil in DIL_PATTERNS:
        ok = nonneg & (dlt <= window) & ((dlt & (dil - 1)) == 0)
        w = w + ok.astype(F32)
    return w


FAR_TILES = 3
assert (FAR_TILES - 1) * TB + 1 > DIL_PATTERNS[1][0] and DIL_PATTERNS[2][0] >= 2048


def _log_sig_pair(z):
    sp = jnp.log(1.0 + jnp.exp(-jnp.abs(z)))
    return jnp.minimum(z, 0.0) - sp, -jnp.maximum(z, 0.0) - sp


def _pair_masks(x, lane_lo):
    z = jnp.zeros_like(x)
    return jnp.where(lane_lo, x, z).astype(BF16), jnp.where(lane_lo, z, x).astype(BF16)


def _rows(i):
    return pl.ds(pl.multiple_of(i * TB, TB), TB)


def _head_spec(s, col0):
    return pl.BlockSpec((s, LANES), lambda p: (0, col0 + p))


def _stat_spec(s):
    return pl.BlockSpec((2, s, 1), lambda p: (p, 0, 0))


def _rowstat_spec(s):
    return pl.BlockSpec((2, 1, s), lambda p: (p, 0, 0))


def _full_spec(shape):
    nd = len(shape)
    return pl.BlockSpec(shape, lambda p: (0,) * nd)


K_COL, V_COL = D_ATTN // LANES, 2 * D_ATTN // LANES


def _bwd_scratch(s):
    return ([pltpu.VMEM((s, LANES), BF16)] * 8 + [pltpu.VMEM((LANES, s), BF16)] * 4
            + [pltpu.VMEM((LANES, s), F32)] * 2)


def _bwd_prep(i, q, k, v, dov, scr, lane_lo, sub_lo):
    qlo, qhi, klo, khi, kbf, vbf, dolo, dohi, qtlo, qthi, dotlo, dothi = scr[:12]
    rows = _rows(i)
    qlo[rows, :], qhi[rows, :] = _pair_masks(q, lane_lo)
    klo[rows, :], khi[rows, :] = _pair_masks(k, lane_lo)
    kbf[rows, :] = k.astype(BF16)
    vbf[rows, :] = v.astype(BF16)
    dolo[rows, :], dohi[rows, :] = _pair_masks(dov, lane_lo)
    qtlo[:, rows], qthi[:, rows] = _pair_masks(q.T, sub_lo)
    dotlo[:, rows], dothi[:, rows] = _pair_masks(dov.T, sub_lo)


def _sb_fwd(qkv, n_pairs, *, name, comm=None):
    s = qkv.shape[0]
    assert s % TB == 0
    nq = s // TB

    def body(q_ref, k_ref, v_ref, o_ref, ct_ref, qlo, qhi, kbf, vlo, vhi):
        lane_lo = _iota2((TB, LANES), 1) < HEAD_DIM

        def prep(i, _):
            rows = _rows(i)
            qlo[rows, :], qhi[rows, :] = _pair_masks(q_ref[rows, :], lane_lo)
            kbf[rows, :] = k_ref[rows, :].astype(BF16)
            vlo[rows, :], vhi[rows, :] = _pair_masks(v_ref[rows, :], lane_lo)
            return 0

        lax.fori_loop(0, nq, prep, 0)
        rmc = _iota2((TB, TB), 0) - _iota2((TB, TB), 1)
        strict = rmc > 0
        u_ge = (rmc >= 0).astype(BF16)
        qm, vm = (qlo, qhi), (vlo, vhi)

        def qloop(i, _):
            rows = _rows(i)

            def tile(kb, carry, diag):
                c, acc = list(carry[:2]), carry[2]
                keys = _rows(kb)
                k = kbf[keys, :]
                for h in range(2):
                    z = _nt(qm[h][rows, :], k) * SCALE
                    lb, lm = _log_sig_pair(z)
                    if diag:
                        lm = jnp.where(strict, lm, 0.0)
                    r_in = _dot2(lm, u_ge)
                    a = jnp.exp(lb + (r_in - lm) + c[h])
                    if diag:
                        a = jnp.where(strict, a, 0.0)
                    acc = acc + _mm32(a.astype(BF16), vm[h][keys, :])
                    c[h] = c[h] + r_in[:, 0:1]
                return c[0], c[1], acc

            z1 = jnp.zeros((TB, 1), F32)
            carry = tile(i, (z1, z1, jnp.zeros((TB, LANES), F32)), True)
            c0, c1, acc = lax.fori_loop(0, i, lambda t, cr: tile(i - 1 - t, cr, False), carry)
            o_ref[rows, :] = acc
            ct_ref[0, rows, :] = c0
            ct_ref[1, rows, :] = c1
            return 0

        lax.fori_loop(0, nq, qloop, 0)

    return _call_pairs(
        body, name=name, n_pairs=n_pairs, comm=comm,
        in_specs=[_head_spec(s, 0), _head_spec(s, K_COL), _head_spec(s, V_COL)], args=(qkv, qkv, qkv),
        out_specs=[_head_spec(s, 0), _stat_spec(s)],
        out_shape=[jax.ShapeDtypeStruct((s, LANES * n_pairs), F32),
                   jax.ShapeDtypeStruct((2 * n_pairs, s, 1), F32)],
        scratch=[pltpu.VMEM((s, LANES), BF16)] * 5)


def _sb_bwd(qkv, do, ctot, n_pairs, do_col0, *, name, comm=None):
    s = qkv.shape[0]
    assert s % TB == 0
    nq = s // TB

    def body(q_ref, k_ref, v_ref, do_ref, ct_ref, dq_ref, dk_ref, dv_ref, *scr):
        qlo, qhi, klo, khi, kbf, vbf, dolo, dohi, qtlo, qthi, dotlo, dothi, dkt, dvt = scr
        lane_lo = _iota2((TB, LANES), 1) < HEAD_DIM
        sub_lo = _iota2((LANES, TB), 0) < HEAD_DIM

        def prep(i, _):
            rows = _rows(i)
            _bwd_prep(i, q_ref[rows, :], k_ref[rows, :], v_ref[rows, :], do_ref[rows, :], scr, lane_lo, sub_lo)
            return 0

        lax.fori_loop(0, nq, prep, 0)
        dkt[...] = jnp.zeros_like(dkt)
        dvt[...] = jnp.zeros_like(dvt)
        rmc = _iota2((TB, TB), 0) - _iota2((TB, TB), 1)
        strict = rmc > 0
        u_le = (rmc <= 0).astype(BF16)
        qm, km, dom, qtm, dotm = (qlo, qhi), (klo, khi), (dolo, dohi), (qtlo, qthi), (dotlo, dothi)

        def qloop(i, _):
            rows = _rows(i)
            ct = (ct_ref[0, rows, :], ct_ref[1, rows, :])

            def tile(kb, carry, diag):
                pre, hl, dq = list(carry[0:2]), list(carry[2:4]), carry[4]
                keys = _rows(kb)
                k, v = kbf[keys, :], vbf[keys, :]
                dk_t, dv_t = dkt[:, keys], dvt[:, keys]
                for h in range(2):
                    z = _nt(qm[h][rows, :], k) * SCALE
                    lb, lm = _log_sig_pair(z)
                    if diag:
                        lm = jnp.where(strict, lm, 0.0)
                    pin = _dot2(lm, u_le)
                    a = jnp.exp(lb + (ct[h] - pre[h] - pin))
                    if diag:
                        a = jnp.where(strict, a, 0.0)
                    g = a * _nt(dom[h][rows, :], v)
                    hin = _dot2(g, u_le)
                    beta = jnp.exp(lb)
                    dz = g * (1.0 - beta) - (hl[h] + hin - g) * beta
                    if diag:
                        dz = jnp.where(strict, dz, 0.0)
                    dzb = (dz * SCALE).astype(BF16)
                    dq = dq + _mm32(dzb, km[h][keys, :])
                    dk_t = dk_t + _mm32(qtm[h][:, rows], dzb)
                    dv_t = dv_t + _mm32(dotm[h][:, rows], a.astype(BF16))
                    pre[h] = pre[h] + pin[:, TB - 1:TB]
                    hl[h] = hl[h] + hin[:, TB - 1:TB]
                dkt[:, keys] = dk_t
                dvt[:, keys] = dv_t
                return pre[0], pre[1], hl[0], hl[1], dq

            z1 = jnp.zeros((TB, 1), F32)
            carry = lax.fori_loop(0, i, lambda kb, cr: tile(kb, cr, False),
                                  (z1, z1, z1, z1, jnp.zeros((TB, LANES), F32)))
            dq = tile(i, carry, True)[4]
            dq_ref[rows, :] = dq.astype(BF16)
            return 0

        lax.fori_loop(0, nq, qloop, 0)

        def wloop(i, _):
            rows = _rows(i)
            dk_ref[rows, :] = dkt[:, rows].T.astype(BF16)
            dv_ref[rows, :] = dvt[:, rows].T.astype(BF16)
            return 0

        lax.fori_loop(0, nq, wloop, 0)

    out = jax.ShapeDtypeStruct((s, LANES * n_pairs), BF16)
    return _call_pairs(
        body, name=name, n_pairs=n_pairs, comm=comm,
        in_specs=[_head_spec(s, 0), _head_spec(s, K_COL), _head_spec(s, V_COL),
                  _head_spec(s, do_col0), _stat_spec(s)], args=(qkv, qkv, qkv, do, ctot),
        out_specs=[_head_spec(s, 0)] * 3, out_shape=[out, out, out], scratch=_bwd_scratch(s))


def _bias_fwd(mode, qkv, head0_col, n_pairs, extra, *, name, comm=None):
    s = qkv.shape[0]
    assert s % TB == 0 and s <= DIL_PATTERNS[2][0]
    nq = s // TB
    fox = mode == "fox"

    def body(q_ref, k_ref, v_ref, e0, e1, *rest):
        if fox:
            o_ref, lse_ref, qlo, qhi, kbf, vx0, vx1 = rest
        else:
            e2, o_ref, lse_ref, qlo, qhi, kbf, vx0, vx1 = rest
        lane_lo = _iota2((TB, LANES), 1) < HEAD_DIM

        def prep(i, _):
            rows = _rows(i)
            q, k, v = q_ref[rows, :], k_ref[rows, :], v_ref[rows, :]
            if not fox:
                c, sn, sw = e0[rows, :], e1[rows, :], e2[...]
                q, k = _rope(q, c, sn, sw), _rope(k, c, sn, sw)
            qlo[rows, :], qhi[rows, :] = _pair_masks(q, lane_lo)
            kbf[rows, :] = k.astype(BF16)
            one = jnp.ones_like(v)
            vx0[rows, :] = jnp.where(lane_lo, v, one).astype(BF16)
            vx1[rows, :] = jnp.where(lane_lo, one, v).astype(BF16)
            return 0

        lax.fori_loop(0, nq, prep, 0)
        rmc = _iota2((TB, TB), 0) - _iota2((TB, TB), 1)
        far_w = ((rmc & (DIL_PATTERNS[2][1] - 1)) == 0).astype(F32)
        qm, vx = (qlo, qhi), (vx0, vx1)

        def qloop(i, _):
            rows = _rows(i)
            if fox:
                fq = (e0[0, rows, :], e0[1, rows, :])

            def tile(kb, carry, near):
                keys = _rows(kb)
                k = kbf[keys, :]
                if not fox:
                    w = _dil_weight((i - kb) * TB + rmc) if near else far_w
                out = []
                for h in range(2):
                    m, acc = carry[2 * h], carry[2 * h + 1]
                    sc = _nt(qm[h][rows, :], k) * SCALE
                    if fox:
                        sc = sc + (fq[h] - e1[h, :, keys])
                        if near:
                            sc = jnp.where(rmc >= 0, sc, NEG)
                    else:
                        sc = jnp.where(w > 0.0, sc, NEG)
                    m_new = jnp.maximum(m, jnp.max(sc, axis=1, keepdims=True))
                    p = jnp.exp(sc - m_new)
                    if not fox:
                        p = p * w
                    acc = jnp.exp(m - m_new) * acc + _mm32(p.astype(BF16), vx[h][keys, :])
                    out += [m_new, acc]
                return tuple(out)

            m0 = jnp.full((TB, 1), NEG, F32)
            a0 = jnp.zeros((TB, LANES), F32)
            n_far = i if fox else jnp.maximum(i + 1 - FAR_TILES, 0)
            carry = lax.fori_loop(0, n_far, lambda kb, cr: tile(kb, cr, False), (m0, a0, m0, a0))
            if fox:
                carry = tile(i, carry, True)
            else:
                carry = lax.fori_loop(n_far, i + 1, lambda kb, cr: tile(kb, cr, True), carry)
            m_0, acc0, m_1, acc1 = carry
            l0, l1 = acc0[:, HEAD_DIM:HEAD_DIM + 1], acc1[:, 0:1]
            o_ref[rows, :] = jnp.where(lane_lo, acc0 / l0, acc1 / l1)
            lse_ref[0, rows, :] = m_0 + jnp.log(l0)
            lse_ref[1, rows, :] = m_1 + jnp.log(l1)
            return 0

        lax.fori_loop(0, nq, qloop, 0)

    hp0 = head0_col
    if fox:
        e_specs = [_stat_spec(s), _rowstat_spec(s)]
    else:
        e_specs = [_full_spec((s, LANES)), _full_spec((s, LANES)), _full_spec((LANES, LANES))]
    return _call_pairs(
        body, name=name, n_pairs=n_pairs, comm=comm,
        in_specs=[_head_spec(s, hp0), _head_spec(s, K_COL + hp0), _head_spec(s, V_COL + hp0)] + e_specs,
        args=(qkv, qkv, qkv, *extra),
        out_specs=[_head_spec(s, 0), _stat_spec(s)],
        out_shape=[jax.ShapeDtypeStruct((s, LANES * n_pairs), F32),
                   jax.ShapeDtypeStruct((2 * n_pairs, s, 1), F32)],
        scratch=[pltpu.VMEM((s, LANES), BF16)] * 5)


def _bias_bwd(mode, qkv, head0_col, n_pairs, extra, o, do, do_col0, lse, *, name, comm=None):
    s = qkv.shape[0]
    assert s % TB == 0 and s <= DIL_PATTERNS[2][0]
    nq = s // TB
    fox = mode == "fox"

    def body(q_ref, k_ref, v_ref, o_ref, do_ref, lse_ref, e0, e1, *rest):
        if fox:
            dq_ref, dk_ref, dv_ref, dfr_ref, dfc_ref = rest[:5]
            scr = rest[5:]
        else:
            e2, dq_ref, dk_ref, dv_ref = rest[:4]
            scr = rest[4:]
        qlo, qhi, klo, khi, kbf, vbf, dolo, dohi, qtlo, qthi, dotlo, dothi, dkt, dvt = scr
        lane_lo = _iota2((TB, LANES), 1) < HEAD_DIM
        sub_lo = _iota2((LANES, TB), 0) < HEAD_DIM

        def prep(i, _):
            rows = _rows(i)
            q, k = q_ref[rows, :], k_ref[rows, :]
            if not fox:
                c, sn, sw = e0[rows, :], e1[rows, :], e2[...]
                q, k = _rope(q, c, sn, sw), _rope(k, c, sn, sw)
            _bwd_prep(i, q, k, v_ref[rows, :], do_ref[rows, :], scr, lane_lo, sub_lo)
            return 0

        lax.fori_loop(0, nq, prep, 0)
        dkt[...] = jnp.zeros_like(dkt)
        dvt[...] = jnp.zeros_like(dvt)
        if fox:
            dfr_ref[...] = jnp.zeros_like(dfr_ref)
        rmc = _iota2((TB, TB), 0) - _iota2((TB, TB), 1)
        far_w = ((rmc & (DIL_PATTERNS[2][1] - 1)) == 0).astype(F32)
        qm, km, dom, qtm, dotm = (qlo, qhi), (klo, khi), (dolo, dohi), (qtlo, qthi), (dotlo, dothi)

        def qloop(i, _):
            rows = _rows(i)
            prod = do_ref[rows, :] * o_ref[rows, :]
            dsum = (jnp.sum(jnp.where(lane_lo, prod, 0.0), axis=1, keepdims=True),
                    jnp.sum(jnp.where(lane_lo, 0.0, prod), axis=1, keepdims=True))
            lse_i = (lse_ref[0, rows, :], lse_ref[1, rows, :])
            if fox:
                fq = (e0[0, rows, :], e0[1, rows, :])

            def tile(kb, carry, near):
                dq, rs = carry[0], list(carry[1:])
                keys = _rows(kb)
                k, v = kbf[keys, :], vbf[keys, :]
                dk_t, dv_t = dkt[:, keys], dvt[:, keys]
                if not fox:
                    w = _dil_weight((i - kb) * TB + rmc) if near else far_w
                for h in range(2):
                    sc = _nt(qm[h][rows, :], k) * SCALE
                    if fox:
                        sc = sc + (fq[h] - e1[h, :, keys])
                        if near:
                            sc = jnp.where(rmc >= 0, sc, NEG)
                        p = jnp.exp(sc - lse_i[h])
                    else:
                        p = w * jnp.exp(jnp.where(w > 0.0, sc, NEG) - lse_i[h])
                    ds = p * (_nt(dom[h][rows, :], v) - dsum[h])
                    dsb = (ds * SCALE).astype(BF16)
                    dq = dq + _mm32(dsb, km[h][keys, :])
                    dk_t = dk_t + _mm32(qtm[h][:, rows], dsb)
                    dv_t = dv_t + _mm32(dotm[h][:, rows], p.astype(BF16))
                    if fox:
                        dfr_ref[h, :, keys] -= jnp.sum(ds, axis=0, keepdims=True)
                        rs[h] = rs[h] + jnp.sum(ds, axis=1, keepdims=True)
                dkt[:, keys] = dk_t
                dvt[:, keys] = dv_t
                return (dq, *rs)

            z1 = jnp.zeros((TB, 1), F32)
            init = (jnp.zeros((TB, LANES), F32), z1, z1) if fox else (jnp.zeros((TB, LANES), F32),)
            n_far = i if fox else jnp.maximum(i + 1 - FAR_TILES, 0)
            carry = lax.fori_loop(0, n_far, lambda kb, cr: tile(kb, cr, False), init)
            if fox:
                carry = tile(i, carry, True)
                dfc_ref[0, rows, :] = carry[1]
                dfc_ref[1, rows, :] = carry[2]
            else:
                carry = lax.fori_loop(n_far, i + 1, lambda kb, cr: tile(kb, cr, True), carry)
            dq = carry[0]
            if not fox:
                dq = _rope_t(dq, e0[rows, :], e1[rows, :], e2[...])
            dq_ref[rows, :] = dq.astype(BF16)
            return 0

        lax.fori_loop(0, nq, qloop, 0)

        def wloop(i, _):
            rows = _rows(i)
            dk = dkt[:, rows].T
            if not fox:
                dk = _rope_t(dk, e0[rows, :], e1[rows, :], e2[...])
            dk_ref[rows, :] = dk.astype(BF16)
            dv_ref[rows, :] = dvt[:, rows].T.astype(BF16)
            return 0

        lax.fori_loop(0, nq, wloop, 0)

    hp0 = head0_col
    out = jax.ShapeDtypeStruct((s, LANES * n_pairs), BF16)
    out_specs = [_head_spec(s, 0)] * 3
    out_shape = [out, out, out]
    if fox:
        e_specs = [_stat_spec(s), _rowstat_spec(s)]
        out_specs += [_rowstat_spec(s), _stat_spec(s)]
        out_shape += [jax.ShapeDtypeStruct((2 * n_pairs, 1, s), F32), jax.ShapeDtypeStruct((2 * n_pairs, s, 1), F32)]
    else:
        e_specs = [_full_spec((s, LANES)), _full_spec((s, LANES)), _full_spec((LANES, LANES))]
    return _call_pairs(
        body, name=name, n_pairs=n_pairs, comm=comm,
        in_specs=[_head_spec(s, hp0), _head_spec(s, K_COL + hp0), _head_spec(s, V_COL + hp0),
                  _head_spec(s, 0), _head_spec(s, do_col0), _stat_spec(s)] + e_specs,
        args=(qkv, qkv, qkv, o, do, lse, *extra),
        out_specs=out_specs, out_shape=out_shape, scratch=_bwd_scratch(s))


F_COL = 3 * D_ATTN // LANES


def _fgate_fwd(qkvf, brow, *, name):
    s = qkvf.shape[0]
    nb = s // BLK

    def body(f_ref, b_ref, fc_ref, fr_ref, fs):
        row, col = _iota2((BLK, BLK), 0), _iota2((BLK, BLK), 1)
        l_incl = (col <= row).astype(BF16)

        def step(i, carry):
            r0 = pl.multiple_of(i * BLK, BLK)
            lf, _ = _log_sig_pair(f_ref[pl.ds(r0, BLK), :] + b_ref[...])
            fblk = carry + _dot3_left(l_incl, lf)
            fs[pl.ds(r0, BLK), :] = fblk
            return fblk[BLK - 1:BLK, :]

        lax.fori_loop(0, nb, step, jnp.zeros((1, LANES), F32))
        ft = fs[...].T
        for h in range(N_HEADS):
            fc_ref[h, :, :] = fs[:, h:h + 1]
            fr_ref[h, :, :] = ft[h:h + 1, :]

    return pl.pallas_call(
        body, name=name, grid=(1,),
        in_specs=[pl.BlockSpec((s, LANES), lambda i: (0, F_COL)), pl.BlockSpec((1, LANES), lambda i: (0, 0))],
        out_specs=[pl.BlockSpec((N_HEADS, s, 1), lambda i: (0, 0, 0)),
                   pl.BlockSpec((N_HEADS, 1, s), lambda i: (0, 0, 0))],
        out_shape=[jax.ShapeDtypeStruct((N_HEADS, s, 1), F32), jax.ShapeDtypeStruct((N_HEADS, 1, s), F32)],
        scratch_shapes=[pltpu.VMEM((s, LANES), F32)],
        compiler_params=_params(1),
    )(qkvf, brow)


def _fgate_bwd(dfr, dfc, qkvf, brow, *, name):
    s = qkvf.shape[0]
    nb = s // BLK

    def body(dfr_ref, dfc_ref, f_ref, b_ref, dfl_ref, db_ref, ts, fs):
        ts[...] = jnp.zeros_like(ts)
        for h in range(N_HEADS):
            ts[h:h + 1, :] = dfr_ref[h]
        fs[...] = ts[...].T
        for h in range(N_HEADS):
            fs[:, h:h + 1] += dfc_ref[h]
        row, col = _iota2((BLK, BLK), 0), _iota2((BLK, BLK), 1)
        u_incl = (col >= row).astype(BF16)
        head_lane = _iota2((BLK, LANES), 1) < N_HEADS

        def step(ii, carry):
            tail, db = carry
            r0 = pl.multiple_of((nb - 1 - ii) * BLK, BLK)
            rblk = tail + _dot3_left(u_incl, fs[pl.ds(r0, BLK), :])
            _, lsn = _log_sig_pair(f_ref[pl.ds(r0, BLK), :] + b_ref[...])
            dfl = jnp.where(head_lane, rblk * jnp.exp(lsn), 0.0)
            dfl_ref[pl.ds(r0, BLK), :] = dfl.astype(BF16)
            return rblk[0:1, :], db + jnp.sum(dfl, axis=0, keepdims=True)

        z = jnp.zeros((1, LANES), F32)
        _, db = lax.fori_loop(0, nb, step, (z, z))
        db_ref[...] = db

    return pl.pallas_call(
        body, name=name, grid=(1,),
        in_specs=[pl.BlockSpec((N_HEADS, 1, s), lambda i: (0, 0, 0)), pl.BlockSpec((N_HEADS, s, 1), lambda i: (0, 0, 0)),
                  pl.BlockSpec((s, LANES), lambda i: (0, F_COL)), pl.BlockSpec((1, LANES), lambda i: (0, 0))],
        out_specs=[pl.BlockSpec((s, LANES), lambda i: (0, 0)), pl.BlockSpec((1, LANES), lambda i: (0, 0))],
        out_shape=[jax.ShapeDtypeStruct((s, LANES), BF16), jax.ShapeDtypeStruct((1, LANES), F32)],
        scratch_shapes=[pltpu.VMEM((LANES, s), F32), pltpu.VMEM((s, LANES), F32)],
        compiler_params=_params(1),
    )(dfr, dfc, qkvf, brow)


def _adamw_math(w, g, m, v):
    m2 = ADAM_B1 * m + (1.0 - ADAM_B1) * g
    v2 = ADAM_B2 * v + (1.0 - ADAM_B2) * (g * g)
    m_hat = m2 / (1.0 - ADAM_B1 ** ADAM_STEP)
    v_hat = v2 / (1.0 - ADAM_B2 ** ADAM_STEP)
    delta = -ADAM_LR * (m_hat / (jnp.sqrt(v_hat) + ADAM_EPS) + ADAM_WD * w)
    return delta, m2, v2


def _row_tile(r, cap=256, mult=16):
    best = None
    for t in range(mult, min(r, cap) + 1, mult):
        if r % t == 0:
            best = t
    assert best is not None, r
    return best


def _adamw_shard(w, m, v, lidx, g_all, r1, r2, sc, prev, *, name):
    nl, r, c = w.shape
    tr = _row_tile(r)

    def body(sc_ref, w_ref, m_ref, v_ref, g_ref, r1_ref, r2_ref, *rest):
        go_ref, d_ref, mo_ref, vo_ref = rest[-4:]
        g = g_ref[...] + r1_ref[...]
        g = g + r2_ref[0].astype(F32)
        g = g + r2_ref[1].astype(F32)
        g = g + r2_ref[2].astype(F32)
        delta, m2, v2 = _adamw_math(w_ref[...], g, m_ref[...], v_ref[...])
        go_ref[...] = g
        d_ref[...] = delta
        mo_ref[...] = m2
        vo_ref[...] = v2

    lay = pl.BlockSpec((None, tr, c), lambda i, s_: (lidx, i, 0))
    in_specs = [lay, lay, lay,
                pl.BlockSpec((None, tr, c), lambda i, s_: (s_[0], i, 0)),
                pl.BlockSpec((None, tr, c), lambda i, s_: (s_[1], i, 0)),
                pl.BlockSpec((3, tr, c), lambda i, s_: (0, i, 0))]
    args = [sc, w, m, v, g_all, r1, r2]
    aliases = {}
    if prev is not None:
        in_specs += [pl.BlockSpec(memory_space=pl.ANY)] * 4
        aliases = {7 + t: t for t in range(4)}
        args += list(prev)
    shp = jax.ShapeDtypeStruct((nl, r, c), F32)
    return pl.pallas_call(
        body, name=name,
        grid_spec=pltpu.PrefetchScalarGridSpec(
            num_scalar_prefetch=1, grid=(r // tr,), in_specs=in_specs, out_specs=[lay] * 4),
        out_shape=[shp] * 4, input_output_aliases=aliases,
        compiler_params=_params(1),
    )(*args)


def _adamw_small(w, g, m, v, *, name):
    def body(w_ref, g_ref, m_ref, v_ref, d_ref, mo_ref, vo_ref):
        delta, m2, v2 = _adamw_math(w_ref[...], g_ref[...], m_ref[...], v_ref[...])
        d_ref[...] = delta
        mo_ref[...] = m2
        vo_ref[...] = v2

    shp = jax.ShapeDtypeStruct(w.shape, F32)
    return pl.pallas_call(body, name=name, out_shape=[shp] * 3, compiler_params=_params())(w, g, m, v)


def _pos():
    return lax.axis_index("x"), lax.axis_index("y"), lax.axis_index("c")


def _other_chips(x, y):
    return [(1 - x, y), (x, 1 - y), (1 - x, 1 - y)]


def _dev_index(x, y, c):
    return 4 * x + 2 * y + c


HBM_SPEC = pl.BlockSpec(memory_space=pltpu.HBM)


class _Comm:
    def __init__(self, inputs, out_shape, scratch, start, mid, finish):
        self.inputs, self.out_shape, self.scratch = list(inputs), list(out_shape), list(scratch)
        self.start, self.mid, self.finish = start, mid, finish

    def run(self, name):
        n_in, n_out = len(self.inputs), len(self.out_shape)

        def body(*refs):
            parts = refs[:n_in], refs[n_in:n_in + n_out], refs[n_in + n_out:]
            self.start(*parts)
            self.mid(*parts)
            self.finish(*parts)

        return pl.pallas_call(
            body, name=name, in_specs=[HBM_SPEC] * n_in, out_specs=[HBM_SPEC] * n_out,
            out_shape=self.out_shape, scratch_shapes=self.scratch)(*self.inputs)


def _call_pairs(body, *, name, n_pairs, in_specs, args, out_specs, out_shape, scratch, comm=None):
    if comm is None:
        res = pl.pallas_call(
            body, name=name, grid=(n_pairs,), in_specs=in_specs, out_specs=out_specs, out_shape=out_shape,
            scratch_shapes=scratch, compiler_params=_params(1))(*args)
        return list(res), []
    sizes = (len(in_specs), len(comm.inputs), len(out_specs), len(comm.out_shape), len(scratch), len(comm.scratch))

    def fused(*refs):
        parts, o = [], 0
        for n in sizes:
            parts.append(refs[o:o + n])
            o += n
        h_in, c_in, h_out, c_out, h_scr, c_scr = parts
        p = pl.program_id(0)

        @pl.when(p == 0)
        def _():
            comm.start(c_in, c_out, c_scr)

        @pl.when(p == n_pairs - 1)
        def _():
            comm.mid(c_in, c_out, c_scr)

        body(*h_in, *h_out, *h_scr)

        @pl.when(p == n_pairs - 1)
        def _():
            comm.finish(c_in, c_out, c_scr)

    res = pl.pallas_call(
        fused, name=name, grid=(n_pairs,),
        in_specs=list(in_specs) + [HBM_SPEC] * sizes[1], out_specs=list(out_specs) + [HBM_SPEC] * sizes[3],
        out_shape=list(out_shape) + comm.out_shape, scratch_shapes=list(scratch) + comm.scratch,
        compiler_params=_params(1))(*args, *comm.inputs)
    return list(res[:sizes[2]]), list(res[sizes[2]:])


def _gather_comm(shards):
    n = len(shards)

    def plan(xs, outs, sems):
        send, recv, loc = sems
        x, y, c = _pos()
        me, sib = (x, y, c), (x, y, 1 - c)
        chips = _other_chips(x, y)

        def copy(a, k, block, to, src=None):
            dst = outs[a].at[_dev_index(*block)]
            return pltpu.make_async_remote_copy(
                src_ref=dst if src is None else src, dst_ref=dst,
                send_sem=send.at[a, k], recv_sem=recv.at[a, k], device_id=to, device_id_type=MESH)

        mine = [pltpu.make_async_copy(xs[a], outs[a].at[_dev_index(*me)], loc.at[a]) for a in range(n)]
        first = []
        for a in range(n):
            first.append(copy(a, 0, me, sib, src=xs[a]))
            first += [copy(a, 1 + j, me, (*chip, c), src=xs[a]) for j, chip in enumerate(chips)]
        passed = [(copy(a, 1 + j, (*chip, c), me), copy(a, 4 + j, (*chip, c), sib))
                  for j, chip in enumerate(chips) for a in range(n)]
        from_sib = [copy(a, 0, sib, me) for a in range(n)]
        from_sib += [copy(a, 4 + j, (*chip, 1 - c), me) for a in range(n) for j, chip in enumerate(chips)]
        return mine, first, passed, from_sib

    def start(xs, outs, sems):
        mine, first, _, _ = plan(xs, outs, sems)
        for cp in mine + first:
            cp.start()

    def mid(xs, outs, sems):
        for arrival, fwd in plan(xs, outs, sems)[2]:
            arrival.wait_recv()
            fwd.start()

    def finish(xs, outs, sems):
        mine, first, passed, from_sib = plan(xs, outs, sems)
        for cp in from_sib:
            cp.wait_recv()
        for cp in first + [fwd for _, fwd in passed]:
            cp.wait_send()
        for cp in mine:
            cp.wait()

    return _Comm(shards, [jax.ShapeDtypeStruct((N_DEV,) + a.shape, a.dtype) for a in shards],
                 [pltpu.SemaphoreType.DMA((n, 7)), pltpu.SemaphoreType.DMA((n, 7)), pltpu.SemaphoreType.DMA((n,))],
                 start, mid, finish)


def _rs_sibling(gs, *, name):
    n = len(gs)

    def body(*refs):
        g_refs, r_refs = refs[:n], refs[n:2 * n]
        send, recv = refs[2 * n:]
        x, y, c = _pos()
        copies = []
        for a in range(n):
            for k in range(4):
                copies.append(pltpu.make_async_remote_copy(
                    src_ref=g_refs[a].at[_dev_index(k // 2, k % 2, 1 - c)], dst_ref=r_refs[a].at[k],
                    send_sem=send.at[a, k], recv_sem=recv.at[a, k],
                    device_id=(x, y, 1 - c), device_id_type=MESH))
        for cp in copies:
            cp.start()
        for cp in copies:
            cp.wait()

    return pl.pallas_call(
        body, name=name,
        in_specs=[HBM_SPEC] * n, out_specs=[HBM_SPEC] * n,
        out_shape=[jax.ShapeDtypeStruct((4,) + g.shape[1:], g.dtype) for g in gs],
        scratch_shapes=[pltpu.SemaphoreType.DMA((n, 4)), pltpu.SemaphoreType.DMA((n, 4))],
    )(*gs)


def _rs_partial(g_all, r1, sc, *, name):
    _, r, c = g_all.shape
    tr = _row_tile(r)

    def body(sc_ref, g_ref, r_ref, o_ref):
        o_ref[...] = (g_ref[...] + r_ref[...]).astype(BF16)

    return pl.pallas_call(
        body, name=name,
        grid_spec=pltpu.PrefetchScalarGridSpec(
            num_scalar_prefetch=1, grid=(3, r // tr),
            in_specs=[pl.BlockSpec((None, tr, c), lambda j, i, s_: (s_[2 + j], i, 0)),
                      pl.BlockSpec((None, tr, c), lambda j, i, s_: (s_[5 + j], i, 0))],
            out_specs=pl.BlockSpec((None, tr, c), lambda j, i, s_: (j, i, 0))),
        out_shape=jax.ShapeDtypeStruct((3, r, c), BF16),
        compiler_params=_params(2),
    )(sc, g_all, r1)


def _cross_comm(ps):
    n = len(ps)

    def plan(p_refs, r_refs, sems):
        send, recv = sems
        x, y, c = _pos()
        return [pltpu.make_async_remote_copy(
            src_ref=p_refs[a].at[j], dst_ref=r_refs[a].at[j], send_sem=send.at[a, j], recv_sem=recv.at[a, j],
            device_id=(*chip, c), device_id_type=MESH)
            for j, chip in enumerate(_other_chips(x, y)) for a in range(n)]

    def start(*parts):
        for cp in plan(*parts):
            cp.start()

    def mid(*parts):
        pass

    def finish(*parts):
        for cp in plan(*parts):
            cp.wait()

    return _Comm(ps, [jax.ShapeDtypeStruct(p.shape, p.dtype) for p in ps],
                 [pltpu.SemaphoreType.DMA((n, 3)), pltpu.SemaphoreType.DMA((n, 3))], start, mid, finish)


SMALL_ROWS = 16


def _all_reduce_small(pack, *, name):
    def body(x_ref, o_ref, buf, send, recv):
        x, y, c = _pos()
        me = _dev_index(x, y, c)
        buf[me] = x_ref[...]
        copies = []
        for k in range(1, N_DEV):
            fx, fy, fc = (k >> 2) & 1, (k >> 1) & 1, k & 1
            peer = (1 - x if fx else x, 1 - y if fy else y, 1 - c if fc else c)
            copies.append(pltpu.make_async_remote_copy(
                src_ref=x_ref, dst_ref=buf.at[me], send_sem=send.at[k - 1], recv_sem=recv.at[k - 1],
                device_id=peer, device_id_type=MESH))
        for cp in copies:
            cp.start()
        for cp in copies:
            cp.wait()
        acc = buf[0]
        for d in range(1, N_DEV):
            acc = acc + buf[d]
        o_ref[...] = acc

    return pl.pallas_call(
        body, name=name,
        in_specs=[pl.BlockSpec(memory_space=pltpu.VMEM)], out_specs=pl.BlockSpec(memory_space=pltpu.VMEM),
        out_shape=jax.ShapeDtypeStruct(pack.shape, F32),
        scratch_shapes=[pltpu.VMEM((N_DEV,) + pack.shape, F32),
                        pltpu.SemaphoreType.DMA((N_DEV - 1,)), pltpu.SemaphoreType.DMA((N_DEV - 1,))],
    )(pack)


def _unshard_cols(g):
    return jnp.transpose(g, (1, 0, 2)).reshape(g.shape[1], N_DEV * g.shape[2])


def _shard_cols(w):
    k, n8 = w.shape
    return jnp.transpose(w.reshape(k, N_DEV, n8 // N_DEV), (1, 0, 2))


def _pad_row(v, width=D_MODEL):
    v = v.reshape(1, -1)
    return jnp.pad(v, ((0, 0), (0, width - v.shape[1])))


def _forward_layer(l, xc, g_mix, g_ffn, w, rope, brow, comm_a=None, comm_b=None):
    wq, wo, win, wout = w
    even = l % 2 == 0
    h1 = _rms_fwd(xc, g_mix, name=f"norm_mix_fwd{l}")
    qkv = _mm(h1, wq, name=f"qkv_fwd{l}", tm=1024, tn=768 if even else 640)
    if even:
        (o_a, st_a), got_a = _sb_fwd(qkv, N_HEADS // 4, name=f"sb_fwd{l}", comm=comm_a)
        (o_b, st_b), got_b = _bias_fwd("dil", qkv, N_HEADS // 4, N_HEADS // 4, rope, name=f"dil_fwd{l}",
                                       comm=comm_b)
        o = jnp.concatenate([o_a, o_b], axis=1)
        att = (o_b, st_a, st_b)
    else:
        assert comm_b is None
        fcol, frow = _fgate_fwd(qkv, brow, name=f"fgate_fwd{l}")
        (o, lse), got_a = _bias_fwd("fox", qkv, 0, N_HEADS // 2, (fcol, frow), name=f"fox_fwd{l}", comm=comm_a)
        got_b = []
        att = (o, lse, fcol, frow)
    o_bf = o.astype(BF16)
    xm = _mm(o_bf, wo, add=xc, name=f"wo_fwd{l}", tm=512, tn=1024)
    h2 = _rms_fwd(xm, g_ffn, name=f"norm_ffn_fwd{l}")
    gu = _mm(h2, win, name=f"ffn_in_fwd{l}", tm=1024, tn=512)
    a = _swiglu_fwd(gu, name=f"swiglu_fwd{l}")
    xo = _mm(a, wout, add=xm, name=f"ffn_out_fwd{l}", tm=512, tn=1024)
    return xo, (xc, h1, qkv, att, o_bf, xm, h2, gu, a), got_a, got_b


def _backward_ffn(l, dx, dxb, saved, g_ffn, w):
    _, _, win, wout = w
    _, _, _, _, _, xm, h2, gu, a = saved
    da = _mm(dxb, wout, tb=True, name=f"ffn_out_dx{l}", tm=1024, tn=FF_BLK)
    d_wout = _mm(a, dxb, ta=True, name=f"ffn_out_dw{l}", tm=FF_BLK, tn=512)
    dgu = _swiglu_bwd(da, gu, name=f"swiglu_bwd{l}")
    d_win = _mm(h2, dgu, ta=True, name=f"ffn_in_dw{l}", tm=1024, tn=512)
    dh2 = _mm(dgu, win, tb=True, name=f"ffn_in_dx{l}", tm=512, tn=1024, tk=FF_BLK)
    dxm, dxmb, dg_ffn = _rms_bwd(xm, g_ffn, dh2, dx, name=f"norm_ffn_bwd{l}")
    return dxm, dxmb, dg_ffn, d_win, d_wout


def _backward_attn(l, dxm, dxmb, saved, g_mix, w, rope, brow, comm_a=None, comm_b=None):
    wq, wo, _, _ = w
    xin, h1, qkv, att, o_bf, _, _, _, _ = saved
    even = l % 2 == 0
    d_wo = _mm(o_bf, dxmb, ta=True, name=f"wo_dw{l}", tm=512, tn=1024)
    do = _mm(dxmb, wo, tb=True, name=f"wo_dx{l}", tm=1024, tn=1024)
    db = None
    if even:
        o_b, st_a, st_b = att
        (dqa, dka, dva), got_a = _sb_bwd(qkv, do, st_a, N_HEADS // 4, 0, name=f"sb_bwd{l}", comm=comm_a)
        (dqb, dkb, dvb), got_b = _bias_bwd("dil", qkv, N_HEADS // 4, N_HEADS // 4, rope, o_b, do,
                                           N_HEADS // 4, st_b, name=f"dil_bwd{l}", comm=comm_b)
        dqkv = jnp.concatenate([dqa, dqb, dka, dkb, dva, dvb], axis=1)
    else:
        assert comm_b is None
        o, lse, fcol, frow = att
        (dq, dk, dv, dfr, dfc), got_a = _bias_bwd("fox", qkv, 0, N_HEADS // 2, (fcol, frow), o, do, 0, lse,
                                                  name=f"fox_bwd{l}", comm=comm_a)
        got_b = []
        dfl, db = _fgate_bwd(dfr, dfc, qkv, brow, name=f"fgate_bwd{l}")
        dqkv = jnp.concatenate([dq, dk, dv, dfl], axis=1)
    d_wq = _mm(h1, dqkv, ta=True, name=f"qkv_dw{l}", tm=1024, tn=768 if even else 640)
    dh1 = _mm(dqkv, wq, tb=True, name=f"qkv_dx{l}", tm=512, tn=1024, tk=1024 if even else 640)
    dx, dxb, dg_mix = _rms_bwd(xin, g_mix, dh1, dxm, name=f"norm_mix_bwd{l}")
    return dx, dxb, dg_mix, d_wq, d_wo, db, got_a, got_b


def kernel(x, norm_mix, w_qkv_even, w_o_even, w_qkvf_odd, b_forget, w_o_odd, norm_ffn, w_ffn_in, w_ffn_out, norm_final, loss_target, m_norm_mix, m_w_qkv_even, m_w_o_even, m_w_qkvf_odd, m_b_forget, m_w_o_odd, m_norm_ffn, m_w_ffn_in, m_w_ffn_out, m_norm_final, v_norm_mix, v_w_qkv_even, v_w_o_even, v_w_qkvf_odd, v_b_forget, v_w_o_odd, v_norm_ffn, v_w_ffn_in, v_w_ffn_out, v_norm_final):
    xi, yi, ci = _pos()
    others = _other_chips(xi, yi)
    sc = jnp.stack([_dev_index(xi, yi, ci), 2 * xi + yi]
                   + [_dev_index(px, py, ci) for px, py in others]
                   + [2 * px + py for px, py in others]).astype(jnp.int32)
    n_odd_cols = w_qkvf_odd.shape[2] * N_DEV

    xs, tgt = x[0], loss_target[0]
    rope = _rope_tables(xs.shape[0])
    brow = [_pad_row(b_forget[i], LANES) for i in range(DEPTH // 2)]

    def shards(l):
        even = l % 2 == 0
        wq_s = (w_qkv_even if even else w_qkvf_odd)[l // 2]
        wo_s = (w_o_even if even else w_o_odd)[l // 2]
        return [wq_s.astype(BF16), wo_s.astype(BF16)], [w_ffn_in[l].astype(BF16), w_ffn_out[l].astype(BF16)]

    def full(l, gq, go, gi, gout):
        wq = _unshard_cols(gq)
        if l % 2 == 1:
            wq = jnp.pad(wq, ((0, 0), (0, QKVF_PAD - n_odd_cols)))
        return wq, go.reshape(D_ATTN, D_MODEL), _unshard_cols(gi), gout.reshape(D_FF, D_MODEL)

    mix0, ffn0 = shards(0)
    weights = [full(0, *_gather_comm(mix0 + ffn0).run("gather_weights0"))]
    saved = []
    xc = xs
    for l in range(DEPTH):
        comm_a = comm_b = None
        if l + 1 < DEPTH:
            mix_n, ffn_n = shards(l + 1)
            if l % 2 == 0:
                comm_a, comm_b = _gather_comm(ffn_n), _gather_comm(mix_n)
            else:
                comm_a = _gather_comm(mix_n + ffn_n)
        xc, sv, got_a, got_b = _forward_layer(l, xc, norm_mix[l:l + 1], norm_ffn[l:l + 1], weights[l], rope,
                                              brow[l // 2], comm_a, comm_b)
        saved.append(sv)
        if l + 1 < DEPTH:
            weights.append(full(l + 1, *(got_b + got_a if l % 2 == 0 else got_a)))

    loss_row, dx, dxb, dg_final = _final_loss(xc, norm_final.reshape(1, -1), tgt, name="final_loss")

    sharded = {
        "qkv_even": (w_qkv_even, m_w_qkv_even, v_w_qkv_even), "o_even": (w_o_even, m_w_o_even, v_w_o_even),
        "qkvf_odd": (w_qkvf_odd, m_w_qkvf_odd, v_w_qkvf_odd), "o_odd": (w_o_odd, m_w_o_odd, v_w_o_odd),
        "ffn_in": (w_ffn_in, m_w_ffn_in, v_w_ffn_in), "ffn_out": (w_ffn_out, m_w_ffn_out, v_w_ffn_out),
    }
    results = {k: None for k in sharded}

    def chip_sums(gs, keys, tag):
        r1s = _rs_sibling(gs, name=f"grads_to_sibling_{tag}")
        ps = [_rs_partial(g, r1, sc, name=f"grads_chip_sum_{tag}_{a}") for a, (g, r1) in enumerate(zip(gs, r1s))]
        return gs, r1s, ps, keys

    def update(group, r2s, tag):
        gs, r1s, _, keys = group
        for a, (key, lidx) in enumerate(keys):
            w, m, v = sharded[key]
            results[key] = _adamw_shard(w, m, v, lidx, gs[a], r1s[a], r2s[a], sc, results[key],
                                        name=f"adamw_{key}_{tag}")

    dg_mix, dg_ffn, db_f = [None] * DEPTH, [None] * DEPTH, [None] * (DEPTH // 2)
    pending = None
    for l in reversed(range(DEPTH)):
        even = l % 2 == 0
        dxm, dxmb, dg_ffn[l], d_win, d_wout = _backward_ffn(l, dx, dxb, saved[l], norm_ffn[l:l + 1], weights[l])
        ffn = chip_sums([_shard_cols(d_win), d_wout.reshape(N_DEV, D_FF // N_DEV, D_MODEL)],
                        [("ffn_in", l), ("ffn_out", l)], f"ffn{l}")
        if even:
            comm_a = _cross_comm(ffn[2])
            comm_b = _cross_comm(pending[2]) if pending is not None else None
        else:
            comm_a = _cross_comm(ffn[2] + (pending[2] if pending is not None else []))
            comm_b = None
        dx, dxb, dg_mix[l], d_wq, d_wo, db, got_a, got_b = _backward_attn(
            l, dxm, dxmb, saved[l], norm_mix[l:l + 1], weights[l], rope, brow[l // 2], comm_a, comm_b)
        update(ffn, got_a[:2], f"ffn{l}")
        if pending is not None:
            update(pending, got_b if even else got_a[2:], f"mix{l + 1}")
        if not even:
            db_f[l // 2] = db
            d_wq = d_wq[:, :n_odd_cols]
        pending = chip_sums([_shard_cols(d_wq), d_wo.reshape(N_DEV, D_ATTN // N_DEV, D_MODEL)],
                            [("qkv_even" if even else "qkvf_odd", l // 2), ("o_even" if even else "o_odd", l // 2)],
                            f"mix{l}")
    update(pending, _cross_comm(pending[2]).run("grads_to_chips_mix0"), "mix0")

    zeros = jnp.zeros((SMALL_ROWS - 11, D_MODEL), F32)
    db_row = _pad_row(jnp.concatenate([d[:, :N_HEADS] for d in db_f], axis=1))
    pack_g = jnp.concatenate(dg_mix + dg_ffn + [dg_final, db_row, _pad_row(loss_row[:, :1]), zeros], axis=0)
    tot = _all_reduce_small(pack_g, name="small_all_reduce")

    def pack(nm, nf, nfin, bf):
        return jnp.concatenate([nm, nf, nfin.reshape(1, -1), _pad_row(bf),
                                jnp.zeros((SMALL_ROWS - 10, D_MODEL), F32)], axis=0)

    d_s, m_s, v_s = _adamw_small(
        pack(norm_mix, norm_ffn, norm_final, b_forget), tot,
        pack(m_norm_mix, m_norm_ffn, m_norm_final, m_b_forget),
        pack(v_norm_mix, v_norm_ffn, v_norm_final, v_b_forget), name="adamw_small")

    def unpack(p):
        nb = b_forget.size
        return {"norm_mix": p[0:DEPTH], "norm_ffn": p[DEPTH:2 * DEPTH], "norm_final": p[2 * DEPTH],
                "b_forget": p[2 * DEPTH + 1, :nb].reshape(b_forget.shape)}

    small = [unpack(tot), unpack(d_s), unpack(m_s), unpack(v_s)]
    loss = tot[2 * DEPTH + 2, 0]

    order = ["norm_mix", "qkv_even", "o_even", "qkvf_odd", "b_forget", "o_odd", "norm_ffn", "ffn_in", "ffn_out",
             "norm_final"]
    outs = [loss, dx[None]]
    for t in range(4):
        for key in order:
            outs.append(small[t][key] if key in small[t] else results[key][t])
    return tuple(outs)
```

```python
import jax
import jax.numpy as jnp
from jax import lax
from jax.experimental import pallas as pl
from jax.experimental.pallas import tpu as pltpu

F32 = jnp.float32
BF16 = jnp.bfloat16

D_MODEL = 1024
HEAD_DIM = 64
N_HEADS = 16
D_ATTN = N_HEADS * HEAD_DIM
D_FF = 2816
DEPTH = 4
ROPE_THETA = 500000.0
ROT_DIM = HEAD_DIM // 4
RMS_EPS = 1e-5
SCALE = HEAD_DIM ** -0.5
DIL_PATTERNS = ((128, 1), (512, 4), (2048, 16))
N_DEV = 8
QKVF_PAD = 3200

ADAM_LR = 0.001
ADAM_B1 = 0.9
ADAM_B2 = 0.999
ADAM_EPS = 1e-08
ADAM_WD = 0.01
ADAM_STEP = 10

LANES = 128
BLK = 128
TB = 256
NEG = -1e30
VMEM_LIMIT = 48 * 1024 * 1024

MESH = pl.DeviceIdType.MESH


def _params(n_grid=0, **kw):
    sem = ("arbitrary",) * n_grid if n_grid else None
    return pltpu.CompilerParams(dimension_semantics=sem, vmem_limit_bytes=VMEM_LIMIT, **kw)


def _mm(a, b, *, name, ta=False, tb=False, add=None, out_dtype=F32, tm=512, tn=512, tk=None):
    m = a.shape[1] if ta else a.shape[0]
    k = a.shape[0] if ta else a.shape[1]
    n = b.shape[0] if tb else b.shape[1]
    assert (b.shape[1] if tb else b.shape[0]) == k
    tm, tn = min(tm, m), min(tn, n)
    tk = k if tk is None else min(tk, k)
    assert m % tm == 0 and n % tn == 0 and k % tk == 0, (name, m, n, k, tm, tn, tk)
    nk = k // tk
    dn = (((0 if ta else 1,), (1 if tb else 0,)), ((), ()))

    def body(*refs):
        a_ref, b_ref = refs[0], refs[1]
        add_ref = refs[2] if add is not None else None
        o_ref = refs[3] if add is not None else refs[2]
        part = lax.dot_general(a_ref[...], b_ref[...], dn, preferred_element_type=F32)
        if nk == 1:
            if add_ref is not None:
                part = part + add_ref[...]
            o_ref[...] = part.astype(out_dtype)
            return
        acc_ref = refs[-1]
        kk = pl.program_id(2)

        @pl.when(kk == 0)
        def _():
            acc_ref[...] = part

        @pl.when(kk > 0)
        def _():
            acc_ref[...] += part

        @pl.when(kk == nk - 1)
        def _():
            res = acc_ref[...]
            if add_ref is not None:
                res = res + add_ref[...]
            o_ref[...] = res.astype(out_dtype)

    a_spec = (pl.BlockSpec((tk, tm), lambda i, j, kk: (kk, i)) if ta
              else pl.BlockSpec((tm, tk), lambda i, j, kk: (i, kk)))
    b_spec = (pl.BlockSpec((tn, tk), lambda i, j, kk: (j, kk)) if tb
              else pl.BlockSpec((tk, tn), lambda i, j, kk: (kk, j)))
    o_spec = pl.BlockSpec((tm, tn), lambda i, j, kk: (i, j))
    in_specs = [a_spec, b_spec] + ([o_spec] if add is not None else [])
    args = (a, b) + ((add,) if add is not None else ())
    return pl.pallas_call(
        body, name=name, grid=(m // tm, n // tn, nk),
        in_specs=in_specs, out_specs=o_spec,
        out_shape=jax.ShapeDtypeStruct((m, n), out_dtype),
        scratch_shapes=[pltpu.VMEM((tm, tn), F32)] if nk > 1 else [],
        compiler_params=_params(3),
    )(*args)


def _rms_fwd(x, g, *, name, tr=256):
    s, d = x.shape

    def body(x_ref, g_ref, h_ref):
        xv = x_ref[...]
        r = lax.rsqrt(jnp.mean(xv * xv, axis=-1, keepdims=True) + RMS_EPS)
        h_ref[...] = (xv * r * g_ref[...]).astype(BF16)

    return pl.pallas_call(
        body, name=name, grid=(s // tr,),
        in_specs=[pl.BlockSpec((tr, d), lambda i: (i, 0)), pl.BlockSpec((1, d), lambda i: (0, 0))],
        out_specs=pl.BlockSpec((tr, d), lambda i: (i, 0)),
        out_shape=jax.ShapeDtypeStruct((s, d), BF16),
        compiler_params=_params(1),
    )(x, g)


def _rms_bwd(x, g, dh, dres, *, name, tr=256):
    s, d = x.shape

    def body(x_ref, g_ref, dh_ref, dres_ref, dx_ref, dxb_ref, dg_ref):
        xv = x_ref[...]
        r = lax.rsqrt(jnp.mean(xv * xv, axis=-1, keepdims=True) + RMS_EPS)
        y = xv * r
        dhv = dh_ref[...]
        dy = dhv * g_ref[...]
        dx = dres_ref[...] + r * (dy - y * jnp.mean(dy * y, axis=-1, keepdims=True))
        dx_ref[...] = dx
        dxb_ref[...] = dx.astype(BF16)
        part = jnp.sum(dhv * y, axis=0, keepdims=True)

        @pl.when(pl.program_id(0) == 0)
        def _():
            dg_ref[...] = part

        @pl.when(pl.program_id(0) > 0)
        def _():
            dg_ref[...] += part

    row = pl.BlockSpec((tr, d), lambda i: (i, 0))
    vec = pl.BlockSpec((1, d), lambda i: (0, 0))
    return pl.pallas_call(
        body, name=name, grid=(s // tr,),
        in_specs=[row, vec, row, row], out_specs=[row, row, vec],
        out_shape=[jax.ShapeDtypeStruct((s, d), F32), jax.ShapeDtypeStruct((s, d), BF16),
                   jax.ShapeDtypeStruct((1, d), F32)],
        compiler_params=_params(1),
    )(x, g, dh, dres)


def _final_loss(x, g, tgt, *, name, tr=256):
    s, d = x.shape

    def body(x_ref, g_ref, t_ref, loss_ref, dx_ref, dxb_ref, dg_ref):
        xv = x_ref[...]
        gv = g_ref[...]
        r = lax.rsqrt(jnp.mean(xv * xv, axis=-1, keepdims=True) + RMS_EPS)
        y = xv * r
        err = y * gv - t_ref[...]
        lpart = 0.5 * jnp.sum(jnp.mean(err * err, axis=-1, keepdims=True), axis=0, keepdims=True)
        dh = err * (1.0 / d)
        dy = dh * gv
        dx = r * (dy - y * jnp.mean(dy * y, axis=-1, keepdims=True))
        dx_ref[...] = dx
        dxb_ref[...] = dx.astype(BF16)
        gpart = jnp.sum(dh * y, axis=0, keepdims=True)
        lrow = jnp.broadcast_to(lpart, (1, LANES))

        @pl.when(pl.program_id(0) == 0)
        def _():
            dg_ref[...] = gpart
            loss_ref[...] = lrow

        @pl.when(pl.program_id(0) > 0)
        def _():
            dg_ref[...] += gpart
            loss_ref[...] += lrow

    row = pl.BlockSpec((tr, d), lambda i: (i, 0))
    vec = pl.BlockSpec((1, d), lambda i: (0, 0))
    lsp = pl.BlockSpec((1, LANES), lambda i: (0, 0))
    return pl.pallas_call(
        body, name=name, grid=(s // tr,),
        in_specs=[row, vec, row], out_specs=[lsp, row, row, vec],
        out_shape=[jax.ShapeDtypeStruct((1, LANES), F32), jax.ShapeDtypeStruct((s, d), F32),
                   jax.ShapeDtypeStruct((s, d), BF16), jax.ShapeDtypeStruct((1, d), F32)],
        compiler_params=_params(1),
    )(x, g, tgt)


FF_BLK = D_FF // 2


def _swiglu_fwd(gu, *, name, tr=256):
    s = gu.shape[0]

    def body(g_ref, u_ref, a_ref):
        gv = g_ref[...]
        a_ref[...] = (gv * jax.nn.sigmoid(gv) * u_ref[...]).astype(BF16)

    return pl.pallas_call(
        body, name=name, grid=(s // tr, 2),
        in_specs=[pl.BlockSpec((tr, FF_BLK), lambda i, j: (i, j)),
                  pl.BlockSpec((tr, FF_BLK), lambda i, j: (i, j + 2))],
        out_specs=pl.BlockSpec((tr, FF_BLK), lambda i, j: (i, j)),
        out_shape=jax.ShapeDtypeStruct((s, D_FF), BF16),
        compiler_params=_params(2),
    )(gu, gu)


def _swiglu_bwd(da, gu, *, name, tr=256):
    s = gu.shape[0]

    def body(da_ref, g_ref, u_ref, o_ref):
        gv = g_ref[...]
        dav = da_ref[...]
        sg = jax.nn.sigmoid(gv)
        j = pl.program_id(1)

        @pl.when(j < 2)
        def _():
            o_ref[...] = (dav * u_ref[...] * (sg * (1.0 + gv * (1.0 - sg)))).astype(BF16)

        @pl.when(j >= 2)
        def _():
            o_ref[...] = (dav * gv * sg).astype(BF16)

    return pl.pallas_call(
        body, name=name, grid=(s // tr, 4),
        in_specs=[pl.BlockSpec((tr, FF_BLK), lambda i, j: (i, j % 2)),
                  pl.BlockSpec((tr, FF_BLK), lambda i, j: (i, j % 2)),
                  pl.BlockSpec((tr, FF_BLK), lambda i, j: (i, 2 + j % 2))],
        out_specs=pl.BlockSpec((tr, FF_BLK), lambda i, j: (i, j)),
        out_shape=jax.ShapeDtypeStruct((s, 2 * D_FF), BF16),
        compiler_params=_params(2),
    )(da, gu, gu)


def _split3(x):
    hi = x.astype(BF16)
    r1 = x - hi.astype(F32)
    mid = r1.astype(BF16)
    lo = (r1 - mid.astype(F32)).astype(BF16)
    return hi, mid, lo


def _dot3(x, m_bf):
    hi, mid, lo = _split3(x)
    return (jnp.dot(hi, m_bf, preferred_element_type=F32)
            + jnp.dot(mid, m_bf, preferred_element_type=F32)
            + jnp.dot(lo, m_bf, preferred_element_type=F32))


def _dot3_left(m_bf, x):
    hi, mid, lo = _split3(x)
    return (jnp.dot(m_bf, hi, preferred_element_type=F32)
            + jnp.dot(m_bf, mid, preferred_element_type=F32)
            + jnp.dot(m_bf, lo, preferred_element_type=F32))


def _dot2(x, m_bf):
    hi = x.astype(BF16)
    lo = (x - hi.astype(F32)).astype(BF16)
    return jnp.dot(hi, m_bf, preferred_element_type=F32) + jnp.dot(lo, m_bf, preferred_element_type=F32)


def _nt(a, b):
    return lax.dot_general(a, b, (((1,), (1,)), ((), ())), preferred_element_type=F32)


def _mm32(a, b):
    return jnp.dot(a, b, preferred_element_type=F32)


def _iota2(shape, dim):
    return lax.broadcasted_iota(jnp.int32, shape, dim)


def _rope_tables(s):
    half = ROT_DIM // 2
    pos = jnp.arange(s, dtype=F32)
    inv_freq = ROPE_THETA ** (-jnp.arange(half, dtype=F32) * 2.0 / ROT_DIM)
    ang = pos[:, None] * inv_freq[None, :]
    cos, sin = jnp.cos(ang), jnp.sin(ang)
    ones = jnp.ones((s, HEAD_DIM - ROT_DIM), F32)
    cos_t = jnp.concatenate([cos, cos, ones], axis=1)
    sin_t = jnp.concatenate([-sin, sin, 0.0 * ones], axis=1)
    idx = jnp.arange(HEAD_DIM)
    partner = jnp.where(idx < half, idx + half, idx - half)
    swap = ((idx[:, None] == partner[None, :]) & (idx[None, :] < ROT_DIM)).astype(F32)
    swap2 = jnp.kron(jnp.eye(2, dtype=F32), swap).astype(BF16)
    return jnp.tile(cos_t, (1, 2)), jnp.tile(sin_t, (1, 2)), swap2


def _rope(x, cos_t, sin_t, swap):
    return x * cos_t + _dot3(x, swap) * sin_t


def _rope_t(g, cos_t, sin_t, swap):
    return g * cos_t + _dot3(g * sin_t, swap)


def _dil_weight(dlt):
    nonneg = dlt >= 0
    w = jnp.zeros(dlt.shape, F32)
    for window, dil in DIL_PATTERNS:
        ok = nonneg & (dlt <= window) & ((dlt & (dil - 1)) == 0)
        w = w + ok.astype(F32)
    return w


FAR_TILES = 3
assert (FAR_TILES - 1) * TB + 1 > DIL_PATTERNS[1][0] and DIL_PATTERNS[2][0] >= 2048


def _dil_bias_scratch():
    return pltpu.VMEM((FAR_TILES + 1, TB, TB), F32)


def _dil_bias_tiles(bias_ref):
    rmc = _iota2((TB, TB), 0) - _iota2((TB, TB), 1)
    for d in range(FAR_TILES + 1):
        w = _dil_weight(d * TB + rmc)
        bias_ref[d] = jnp.where(w > 0.0, jnp.log(jnp.maximum(w, 1.0)), NEG)


def _log_sig_pair(z):
    sp = jnp.log(1.0 + jnp.exp(-jnp.abs(z)))
    return jnp.minimum(z, 0.0) - sp, -jnp.maximum(z, 0.0) - sp


def _log_one_minus_beta(z):
    return -(jnp.maximum(z, 0.0) + jnp.log(1.0 + jnp.exp(-jnp.abs(z))))


def _pair_masks(x, lane_lo):
    z = jnp.zeros_like(x)
    return jnp.where(lane_lo, x, z).astype(BF16), jnp.where(lane_lo, z, x).astype(BF16)


def _rows(i):
    return pl.ds(pl.multiple_of(i * TB, TB), TB)


def _head_spec(s, col0):
    return pl.BlockSpec((s, LANES), lambda p: (0, col0 + p))


def _stat_spec(s):
    return pl.BlockSpec((2, s, 1), lambda p: (p, 0, 0))


def _rowstat_spec(s):
    return pl.BlockSpec((2, 1, s), lambda p: (p, 0, 0))


def _full_spec(shape):
    nd = len(shape)
    return pl.BlockSpec(shape, lambda p: (0,) * nd)


K_COL, V_COL = D_ATTN // LANES, 2 * D_ATTN // LANES


def _bwd_scratch(s):
    return ([pltpu.VMEM((s, LANES), BF16)] * 8 + [pltpu.VMEM((LANES, s), BF16)] * 4
            + [pltpu.VMEM((LANES, s), F32)] * 2)


def _bwd_prep(i, q, k, v, dov, scr, lane_lo, sub_lo):
    qlo, qhi, klo, khi, kbf, vbf, dolo, dohi, qtlo, qthi, dotlo, dothi = scr[:12]
    rows = _rows(i)
    qs = q * SCALE
    qlo[rows, :], qhi[rows, :] = _pair_masks(qs, lane_lo)
    klo[rows, :], khi[rows, :] = _pair_masks(k * SCALE, lane_lo)
    kbf[rows, :] = k.astype(BF16)
    vbf[rows, :] = v.astype(BF16)
    dolo[rows, :], dohi[rows, :] = _pair_masks(dov, lane_lo)
    qtlo[:, rows], qthi[:, rows] = _pair_masks(qs.T, sub_lo)
    dotlo[:, rows], dothi[:, rows] = _pair_masks(dov.T, sub_lo)


def _sb_fwd(qkv, n_pairs, *, name, comm=None):
    s = qkv.shape[0]
    assert s % TB == 0
    nq = s // TB

    def body(q_ref, k_ref, v_ref, o_ref, ct_ref, qlo, qhi, kbf, vlo, vhi):
        lane_lo = _iota2((TB, LANES), 1) < HEAD_DIM

        def prep(i, _):
            rows = _rows(i)
            qlo[rows, :], qhi[rows, :] = _pair_masks(q_ref[rows, :] * SCALE, lane_lo)
            kbf[rows, :] = k_ref[rows, :].astype(BF16)
            vlo[rows, :], vhi[rows, :] = _pair_masks(v_ref[rows, :], lane_lo)
            return 0

        lax.fori_loop(0, nq, prep, 0)
        rmc = _iota2((TB, TB), 0) - _iota2((TB, TB), 1)
        strict = rmc > 0
        u_ge = (rmc >= 0).astype(BF16)
        qm, vm = (qlo, qhi), (vlo, vhi)

        def qloop(i, _):
            rows = _rows(i)

            def tile(kb, carry, diag):
                c, acc = list(carry[:2]), carry[2]
                keys = _rows(kb)
                k = kbf[keys, :]
                zs = [_nt(qm[h][rows, :], k) for h in range(2)]
                lms = [_log_one_minus_beta(z) for z in zs]
                if diag:
                    lms = [jnp.where(strict, lm, 0.0) for lm in lms]
                r_ins = [_dot2(lm, u_ge) for lm in lms]
                for h in range(2):
                    a = jnp.exp(zs[h] + r_ins[h] + c[h])
                    if diag:
                        a = jnp.where(strict, a, 0.0)
                    acc = acc + _mm32(a.astype(BF16), vm[h][keys, :])
                    c[h] = c[h] + r_ins[h][:, 0:1]
                return c[0], c[1], acc

            z1 = jnp.zeros((TB, 1), F32)
            carry = tile(i, (z1, z1, jnp.zeros((TB, LANES), F32)), True)
            c0, c1, acc = lax.fori_loop(0, i, lambda t, cr: tile(i - 1 - t, cr, False), carry)
            o_ref[rows, :] = acc
            ct_ref[0, rows, :] = c0
            ct_ref[1, rows, :] = c1
            return 0

        lax.fori_loop(0, nq, qloop, 0)

    return _call_pairs(
        body, name=name, n_pairs=n_pairs, comm=comm,
        in_specs=[_head_spec(s, 0), _head_spec(s, K_COL), _head_spec(s, V_COL)], args=(qkv, qkv, qkv),
        out_specs=[_head_spec(s, 0), _stat_spec(s)],
        out_shape=[jax.ShapeDtypeStruct((s, LANES * n_pairs), F32),
                   jax.ShapeDtypeStruct((2 * n_pairs, s, 1), F32)],
        scratch=[pltpu.VMEM((s, LANES), BF16)] * 5)


def _sb_bwd(qkv, do, ctot, n_pairs, do_col0, *, name, comm=None):
    s = qkv.shape[0]
    assert s % TB == 0
    nq = s // TB

    def body(q_ref, k_ref, v_ref, do_ref, ct_ref, dq_ref, dk_ref, dv_ref, *scr):
        qlo, qhi, klo, khi, kbf, vbf, dolo, dohi, qtlo, qthi, dotlo, dothi, dkt, dvt = scr
        lane_lo = _iota2((TB, LANES), 1) < HEAD_DIM
        sub_lo = _iota2((LANES, TB), 0) < HEAD_DIM

        def prep(i, _):
            rows = _rows(i)
            _bwd_prep(i, q_ref[rows, :], k_ref[rows, :], v_ref[rows, :], do_ref[rows, :], scr, lane_lo, sub_lo)
            return 0

        lax.fori_loop(0, nq, prep, 0)
        dkt[...] = jnp.zeros_like(dkt)
        dvt[...] = jnp.zeros_like(dvt)
        rmc = _iota2((TB, TB), 0) - _iota2((TB, TB), 1)
        strict = rmc > 0
        u_le = (rmc <= 0).astype(BF16)
        qm, km, dom, qtm, dotm = (qlo, qhi), (klo, khi), (dolo, dohi), (qtlo, qthi), (dotlo, dothi)

        def qloop(i, _):
            rows = _rows(i)
            ct = (ct_ref[0, rows, :], ct_ref[1, rows, :])

            def tile(kb, carry, diag):
                pre, hl, dq = list(carry[0:2]), list(carry[2:4]), carry[4]
                keys = _rows(kb)
                k, v = kbf[keys, :], vbf[keys, :]
                zs = [_nt(qm[h][rows, :], k) for h in range(2)]
                das = [_nt(dom[h][rows, :], v) for h in range(2)]
                lms = [_log_one_minus_beta(z) for z in zs]
                if diag:
                    lms = [jnp.where(strict, lm, 0.0) for lm in lms]
                pins = [_dot2(lm, u_le) for lm in lms]
                gs, lbs = [], []
                a_bf = []
                for h in range(2):
                    lb = zs[h] + lms[h]
                    a = jnp.exp(lb + (ct[h] - pre[h]) - pins[h])
                    if diag:
                        a = jnp.where(strict, a, 0.0)
                    gs.append(a * das[h])
                    lbs.append(lb)
                    a_bf.append(a.astype(BF16))
                hins = [_dot2(g, u_le) for g in gs]
                dk_t, dv_t = dkt[:, keys], dvt[:, keys]
                for h in range(2):
                    g = gs[h]
                    dz = g - jnp.exp(lbs[h]) * (hl[h] + hins[h])
                    if diag:
                        dz = jnp.where(strict, dz, 0.0)
                    dzb = dz.astype(BF16)
                    dq = dq + _mm32(dzb, km[h][keys, :])
                    dk_t = dk_t + _mm32(qtm[h][:, rows], dzb)
                    dv_t = dv_t + _mm32(dotm[h][:, rows], a_bf[h])
                    pre[h] = pre[h] + pins[h][:, TB - 1:TB]
                    hl[h] = hl[h] + hins[h][:, TB - 1:TB]
                dkt[:, keys] = dk_t
                dvt[:, keys] = dv_t
                return pre[0], pre[1], hl[0], hl[1], dq

            z1 = jnp.zeros((TB, 1), F32)
            carry = lax.fori_loop(0, i, lambda kb, cr: tile(kb, cr, False),
                                  (z1, z1, z1, z1, jnp.zeros((TB, LANES), F32)))
            dq = tile(i, carry, True)[4]
            dq_ref[rows, :] = dq.astype(BF16)
            return 0

        lax.fori_loop(0, nq, qloop, 0)

        def wloop(i, _):
            rows = _rows(i)
            dk_ref[rows, :] = dkt[:, rows].T.astype(BF16)
            dv_ref[rows, :] = dvt[:, rows].T.astype(BF16)
            return 0

        lax.fori_loop(0, nq, wloop, 0)

    out = jax.ShapeDtypeStruct((s, LANES * n_pairs), BF16)
    return _call_pairs(
        body, name=name, n_pairs=n_pairs, comm=comm,
        in_specs=[_head_spec(s, 0), _head_spec(s, K_COL), _head_spec(s, V_COL),
                  _head_spec(s, do_col0), _stat_spec(s)], args=(qkv, qkv, qkv, do, ctot),
        out_specs=[_head_spec(s, 0)] * 3, out_shape=[out, out, out], scratch=_bwd_scratch(s))


def _bias_fwd(mode, qkv, head0_col, n_pairs, extra, *, name, comm=None):
    s = qkv.shape[0]
    assert s % TB == 0 and s <= DIL_PATTERNS[2][0]
    nq = s // TB
    fox = mode == "fox"

    def body(q_ref, k_ref, v_ref, e0, e1, *rest):
        if fox:
            o_ref, lse_ref, qlo, qhi, kbf, vx0, vx1 = rest
        else:
            e2, o_ref, lse_ref, qlo, qhi, kbf, vx0, vx1, bias = rest
            _dil_bias_tiles(bias)
        lane_lo = _iota2((TB, LANES), 1) < HEAD_DIM

        def prep(i, _):
            rows = _rows(i)
            q, k, v = q_ref[rows, :], k_ref[rows, :], v_ref[rows, :]
            if not fox:
                c, sn, sw = e0[rows, :], e1[rows, :], e2[...]
                q, k = _rope(q, c, sn, sw), _rope(k, c, sn, sw)
            qlo[rows, :], qhi[rows, :] = _pair_masks(q * SCALE, lane_lo)
            kbf[rows, :] = k.astype(BF16)
            one = jnp.ones_like(v)
            vx0[rows, :] = jnp.where(lane_lo, v, one).astype(BF16)
            vx1[rows, :] = jnp.where(lane_lo, one, v).astype(BF16)
            return 0

        lax.fori_loop(0, nq, prep, 0)
        rmc = _iota2((TB, TB), 0) - _iota2((TB, TB), 1)
        qm, vx = (qlo, qhi), (vx0, vx1)

        def qloop(i, _):
            rows = _rows(i)
            if fox:
                fq = (e0[0, rows, :], e0[1, rows, :])

            def tile(kb, carry, diag):
                keys = _rows(kb)
                k = kbf[keys, :]
                scs = [_nt(qm[h][rows, :], k) for h in range(2)]
                if not fox:
                    b = bias[jnp.minimum(i - kb, FAR_TILES)]
                out = []
                for h in range(2):
                    m, acc = carry[2 * h], carry[2 * h + 1]
                    if fox:
                        sc = scs[h] + (fq[h] - e1[h, :, keys])
                        if diag:
                            sc = jnp.where(rmc >= 0, sc, NEG)
                    else:
                        sc = scs[h] + b
                    m_new = jnp.maximum(m, jnp.max(sc, axis=1, keepdims=True))
                    p = jnp.exp(sc - m_new)
                    acc = jnp.exp(m - m_new) * acc + _mm32(p.astype(BF16), vx[h][keys, :])
                    out += [m_new, acc]
                return tuple(out)

            m0 = jnp.full((TB, 1), NEG, F32)
            a0 = jnp.zeros((TB, LANES), F32)
            carry = lax.fori_loop(0, i, lambda kb, cr: tile(kb, cr, False), (m0, a0, m0, a0))
            carry = tile(i, carry, True)
            m_0, acc0, m_1, acc1 = carry
            l0, l1 = acc0[:, HEAD_DIM:HEAD_DIM + 1], acc1[:, 0:1]
            o_ref[rows, :] = jnp.where(lane_lo, acc0 / l0, acc1 / l1)
            lse_ref[0, rows, :] = m_0 + jnp.log(l0)
            lse_ref[1, rows, :] = m_1 + jnp.log(l1)
            return 0

        lax.fori_loop(0, nq, qloop, 0)

    hp0 = head0_col
    if fox:
        e_specs = [_stat_spec(s), _rowstat_spec(s)]
    else:
        e_specs = [_full_spec((s, LANES)), _full_spec((s, LANES)), _full_spec((LANES, LANES))]
    return _call_pairs(
        body, name=name, n_pairs=n_pairs, comm=comm,
        in_specs=[_head_spec(s, hp0), _head_spec(s, K_COL + hp0), _head_spec(s, V_COL + hp0)] + e_specs,
        args=(qkv, qkv, qkv, *extra),
        out_specs=[_head_spec(s, 0), _stat_spec(s)],
        out_shape=[jax.ShapeDtypeStruct((s, LANES * n_pairs), F32),
                   jax.ShapeDtypeStruct((2 * n_pairs, s, 1), F32)],
        scratch=[pltpu.VMEM((s, LANES), BF16)] * 5 + ([] if fox else [_dil_bias_scratch()]))


def _bias_bwd(mode, qkv, head0_col, n_pairs, extra, o, do, do_col0, lse, *, name, comm=None):
    s = qkv.shape[0]
    assert s % TB == 0 and s <= DIL_PATTERNS[2][0]
    nq = s // TB
    fox = mode == "fox"

    def body(q_ref, k_ref, v_ref, o_ref, do_ref, lse_ref, e0, e1, *rest):
        if fox:
            dq_ref, dk_ref, dv_ref, dfr_ref, dfc_ref = rest[:5]
            scr = rest[5:]
        else:
            e2, dq_ref, dk_ref, dv_ref = rest[:4]
            scr = rest[4:]
        qlo, qhi, klo, khi, kbf, vbf, dolo, dohi, qtlo, qthi, dotlo, dothi, dkt, dvt = scr[:14]
        if not fox:
            bias = scr[14]
            _dil_bias_tiles(bias)
        lane_lo = _iota2((TB, LANES), 1) < HEAD_DIM
        sub_lo = _iota2((LANES, TB), 0) < HEAD_DIM

        def prep(i, _):
            rows = _rows(i)
            q, k = q_ref[rows, :], k_ref[rows, :]
            if not fox:
                c, sn, sw = e0[rows, :], e1[rows, :], e2[...]
                q, k = _rope(q, c, sn, sw), _rope(k, c, sn, sw)
            _bwd_prep(i, q, k, v_ref[rows, :], do_ref[rows, :], scr, lane_lo, sub_lo)
            return 0

        lax.fori_loop(0, nq, prep, 0)
        dkt[...] = jnp.zeros_like(dkt)
        dvt[...] = jnp.zeros_like(dvt)
        if fox:
            dfr_ref[...] = jnp.zeros_like(dfr_ref)
        rmc = _iota2((TB, TB), 0) - _iota2((TB, TB), 1)
        qm, km, dom, qtm, dotm = (qlo, qhi), (klo, khi), (dolo, dohi), (qtlo, qthi), (dotlo, dothi)

        def qloop(i, _):
            rows = _rows(i)
            prod = do_ref[rows, :] * o_ref[rows, :]
            dsum = (jnp.sum(jnp.where(lane_lo, prod, 0.0), axis=1, keepdims=True),
                    jnp.sum(jnp.where(lane_lo, 0.0, prod), axis=1, keepdims=True))
            lse_i = (lse_ref[0, rows, :], lse_ref[1, rows, :])
            if fox:
                fql = (e0[0, rows, :] - lse_i[0], e0[1, rows, :] - lse_i[1])

            def tile(kb, carry, diag):
                dq, rs = carry[0], list(carry[1:])
                keys = _rows(kb)
                k, v = kbf[keys, :], vbf[keys, :]
                scs = [_nt(qm[h][rows, :], k) for h in range(2)]
                dps = [_nt(dom[h][rows, :], v) for h in range(2)]
                if not fox:
                    b = bias[jnp.minimum(i - kb, FAR_TILES)]
                ps, dss = [], []
                for h in range(2):
                    if fox:
                        sc = scs[h] + (fql[h] - e1[h, :, keys])
                        if diag:
                            sc = jnp.where(rmc >= 0, sc, NEG)
                    else:
                        sc = scs[h] + (b - lse_i[h])
                    p = jnp.exp(sc)
                    dss.append(p * (dps[h] - dsum[h]))
                    ps.append(p.astype(BF16))
                dk_t, dv_t = dkt[:, keys], dvt[:, keys]
                for h in range(2):
                    dsb = dss[h].astype(BF16)
                    dq = dq + _mm32(dsb, km[h][keys, :])
                    dk_t = dk_t + _mm32(qtm[h][:, rows], dsb)
                    dv_t = dv_t + _mm32(dotm[h][:, rows], ps[h])
                    if fox:
                        dfr_ref[h, :, keys] -= jnp.sum(dss[h], axis=0, keepdims=True)
                        for j in range(TB // LANES):
                            rs[h] = rs[h] + dss[h][:, j * LANES:(j + 1) * LANES]
                dkt[:, keys] = dk_t
                dvt[:, keys] = dv_t
                return (dq, *rs)

            z2 = jnp.zeros((TB, LANES), F32)
            carry = lax.fori_loop(0, i, lambda kb, cr: tile(kb, cr, False), (z2, z2, z2) if fox else (z2,))
            carry = tile(i, carry, True)
            if fox:
                dfc_ref[0, rows, :] = jnp.sum(carry[1], axis=1, keepdims=True)
                dfc_ref[1, rows, :] = jnp.sum(carry[2], axis=1, keepdims=True)
            dq = carry[0]
            if not fox:
                dq = _rope_t(dq, e0[rows, :], e1[rows, :], e2[...])
            dq_ref[rows, :] = dq.astype(BF16)
            return 0

        lax.fori_loop(0, nq, qloop, 0)

        def wloop(i, _):
            rows = _rows(i)
            dk = dkt[:, rows].T
            if not fox:
                dk = _rope_t(dk, e0[rows, :], e1[rows, :], e2[...])
            dk_ref[rows, :] = dk.astype(BF16)
            dv_ref[rows, :] = dvt[:, rows].T.astype(BF16)
            return 0

        lax.fori_loop(0, nq, wloop, 0)

    hp0 = head0_col
    out = jax.ShapeDtypeStruct((s, LANES * n_pairs), BF16)
    out_specs = [_head_spec(s, 0)] * 3
    out_shape = [out, out, out]
    if fox:
        e_specs = [_stat_spec(s), _rowstat_spec(s)]
        out_specs += [_rowstat_spec(s), _stat_spec(s)]
        out_shape += [jax.ShapeDtypeStruct((2 * n_pairs, 1, s), F32), jax.ShapeDtypeStruct((2 * n_pairs, s, 1), F32)]
    else:
        e_specs = [_full_spec((s, LANES)), _full_spec((s, LANES)), _full_spec((LANES, LANES))]
    return _call_pairs(
        body, name=name, n_pairs=n_pairs, comm=comm,
        in_specs=[_head_spec(s, hp0), _head_spec(s, K_COL + hp0), _head_spec(s, V_COL + hp0),
                  _head_spec(s, 0), _head_spec(s, do_col0), _stat_spec(s)] + e_specs,
        args=(qkv, qkv, qkv, o, do, lse, *extra),
        out_specs=out_specs, out_shape=out_shape,
        scratch=_bwd_scratch(s) + ([] if fox else [_dil_bias_scratch()]))


F_COL = 3 * D_ATTN // LANES


def _fgate_fwd(qkvf, brow, *, name):
    s = qkvf.shape[0]
    nb = s // BLK

    def body(f_ref, b_ref, fc_ref, fr_ref, fs):
        row, col = _iota2((BLK, BLK), 0), _iota2((BLK, BLK), 1)
        l_incl = (col <= row).astype(BF16)

        def step(i, carry):
            r0 = pl.multiple_of(i * BLK, BLK)
            lf, _ = _log_sig_pair(f_ref[pl.ds(r0, BLK), :] + b_ref[...])
            fblk = carry + _dot3_left(l_incl, lf)
            fs[pl.ds(r0, BLK), :] = fblk
            return fblk[BLK - 1:BLK, :]

        lax.fori_loop(0, nb, step, jnp.zeros((1, LANES), F32))
        ft = fs[...].T
        for h in range(N_HEADS):
            fc_ref[h, :, :] = fs[:, h:h + 1]
            fr_ref[h, :, :] = ft[h:h + 1, :]

    return pl.pallas_call(
        body, name=name, grid=(1,),
        in_specs=[pl.BlockSpec((s, LANES), lambda i: (0, F_COL)), pl.BlockSpec((1, LANES), lambda i: (0, 0))],
        out_specs=[pl.BlockSpec((N_HEADS, s, 1), lambda i: (0, 0, 0)),
                   pl.BlockSpec((N_HEADS, 1, s), lambda i: (0, 0, 0))],
        out_shape=[jax.ShapeDtypeStruct((N_HEADS, s, 1), F32), jax.ShapeDtypeStruct((N_HEADS, 1, s), F32)],
        scratch_shapes=[pltpu.VMEM((s, LANES), F32)],
        compiler_params=_params(1),
    )(qkvf, brow)


def _fgate_bwd(dfr, dfc, qkvf, brow, *, name):
    s = qkvf.shape[0]
    nb = s // BLK

    def body(dfr_ref, dfc_ref, f_ref, b_ref, dfl_ref, db_ref, ts, fs):
        ts[...] = jnp.zeros_like(ts)
        for h in range(N_HEADS):
            ts[h:h + 1, :] = dfr_ref[h]
        fs[...] = ts[...].T
        for h in range(N_HEADS):
            fs[:, h:h + 1] += dfc_ref[h]
        row, col = _iota2((BLK, BLK), 0), _iota2((BLK, BLK), 1)
        u_incl = (col >= row).astype(BF16)
        head_lane = _iota2((BLK, LANES), 1) < N_HEADS

        def step(ii, carry):
            tail, db = carry
            r0 = pl.multiple_of((nb - 1 - ii) * BLK, BLK)
            rblk = tail + _dot3_left(u_incl, fs[pl.ds(r0, BLK), :])
            _, lsn = _log_sig_pair(f_ref[pl.ds(r0, BLK), :] + b_ref[...])
            dfl = jnp.where(head_lane, rblk * jnp.exp(lsn), 0.0)
            dfl_ref[pl.ds(r0, BLK), :] = dfl.astype(BF16)
            return rblk[0:1, :], db + jnp.sum(dfl, axis=0, keepdims=True)

        z = jnp.zeros((1, LANES), F32)
        _, db = lax.fori_loop(0, nb, step, (z, z))
        db_ref[...] = db

    return pl.pallas_call(
        body, name=name, grid=(1,),
        in_specs=[pl.BlockSpec((N_HEADS, 1, s), lambda i: (0, 0, 0)), pl.BlockSpec((N_HEADS, s, 1), lambda i: (0, 0, 0)),
                  pl.BlockSpec((s, LANES), lambda i: (0, F_COL)), pl.BlockSpec((1, LANES), lambda i: (0, 0))],
        out_specs=[pl.BlockSpec((s, LANES), lambda i: (0, 0)), pl.BlockSpec((1, LANES), lambda i: (0, 0))],
        out_shape=[jax.ShapeDtypeStruct((s, LANES), BF16), jax.ShapeDtypeStruct((1, LANES), F32)],
        scratch_shapes=[pltpu.VMEM((LANES, s), F32), pltpu.VMEM((s, LANES), F32)],
        compiler_params=_params(1),
    )(dfr, dfc, qkvf, brow)


def _adamw_math(w, g, m, v):
    m2 = ADAM_B1 * m + (1.0 - ADAM_B1) * g
    v2 = ADAM_B2 * v + (1.0 - ADAM_B2) * (g * g)
    m_hat = m2 / (1.0 - ADAM_B1 ** ADAM_STEP)
    v_hat = v2 / (1.0 - ADAM_B2 ** ADAM_STEP)
    delta = -ADAM_LR * (m_hat / (jnp.sqrt(v_hat) + ADAM_EPS) + ADAM_WD * w)
    return delta, m2, v2


def _row_tile(r, cap=256, mult=16):
    best = None
    for t in range(mult, min(r, cap) + 1, mult):
        if r % t == 0:
            best = t
    assert best is not None, r
    return best


def _adamw_shard(w, m, v, lidx, g_all, r1, r2, sc, prev, *, name):
    nl, r, c = w.shape
    tr = _row_tile(r)

    def body(sc_ref, w_ref, m_ref, v_ref, g_ref, r1_ref, r2_ref, *rest):
        go_ref, d_ref, mo_ref, vo_ref = rest[-4:]
        g = g_ref[...] + r1_ref[...]
        g = g + r2_ref[0].astype(F32)
        g = g + r2_ref[1].astype(F32)
        g = g + r2_ref[2].astype(F32)
        delta, m2, v2 = _adamw_math(w_ref[...], g, m_ref[...], v_ref[...])
        go_ref[...] = g
        d_ref[...] = delta
        mo_ref[...] = m2
        vo_ref[...] = v2

    lay = pl.BlockSpec((None, tr, c), lambda i, s_: (lidx, i, 0))
    in_specs = [lay, lay, lay,
                pl.BlockSpec((None, tr, c), lambda i, s_: (s_[0], i, 0)),
                pl.BlockSpec((None, tr, c), lambda i, s_: (s_[1], i, 0)),
                pl.BlockSpec((3, tr, c), lambda i, s_: (0, i, 0))]
    args = [sc, w, m, v, g_all, r1, r2]
    aliases = {}
    if prev is not None:
        in_specs += [pl.BlockSpec(memory_space=pl.ANY)] * 4
        aliases = {7 + t: t for t in range(4)}
        args += list(prev)
    shp = jax.ShapeDtypeStruct((nl, r, c), F32)
    return pl.pallas_call(
        body, name=name,
        grid_spec=pltpu.PrefetchScalarGridSpec(
            num_scalar_prefetch=1, grid=(r // tr,), in_specs=in_specs, out_specs=[lay] * 4),
        out_shape=[shp] * 4, input_output_aliases=aliases,
        compiler_params=_params(1),
    )(*args)


def _adamw_small(w, g, m, v, *, name):
    def body(w_ref, g_ref, m_ref, v_ref, d_ref, mo_ref, vo_ref):
        delta, m2, v2 = _adamw_math(w_ref[...], g_ref[...], m_ref[...], v_ref[...])
        d_ref[...] = delta
        mo_ref[...] = m2
        vo_ref[...] = v2

    shp = jax.ShapeDtypeStruct(w.shape, F32)
    return pl.pallas_call(body, name=name, out_shape=[shp] * 3, compiler_params=_params())(w, g, m, v)


def _pos():
    return lax.axis_index("x"), lax.axis_index("y"), lax.axis_index("c")


def _other_chips(x, y):
    return [(1 - x, y), (x, 1 - y), (1 - x, 1 - y)]


def _dev_index(x, y, c):
    return 4 * x + 2 * y + c


HBM_SPEC = pl.BlockSpec(memory_space=pltpu.HBM)


class _Comm:
    def __init__(self, inputs, out_shape, scratch, start, mid, finish):
        self.inputs, self.out_shape, self.scratch = list(inputs), list(out_shape), list(scratch)
        self.start, self.mid, self.finish = start, mid, finish

    def run(self, name):
        n_in, n_out = len(self.inputs), len(self.out_shape)

        def body(*refs):
            parts = refs[:n_in], refs[n_in:n_in + n_out], refs[n_in + n_out:]
            self.start(*parts)
            self.mid(*parts)
            self.finish(*parts)

        return pl.pallas_call(
            body, name=name, in_specs=[HBM_SPEC] * n_in, out_specs=[HBM_SPEC] * n_out,
            out_shape=self.out_shape, scratch_shapes=self.scratch)(*self.inputs)


def _call_pairs(body, *, name, n_pairs, in_specs, args, out_specs, out_shape, scratch, comm=None):
    if comm is None:
        res = pl.pallas_call(
            body, name=name, grid=(n_pairs,), in_specs=in_specs, out_specs=out_specs, out_shape=out_shape,
            scratch_shapes=scratch, compiler_params=_params(1))(*args)
        return list(res), []
    sizes = (len(in_specs), len(comm.inputs), len(out_specs), len(comm.out_shape), len(scratch), len(comm.scratch))

    def fused(*refs):
        parts, o = [], 0
        for n in sizes:
            parts.append(refs[o:o + n])
            o += n
        h_in, c_in, h_out, c_out, h_scr, c_scr = parts
        p = pl.program_id(0)

        @pl.when(p == 0)
        def _():
            comm.start(c_in, c_out, c_scr)

        @pl.when(p == n_pairs - 1)
        def _():
            comm.mid(c_in, c_out, c_scr)

        body(*h_in, *h_out, *h_scr)

        @pl.when(p == n_pairs - 1)
        def _():
            comm.finish(c_in, c_out, c_scr)

    res = pl.pallas_call(
        fused, name=name, grid=(n_pairs,),
        in_specs=list(in_specs) + [HBM_SPEC] * sizes[1], out_specs=list(out_specs) + [HBM_SPEC] * sizes[3],
        out_shape=list(out_shape) + comm.out_shape, scratch_shapes=list(scratch) + comm.scratch,
        compiler_params=_params(1))(*args, *comm.inputs)
    return list(res[:sizes[2]]), list(res[sizes[2]:])


def _gather_comm(shards):
    n = len(shards)

    def plan(xs, outs, sems):
        send, recv, loc = sems
        x, y, c = _pos()
        me, sib = (x, y, c), (x, y, 1 - c)
        chips = _other_chips(x, y)

        def copy(a, k, block, to, src=None):
            dst = outs[a].at[_dev_index(*block)]
            return pltpu.make_async_remote_copy(
                src_ref=dst if src is None else src, dst_ref=dst,
                send_sem=send.at[a, k], recv_sem=recv.at[a, k], device_id=to, device_id_type=MESH)

        mine = [pltpu.make_async_copy(xs[a], outs[a].at[_dev_index(*me)], loc.at[a]) for a in range(n)]
        first = []
        for a in range(n):
            first.append(copy(a, 0, me, sib, src=xs[a]))
            first += [copy(a, 1 + j, me, (*chip, c), src=xs[a]) for j, chip in enumerate(chips)]
        passed = [(copy(a, 1 + j, (*chip, c), me), copy(a, 4 + j, (*chip, c), sib))
                  for j, chip in enumerate(chips) for a in range(n)]
        from_sib = [copy(a, 0, sib, me) for a in range(n)]
        from_sib += [copy(a, 4 + j, (*chip, 1 - c), me) for a in range(n) for j, chip in enumerate(chips)]
        return mine, first, passed, from_sib

    def start(xs, outs, sems):
        mine, first, _, _ = plan(xs, outs, sems)
        for cp in mine + first:
            cp.start()

    def mid(xs, outs, sems):
        for arrival, fwd in plan(xs, outs, sems)[2]:
            arrival.wait_recv()
            fwd.start()

    def finish(xs, outs, sems):
        mine, first, passed, from_sib = plan(xs, outs, sems)
        for cp in from_sib:
            cp.wait_recv()
        for cp in first + [fwd for _, fwd in passed]:
            cp.wait_send()
        for cp in mine:
            cp.wait()

    return _Comm(shards, [jax.ShapeDtypeStruct((N_DEV,) + a.shape, a.dtype) for a in shards],
                 [pltpu.SemaphoreType.DMA((n, 7)), pltpu.SemaphoreType.DMA((n, 7)), pltpu.SemaphoreType.DMA((n,))],
                 start, mid, finish)


def _rs_sibling(gs, *, name):
    n = len(gs)

    def body(*refs):
        g_refs, r_refs = refs[:n], refs[n:2 * n]
        send, recv = refs[2 * n:]
        x, y, c = _pos()
        copies = []
        for a in range(n):
            for k in range(4):
                copies.append(pltpu.make_async_remote_copy(
                    src_ref=g_refs[a].at[_dev_index(k // 2, k % 2, 1 - c)], dst_ref=r_refs[a].at[k],
                    send_sem=send.at[a, k], recv_sem=recv.at[a, k],
                    device_id=(x, y, 1 - c), device_id_type=MESH))
        for cp in copies:
            cp.start()
        for cp in copies:
            cp.wait()

    return pl.pallas_call(
        body, name=name,
        in_specs=[HBM_SPEC] * n, out_specs=[HBM_SPEC] * n,
        out_shape=[jax.ShapeDtypeStruct((4,) + g.shape[1:], g.dtype) for g in gs],
        scratch_shapes=[pltpu.SemaphoreType.DMA((n, 4)), pltpu.SemaphoreType.DMA((n, 4))],
    )(*gs)


def _rs_partial(g_all, r1, sc, *, name):
    _, r, c = g_all.shape
    tr = _row_tile(r)

    def body(sc_ref, g_ref, r_ref, o_ref):
        o_ref[...] = (g_ref[...] + r_ref[...]).astype(BF16)

    return pl.pallas_call(
        body, name=name,
        grid_spec=pltpu.PrefetchScalarGridSpec(
            num_scalar_prefetch=1, grid=(3, r // tr),
            in_specs=[pl.BlockSpec((None, tr, c), lambda j, i, s_: (s_[2 + j], i, 0)),
                      pl.BlockSpec((None, tr, c), lambda j, i, s_: (s_[5 + j], i, 0))],
            out_specs=pl.BlockSpec((None, tr, c), lambda j, i, s_: (j, i, 0))),
        out_shape=jax.ShapeDtypeStruct((3, r, c), BF16),
        compiler_params=_params(2),
    )(sc, g_all, r1)


def _cross_comm(ps):
    n = len(ps)

    def plan(p_refs, r_refs, sems):
        send, recv = sems
        x, y, c = _pos()
        return [pltpu.make_async_remote_copy(
            src_ref=p_refs[a].at[j], dst_ref=r_refs[a].at[j], send_sem=send.at[a, j], recv_sem=recv.at[a, j],
            device_id=(*chip, c), device_id_type=MESH)
            for j, chip in enumerate(_other_chips(x, y)) for a in range(n)]

    def start(*parts):
        for cp in plan(*parts):
            cp.start()

    def mid(*parts):
        pass

    def finish(*parts):
        for cp in plan(*parts):
            cp.wait()

    return _Comm(ps, [jax.ShapeDtypeStruct(p.shape, p.dtype) for p in ps],
                 [pltpu.SemaphoreType.DMA((n, 3)), pltpu.SemaphoreType.DMA((n, 3))], start, mid, finish)


SMALL_ROWS = 16


def _all_reduce_small(pack, *, name):
    def body(x_ref, o_ref, buf, send, recv):
        x, y, c = _pos()
        me = _dev_index(x, y, c)
        buf[me] = x_ref[...]
        copies = []
        for k in range(1, N_DEV):
            fx, fy, fc = (k >> 2) & 1, (k >> 1) & 1, k & 1
            peer = (1 - x if fx else x, 1 - y if fy else y, 1 - c if fc else c)
            copies.append(pltpu.make_async_remote_copy(
                src_ref=x_ref, dst_ref=buf.at[me], send_sem=send.at[k - 1], recv_sem=recv.at[k - 1],
                device_id=peer, device_id_type=MESH))
        for cp in copies:
            cp.start()
        for cp in copies:
            cp.wait()
        acc = buf[0]
        for d in range(1, N_DEV):
            acc = acc + buf[d]
        o_ref[...] = acc

    return pl.pallas_call(
        body, name=name,
        in_specs=[pl.BlockSpec(memory_space=pltpu.VMEM)], out_specs=pl.BlockSpec(memory_space=pltpu.VMEM),
        out_shape=jax.ShapeDtypeStruct(pack.shape, F32),
        scratch_shapes=[pltpu.VMEM((N_DEV,) + pack.shape, F32),
                        pltpu.SemaphoreType.DMA((N_DEV - 1,)), pltpu.SemaphoreType.DMA((N_DEV - 1,))],
    )(pack)


def _unshard_cols(g):
    return jnp.transpose(g, (1, 0, 2)).reshape(g.shape[1], N_DEV * g.shape[2])


def _shard_cols(w):
    k, n8 = w.shape
    return jnp.transpose(w.reshape(k, N_DEV, n8 // N_DEV), (1, 0, 2))


def _pad_row(v, width=D_MODEL):
    v = v.reshape(1, -1)
    return jnp.pad(v, ((0, 0), (0, width - v.shape[1])))


def _forward_mixer(l, xc, g_mix, wq, wo, rope, brow, comm_a=None, comm_b=None):
    even = l % 2 == 0
    h1 = _rms_fwd(xc, g_mix, name=f"norm_mix_fwd{l}")
    qkv = _mm(h1, wq, name=f"qkv_fwd{l}", tm=1024, tn=768 if even else 640)
    if even:
        (o_a, st_a), got_a = _sb_fwd(qkv, N_HEADS // 4, name=f"sb_fwd{l}", comm=comm_a)
        (o_b, st_b), got_b = _bias_fwd("dil", qkv, N_HEADS // 4, N_HEADS // 4, rope, name=f"dil_fwd{l}",
                                       comm=comm_b)
        o = jnp.concatenate([o_a, o_b], axis=1)
        att = (o_b, st_a, st_b)
    else:
        assert comm_b is None
        fcol, frow = _fgate_fwd(qkv, brow, name=f"fgate_fwd{l}")
        (o, lse), got_a = _bias_fwd("fox", qkv, 0, N_HEADS // 2, (fcol, frow), name=f"fox_fwd{l}", comm=comm_a)
        got_b = []
        att = (o, lse, fcol, frow)
    o_bf = o.astype(BF16)
    xm = _mm(o_bf, wo, add=xc, name=f"wo_fwd{l}", tm=512, tn=1024)
    return xm, (xc, h1, qkv, att, o_bf), got_a, got_b


def _forward_ffn(l, xm, g_ffn, win, wout):
    h2 = _rms_fwd(xm, g_ffn, name=f"norm_ffn_fwd{l}")
    gu = _mm(h2, win, name=f"ffn_in_fwd{l}", tm=1024, tn=512)
    a = _swiglu_fwd(gu, name=f"swiglu_fwd{l}")
    xo = _mm(a, wout, add=xm, name=f"ffn_out_fwd{l}", tm=512, tn=1024)
    return xo, (xm, h2, gu, a)


def _backward_ffn(l, dx, dxb, saved, g_ffn, w):
    _, _, win, wout = w
    _, _, _, _, _, xm, h2, gu, a = saved
    da = _mm(dxb, wout, tb=True, name=f"ffn_out_dx{l}", tm=1024, tn=FF_BLK)
    d_wout = _mm(a, dxb, ta=True, name=f"ffn_out_dw{l}", tm=FF_BLK, tn=512)
    dgu = _swiglu_bwd(da, gu, name=f"swiglu_bwd{l}")
    d_win = _mm(h2, dgu, ta=True, name=f"ffn_in_dw{l}", tm=1024, tn=512)
    dh2 = _mm(dgu, win, tb=True, name=f"ffn_in_dx{l}", tm=512, tn=1024, tk=FF_BLK)
    dxm, dxmb, dg_ffn = _rms_bwd(xm, g_ffn, dh2, dx, name=f"norm_ffn_bwd{l}")
    return dxm, dxmb, dg_ffn, d_win, d_wout


def _backward_attn(l, dxm, dxmb, saved, g_mix, w, rope, brow, comm_a=None, comm_b=None):
    wq, wo, _, _ = w
    xin, h1, qkv, att, o_bf, _, _, _, _ = saved
    even = l % 2 == 0
    d_wo = _mm(o_bf, dxmb, ta=True, name=f"wo_dw{l}", tm=512, tn=1024)
    do = _mm(dxmb, wo, tb=True, name=f"wo_dx{l}", tm=1024, tn=1024)
    db = None
    if even:
        o_b, st_a, st_b = att
        (dqa, dka, dva), got_a = _sb_bwd(qkv, do, st_a, N_HEADS // 4, 0, name=f"sb_bwd{l}", comm=comm_a)
        (dqb, dkb, dvb), got_b = _bias_bwd("dil", qkv, N_HEADS // 4, N_HEADS // 4, rope, o_b, do,
                                           N_HEADS // 4, st_b, name=f"dil_bwd{l}", comm=comm_b)
        dqkv = jnp.concatenate([dqa, dqb, dka, dkb, dva, dvb], axis=1)
    else:
        assert comm_b is None
        o, lse, fcol, frow = att
        (dq, dk, dv, dfr, dfc), got_a = _bias_bwd("fox", qkv, 0, N_HEADS // 2, (fcol, frow), o, do, 0, lse,
                                                  name=f"fox_bwd{l}", comm=comm_a)
        got_b = []
        dfl, db = _fgate_bwd(dfr, dfc, qkv, brow, name=f"fgate_bwd{l}")
        dqkv = jnp.concatenate([dq, dk, dv, dfl], axis=1)
    d_wq = _mm(h1, dqkv, ta=True, name=f"qkv_dw{l}", tm=1024, tn=768 if even else 640)
    dh1 = _mm(dqkv, wq, tb=True, name=f"qkv_dx{l}", tm=512, tn=1024, tk=1024 if even else 640)
    dx, dxb, dg_mix = _rms_bwd(xin, g_mix, dh1, dxm, name=f"norm_mix_bwd{l}")
    return dx, dxb, dg_mix, d_wq, d_wo, db, got_a, got_b


def kernel(x, norm_mix, w_qkv_even, w_o_even, w_qkvf_odd, b_forget, w_o_odd, norm_ffn, w_ffn_in, w_ffn_out, norm_final, loss_target, m_norm_mix, m_w_qkv_even, m_w_o_even, m_w_qkvf_odd, m_b_forget, m_w_o_odd, m_norm_ffn, m_w_ffn_in, m_w_ffn_out, m_norm_final, v_norm_mix, v_w_qkv_even, v_w_o_even, v_w_qkvf_odd, v_b_forget, v_w_o_odd, v_norm_ffn, v_w_ffn_in, v_w_ffn_out, v_norm_final):
    xi, yi, ci = _pos()
    others = _other_chips(xi, yi)
    sc = jnp.stack([_dev_index(xi, yi, ci), 2 * xi + yi]
                   + [_dev_index(px, py, ci) for px, py in others]
                   + [2 * px + py for px, py in others]).astype(jnp.int32)
    n_odd_cols = w_qkvf_odd.shape[2] * N_DEV

    xs, tgt = x[0], loss_target[0]
    rope = _rope_tables(xs.shape[0])
    brow = [_pad_row(b_forget[i], LANES) for i in range(DEPTH // 2)]

    def shards(l):
        even = l % 2 == 0
        wq_s = (w_qkv_even if even else w_qkvf_odd)[l // 2]
        wo_s = (w_o_even if even else w_o_odd)[l // 2]
        return [wq_s.astype(BF16), wo_s.astype(BF16)], [w_ffn_in[l].astype(BF16), w_ffn_out[l].astype(BF16)]

    def full_mix(l, gq, go):
        wq = _unshard_cols(gq)
        if l % 2 == 1:
            wq = jnp.pad(wq, ((0, 0), (0, QKVF_PAD - n_odd_cols)))
        return wq, go.reshape(D_ATTN, D_MODEL)

    def full_ffn(gi, gout):
        return _unshard_cols(gi), gout.reshape(D_FF, D_MODEL)

    mix0, ffn0 = shards(0)
    w_mix = {0: full_mix(0, *_gather_comm(mix0).run("gather_weights0"))}
    w_ffn = {}
    weights, saved = [], []
    xc = xs
    for l in range(DEPTH):
        comm_a = comm_b = None
        if l + 1 < DEPTH:
            mix_n, ffn_n = shards(l + 1)
            if l == 0:
                comm_a, comm_b = _gather_comm(ffn0), _gather_comm(mix_n + ffn_n)
            elif l % 2 == 0:
                comm_a, comm_b = _gather_comm(ffn_n), _gather_comm(mix_n)
            else:
                comm_a = _gather_comm(mix_n + ffn_n)
        xm, sv_mix, got_a, got_b = _forward_mixer(l, xc, norm_mix[l:l + 1], *w_mix[l], rope, brow[l // 2],
                                                  comm_a, comm_b)
        if l + 1 < DEPTH:
            if l == 0:
                w_ffn[0] = full_ffn(*got_a)
                w_mix[1], w_ffn[1] = full_mix(1, *got_b[:2]), full_ffn(*got_b[2:])
            elif l % 2 == 0:
                w_mix[l + 1], w_ffn[l + 1] = full_mix(l + 1, *got_b), full_ffn(*got_a)
            else:
                w_mix[l + 1], w_ffn[l + 1] = full_mix(l + 1, *got_a[:2]), full_ffn(*got_a[2:])
        xc, sv_ffn = _forward_ffn(l, xm, norm_ffn[l:l + 1], *w_ffn[l])
        weights.append(w_mix[l] + w_ffn[l])
        saved.append(sv_mix + sv_ffn)

    loss_row, dx, dxb, dg_final = _final_loss(xc, norm_final.reshape(1, -1), tgt, name="final_loss")

    sharded = {
        "qkv_even": (w_qkv_even, m_w_qkv_even, v_w_qkv_even), "o_even": (w_o_even, m_w_o_even, v_w_o_even),
        "qkvf_odd": (w_qkvf_odd, m_w_qkvf_odd, v_w_qkvf_odd), "o_odd": (w_o_odd, m_w_o_odd, v_w_o_odd),
        "ffn_in": (w_ffn_in, m_w_ffn_in, v_w_ffn_in), "ffn_out": (w_ffn_out, m_w_ffn_out, v_w_ffn_out),
    }
    results = {k: None for k in sharded}

    def chip_sums(gs, keys, tag):
        r1s = _rs_sibling(gs, name=f"grads_to_sibling_{tag}")
        ps = [_rs_partial(g, r1, sc, name=f"grads_chip_sum_{tag}_{a}") for a, (g, r1) in enumerate(zip(gs, r1s))]
        return gs, r1s, ps, keys

    def update(group, r2s, tag):
        gs, r1s, _, keys = group
        for a, (key, lidx) in enumerate(keys):
            w, m, v = sharded[key]
            results[key] = _adamw_shard(w, m, v, lidx, gs[a], r1s[a], r2s[a], sc, results[key],
                                        name=f"adamw_{key}_{tag}")

    dg_mix, dg_ffn, db_f = [None] * DEPTH, [None] * DEPTH, [None] * (DEPTH // 2)
    pending = None
    for l in reversed(range(DEPTH)):
        even = l % 2 == 0
        dxm, dxmb, dg_ffn[l], d_win, d_wout = _backward_ffn(l, dx, dxb, saved[l], norm_ffn[l:l + 1], weights[l])
        ffn = chip_sums([_shard_cols(d_win), d_wout.reshape(N_DEV, D_FF // N_DEV, D_MODEL)],
                        [("ffn_in", l), ("ffn_out", l)], f"ffn{l}")
        if even:
            comm_a = _cross_comm(ffn[2])
            comm_b = _cross_comm(pending[2]) if pending is not None else None
        else:
            comm_a = _cross_comm(ffn[2] + (pending[2] if pending is not None else []))
            comm_b = None
        dx, dxb, dg_mix[l], d_wq, d_wo, db, got_a, got_b = _backward_attn(
            l, dxm, dxmb, saved[l], norm_mix[l:l + 1], weights[l], rope, brow[l // 2], comm_a, comm_b)
        update(ffn, got_a[:2], f"ffn{l}")
        if pending is not None:
            update(pending, got_b if even else got_a[2:], f"mix{l + 1}")
        if not even:
            db_f[l // 2] = db
            d_wq = d_wq[:, :n_odd_cols]
        pending = chip_sums([_shard_cols(d_wq), d_wo.reshape(N_DEV, D_ATTN // N_DEV, D_MODEL)],
                            [("qkv_even" if even else "qkvf_odd", l // 2), ("o_even" if even else "o_odd", l // 2)],
                            f"mix{l}")
    update(pending, _cross_comm(pending[2]).run("grads_to_chips_mix0"), "mix0")

    zeros = jnp.zeros((SMALL_ROWS - 11, D_MODEL), F32)
    db_row = _pad_row(jnp.concatenate([d[:, :N_HEADS] for d in db_f], axis=1))
    pack_g = jnp.concatenate(dg_mix + dg_ffn + [dg_final, db_row, _pad_row(loss_row[:, :1]), zeros], axis=0)
    tot = _all_reduce_small(pack_g, name="small_all_reduce")

    def pack(nm, nf, nfin, bf):
        return jnp.concatenate([nm, nf, nfin.reshape(1, -1), _pad_row(bf),
                                jnp.zeros((SMALL_ROWS - 10, D_MODEL), F32)], axis=0)

    d_s, m_s, v_s = _adamw_small(
        pack(norm_mix, norm_ffn, norm_final, b_forget), tot,
        pack(m_norm_mix, m_norm_ffn, m_norm_final, m_b_forget),
        pack(v_norm_mix, v_norm_ffn, v_norm_final, v_b_forget), name="adamw_small")

    def unpack(p):
        nb = b_forget.size
        return {"norm_mix": p[0:DEPTH], "norm_ffn": p[DEPTH:2 * DEPTH], "norm_final": p[2 * DEPTH],
                "b_forget": p[2 * DEPTH + 1, :nb].reshape(b_forget.shape)}

    small = [unpack(tot), unpack(d_s), unpack(m_s), unpack(v_s)]
    loss = tot[2 * DEPTH + 2, 0]

    order = ["norm_mix", "qkv_even", "o_even", "qkvf_odd", "b_forget", "o_odd", "norm_ffn", "ffn_in", "ffn_out",
             "norm_final"]
    outs = [loss, dx[None]]
    for t in range(4):
        for key in order:
            outs.append(small[t][key] if key in small[t] else results[key][t])
    return tuple(outs)
```

```python
import jax
import jax.numpy as jnp
from jax import lax
from jax.experimental import pallas as pl
from jax.experimental.pallas import tpu as pltpu

F32 = jnp.float32
BF16 = jnp.bfloat16

D_MODEL = 1024
HEAD_DIM = 64
N_HEADS = 16
D_ATTN = N_HEADS * HEAD_DIM
D_FF = 2816
DEPTH = 4
ROPE_THETA = 500000.0
ROT_DIM = HEAD_DIM // 4
RMS_EPS = 1e-5
SCALE = HEAD_DIM ** -0.5
DIL_PATTERNS = ((128, 1), (512, 4), (2048, 16))
N_DEV = 8
QKVF_PAD = 3200

ADAM_LR = 0.001
ADAM_B1 = 0.9
ADAM_B2 = 0.999
ADAM_EPS = 1e-08
ADAM_WD = 0.01
ADAM_STEP = 10

LANES = 128
BLK = 128
TB = 256
NEG = -1e30
VMEM_LIMIT = 48 * 1024 * 1024

MESH = pl.DeviceIdType.MESH


def _params(n_grid=0, **kw):
    sem = ("arbitrary",) * n_grid if n_grid else None
    return pltpu.CompilerParams(dimension_semantics=sem, vmem_limit_bytes=VMEM_LIMIT, **kw)


def _mm(a, b, *, name, ta=False, tb=False, add=None, out_dtype=F32, tm=512, tn=512, tk=None, comm=None):
    a_planes, b_planes = a.ndim == 3, b.ndim == 3
    assert not (a_planes and ta) and not (b_planes and tb)
    if a_planes:
        m, k = a.shape[1], a.shape[0] * a.shape[2]
        tk = a.shape[2] if tk is None else tk
        assert a.shape[2] % tk == 0
    else:
        m = a.shape[1] if ta else a.shape[0]
        k = a.shape[0] if ta else a.shape[1]
    if b_planes:
        n = b.shape[0] * b.shape[2]
        tn = min(tn, b.shape[2])
        assert b.shape[1] == k and b.shape[2] % tn == 0
    else:
        n = b.shape[0] if tb else b.shape[1]
        assert (b.shape[1] if tb else b.shape[0]) == k
    tm, tn = min(tm, m), min(tn, n)
    tk = k if tk is None else min(tk, k)
    assert m % tm == 0 and n % tn == 0 and k % tk == 0, (name, m, n, k, tm, tn, tk)
    nk = k // tk
    dn = (((0 if ta else 1,), (1 if tb else 0,)), ((), ()))

    def body(*refs):
        a_ref, b_ref = refs[0], refs[1]
        add_ref = refs[2] if add is not None else None
        o_ref = refs[3] if add is not None else refs[2]
        part = lax.dot_general(a_ref[...], b_ref[...], dn, preferred_element_type=F32)
        if nk == 1:
            if add_ref is not None:
                part = part + add_ref[...]
            o_ref[...] = part.astype(out_dtype)
            return
        acc_ref = refs[-1]
        kk = pl.program_id(2)

        @pl.when(kk == 0)
        def _():
            acc_ref[...] = part

        @pl.when(kk > 0)
        def _():
            acc_ref[...] += part

        @pl.when(kk == nk - 1)
        def _():
            res = acc_ref[...]
            if add_ref is not None:
                res = res + add_ref[...]
            o_ref[...] = res.astype(out_dtype)

    if a_planes:
        a_per = a.shape[2] // tk
        a_spec = pl.BlockSpec((None, tm, tk), lambda i, j, kk: (kk // a_per, i, kk % a_per))
    elif ta:
        a_spec = pl.BlockSpec((tk, tm), lambda i, j, kk: (kk, i))
    else:
        a_spec = pl.BlockSpec((tm, tk), lambda i, j, kk: (i, kk))
    if b_planes:
        b_per = b.shape[2] // tn
        b_spec = pl.BlockSpec((None, tk, tn), lambda i, j, kk: (j // b_per, kk, j % b_per))
    elif tb:
        b_spec = pl.BlockSpec((tn, tk), lambda i, j, kk: (j, kk))
    else:
        b_spec = pl.BlockSpec((tk, tn), lambda i, j, kk: (kk, j))
    o_spec = pl.BlockSpec((tm, tn), lambda i, j, kk: (i, j))
    in_specs = [a_spec, b_spec] + ([o_spec] if add is not None else [])
    args = (a, b) + ((add,) if add is not None else ())
    (out,), got = _call_hosting(
        body, name=name, grid=(m // tm, n // tn, nk), in_specs=in_specs, args=args, out_specs=[o_spec],
        out_shape=[jax.ShapeDtypeStruct((m, n), out_dtype)],
        scratch=[pltpu.VMEM((tm, tn), F32)] if nk > 1 else [], comm=comm)
    return out if comm is None else (out, got)


def _rms_fwd(x, g, *, name, tr=256):
    s, d = x.shape

    def body(x_ref, g_ref, h_ref):
        xv = x_ref[...]
        r = lax.rsqrt(jnp.mean(xv * xv, axis=-1, keepdims=True) + RMS_EPS)
        h_ref[...] = (xv * r * g_ref[...]).astype(BF16)

    return pl.pallas_call(
        body, name=name, grid=(s // tr,),
        in_specs=[pl.BlockSpec((tr, d), lambda i: (i, 0)), pl.BlockSpec((1, d), lambda i: (0, 0))],
        out_specs=pl.BlockSpec((tr, d), lambda i: (i, 0)),
        out_shape=jax.ShapeDtypeStruct((s, d), BF16),
        compiler_params=_params(1),
    )(x, g)


def _rms_bwd(x, g, dh, dres, *, name, tr=256):
    s, d = x.shape

    def body(x_ref, g_ref, dh_ref, dres_ref, dx_ref, dxb_ref, dg_ref):
        xv = x_ref[...]
        r = lax.rsqrt(jnp.mean(xv * xv, axis=-1, keepdims=True) + RMS_EPS)
        y = xv * r
        dhv = dh_ref[...]
        dy = dhv * g_ref[...]
        dx = dres_ref[...] + r * (dy - y * jnp.mean(dy * y, axis=-1, keepdims=True))
        dx_ref[...] = dx
        dxb_ref[...] = dx.astype(BF16)
        part = jnp.sum(dhv * y, axis=0, keepdims=True)

        @pl.when(pl.program_id(0) == 0)
        def _():
            dg_ref[...] = part

        @pl.when(pl.program_id(0) > 0)
        def _():
            dg_ref[...] += part

    row = pl.BlockSpec((tr, d), lambda i: (i, 0))
    vec = pl.BlockSpec((1, d), lambda i: (0, 0))
    return pl.pallas_call(
        body, name=name, grid=(s // tr,),
        in_specs=[row, vec, row, row], out_specs=[row, row, vec],
        out_shape=[jax.ShapeDtypeStruct((s, d), F32), jax.ShapeDtypeStruct((s, d), BF16),
                   jax.ShapeDtypeStruct((1, d), F32)],
        compiler_params=_params(1),
    )(x, g, dh, dres)


def _final_loss(x, g, tgt, *, name, tr=256):
    s, d = x.shape

    def body(x_ref, g_ref, t_ref, loss_ref, dx_ref, dxb_ref, dg_ref):
        xv = x_ref[...]
        gv = g_ref[...]
        r = lax.rsqrt(jnp.mean(xv * xv, axis=-1, keepdims=True) + RMS_EPS)
        y = xv * r
        err = y * gv - t_ref[...]
        lpart = 0.5 * jnp.sum(jnp.mean(err * err, axis=-1, keepdims=True), axis=0, keepdims=True)
        dh = err * (1.0 / d)
        dy = dh * gv
        dx = r * (dy - y * jnp.mean(dy * y, axis=-1, keepdims=True))
        dx_ref[...] = dx
        dxb_ref[...] = dx.astype(BF16)
        gpart = jnp.sum(dh * y, axis=0, keepdims=True)
        lrow = jnp.broadcast_to(lpart, (1, LANES))

        @pl.when(pl.program_id(0) == 0)
        def _():
            dg_ref[...] = gpart
            loss_ref[...] = lrow

        @pl.when(pl.program_id(0) > 0)
        def _():
            dg_ref[...] += gpart
            loss_ref[...] += lrow

    row = pl.BlockSpec((tr, d), lambda i: (i, 0))
    vec = pl.BlockSpec((1, d), lambda i: (0, 0))
    lsp = pl.BlockSpec((1, LANES), lambda i: (0, 0))
    return pl.pallas_call(
        body, name=name, grid=(s // tr,),
        in_specs=[row, vec, row], out_specs=[lsp, row, row, vec],
        out_shape=[jax.ShapeDtypeStruct((1, LANES), F32), jax.ShapeDtypeStruct((s, d), F32),
                   jax.ShapeDtypeStruct((s, d), BF16), jax.ShapeDtypeStruct((1, d), F32)],
        compiler_params=_params(1),
    )(x, g, tgt)


FF_BLK = D_FF // 2


def _swiglu_fwd(gu, *, name, tr=256):
    s = gu.shape[0]

    def body(g_ref, u_ref, a_ref):
        gv = g_ref[...]
        a_ref[...] = (gv * jax.nn.sigmoid(gv) * u_ref[...]).astype(BF16)

    return pl.pallas_call(
        body, name=name, grid=(s // tr, 2),
        in_specs=[pl.BlockSpec((tr, FF_BLK), lambda i, j: (i, j)),
                  pl.BlockSpec((tr, FF_BLK), lambda i, j: (i, j + 2))],
        out_specs=pl.BlockSpec((tr, FF_BLK), lambda i, j: (i, j)),
        out_shape=jax.ShapeDtypeStruct((s, D_FF), BF16),
        compiler_params=_params(2),
    )(gu, gu)


def _swiglu_bwd(da, gu, *, name, tr=256):
    s = gu.shape[0]

    def body(da_ref, g_ref, u_ref, o_ref):
        gv = g_ref[...]
        dav = da_ref[...]
        sg = jax.nn.sigmoid(gv)
        o_ref[0] = (dav * u_ref[...] * (sg * (1.0 + gv * (1.0 - sg)))).astype(BF16)
        o_ref[1] = (dav * gv * sg).astype(BF16)

    return pl.pallas_call(
        body, name=name, grid=(s // tr, 2),
        in_specs=[pl.BlockSpec((tr, FF_BLK), lambda i, j: (i, j)),
                  pl.BlockSpec((tr, FF_BLK), lambda i, j: (i, j)),
                  pl.BlockSpec((tr, FF_BLK), lambda i, j: (i, 2 + j))],
        out_specs=pl.BlockSpec((2, tr, FF_BLK), lambda i, j: (0, i, j)),
        out_shape=jax.ShapeDtypeStruct((2, s, D_FF), BF16),
        compiler_params=_params(2),
    )(da, gu, gu)


def _split3(x):
    hi = x.astype(BF16)
    r1 = x - hi.astype(F32)
    mid = r1.astype(BF16)
    lo = (r1 - mid.astype(F32)).astype(BF16)
    return hi, mid, lo


def _dot3(x, m_bf):
    hi, mid, lo = _split3(x)
    return (jnp.dot(hi, m_bf, preferred_element_type=F32)
            + jnp.dot(mid, m_bf, preferred_element_type=F32)
            + jnp.dot(lo, m_bf, preferred_element_type=F32))


def _dot3_left(m_bf, x):
    hi, mid, lo = _split3(x)
    return (jnp.dot(m_bf, hi, preferred_element_type=F32)
            + jnp.dot(m_bf, mid, preferred_element_type=F32)
            + jnp.dot(m_bf, lo, preferred_element_type=F32))


def _dot2(x, m_bf):
    hi = x.astype(BF16)
    lo = (x - hi.astype(F32)).astype(BF16)
    return jnp.dot(hi, m_bf, preferred_element_type=F32) + jnp.dot(lo, m_bf, preferred_element_type=F32)


def _nt(a, b):
    return lax.dot_general(a, b, (((1,), (1,)), ((), ())), preferred_element_type=F32)


def _mm32(a, b):
    return jnp.dot(a, b, preferred_element_type=F32)


def _iota2(shape, dim):
    return lax.broadcasted_iota(jnp.int32, shape, dim)


def _rope_tables(s):
    half = ROT_DIM // 2
    pos = jnp.arange(s, dtype=F32)
    inv_freq = ROPE_THETA ** (-jnp.arange(half, dtype=F32) * 2.0 / ROT_DIM)
    ang = pos[:, None] * inv_freq[None, :]
    cos, sin = jnp.cos(ang), jnp.sin(ang)
    ones = jnp.ones((s, HEAD_DIM - ROT_DIM), F32)
    cos_t = jnp.concatenate([cos, cos, ones], axis=1)
    sin_t = jnp.concatenate([-sin, sin, 0.0 * ones], axis=1)
    idx = jnp.arange(HEAD_DIM)
    partner = jnp.where(idx < half, idx + half, idx - half)
    swap = ((idx[:, None] == partner[None, :]) & (idx[None, :] < ROT_DIM)).astype(F32)
    swap2 = jnp.kron(jnp.eye(2, dtype=F32), swap).astype(BF16)
    return jnp.tile(cos_t, (1, 2)), jnp.tile(sin_t, (1, 2)), swap2


def _rope(x, cos_t, sin_t, swap):
    return x * cos_t + _dot3(x, swap) * sin_t


def _rope_t(g, cos_t, sin_t, swap):
    return g * cos_t + _dot3(g * sin_t, swap)


def _dil_weight(dlt):
    nonneg = dlt >= 0
    w = jnp.zeros(dlt.shape, F32)
    for window, dil in DIL_PATTERNS:
        ok = nonneg & (dlt <= window) & ((dlt & (dil - 1)) == 0)
        w = w + ok.astype(F32)
    return w


FAR_TILES = 3
assert (FAR_TILES - 1) * TB + 1 > DIL_PATTERNS[1][0] and DIL_PATTERNS[2][0] >= 2048


def _dil_bias_scratch():
    return pltpu.VMEM((FAR_TILES + 1, TB, TB), F32)


def _dil_bias_tiles(bias_ref):
    rmc = _iota2((TB, TB), 0) - _iota2((TB, TB), 1)
    for d in range(FAR_TILES + 1):
        w = _dil_weight(d * TB + rmc)
        bias_ref[d] = jnp.where(w > 0.0, jnp.log(jnp.maximum(w, 1.0)), NEG)


def _log_sig_pair(z):
    sp = jnp.log(1.0 + jnp.exp(-jnp.abs(z)))
    return jnp.minimum(z, 0.0) - sp, -jnp.maximum(z, 0.0) - sp


def _log_one_minus_beta(z):
    return -(jnp.maximum(z, 0.0) + jnp.log(1.0 + jnp.exp(-jnp.abs(z))))


def _pair_masks(x, lane_lo):
    z = jnp.zeros_like(x)
    return jnp.where(lane_lo, x, z).astype(BF16), jnp.where(lane_lo, z, x).astype(BF16)


def _rows(i):
    return pl.ds(pl.multiple_of(i * TB, TB), TB)


def _head_spec(s, col0):
    return pl.BlockSpec((s, LANES), lambda p: (0, col0 + p))


def _stat_spec(s):
    return pl.BlockSpec((2, s, 1), lambda p: (p, 0, 0))


def _rowstat_spec(s):
    return pl.BlockSpec((2, 1, s), lambda p: (p, 0, 0))


def _full_spec(shape):
    nd = len(shape)
    return pl.BlockSpec(shape, lambda p: (0,) * nd)


K_COL, V_COL = D_ATTN // LANES, 2 * D_ATTN // LANES


def _bwd_scratch(s):
    return ([pltpu.VMEM((s, LANES), BF16)] * 8 + [pltpu.VMEM((LANES, s), BF16)] * 4
            + [pltpu.VMEM((LANES, s), F32)] * 2)


def _bwd_prep(i, q, k, v, dov, scr, lane_lo, sub_lo):
    qlo, qhi, klo, khi, kbf, vbf, dolo, dohi, qtlo, qthi, dotlo, dothi = scr[:12]
    rows = _rows(i)
    qs = q * SCALE
    qlo[rows, :], qhi[rows, :] = _pair_masks(qs, lane_lo)
    klo[rows, :], khi[rows, :] = _pair_masks(k * SCALE, lane_lo)
    kbf[rows, :] = k.astype(BF16)
    vbf[rows, :] = v.astype(BF16)
    dolo[rows, :], dohi[rows, :] = _pair_masks(dov, lane_lo)
    qtlo[:, rows], qthi[:, rows] = _pair_masks(qs.T, sub_lo)
    dotlo[:, rows], dothi[:, rows] = _pair_masks(dov.T, sub_lo)


def _sb_fwd(qkv, n_pairs, *, name, comm=None):
    s = qkv.shape[0]
    assert s % TB == 0
    nq = s // TB

    def body(q_ref, k_ref, v_ref, o_ref, ct_ref, qlo, qhi, kbf, vlo, vhi):
        lane_lo = _iota2((TB, LANES), 1) < HEAD_DIM

        def prep(i, _):
            rows = _rows(i)
            qlo[rows, :], qhi[rows, :] = _pair_masks(q_ref[rows, :] * SCALE, lane_lo)
            kbf[rows, :] = k_ref[rows, :].astype(BF16)
            vlo[rows, :], vhi[rows, :] = _pair_masks(v_ref[rows, :], lane_lo)
            return 0

        lax.fori_loop(0, nq, prep, 0)
        rmc = _iota2((TB, TB), 0) - _iota2((TB, TB), 1)
        strict = rmc > 0
        u_ge = (rmc >= 0).astype(BF16)
        qm, vm = (qlo, qhi), (vlo, vhi)

        def qloop(i, _):
            rows = _rows(i)

            def tile(kb, carry, diag):
                c, acc = list(carry[:2]), carry[2]
                keys = _rows(kb)
                k = kbf[keys, :]
                zs = [_nt(qm[h][rows, :], k) for h in range(2)]
                lms = [_log_one_minus_beta(z) for z in zs]
                if diag:
                    lms = [jnp.where(strict, lm, 0.0) for lm in lms]
                r_ins = [_dot2(lm, u_ge) for lm in lms]
                for h in range(2):
                    a = jnp.exp(zs[h] + r_ins[h] + c[h])
                    if diag:
                        a = jnp.where(strict, a, 0.0)
                    acc = acc + _mm32(a.astype(BF16), vm[h][keys, :])
                    c[h] = c[h] + r_ins[h][:, 0:1]
                return c[0], c[1], acc

            z1 = jnp.zeros((TB, 1), F32)
            carry = tile(i, (z1, z1, jnp.zeros((TB, LANES), F32)), True)
            c0, c1, acc = lax.fori_loop(0, i, lambda t, cr: tile(i - 1 - t, cr, False), carry)
            o_ref[rows, :] = acc
            ct_ref[0, rows, :] = c0
            ct_ref[1, rows, :] = c1
            return 0

        lax.fori_loop(0, nq, qloop, 0)

    return _call_pairs(
        body, name=name, n_pairs=n_pairs, comm=comm,
        in_specs=[_head_spec(s, 0), _head_spec(s, K_COL), _head_spec(s, V_COL)], args=(qkv, qkv, qkv),
        out_specs=[_head_spec(s, 0), _stat_spec(s)],
        out_shape=[jax.ShapeDtypeStruct((s, LANES * n_pairs), F32),
                   jax.ShapeDtypeStruct((2 * n_pairs, s, 1), F32)],
        scratch=[pltpu.VMEM((s, LANES), BF16)] * 5)


def _sb_bwd(qkv, do, ctot, n_pairs, do_col0, *, name, comm=None):
    s = qkv.shape[0]
    assert s % TB == 0
    nq = s // TB

    def body(q_ref, k_ref, v_ref, do_ref, ct_ref, dq_ref, dk_ref, dv_ref, *scr):
        qlo, qhi, klo, khi, kbf, vbf, dolo, dohi, qtlo, qthi, dotlo, dothi, dkt, dvt = scr
        lane_lo = _iota2((TB, LANES), 1) < HEAD_DIM
        sub_lo = _iota2((LANES, TB), 0) < HEAD_DIM

        def prep(i, _):
            rows = _rows(i)
            _bwd_prep(i, q_ref[rows, :], k_ref[rows, :], v_ref[rows, :], do_ref[rows, :], scr, lane_lo, sub_lo)
            return 0

        lax.fori_loop(0, nq, prep, 0)
        dkt[...] = jnp.zeros_like(dkt)
        dvt[...] = jnp.zeros_like(dvt)
        rmc = _iota2((TB, TB), 0) - _iota2((TB, TB), 1)
        strict = rmc > 0
        u_le = (rmc <= 0).astype(BF16)
        qm, km, dom, qtm, dotm = (qlo, qhi), (klo, khi), (dolo, dohi), (qtlo, qthi), (dotlo, dothi)

        def qloop(i, _):
            rows = _rows(i)
            ct = (ct_ref[0, rows, :], ct_ref[1, rows, :])

            def tile(kb, carry, diag):
                pre, hl, dq = list(carry[0:2]), list(carry[2:4]), carry[4]
                keys = _rows(kb)
                k, v = kbf[keys, :], vbf[keys, :]
                zs = [_nt(qm[h][rows, :], k) for h in range(2)]
                das = [_nt(dom[h][rows, :], v) for h in range(2)]
                lms = [_log_one_minus_beta(z) for z in zs]
                if diag:
                    lms = [jnp.where(strict, lm, 0.0) for lm in lms]
                pins = [_dot2(lm, u_le) for lm in lms]
                gs, lbs = [], []
                a_bf = []
                for h in range(2):
                    lb = zs[h] + lms[h]
                    a = jnp.exp(lb + (ct[h] - pre[h]) - pins[h])
                    if diag:
                        a = jnp.where(strict, a, 0.0)
                    gs.append(a * das[h])
                    lbs.append(lb)
                    a_bf.append(a.astype(BF16))
                hins = [_dot2(g, u_le) for g in gs]
                dk_t, dv_t = dkt[:, keys], dvt[:, keys]
                for h in range(2):
                    g = gs[h]
                    dz = g - jnp.exp(lbs[h]) * (hl[h] + hins[h])
                    if diag:
                        dz = jnp.where(strict, dz, 0.0)
                    dzb = dz.astype(BF16)
                    dq = dq + _mm32(dzb, km[h][keys, :])
                    dk_t = dk_t + _mm32(qtm[h][:, rows], dzb)
                    dv_t = dv_t + _mm32(dotm[h][:, rows], a_bf[h])
                    pre[h] = pre[h] + pins[h][:, TB - 1:TB]
                    hl[h] = hl[h] + hins[h][:, TB - 1:TB]
                dkt[:, keys] = dk_t
                dvt[:, keys] = dv_t
                return pre[0], pre[1], hl[0], hl[1], dq

            z1 = jnp.zeros((TB, 1), F32)
            carry = lax.fori_loop(0, i, lambda kb, cr: tile(kb, cr, False),
                                  (z1, z1, z1, z1, jnp.zeros((TB, LANES), F32)))
            dq = tile(i, carry, True)[4]
            dq_ref[rows, :] = dq.astype(BF16)
            return 0

        lax.fori_loop(0, nq, qloop, 0)

        def wloop(i, _):
            rows = _rows(i)
            dk_ref[rows, :] = dkt[:, rows].T.astype(BF16)
            dv_ref[rows, :] = dvt[:, rows].T.astype(BF16)
            return 0

        lax.fori_loop(0, nq, wloop, 0)

    out = jax.ShapeDtypeStruct((s, LANES * n_pairs), BF16)
    return _call_pairs(
        body, name=name, n_pairs=n_pairs, comm=comm,
        in_specs=[_head_spec(s, 0), _head_spec(s, K_COL), _head_spec(s, V_COL),
                  _head_spec(s, do_col0), _stat_spec(s)], args=(qkv, qkv, qkv, do, ctot),
        out_specs=[_head_spec(s, 0)] * 3, out_shape=[out, out, out], scratch=_bwd_scratch(s))


def _bias_fwd(mode, qkv, head0_col, n_pairs, extra, *, name, comm=None):
    s = qkv.shape[0]
    assert s % TB == 0 and s <= DIL_PATTERNS[2][0]
    nq = s // TB
    fox = mode == "fox"

    def body(q_ref, k_ref, v_ref, e0, e1, *rest):
        if fox:
            o_ref, lse_ref, qlo, qhi, kbf, vx0, vx1 = rest
        else:
            e2, o_ref, lse_ref, qlo, qhi, kbf, vx0, vx1, bias = rest
            _dil_bias_tiles(bias)
        lane_lo = _iota2((TB, LANES), 1) < HEAD_DIM

        def prep(i, _):
            rows = _rows(i)
            q, k, v = q_ref[rows, :], k_ref[rows, :], v_ref[rows, :]
            if not fox:
                c, sn, sw = e0[rows, :], e1[rows, :], e2[...]
                q, k = _rope(q, c, sn, sw), _rope(k, c, sn, sw)
            qlo[rows, :], qhi[rows, :] = _pair_masks(q * SCALE, lane_lo)
            kbf[rows, :] = k.astype(BF16)
            one = jnp.ones_like(v)
            vx0[rows, :] = jnp.where(lane_lo, v, one).astype(BF16)
            vx1[rows, :] = jnp.where(lane_lo, one, v).astype(BF16)
            return 0

        lax.fori_loop(0, nq, prep, 0)
        rmc = _iota2((TB, TB), 0) - _iota2((TB, TB), 1)
        qm, vx = (qlo, qhi), (vx0, vx1)

        def qloop(i, _):
            rows = _rows(i)
            if fox:
                fq = (e0[0, rows, :], e0[1, rows, :])

            def tile(kb, carry, diag):
                keys = _rows(kb)
                k = kbf[keys, :]
                scs = [_nt(qm[h][rows, :], k) for h in range(2)]
                if not fox:
                    b = bias[jnp.minimum(i - kb, FAR_TILES)]
                out = []
                for h in range(2):
                    m, acc = carry[2 * h], carry[2 * h + 1]
                    if fox:
                        sc = scs[h] + (fq[h] - e1[h, :, keys])
                        if diag:
                            sc = jnp.where(rmc >= 0, sc, NEG)
                    else:
                        sc = scs[h] + b
                    m_new = jnp.maximum(m, jnp.max(sc, axis=1, keepdims=True))
                    p = jnp.exp(sc - m_new)
                    acc = jnp.exp(m - m_new) * acc + _mm32(p.astype(BF16), vx[h][keys, :])
                    out += [m_new, acc]
                return tuple(out)

            m0 = jnp.full((TB, 1), NEG, F32)
            a0 = jnp.zeros((TB, LANES), F32)
            carry = lax.fori_loop(0, i, lambda kb, cr: tile(kb, cr, False), (m0, a0, m0, a0))
            carry = tile(i, carry, True)
            m_0, acc0, m_1, acc1 = carry
            l0, l1 = acc0[:, HEAD_DIM:HEAD_DIM + 1], acc1[:, 0:1]
            o_ref[rows, :] = jnp.where(lane_lo, acc0 / l0, acc1 / l1)
            lse_ref[0, rows, :] = m_0 + jnp.log(l0)
            lse_ref[1, rows, :] = m_1 + jnp.log(l1)
            return 0

        lax.fori_loop(0, nq, qloop, 0)

    hp0 = head0_col
    if fox:
        e_specs = [_stat_spec(s), _rowstat_spec(s)]
    else:
        e_specs = [_full_spec((s, LANES)), _full_spec((s, LANES)), _full_spec((LANES, LANES))]
    return _call_pairs(
        body, name=name, n_pairs=n_pairs, comm=comm,
        in_specs=[_head_spec(s, hp0), _head_spec(s, K_COL + hp0), _head_spec(s, V_COL + hp0)] + e_specs,
        args=(qkv, qkv, qkv, *extra),
        out_specs=[_head_spec(s, 0), _stat_spec(s)],
        out_shape=[jax.ShapeDtypeStruct((s, LANES * n_pairs), F32),
                   jax.ShapeDtypeStruct((2 * n_pairs, s, 1), F32)],
        scratch=[pltpu.VMEM((s, LANES), BF16)] * 5 + ([] if fox else [_dil_bias_scratch()]))


def _bias_bwd(mode, qkv, head0_col, n_pairs, extra, o, do, do_col0, lse, *, name, comm=None):
    s = qkv.shape[0]
    assert s % TB == 0 and s <= DIL_PATTERNS[2][0]
    nq = s // TB
    fox = mode == "fox"

    def body(q_ref, k_ref, v_ref, o_ref, do_ref, lse_ref, e0, e1, *rest):
        if fox:
            dq_ref, dk_ref, dv_ref, dfr_ref, dfc_ref = rest[:5]
            scr = rest[5:]
        else:
            e2, dq_ref, dk_ref, dv_ref = rest[:4]
            scr = rest[4:]
        qlo, qhi, klo, khi, kbf, vbf, dolo, dohi, qtlo, qthi, dotlo, dothi, dkt, dvt = scr[:14]
        if not fox:
            bias = scr[14]
            _dil_bias_tiles(bias)
        lane_lo = _iota2((TB, LANES), 1) < HEAD_DIM
        sub_lo = _iota2((LANES, TB), 0) < HEAD_DIM

        def prep(i, _):
            rows = _rows(i)
            q, k = q_ref[rows, :], k_ref[rows, :]
            if not fox:
                c, sn, sw = e0[rows, :], e1[rows, :], e2[...]
                q, k = _rope(q, c, sn, sw), _rope(k, c, sn, sw)
            _bwd_prep(i, q, k, v_ref[rows, :], do_ref[rows, :], scr, lane_lo, sub_lo)
            return 0

        lax.fori_loop(0, nq, prep, 0)
        dkt[...] = jnp.zeros_like(dkt)
        dvt[...] = jnp.zeros_like(dvt)
        if fox:
            dfr_ref[...] = jnp.zeros_like(dfr_ref)
        rmc = _iota2((TB, TB), 0) - _iota2((TB, TB), 1)
        qm, km, dom, qtm, dotm = (qlo, qhi), (klo, khi), (dolo, dohi), (qtlo, qthi), (dotlo, dothi)

        def qloop(i, _):
            rows = _rows(i)
            prod = do_ref[rows, :] * o_ref[rows, :]
            dsum = (jnp.sum(jnp.where(lane_lo, prod, 0.0), axis=1, keepdims=True),
                    jnp.sum(jnp.where(lane_lo, 0.0, prod), axis=1, keepdims=True))
            lse_i = (lse_ref[0, rows, :], lse_ref[1, rows, :])
            if fox:
                fql = (e0[0, rows, :] - lse_i[0], e0[1, rows, :] - lse_i[1])

            def tile(kb, carry, diag):
                dq, rs = carry[0], list(carry[1:])
                keys = _rows(kb)
                k, v = kbf[keys, :], vbf[keys, :]
                scs = [_nt(qm[h][rows, :], k) for h in range(2)]
                dps = [_nt(dom[h][rows, :], v) for h in range(2)]
                if not fox:
                    b = bias[jnp.minimum(i - kb, FAR_TILES)]
                ps, dss = [], []
                for h in range(2):
                    if fox:
                        sc = scs[h] + (fql[h] - e1[h, :, keys])
                        if diag:
                            sc = jnp.where(rmc >= 0, sc, NEG)
                    else:
                        sc = scs[h] + (b - lse_i[h])
                    p = jnp.exp(sc)
                    dss.append(p * (dps[h] - dsum[h]))
                    ps.append(p.astype(BF16))
                dk_t, dv_t = dkt[:, keys], dvt[:, keys]
                for h in range(2):
                    dsb = dss[h].astype(BF16)
                    dq = dq + _mm32(dsb, km[h][keys, :])
                    dk_t = dk_t + _mm32(qtm[h][:, rows], dsb)
                    dv_t = dv_t + _mm32(dotm[h][:, rows], ps[h])
                    if fox:
                        dfr_ref[h, :, keys] -= jnp.sum(dss[h], axis=0, keepdims=True)
                        for j in range(TB // LANES):
                            rs[h] = rs[h] + dss[h][:, j * LANES:(j + 1) * LANES]
                dkt[:, keys] = dk_t
                dvt[:, keys] = dv_t
                return (dq, *rs)

            z2 = jnp.zeros((TB, LANES), F32)
            carry = lax.fori_loop(0, i, lambda kb, cr: tile(kb, cr, False), (z2, z2, z2) if fox else (z2,))
            carry = tile(i, carry, True)
            if fox:
                dfc_ref[0, rows, :] = jnp.sum(carry[1], axis=1, keepdims=True)
                dfc_ref[1, rows, :] = jnp.sum(carry[2], axis=1, keepdims=True)
            dq = carry[0]
            if not fox:
                dq = _rope_t(dq, e0[rows, :], e1[rows, :], e2[...])
            dq_ref[rows, :] = dq.astype(BF16)
            return 0

        lax.fori_loop(0, nq, qloop, 0)

        def wloop(i, _):
            rows = _rows(i)
            dk = dkt[:, rows].T
            if not fox:
                dk = _rope_t(dk, e0[rows, :], e1[rows, :], e2[...])
            dk_ref[rows, :] = dk.astype(BF16)
            dv_ref[rows, :] = dvt[:, rows].T.astype(BF16)
            return 0

        lax.fori_loop(0, nq, wloop, 0)

    hp0 = head0_col
    out = jax.ShapeDtypeStruct((s, LANES * n_pairs), BF16)
    out_specs = [_head_spec(s, 0)] * 3
    out_shape = [out, out, out]
    if fox:
        e_specs = [_stat_spec(s), _rowstat_spec(s)]
        out_specs += [_rowstat_spec(s), _stat_spec(s)]
        out_shape += [jax.ShapeDtypeStruct((2 * n_pairs, 1, s), F32), jax.ShapeDtypeStruct((2 * n_pairs, s, 1), F32)]
    else:
        e_specs = [_full_spec((s, LANES)), _full_spec((s, LANES)), _full_spec((LANES, LANES))]
    return _call_pairs(
        body, name=name, n_pairs=n_pairs, comm=comm,
        in_specs=[_head_spec(s, hp0), _head_spec(s, K_COL + hp0), _head_spec(s, V_COL + hp0),
                  _head_spec(s, 0), _head_spec(s, do_col0), _stat_spec(s)] + e_specs,
        args=(qkv, qkv, qkv, o, do, lse, *extra),
        out_specs=out_specs, out_shape=out_shape,
        scratch=_bwd_scratch(s) + ([] if fox else [_dil_bias_scratch()]))


F_COL = 3 * D_ATTN // LANES


def _fgate_fwd(qkvf, brow, *, name):
    s = qkvf.shape[0]
    nb = s // BLK

    def body(f_ref, b_ref, fc_ref, fr_ref, fs):
        row, col = _iota2((BLK, BLK), 0), _iota2((BLK, BLK), 1)
        l_incl = (col <= row).astype(BF16)

        def step(i, carry):
            r0 = pl.multiple_of(i * BLK, BLK)
            lf, _ = _log_sig_pair(f_ref[pl.ds(r0, BLK), :] + b_ref[...])
            fblk = carry + _dot3_left(l_incl, lf)
            fs[pl.ds(r0, BLK), :] = fblk
            return fblk[BLK - 1:BLK, :]

        lax.fori_loop(0, nb, step, jnp.zeros((1, LANES), F32))
        ft = fs[...].T
        for h in range(N_HEADS):
            fc_ref[h, :, :] = fs[:, h:h + 1]
            fr_ref[h, :, :] = ft[h:h + 1, :]

    return pl.pallas_call(
        body, name=name, grid=(1,),
        in_specs=[pl.BlockSpec((s, LANES), lambda i: (0, F_COL)), pl.BlockSpec((1, LANES), lambda i: (0, 0))],
        out_specs=[pl.BlockSpec((N_HEADS, s, 1), lambda i: (0, 0, 0)),
                   pl.BlockSpec((N_HEADS, 1, s), lambda i: (0, 0, 0))],
        out_shape=[jax.ShapeDtypeStruct((N_HEADS, s, 1), F32), jax.ShapeDtypeStruct((N_HEADS, 1, s), F32)],
        scratch_shapes=[pltpu.VMEM((s, LANES), F32)],
        compiler_params=_params(1),
    )(qkvf, brow)


def _fgate_bwd(dfr, dfc, qkvf, brow, *, name):
    s = qkvf.shape[0]
    nb = s // BLK

    def body(dfr_ref, dfc_ref, f_ref, b_ref, dfl_ref, db_ref, ts, fs):
        ts[...] = jnp.zeros_like(ts)
        for h in range(N_HEADS):
            ts[h:h + 1, :] = dfr_ref[h]
        fs[...] = ts[...].T
        for h in range(N_HEADS):
            fs[:, h:h + 1] += dfc_ref[h]
        row, col = _iota2((BLK, BLK), 0), _iota2((BLK, BLK), 1)
        u_incl = (col >= row).astype(BF16)
        head_lane = _iota2((BLK, LANES), 1) < N_HEADS

        def step(ii, carry):
            tail, db = carry
            r0 = pl.multiple_of((nb - 1 - ii) * BLK, BLK)
            rblk = tail + _dot3_left(u_incl, fs[pl.ds(r0, BLK), :])
            _, lsn = _log_sig_pair(f_ref[pl.ds(r0, BLK), :] + b_ref[...])
            dfl = jnp.where(head_lane, rblk * jnp.exp(lsn), 0.0)
            dfl_ref[pl.ds(r0, BLK), :] = dfl.astype(BF16)
            return rblk[0:1, :], db + jnp.sum(dfl, axis=0, keepdims=True)

        z = jnp.zeros((1, LANES), F32)
        _, db = lax.fori_loop(0, nb, step, (z, z))
        db_ref[...] = db

    return pl.pallas_call(
        body, name=name, grid=(1,),
        in_specs=[pl.BlockSpec((N_HEADS, 1, s), lambda i: (0, 0, 0)), pl.BlockSpec((N_HEADS, s, 1), lambda i: (0, 0, 0)),
                  pl.BlockSpec((s, LANES), lambda i: (0, F_COL)), pl.BlockSpec((1, LANES), lambda i: (0, 0))],
        out_specs=[pl.BlockSpec((s, LANES), lambda i: (0, 0)), pl.BlockSpec((1, LANES), lambda i: (0, 0))],
        out_shape=[jax.ShapeDtypeStruct((s, LANES), BF16), jax.ShapeDtypeStruct((1, LANES), F32)],
        scratch_shapes=[pltpu.VMEM((LANES, s), F32), pltpu.VMEM((s, LANES), F32)],
        compiler_params=_params(1),
    )(dfr, dfc, qkvf, brow)


def _adamw_math(w, g, m, v):
    m2 = ADAM_B1 * m + (1.0 - ADAM_B1) * g
    v2 = ADAM_B2 * v + (1.0 - ADAM_B2) * (g * g)
    m_hat = m2 / (1.0 - ADAM_B1 ** ADAM_STEP)
    v_hat = v2 / (1.0 - ADAM_B2 ** ADAM_STEP)
    delta = -ADAM_LR * (m_hat / (jnp.sqrt(v_hat) + ADAM_EPS) + ADAM_WD * w)
    return delta, m2, v2


def _row_tile(r, cap=256, mult=16):
    best = None
    for t in range(mult, min(r, cap) + 1, mult):
        if r % t == 0:
            best = t
    assert best is not None, r
    return best


def _adamw_shard(w, m, v, lidx, g_all, r1, r2, sc, prev, *, name):
    nl, r, c = w.shape
    tr = _row_tile(r)

    def body(sc_ref, w_ref, m_ref, v_ref, g_ref, r1_ref, r2_ref, *rest):
        go_ref, d_ref, mo_ref, vo_ref = rest[-4:]
        g = g_ref[...] + r1_ref[...]
        g = g + r2_ref[0].astype(F32)
        g = g + r2_ref[1].astype(F32)
        g = g + r2_ref[2].astype(F32)
        delta, m2, v2 = _adamw_math(w_ref[...], g, m_ref[...], v_ref[...])
        go_ref[...] = g
        d_ref[...] = delta
        mo_ref[...] = m2
        vo_ref[...] = v2

    lay = pl.BlockSpec((None, tr, c), lambda i, s_: (lidx, i, 0))
    in_specs = [lay, lay, lay,
                pl.BlockSpec((None, tr, c), lambda i, s_: (s_[0], i, 0)),
                pl.BlockSpec((None, tr, c), lambda i, s_: (s_[1], i, 0)),
                pl.BlockSpec((3, tr, c), lambda i, s_: (0, i, 0))]
    args = [sc, w, m, v, g_all, r1, r2]
    aliases = {}
    if prev is not None:
        in_specs += [pl.BlockSpec(memory_space=pl.ANY)] * 4
        aliases = {7 + t: t for t in range(4)}
        args += list(prev)
    shp = jax.ShapeDtypeStruct((nl, r, c), F32)
    return pl.pallas_call(
        body, name=name,
        grid_spec=pltpu.PrefetchScalarGridSpec(
            num_scalar_prefetch=1, grid=(r // tr,), in_specs=in_specs, out_specs=[lay] * 4),
        out_shape=[shp] * 4, input_output_aliases=aliases,
        compiler_params=_params(1),
    )(*args)


def _adamw_small(w, g, m, v, *, name):
    def body(w_ref, g_ref, m_ref, v_ref, d_ref, mo_ref, vo_ref):
        delta, m2, v2 = _adamw_math(w_ref[...], g_ref[...], m_ref[...], v_ref[...])
        d_ref[...] = delta
        mo_ref[...] = m2
        vo_ref[...] = v2

    shp = jax.ShapeDtypeStruct(w.shape, F32)
    return pl.pallas_call(body, name=name, out_shape=[shp] * 3, compiler_params=_params())(w, g, m, v)


def _pos():
    return lax.axis_index("x"), lax.axis_index("y"), lax.axis_index("c")


def _other_chips(x, y):
    return [(1 - x, y), (x, 1 - y), (1 - x, 1 - y)]


def _dev_index(x, y, c):
    return 4 * x + 2 * y + c


HBM_SPEC = pl.BlockSpec(memory_space=pltpu.HBM)


class _Comm:
    def __init__(self, inputs, out_shape, scratch, start, mid, finish):
        self.inputs, self.out_shape, self.scratch = list(inputs), list(out_shape), list(scratch)
        self.start, self.mid, self.finish = start, mid, finish

    def run(self, name):
        n_in, n_out = len(self.inputs), len(self.out_shape)

        def body(*refs):
            parts = refs[:n_in], refs[n_in:n_in + n_out], refs[n_in + n_out:]
            self.start(*parts)
            self.mid(*parts)
            self.finish(*parts)

        return pl.pallas_call(
            body, name=name, in_specs=[HBM_SPEC] * n_in, out_specs=[HBM_SPEC] * n_out,
            out_shape=self.out_shape, scratch_shapes=self.scratch)(*self.inputs)


def _call_hosting(body, *, name, grid, in_specs, args, out_specs, out_shape, scratch, comm=None):
    if comm is None:
        res = pl.pallas_call(
            body, name=name, grid=grid, in_specs=in_specs, out_specs=out_specs, out_shape=out_shape,
            scratch_shapes=scratch, compiler_params=_params(len(grid)))(*args)
        return list(res), []
    sizes = (len(in_specs), len(comm.inputs), len(out_specs), len(comm.out_shape), len(scratch), len(comm.scratch))

    def fused(*refs):
        parts, o = [], 0
        for n in sizes:
            parts.append(refs[o:o + n])
            o += n
        h_in, c_in, h_out, c_out, h_scr, c_scr = parts
        first = last = None
        for d, n in enumerate(grid):
            p = pl.program_id(d)
            first = (p == 0) if first is None else jnp.logical_and(first, p == 0)
            last = (p == n - 1) if last is None else jnp.logical_and(last, p == n - 1)

        @pl.when(first)
        def _():
            comm.start(c_in, c_out, c_scr)

        @pl.when(last)
        def _():
            comm.mid(c_in, c_out, c_scr)

        body(*h_in, *h_out, *h_scr)

        @pl.when(last)
        def _():
            comm.finish(c_in, c_out, c_scr)

    res = pl.pallas_call(
        fused, name=name, grid=grid,
        in_specs=list(in_specs) + [HBM_SPEC] * sizes[1], out_specs=list(out_specs) + [HBM_SPEC] * sizes[3],
        out_shape=list(out_shape) + comm.out_shape, scratch_shapes=list(scratch) + comm.scratch,
        compiler_params=_params(len(grid)))(*args, *comm.inputs)
    return list(res[:sizes[2]]), list(res[sizes[2]:])


def _call_pairs(body, *, name, n_pairs, **kw):
    return _call_hosting(body, name=name, grid=(n_pairs,), **kw)


def _gather_comm(shards):
    n = len(shards)

    def plan(xs, outs, sems):
        send, recv, loc = sems
        x, y, c = _pos()
        me, sib = (x, y, c), (x, y, 1 - c)
        chips = _other_chips(x, y)

        def copy(a, k, block, to, src=None):
            dst = outs[a].at[_dev_index(*block)]
            return pltpu.make_async_remote_copy(
                src_ref=dst if src is None else src, dst_ref=dst,
                send_sem=send.at[a, k], recv_sem=recv.at[a, k], device_id=to, device_id_type=MESH)

        mine = [pltpu.make_async_copy(xs[a], outs[a].at[_dev_index(*me)], loc.at[a]) for a in range(n)]
        first = []
        for a in range(n):
            first.append(copy(a, 0, me, sib, src=xs[a]))
            first += [copy(a, 1 + j, me, (*chip, c), src=xs[a]) for j, chip in enumerate(chips)]
        passed = [(copy(a, 1 + j, (*chip, c), me), copy(a, 4 + j, (*chip, c), sib))
                  for j, chip in enumerate(chips) for a in range(n)]
        from_sib = [copy(a, 0, sib, me) for a in range(n)]
        from_sib += [copy(a, 4 + j, (*chip, 1 - c), me) for a in range(n) for j, chip in enumerate(chips)]
        return mine, first, passed, from_sib

    def start(xs, outs, sems):
        mine, first, _, _ = plan(xs, outs, sems)
        for cp in mine + first:
            cp.start()

    def mid(xs, outs, sems):
        for arrival, fwd in plan(xs, outs, sems)[2]:
            arrival.wait_recv()
            fwd.start()

    def finish(xs, outs, sems):
        mine, first, passed, from_sib = plan(xs, outs, sems)
        for cp in from_sib:
            cp.wait_recv()
        for cp in first + [fwd for _, fwd in passed]:
            cp.wait_send()
        for cp in mine:
            cp.wait()

    return _Comm(shards, [jax.ShapeDtypeStruct((N_DEV,) + a.shape, a.dtype) for a in shards],
                 [pltpu.SemaphoreType.DMA((n, 7)), pltpu.SemaphoreType.DMA((n, 7)), pltpu.SemaphoreType.DMA((n,))],
                 start, mid, finish)


def _sibling_comm(gs):
    n = len(gs)

    def plan(g_refs, r_refs, sems):
        send, recv = sems
        x, y, c = _pos()
        return [pltpu.make_async_remote_copy(
            src_ref=g_refs[a].at[_dev_index(k // 2, k % 2, 1 - c)], dst_ref=r_refs[a].at[k],
            send_sem=send.at[a, k], recv_sem=recv.at[a, k], device_id=(x, y, 1 - c), device_id_type=MESH)
            for a in range(n) for k in range(4)]

    def start(*parts):
        for cp in plan(*parts):
            cp.start()

    def mid(*parts):
        pass

    def finish(*parts):
        for cp in plan(*parts):
            cp.wait()

    return _Comm(gs, [jax.ShapeDtypeStruct((4,) + g.shape[1:], g.dtype) for g in gs],
                 [pltpu.SemaphoreType.DMA((n, 4)), pltpu.SemaphoreType.DMA((n, 4))], start, mid, finish)


def _rs_partial(g_all, r1, sc, *, name):
    _, r, c = g_all.shape
    tr = _row_tile(r)

    def body(sc_ref, g_ref, r_ref, o_ref):
        o_ref[...] = (g_ref[...] + r_ref[...]).astype(BF16)

    return pl.pallas_call(
        body, name=name,
        grid_spec=pltpu.PrefetchScalarGridSpec(
            num_scalar_prefetch=1, grid=(3, r // tr),
            in_specs=[pl.BlockSpec((None, tr, c), lambda j, i, s_: (s_[2 + j], i, 0)),
                      pl.BlockSpec((None, tr, c), lambda j, i, s_: (s_[5 + j], i, 0))],
            out_specs=pl.BlockSpec((None, tr, c), lambda j, i, s_: (j, i, 0))),
        out_shape=jax.ShapeDtypeStruct((3, r, c), BF16),
        compiler_params=_params(2),
    )(sc, g_all, r1)


def _cross_comm(ps):
    n = len(ps)

    def plan(p_refs, r_refs, sems):
        send, recv = sems
        x, y, c = _pos()
        return [pltpu.make_async_remote_copy(
            src_ref=p_refs[a].at[j], dst_ref=r_refs[a].at[j], send_sem=send.at[a, j], recv_sem=recv.at[a, j],
            device_id=(*chip, c), device_id_type=MESH)
            for j, chip in enumerate(_other_chips(x, y)) for a in range(n)]

    def start(*parts):
        for cp in plan(*parts):
            cp.start()

    def mid(*parts):
        pass

    def finish(*parts):
        for cp in plan(*parts):
            cp.wait()

    return _Comm(ps, [jax.ShapeDtypeStruct(p.shape, p.dtype) for p in ps],
                 [pltpu.SemaphoreType.DMA((n, 3)), pltpu.SemaphoreType.DMA((n, 3))], start, mid, finish)


SMALL_ROWS = 16


def _all_reduce_small(pack, *, name):
    def body(x_ref, o_ref, buf, send, recv):
        x, y, c = _pos()
        me = _dev_index(x, y, c)
        buf[me] = x_ref[...]
        copies = []
        for k in range(1, N_DEV):
            fx, fy, fc = (k >> 2) & 1, (k >> 1) & 1, k & 1
            peer = (1 - x if fx else x, 1 - y if fy else y, 1 - c if fc else c)
            copies.append(pltpu.make_async_remote_copy(
                src_ref=x_ref, dst_ref=buf.at[me], send_sem=send.at[k - 1], recv_sem=recv.at[k - 1],
                device_id=peer, device_id_type=MESH))
        for cp in copies:
            cp.start()
        for cp in copies:
            cp.wait()
        acc = buf[0]
        for d in range(1, N_DEV):
            acc = acc + buf[d]
        o_ref[...] = acc

    return pl.pallas_call(
        body, name=name,
        in_specs=[pl.BlockSpec(memory_space=pltpu.VMEM)], out_specs=pl.BlockSpec(memory_space=pltpu.VMEM),
        out_shape=jax.ShapeDtypeStruct(pack.shape, F32),
        scratch_shapes=[pltpu.VMEM((N_DEV,) + pack.shape, F32),
                        pltpu.SemaphoreType.DMA((N_DEV - 1,)), pltpu.SemaphoreType.DMA((N_DEV - 1,))],
    )(pack)


def _unshard_cols(g):
    return jnp.transpose(g, (1, 0, 2)).reshape(g.shape[1], N_DEV * g.shape[2])


def _shard_cols(w):
    k, n8 = w.shape
    return jnp.transpose(w.reshape(k, N_DEV, n8 // N_DEV), (1, 0, 2))


def _pad_row(v, width=D_MODEL):
    v = v.reshape(1, -1)
    return jnp.pad(v, ((0, 0), (0, width - v.shape[1])))


def _forward_mixer(l, xc, g_mix, wq, wo, rope, brow, comm_a=None, comm_b=None):
    even = l % 2 == 0
    h1 = _rms_fwd(xc, g_mix, name=f"norm_mix_fwd{l}")
    qkv = _mm(h1, wq, name=f"qkv_fwd{l}", tm=1024, tn=768 if even else 640)
    if even:
        (o_a, st_a), got_a = _sb_fwd(qkv, N_HEADS // 4, name=f"sb_fwd{l}", comm=comm_a)
        (o_b, st_b), got_b = _bias_fwd("dil", qkv, N_HEADS // 4, N_HEADS // 4, rope, name=f"dil_fwd{l}",
                                       comm=comm_b)
        o = jnp.concatenate([o_a, o_b], axis=1)
        att = (o_b, st_a, st_b)
    else:
        assert comm_b is None
        fcol, frow = _fgate_fwd(qkv, brow, name=f"fgate_fwd{l}")
        (o, lse), got_a = _bias_fwd("fox", qkv, 0, N_HEADS // 2, (fcol, frow), name=f"fox_fwd{l}", comm=comm_a)
        got_b = []
        att = (o, lse, fcol, frow)
    o_bf = o.astype(BF16)
    xm = _mm(o_bf, wo, add=xc, name=f"wo_fwd{l}", tm=512, tn=1024)
    return xm, (xc, h1, qkv, att, o_bf), got_a, got_b


def _forward_ffn(l, xm, g_ffn, win, wout):
    h2 = _rms_fwd(xm, g_ffn, name=f"norm_ffn_fwd{l}")
    gu = _mm(h2, win, name=f"ffn_in_fwd{l}", tm=1024, tn=512)
    a = _swiglu_fwd(gu, name=f"swiglu_fwd{l}")
    xo = _mm(a, wout, add=xm, name=f"ffn_out_fwd{l}", tm=512, tn=1024)
    return xo, (xm, h2, gu, a)


def _backward_ffn(l, dx, dxb, saved, g_ffn, w, exchange=None):
    _, _, win, wout = w
    _, _, _, _, _, xm, h2, gu, a = saved
    da = _mm(dxb, wout, tb=True, name=f"ffn_out_dx{l}", tm=1024, tn=FF_BLK)
    d_wout = _mm(a, dxb, ta=True, name=f"ffn_out_dw{l}", tm=FF_BLK, tn=512)
    dgu = _swiglu_bwd(da, gu, name=f"swiglu_bwd{l}")
    d_win = _mm(h2, dgu, ta=True, name=f"ffn_in_dw{l}", tm=1024, tn=FF_BLK)
    comm = exchange(d_win, d_wout) if exchange is not None else None
    dh2 = _mm(dgu, win, tb=True, name=f"ffn_in_dx{l}", tm=512, tn=1024, tk=FF_BLK, comm=comm)
    dh2, got = dh2 if comm is not None else (dh2, [])
    dxm, dxmb, dg_ffn = _rms_bwd(xm, g_ffn, dh2, dx, name=f"norm_ffn_bwd{l}")
    return dxm, dxmb, dg_ffn, d_win, d_wout, got


def _backward_attn(l, dxm, dxmb, saved, g_mix, w, rope, brow, comm_a=None, comm_b=None, exchange=None):
    wq, wo, _, _ = w
    xin, h1, qkv, att, o_bf, _, _, _, _ = saved
    even = l % 2 == 0
    d_wo = _mm(o_bf, dxmb, ta=True, name=f"wo_dw{l}", tm=512, tn=1024)
    do = _mm(dxmb, wo, tb=True, name=f"wo_dx{l}", tm=1024, tn=1024)
    db = None
    if even:
        o_b, st_a, st_b = att
        (dqa, dka, dva), got_a = _sb_bwd(qkv, do, st_a, N_HEADS // 4, 0, name=f"sb_bwd{l}", comm=comm_a)
        (dqb, dkb, dvb), got_b = _bias_bwd("dil", qkv, N_HEADS // 4, N_HEADS // 4, rope, o_b, do,
                                           N_HEADS // 4, st_b, name=f"dil_bwd{l}", comm=comm_b)
        dqkv = jnp.concatenate([dqa, dqb, dka, dkb, dva, dvb], axis=1)
    else:
        assert comm_b is None
        o, lse, fcol, frow = att
        (dq, dk, dv, dfr, dfc), got_a = _bias_bwd("fox", qkv, 0, N_HEADS // 2, (fcol, frow), o, do, 0, lse,
                                                  name=f"fox_bwd{l}", comm=comm_a)
        got_b = []
        dfl, db = _fgate_bwd(dfr, dfc, qkv, brow, name=f"fgate_bwd{l}")
        dqkv = jnp.concatenate([dq, dk, dv, dfl], axis=1)
    d_wq = _mm(h1, dqkv, ta=True, name=f"qkv_dw{l}", tm=1024, tn=768 if even else 640)
    comm = exchange(d_wq, d_wo) if exchange is not None else None
    dh1 = _mm(dqkv, wq, tb=True, name=f"qkv_dx{l}", tm=512, tn=1024, comm=comm)
    dh1, got_x = dh1 if comm is not None else (dh1, [])
    dx, dxb, dg_mix = _rms_bwd(xin, g_mix, dh1, dxm, name=f"norm_mix_bwd{l}")
    return dx, dxb, dg_mix, d_wq, d_wo, db, got_a, got_b, got_x


def kernel(x, norm_mix, w_qkv_even, w_o_even, w_qkvf_odd, b_forget, w_o_odd, norm_ffn, w_ffn_in, w_ffn_out, norm_final, loss_target, m_norm_mix, m_w_qkv_even, m_w_o_even, m_w_qkvf_odd, m_b_forget, m_w_o_odd, m_norm_ffn, m_w_ffn_in, m_w_ffn_out, m_norm_final, v_norm_mix, v_w_qkv_even, v_w_o_even, v_w_qkvf_odd, v_b_forget, v_w_o_odd, v_norm_ffn, v_w_ffn_in, v_w_ffn_out, v_norm_final):
    xi, yi, ci = _pos()
    others = _other_chips(xi, yi)
    sc = jnp.stack([_dev_index(xi, yi, ci), 2 * xi + yi]
                   + [_dev_index(px, py, ci) for px, py in others]
                   + [2 * px + py for px, py in others]).astype(jnp.int32)
    n_odd_cols = w_qkvf_odd.shape[2] * N_DEV

    xs, tgt = x[0], loss_target[0]
    rope = _rope_tables(xs.shape[0])
    brow = [_pad_row(b_forget[i], LANES) for i in range(DEPTH // 2)]

    def shards(l):
        even = l % 2 == 0
        wq_s = (w_qkv_even if even else w_qkvf_odd)[l // 2]
        wo_s = (w_o_even if even else w_o_odd)[l // 2]
        return [wq_s.astype(BF16), wo_s.astype(BF16)], [w_ffn_in[l].astype(BF16), w_ffn_out[l].astype(BF16)]

    def full_mix(l, gq, go):
        wq = _unshard_cols(gq)
        if l % 2 == 1:
            wq = jnp.pad(wq, ((0, 0), (0, QKVF_PAD - n_odd_cols)))
        return wq, go.reshape(D_ATTN, D_MODEL)

    def full_ffn(gi, gout):
        return _unshard_cols(gi), gout.reshape(D_FF, D_MODEL)

    mix0, ffn0 = shards(0)
    w_mix = {0: full_mix(0, *_gather_comm(mix0).run("gather_weights0"))}
    w_ffn = {}
    weights, saved = [], []
    xc = xs
    for l in range(DEPTH):
        comm_a = comm_b = None
        if l + 1 < DEPTH:
            mix_n, ffn_n = shards(l + 1)
            if l == 0:
                comm_a, comm_b = _gather_comm(ffn0 + mix_n), _gather_comm(ffn_n)
            elif l % 2 == 0:
                comm_a, comm_b = _gather_comm(ffn_n), _gather_comm(mix_n)
            else:
                comm_a = _gather_comm(mix_n + ffn_n)
        xm, sv_mix, got_a, got_b = _forward_mixer(l, xc, norm_mix[l:l + 1], *w_mix[l], rope, brow[l // 2],
                                                  comm_a, comm_b)
        if l + 1 < DEPTH:
            if l == 0:
                w_ffn[0] = full_ffn(*got_a[:2])
                w_mix[1], w_ffn[1] = full_mix(1, *got_a[2:]), full_ffn(*got_b)
            elif l % 2 == 0:
                w_mix[l + 1], w_ffn[l + 1] = full_mix(l + 1, *got_b), full_ffn(*got_a)
            else:
                w_mix[l + 1], w_ffn[l + 1] = full_mix(l + 1, *got_a[:2]), full_ffn(*got_a[2:])
        xc, sv_ffn = _forward_ffn(l, xm, norm_ffn[l:l + 1], *w_ffn[l])
        weights.append(w_mix[l] + w_ffn[l])
        saved.append(sv_mix + sv_ffn)

    loss_row, dx, dxb, dg_final = _final_loss(xc, norm_final.reshape(1, -1), tgt, name="final_loss")

    sharded = {
        "qkv_even": (w_qkv_even, m_w_qkv_even, v_w_qkv_even), "o_even": (w_o_even, m_w_o_even, v_w_o_even),
        "qkvf_odd": (w_qkvf_odd, m_w_qkvf_odd, v_w_qkvf_odd), "o_odd": (w_o_odd, m_w_o_odd, v_w_o_odd),
        "ffn_in": (w_ffn_in, m_w_ffn_in, v_w_ffn_in), "ffn_out": (w_ffn_out, m_w_ffn_out, v_w_ffn_out),
    }
    results = {k: None for k in sharded}

    def chip_sums(gs, r1s, keys, tag):
        ps = [_rs_partial(g, r1, sc, name=f"grads_chip_sum_{tag}_{a}") for a, (g, r1) in enumerate(zip(gs, r1s))]
        return gs, r1s, ps, keys

    held = {}

    def to_sibling(tag, odd_qkv=False):
        def make(d_cols, d_rows):
            if odd_qkv:
                d_cols = d_cols[:, :n_odd_cols]
            held[tag] = [_shard_cols(d_cols), d_rows.reshape(N_DEV, d_rows.shape[0] // N_DEV, D_MODEL)]
            return _sibling_comm(held[tag])
        return make

    def update(group, r2s, tag):
        gs, r1s, _, keys = group
        for a, (key, lidx) in enumerate(keys):
            w, m, v = sharded[key]
            results[key] = _adamw_shard(w, m, v, lidx, gs[a], r1s[a], r2s[a], sc, results[key],
                                        name=f"adamw_{key}_{tag}")

    dg_mix, dg_ffn, db_f = [None] * DEPTH, [None] * DEPTH, [None] * (DEPTH // 2)
    pending = None
    for l in reversed(range(DEPTH)):
        even = l % 2 == 0
        dxm, dxmb, dg_ffn[l], _, _, r1s = _backward_ffn(l, dx, dxb, saved[l], norm_ffn[l:l + 1], weights[l],
                                                        to_sibling(f"ffn{l}"))
        ffn = chip_sums(held[f"ffn{l}"], r1s, [("ffn_in", l), ("ffn_out", l)], f"ffn{l}")
        if even:
            comm_a = _cross_comm(ffn[2])
            comm_b = _cross_comm(pending[2]) if pending is not None else None
        else:
            comm_a = _cross_comm(ffn[2] + (pending[2] if pending is not None else []))
            comm_b = None
        dx, dxb, dg_mix[l], _, _, db, got_a, got_b, r1s = _backward_attn(
            l, dxm, dxmb, saved[l], norm_mix[l:l + 1], weights[l], rope, brow[l // 2], comm_a, comm_b,
            to_sibling(f"mix{l}", odd_qkv=not even))
        update(ffn, got_a[:2], f"ffn{l}")
        if pending is not None:
            update(pending, got_b if even else got_a[2:], f"mix{l + 1}")
        if not even:
            db_f[l // 2] = db
        pending = chip_sums(held[f"mix{l}"], r1s,
                            [("qkv_even" if even else "qkvf_odd", l // 2), ("o_even" if even else "o_odd", l // 2)],
                            f"mix{l}")
    update(pending, _cross_comm(pending[2]).run("grads_to_chips_mix0"), "mix0")

    zeros = jnp.zeros((SMALL_ROWS - 11, D_MODEL), F32)
    db_row = _pad_row(jnp.concatenate([d[:, :N_HEADS] for d in db_f], axis=1))
    pack_g = jnp.concatenate(dg_mix + dg_ffn + [dg_final, db_row, _pad_row(loss_row[:, :1]), zeros], axis=0)
    tot = _all_reduce_small(pack_g, name="small_all_reduce")

    def pack(nm, nf, nfin, bf):
        return jnp.concatenate([nm, nf, nfin.reshape(1, -1), _pad_row(bf),
                                jnp.zeros((SMALL_ROWS - 10, D_MODEL), F32)], axis=0)

    d_s, m_s, v_s = _adamw_small(
        pack(norm_mix, norm_ffn, norm_final, b_forget), tot,
        pack(m_norm_mix, m_norm_ffn, m_norm_final, m_b_forget),
        pack(v_norm_mix, v_norm_ffn, v_norm_final, v_b_forget), name="adamw_small")

    def unpack(p):
        nb = b_forget.size
        return {"norm_mix": p[0:DEPTH], "norm_ffn": p[DEPTH:2 * DEPTH], "norm_final": p[2 * DEPTH],
                "b_forget": p[2 * DEPTH + 1, :nb].reshape(b_forget.shape)}

    small = [unpack(tot), unpack(d_s), unpack(m_s), unpack(v_s)]
    loss = tot[2 * DEPTH + 2, 0]

    order = ["norm_mix", "qkv_even", "o_even", "qkvf_odd", "b_forget", "o_odd", "norm_ffn", "ffn_in", "ffn_out",
             "norm_final"]
    outs = [loss, dx[None]]
    for t in range(4):
        for key in order:
            outs.append(small[t][key] if key in small[t] else results[key][t])
    return tuple(outs)
```

```python
import jax
import jax.numpy as jnp
from jax import lax
from jax.experimental import pallas as pl
from jax.experimental.pallas import tpu as pltpu

F32 = jnp.float32
BF16 = jnp.bfloat16

D_MODEL = 1024
HEAD_DIM = 64
N_HEADS = 16
D_ATTN = N_HEADS * HEAD_DIM
D_FF = 2816
DEPTH = 4
ROPE_THETA = 500000.0
ROT_DIM = HEAD_DIM // 4
RMS_EPS = 1e-5
SCALE = HEAD_DIM ** -0.5
DIL_PATTERNS = ((128, 1), (512, 4), (2048, 16))
N_DEV = 8
QKVF_PAD = 3200

ADAM_LR = 0.001
ADAM_B1 = 0.9
ADAM_B2 = 0.999
ADAM_EPS = 1e-08
ADAM_WD = 0.01
ADAM_STEP = 10

LANES = 128
BLK = 128
TB = 256
NEG = -1e30
VMEM_LIMIT = 48 * 1024 * 1024

MESH = pl.DeviceIdType.MESH


def _params(n_grid=0, **kw):
    sem = ("arbitrary",) * n_grid if n_grid else None
    return pltpu.CompilerParams(dimension_semantics=sem, vmem_limit_bytes=VMEM_LIMIT, **kw)


def _mm(a, b, *, name, ta=False, tb=False, add=None, out_dtype=F32, tm=512, tn=512, tk=None, comm=None):
    a_planes, b_planes = a.ndim == 3, b.ndim == 3
    assert not (b_planes and tb)
    if a_planes and ta:
        m, k = a.shape[0] * a.shape[2], a.shape[1]
        tm = min(tm, a.shape[2])
        assert a.shape[2] % tm == 0
    elif a_planes:
        m, k = a.shape[1], a.shape[0] * a.shape[2]
        tk = a.shape[2] if tk is None else tk
        assert a.shape[2] % tk == 0
    else:
        m = a.shape[1] if ta else a.shape[0]
        k = a.shape[0] if ta else a.shape[1]
    if b_planes:
        n = b.shape[0] * b.shape[2]
        tn = min(tn, b.shape[2])
        assert b.shape[1] == k and b.shape[2] % tn == 0
    else:
        n = b.shape[0] if tb else b.shape[1]
        assert (b.shape[1] if tb else b.shape[0]) == k
    tm, tn = min(tm, m), min(tn, n)
    tk = k if tk is None else min(tk, k)
    assert m % tm == 0 and n % tn == 0 and k % tk == 0, (name, m, n, k, tm, tn, tk)
    nk = k // tk
    dn = (((0 if ta else 1,), (1 if tb else 0,)), ((), ()))

    def body(*refs):
        a_ref, b_ref = refs[0], refs[1]
        add_ref = refs[2] if add is not None else None
        o_ref = refs[3] if add is not None else refs[2]
        part = lax.dot_general(a_ref[...], b_ref[...], dn, preferred_element_type=F32)
        if nk == 1:
            if add_ref is not None:
                part = part + add_ref[...]
            o_ref[...] = part.astype(out_dtype)
            return
        acc_ref = refs[-1]
        kk = pl.program_id(2)

        @pl.when(kk == 0)
        def _():
            acc_ref[...] = part

        @pl.when(kk > 0)
        def _():
            acc_ref[...] += part

        @pl.when(kk == nk - 1)
        def _():
            res = acc_ref[...]
            if add_ref is not None:
                res = res + add_ref[...]
            o_ref[...] = res.astype(out_dtype)

    if a_planes and ta:
        a_per = a.shape[2] // tm
        a_spec = pl.BlockSpec((None, tk, tm), lambda i, j, kk: (i // a_per, kk, i % a_per))
    elif a_planes:
        a_per = a.shape[2] // tk
        a_spec = pl.BlockSpec((None, tm, tk), lambda i, j, kk: (kk // a_per, i, kk % a_per))
    elif ta:
        a_spec = pl.BlockSpec((tk, tm), lambda i, j, kk: (kk, i))
    else:
        a_spec = pl.BlockSpec((tm, tk), lambda i, j, kk: (i, kk))
    if b_planes:
        b_per = b.shape[2] // tn
        b_spec = pl.BlockSpec((None, tk, tn), lambda i, j, kk: (j // b_per, kk, j % b_per))
    elif tb:
        b_spec = pl.BlockSpec((tn, tk), lambda i, j, kk: (j, kk))
    else:
        b_spec = pl.BlockSpec((tk, tn), lambda i, j, kk: (kk, j))
    o_spec = pl.BlockSpec((tm, tn), lambda i, j, kk: (i, j))
    in_specs = [a_spec, b_spec] + ([o_spec] if add is not None else [])
    args = (a, b) + ((add,) if add is not None else ())
    (out,), got = _call_hosting(
        body, name=name, grid=(m // tm, n // tn, nk), in_specs=in_specs, args=args, out_specs=[o_spec],
        out_shape=[jax.ShapeDtypeStruct((m, n), out_dtype)],
        scratch=[pltpu.VMEM((tm, tn), F32)] if nk > 1 else [], comm=comm)
    return out if comm is None else (out, got)


def _rms_fwd(x, g, *, name, tr=256):
    s, d = x.shape

    def body(x_ref, g_ref, h_ref):
        xv = x_ref[...]
        r = lax.rsqrt(jnp.mean(xv * xv, axis=-1, keepdims=True) + RMS_EPS)
        h_ref[...] = (xv * r * g_ref[...]).astype(BF16)

    return pl.pallas_call(
        body, name=name, grid=(s // tr,),
        in_specs=[pl.BlockSpec((tr, d), lambda i: (i, 0)), pl.BlockSpec((1, d), lambda i: (0, 0))],
        out_specs=pl.BlockSpec((tr, d), lambda i: (i, 0)),
        out_shape=jax.ShapeDtypeStruct((s, d), BF16),
        compiler_params=_params(1),
    )(x, g)


def _rms_bwd(x, g, dh, dres, *, name, tr=256):
    s, d = x.shape

    def body(x_ref, g_ref, dh_ref, dres_ref, dx_ref, dxb_ref, dg_ref):
        xv = x_ref[...]
        r = lax.rsqrt(jnp.mean(xv * xv, axis=-1, keepdims=True) + RMS_EPS)
        y = xv * r
        dhv = dh_ref[...]
        dy = dhv * g_ref[...]
        dx = dres_ref[...] + r * (dy - y * jnp.mean(dy * y, axis=-1, keepdims=True))
        dx_ref[...] = dx
        dxb_ref[...] = dx.astype(BF16)
        part = jnp.sum(dhv * y, axis=0, keepdims=True)

        @pl.when(pl.program_id(0) == 0)
        def _():
            dg_ref[...] = part

        @pl.when(pl.program_id(0) > 0)
        def _():
            dg_ref[...] += part

    row = pl.BlockSpec((tr, d), lambda i: (i, 0))
    vec = pl.BlockSpec((1, d), lambda i: (0, 0))
    return pl.pallas_call(
        body, name=name, grid=(s // tr,),
        in_specs=[row, vec, row, row], out_specs=[row, row, vec],
        out_shape=[jax.ShapeDtypeStruct((s, d), F32), jax.ShapeDtypeStruct((s, d), BF16),
                   jax.ShapeDtypeStruct((1, d), F32)],
        compiler_params=_params(1),
    )(x, g, dh, dres)


def _final_loss(x, g, tgt, *, name, tr=256):
    s, d = x.shape

    def body(x_ref, g_ref, t_ref, loss_ref, dx_ref, dxb_ref, dg_ref):
        xv = x_ref[...]
        gv = g_ref[...]
        r = lax.rsqrt(jnp.mean(xv * xv, axis=-1, keepdims=True) + RMS_EPS)
        y = xv * r
        err = y * gv - t_ref[...]
        lpart = 0.5 * jnp.sum(jnp.mean(err * err, axis=-1, keepdims=True), axis=0, keepdims=True)
        dh = err * (1.0 / d)
        dy = dh * gv
        dx = r * (dy - y * jnp.mean(dy * y, axis=-1, keepdims=True))
        dx_ref[...] = dx
        dxb_ref[...] = dx.astype(BF16)
        gpart = jnp.sum(dh * y, axis=0, keepdims=True)
        lrow = jnp.broadcast_to(lpart, (1, LANES))

        @pl.when(pl.program_id(0) == 0)
        def _():
            dg_ref[...] = gpart
            loss_ref[...] = lrow

        @pl.when(pl.program_id(0) > 0)
        def _():
            dg_ref[...] += gpart
            loss_ref[...] += lrow

    row = pl.BlockSpec((tr, d), lambda i: (i, 0))
    vec = pl.BlockSpec((1, d), lambda i: (0, 0))
    lsp = pl.BlockSpec((1, LANES), lambda i: (0, 0))
    return pl.pallas_call(
        body, name=name, grid=(s // tr,),
        in_specs=[row, vec, row], out_specs=[lsp, row, row, vec],
        out_shape=[jax.ShapeDtypeStruct((1, LANES), F32), jax.ShapeDtypeStruct((s, d), F32),
                   jax.ShapeDtypeStruct((s, d), BF16), jax.ShapeDtypeStruct((1, d), F32)],
        compiler_params=_params(1),
    )(x, g, tgt)


FF_BLK = D_FF // 2


def _swiglu_fwd(gu, *, name, tr=256):
    s = gu.shape[0]

    def body(g_ref, u_ref, a_ref):
        gv = g_ref[...]
        a_ref[...] = (gv * jax.nn.sigmoid(gv) * u_ref[...]).astype(BF16)

    return pl.pallas_call(
        body, name=name, grid=(s // tr, 2),
        in_specs=[pl.BlockSpec((tr, FF_BLK), lambda i, j: (i, j)),
                  pl.BlockSpec((tr, FF_BLK), lambda i, j: (i, j + 2))],
        out_specs=pl.BlockSpec((tr, FF_BLK), lambda i, j: (i, j)),
        out_shape=jax.ShapeDtypeStruct((s, D_FF), BF16),
        compiler_params=_params(2),
    )(gu, gu)


def _swiglu_bwd(da, gu, *, name, tr=256):
    s = gu.shape[0]

    def body(da_ref, g_ref, u_ref, o_ref):
        gv = g_ref[...]
        dav = da_ref[...]
        sg = jax.nn.sigmoid(gv)
        o_ref[0] = (dav * u_ref[...] * (sg * (1.0 + gv * (1.0 - sg)))).astype(BF16)
        o_ref[1] = (dav * gv * sg).astype(BF16)

    return pl.pallas_call(
        body, name=name, grid=(s // tr, 2),
        in_specs=[pl.BlockSpec((tr, FF_BLK), lambda i, j: (i, j)),
                  pl.BlockSpec((tr, FF_BLK), lambda i, j: (i, j)),
                  pl.BlockSpec((tr, FF_BLK), lambda i, j: (i, 2 + j))],
        out_specs=pl.BlockSpec((2, tr, FF_BLK), lambda i, j: (0, i, j)),
        out_shape=jax.ShapeDtypeStruct((2, s, D_FF), BF16),
        compiler_params=_params(2),
    )(da, gu, gu)


def _split3(x):
    hi = x.astype(BF16)
    r1 = x - hi.astype(F32)
    mid = r1.astype(BF16)
    lo = (r1 - mid.astype(F32)).astype(BF16)
    return hi, mid, lo


def _dot3(x, m_bf):
    hi, mid, lo = _split3(x)
    return (jnp.dot(hi, m_bf, preferred_element_type=F32)
            + jnp.dot(mid, m_bf, preferred_element_type=F32)
            + jnp.dot(lo, m_bf, preferred_element_type=F32))


def _dot3_left(m_bf, x):
    hi, mid, lo = _split3(x)
    return (jnp.dot(m_bf, hi, preferred_element_type=F32)
            + jnp.dot(m_bf, mid, preferred_element_type=F32)
            + jnp.dot(m_bf, lo, preferred_element_type=F32))


def _dot2(x, m_bf):
    hi = x.astype(BF16)
    lo = (x - hi.astype(F32)).astype(BF16)
    return jnp.dot(hi, m_bf, preferred_element_type=F32) + jnp.dot(lo, m_bf, preferred_element_type=F32)


def _nt(a, b):
    return lax.dot_general(a, b, (((1,), (1,)), ((), ())), preferred_element_type=F32)


def _mm32(a, b):
    return jnp.dot(a, b, preferred_element_type=F32)


def _iota2(shape, dim):
    return lax.broadcasted_iota(jnp.int32, shape, dim)


def _rope_tables(s):
    half = ROT_DIM // 2
    pos = jnp.arange(s, dtype=F32)
    inv_freq = ROPE_THETA ** (-jnp.arange(half, dtype=F32) * 2.0 / ROT_DIM)
    ang = pos[:, None] * inv_freq[None, :]
    cos, sin = jnp.cos(ang), jnp.sin(ang)
    ones = jnp.ones((s, HEAD_DIM - ROT_DIM), F32)
    cos_t = jnp.concatenate([cos, cos, ones], axis=1)
    sin_t = jnp.concatenate([-sin, sin, 0.0 * ones], axis=1)
    idx = jnp.arange(HEAD_DIM)
    partner = jnp.where(idx < half, idx + half, idx - half)
    swap = ((idx[:, None] == partner[None, :]) & (idx[None, :] < ROT_DIM)).astype(F32)
    swap2 = jnp.kron(jnp.eye(2, dtype=F32), swap).astype(BF16)
    return jnp.tile(cos_t, (1, 2)), jnp.tile(sin_t, (1, 2)), swap2


def _rope(x, cos_t, sin_t, swap):
    return x * cos_t + _dot3(x, swap) * sin_t


def _rope_t(g, cos_t, sin_t, swap):
    return g * cos_t + _dot3(g * sin_t, swap)


def _dil_weight(dlt):
    nonneg = dlt >= 0
    w = jnp.zeros(dlt.shape, F32)
    for window, dil in DIL_PATTERNS:
        ok = nonneg & (dlt <= window) & ((dlt & (dil - 1)) == 0)
        w = w + ok.astype(F32)
    return w


FAR_TILES = 3
assert (FAR_TILES - 1) * TB + 1 > DIL_PATTERNS[1][0] and DIL_PATTERNS[2][0] >= 2048


def _dil_bias_scratch():
    return pltpu.VMEM((FAR_TILES + 1, TB, TB), F32)


def _dil_bias_tiles(bias_ref):
    rmc = _iota2((TB, TB), 0) - _iota2((TB, TB), 1)
    for d in range(FAR_TILES + 1):
        w = _dil_weight(d * TB + rmc)
        bias_ref[d] = jnp.where(w > 0.0, jnp.log(jnp.maximum(w, 1.0)), NEG)


def _log_sig_pair(z):
    sp = jnp.log(1.0 + jnp.exp(-jnp.abs(z)))
    return jnp.minimum(z, 0.0) - sp, -jnp.maximum(z, 0.0) - sp


def _log_one_minus_beta(z):
    return -(jnp.maximum(z, 0.0) + jnp.log(1.0 + jnp.exp(-jnp.abs(z))))


def _pair_masks(x, lane_lo):
    z = jnp.zeros_like(x)
    return jnp.where(lane_lo, x, z).astype(BF16), jnp.where(lane_lo, z, x).astype(BF16)


def _rows(i):
    return pl.ds(pl.multiple_of(i * TB, TB), TB)


def _head_spec(s, col0):
    return pl.BlockSpec((s, LANES), lambda p: (0, col0 + p))


def _stat_spec(s):
    return pl.BlockSpec((2, s, 1), lambda p: (p, 0, 0))


def _rowstat_spec(s):
    return pl.BlockSpec((2, 1, s), lambda p: (p, 0, 0))


def _full_spec(shape):
    nd = len(shape)
    return pl.BlockSpec(shape, lambda p: (0,) * nd)


K_COL, V_COL = D_ATTN // LANES, 2 * D_ATTN // LANES


def _bwd_scratch(s):
    return ([pltpu.VMEM((s, LANES), BF16)] * 8 + [pltpu.VMEM((LANES, s), BF16)] * 4
            + [pltpu.VMEM((LANES, s), F32)] * 2)


def _bwd_prep(i, q, k, v, dov, scr, lane_lo, sub_lo):
    qlo, qhi, klo, khi, kbf, vbf, dolo, dohi, qtlo, qthi, dotlo, dothi = scr[:12]
    rows = _rows(i)
    qs = q * SCALE
    qlo[rows, :], qhi[rows, :] = _pair_masks(qs, lane_lo)
    klo[rows, :], khi[rows, :] = _pair_masks(k * SCALE, lane_lo)
    kbf[rows, :] = k.astype(BF16)
    vbf[rows, :] = v.astype(BF16)
    dolo[rows, :], dohi[rows, :] = _pair_masks(dov, lane_lo)
    qtlo[:, rows], qthi[:, rows] = _pair_masks(qs.T, sub_lo)
    dotlo[:, rows], dothi[:, rows] = _pair_masks(dov.T, sub_lo)


def _sb_fwd(qkv, n_pairs, *, name, comm=None):
    s = qkv.shape[0]
    assert s % TB == 0
    nq = s // TB

    def body(q_ref, k_ref, v_ref, o_ref, ct_ref, qlo, qhi, kbf, vlo, vhi):
        lane_lo = _iota2((TB, LANES), 1) < HEAD_DIM

        def prep(i, _):
            rows = _rows(i)
            qlo[rows, :], qhi[rows, :] = _pair_masks(q_ref[rows, :] * SCALE, lane_lo)
            kbf[rows, :] = k_ref[rows, :].astype(BF16)
            vlo[rows, :], vhi[rows, :] = _pair_masks(v_ref[rows, :], lane_lo)
            return 0

        lax.fori_loop(0, nq, prep, 0)
        rmc = _iota2((TB, TB), 0) - _iota2((TB, TB), 1)
        strict = rmc > 0
        u_ge = (rmc >= 0).astype(BF16)
        qm, vm = (qlo, qhi), (vlo, vhi)

        def qloop(i, _):
            rows = _rows(i)

            def tile(kb, carry, diag):
                c, acc = list(carry[:2]), carry[2]
                keys = _rows(kb)
                k = kbf[keys, :]
                zs = [_nt(qm[h][rows, :], k) for h in range(2)]
                lms = [_log_one_minus_beta(z) for z in zs]
                if diag:
                    lms = [jnp.where(strict, lm, 0.0) for lm in lms]
                r_ins = [_dot2(lm, u_ge) for lm in lms]
                for h in range(2):
                    a = jnp.exp(zs[h] + r_ins[h] + c[h])
                    if diag:
                        a = jnp.where(strict, a, 0.0)
                    acc = acc + _mm32(a.astype(BF16), vm[h][keys, :])
                    c[h] = c[h] + r_ins[h][:, 0:1]
                return c[0], c[1], acc

            z1 = jnp.zeros((TB, 1), F32)
            carry = tile(i, (z1, z1, jnp.zeros((TB, LANES), F32)), True)
            c0, c1, acc = lax.fori_loop(0, i, lambda t, cr: tile(i - 1 - t, cr, False), carry)
            o_ref[rows, :] = acc
            ct_ref[0, rows, :] = c0
            ct_ref[1, rows, :] = c1
            return 0

        lax.fori_loop(0, nq, qloop, 0)

    return _call_pairs(
        body, name=name, n_pairs=n_pairs, comm=comm,
        in_specs=[_head_spec(s, 0), _head_spec(s, K_COL), _head_spec(s, V_COL)], args=(qkv, qkv, qkv),
        out_specs=[_head_spec(s, 0), _stat_spec(s)],
        out_shape=[jax.ShapeDtypeStruct((s, LANES * n_pairs), F32),
                   jax.ShapeDtypeStruct((2 * n_pairs, s, 1), F32)],
        scratch=[pltpu.VMEM((s, LANES), BF16)] * 5)


def _sb_bwd(qkv, do, ctot, n_pairs, do_col0, *, name, comm=None):
    s = qkv.shape[0]
    assert s % TB == 0
    nq = s // TB

    def body(q_ref, k_ref, v_ref, do_ref, ct_ref, dq_ref, dk_ref, dv_ref, *scr):
        qlo, qhi, klo, khi, kbf, vbf, dolo, dohi, qtlo, qthi, dotlo, dothi, dkt, dvt = scr
        lane_lo = _iota2((TB, LANES), 1) < HEAD_DIM
        sub_lo = _iota2((LANES, TB), 0) < HEAD_DIM

        def prep(i, _):
            rows = _rows(i)
            _bwd_prep(i, q_ref[rows, :], k_ref[rows, :], v_ref[rows, :], do_ref[rows, :], scr, lane_lo, sub_lo)
            return 0

        lax.fori_loop(0, nq, prep, 0)
        dkt[...] = jnp.zeros_like(dkt)
        dvt[...] = jnp.zeros_like(dvt)
        rmc = _iota2((TB, TB), 0) - _iota2((TB, TB), 1)
        strict = rmc > 0
        u_le = (rmc <= 0).astype(BF16)
        qm, km, dom, qtm, dotm = (qlo, qhi), (klo, khi), (dolo, dohi), (qtlo, qthi), (dotlo, dothi)

        def qloop(i, _):
            rows = _rows(i)
            ct = (ct_ref[0, rows, :], ct_ref[1, rows, :])

            def tile(kb, carry, diag):
                pre, hl, dq = list(carry[0:2]), list(carry[2:4]), carry[4]
                keys = _rows(kb)
                k, v = kbf[keys, :], vbf[keys, :]
                zs = [_nt(qm[h][rows, :], k) for h in range(2)]
                das = [_nt(dom[h][rows, :], v) for h in range(2)]
                lms = [_log_one_minus_beta(z) for z in zs]
                if diag:
                    lms = [jnp.where(strict, lm, 0.0) for lm in lms]
                pins = [_dot2(lm, u_le) for lm in lms]
                gs, lbs = [], []
                a_bf = []
                for h in range(2):
                    lb = zs[h] + lms[h]
                    a = jnp.exp(lb + (ct[h] - pre[h]) - pins[h])
                    if diag:
                        a = jnp.where(strict, a, 0.0)
                    gs.append(a * das[h])
                    lbs.append(lb)
                    a_bf.append(a.astype(BF16))
                hins = [_dot2(g, u_le) for g in gs]
                dk_t, dv_t = dkt[:, keys], dvt[:, keys]
                for h in range(2):
                    g = gs[h]
                    dz = g - jnp.exp(lbs[h]) * (hl[h] + hins[h])
                    if diag:
                        dz = jnp.where(strict, dz, 0.0)
                    dzb = dz.astype(BF16)
                    dq = dq + _mm32(dzb, km[h][keys, :])
                    dk_t = dk_t + _mm32(qtm[h][:, rows], dzb)
                    dv_t = dv_t + _mm32(dotm[h][:, rows], a_bf[h])
                    pre[h] = pre[h] + pins[h][:, TB - 1:TB]
                    hl[h] = hl[h] + hins[h][:, TB - 1:TB]
                dkt[:, keys] = dk_t
                dvt[:, keys] = dv_t
                return pre[0], pre[1], hl[0], hl[1], dq

            z1 = jnp.zeros((TB, 1), F32)
            carry = lax.fori_loop(0, i, lambda kb, cr: tile(kb, cr, False),
                                  (z1, z1, z1, z1, jnp.zeros((TB, LANES), F32)))
            dq = tile(i, carry, True)[4]
            dq_ref[rows, :] = dq.astype(BF16)
            return 0

        lax.fori_loop(0, nq, qloop, 0)

        def wloop(i, _):
            rows = _rows(i)
            dk_ref[rows, :] = dkt[:, rows].T.astype(BF16)
            dv_ref[rows, :] = dvt[:, rows].T.astype(BF16)
            return 0

        lax.fori_loop(0, nq, wloop, 0)

    out = jax.ShapeDtypeStruct((s, LANES * n_pairs), BF16)
    return _call_pairs(
        body, name=name, n_pairs=n_pairs, comm=comm,
        in_specs=[_head_spec(s, 0), _head_spec(s, K_COL), _head_spec(s, V_COL),
                  _head_spec(s, do_col0), _stat_spec(s)], args=(qkv, qkv, qkv, do, ctot),
        out_specs=[_head_spec(s, 0)] * 3, out_shape=[out, out, out], scratch=_bwd_scratch(s))


def _bias_fwd(mode, qkv, head0_col, n_pairs, extra, *, name, comm=None):
    s = qkv.shape[0]
    assert s % TB == 0 and s <= DIL_PATTERNS[2][0]
    nq = s // TB
    fox = mode == "fox"

    def body(q_ref, k_ref, v_ref, e0, e1, *rest):
        if fox:
            o_ref, lse_ref, qlo, qhi, kbf, vx0, vx1 = rest
        else:
            e2, o_ref, lse_ref, qlo, qhi, kbf, vx0, vx1, bias = rest
            _dil_bias_tiles(bias)
        lane_lo = _iota2((TB, LANES), 1) < HEAD_DIM

        def prep(i, _):
            rows = _rows(i)
            q, k, v = q_ref[rows, :], k_ref[rows, :], v_ref[rows, :]
            if not fox:
                c, sn, sw = e0[rows, :], e1[rows, :], e2[...]
                q, k = _rope(q, c, sn, sw), _rope(k, c, sn, sw)
            qlo[rows, :], qhi[rows, :] = _pair_masks(q * SCALE, lane_lo)
            kbf[rows, :] = k.astype(BF16)
            one = jnp.ones_like(v)
            vx0[rows, :] = jnp.where(lane_lo, v, one).astype(BF16)
            vx1[rows, :] = jnp.where(lane_lo, one, v).astype(BF16)
            return 0

        lax.fori_loop(0, nq, prep, 0)
        rmc = _iota2((TB, TB), 0) - _iota2((TB, TB), 1)
        qm, vx = (qlo, qhi), (vx0, vx1)

        def qloop(i, _):
            rows = _rows(i)
            if fox:
                fq = (e0[0, rows, :], e0[1, rows, :])

            def tile(kb, carry, diag):
                keys = _rows(kb)
                k = kbf[keys, :]
                scs = [_nt(qm[h][rows, :], k) for h in range(2)]
                if not fox:
                    b = bias[jnp.minimum(i - kb, FAR_TILES)]
                out = []
                for h in range(2):
                    m, acc = carry[2 * h], carry[2 * h + 1]
                    if fox:
                        sc = scs[h] + (fq[h] - e1[h, :, keys])
                        if diag:
                            sc = jnp.where(rmc >= 0, sc, NEG)
                    else:
                        sc = scs[h] + b
                    m_new = jnp.maximum(m, jnp.max(sc, axis=1, keepdims=True))
                    p = jnp.exp(sc - m_new)
                    acc = jnp.exp(m - m_new) * acc + _mm32(p.astype(BF16), vx[h][keys, :])
                    out += [m_new, acc]
                return tuple(out)

            m0 = jnp.full((TB, 1), NEG, F32)
            a0 = jnp.zeros((TB, LANES), F32)
            carry = lax.fori_loop(0, i, lambda kb, cr: tile(kb, cr, False), (m0, a0, m0, a0))
            carry = tile(i, carry, True)
            m_0, acc0, m_1, acc1 = carry
            l0, l1 = acc0[:, HEAD_DIM:HEAD_DIM + 1], acc1[:, 0:1]
            o_ref[rows, :] = jnp.where(lane_lo, acc0 / l0, acc1 / l1)
            lse_ref[0, rows, :] = m_0 + jnp.log(l0)
            lse_ref[1, rows, :] = m_1 + jnp.log(l1)
            return 0

        lax.fori_loop(0, nq, qloop, 0)

    hp0 = head0_col
    if fox:
        e_specs = [_stat_spec(s), _rowstat_spec(s)]
    else:
        e_specs = [_full_spec((s, LANES)), _full_spec((s, LANES)), _full_spec((LANES, LANES))]
    return _call_pairs(
        body, name=name, n_pairs=n_pairs, comm=comm,
        in_specs=[_head_spec(s, hp0), _head_spec(s, K_COL + hp0), _head_spec(s, V_COL + hp0)] + e_specs,
        args=(qkv, qkv, qkv, *extra),
        out_specs=[_head_spec(s, 0), _stat_spec(s)],
        out_shape=[jax.ShapeDtypeStruct((s, LANES * n_pairs), F32),
                   jax.ShapeDtypeStruct((2 * n_pairs, s, 1), F32)],
        scratch=[pltpu.VMEM((s, LANES), BF16)] * 5 + ([] if fox else [_dil_bias_scratch()]))


def _bias_bwd(mode, qkv, head0_col, n_pairs, extra, o, do, do_col0, lse, *, name, comm=None):
    s = qkv.shape[0]
    assert s % TB == 0 and s <= DIL_PATTERNS[2][0]
    nq = s // TB
    fox = mode == "fox"

    def body(q_ref, k_ref, v_ref, o_ref, do_ref, lse_ref, e0, e1, *rest):
        if fox:
            dq_ref, dk_ref, dv_ref, dfr_ref, dfc_ref = rest[:5]
            scr = rest[5:]
        else:
            e2, dq_ref, dk_ref, dv_ref = rest[:4]
            scr = rest[4:]
        qlo, qhi, klo, khi, kbf, vbf, dolo, dohi, qtlo, qthi, dotlo, dothi, dkt, dvt = scr[:14]
        if not fox:
            bias = scr[14]
            _dil_bias_tiles(bias)
        lane_lo = _iota2((TB, LANES), 1) < HEAD_DIM
        sub_lo = _iota2((LANES, TB), 0) < HEAD_DIM

        def prep(i, _):
            rows = _rows(i)
            q, k = q_ref[rows, :], k_ref[rows, :]
            if not fox:
                c, sn, sw = e0[rows, :], e1[rows, :], e2[...]
                q, k = _rope(q, c, sn, sw), _rope(k, c, sn, sw)
            _bwd_prep(i, q, k, v_ref[rows, :], do_ref[rows, :], scr, lane_lo, sub_lo)
            return 0

        lax.fori_loop(0, nq, prep, 0)
        dkt[...] = jnp.zeros_like(dkt)
        dvt[...] = jnp.zeros_like(dvt)
        if fox:
            dfr_ref[...] = jnp.zeros_like(dfr_ref)
        rmc = _iota2((TB, TB), 0) - _iota2((TB, TB), 1)
        qm, km, dom, qtm, dotm = (qlo, qhi), (klo, khi), (dolo, dohi), (qtlo, qthi), (dotlo, dothi)

        def qloop(i, _):
            rows = _rows(i)
            prod = do_ref[rows, :] * o_ref[rows, :]
            dsum = (jnp.sum(jnp.where(lane_lo, prod, 0.0), axis=1, keepdims=True),
                    jnp.sum(jnp.where(lane_lo, 0.0, prod), axis=1, keepdims=True))
            lse_i = (lse_ref[0, rows, :], lse_ref[1, rows, :])
            if fox:
                fql = (e0[0, rows, :] - lse_i[0], e0[1, rows, :] - lse_i[1])

            def tile(kb, carry, diag):
                dq, rs = carry[0], list(carry[1:])
                keys = _rows(kb)
                k, v = kbf[keys, :], vbf[keys, :]
                scs = [_nt(qm[h][rows, :], k) for h in range(2)]
                dps = [_nt(dom[h][rows, :], v) for h in range(2)]
                if not fox:
                    b = bias[jnp.minimum(i - kb, FAR_TILES)]
                ps, dss = [], []
                for h in range(2):
                    if fox:
                        sc = scs[h] + (fql[h] - e1[h, :, keys])
                        if diag:
                            sc = jnp.where(rmc >= 0, sc, NEG)
                    else:
                        sc = scs[h] + (b - lse_i[h])
                    p = jnp.exp(sc)
                    dss.append(p * (dps[h] - dsum[h]))
                    ps.append(p.astype(BF16))
                dk_t, dv_t = dkt[:, keys], dvt[:, keys]
                for h in range(2):
                    dsb = dss[h].astype(BF16)
                    dq = dq + _mm32(dsb, km[h][keys, :])
                    dk_t = dk_t + _mm32(qtm[h][:, rows], dsb)
                    dv_t = dv_t + _mm32(dotm[h][:, rows], ps[h])
                    if fox:
                        dfr_ref[h, :, keys] -= jnp.sum(dss[h], axis=0, keepdims=True)
                        for j in range(TB // LANES):
                            rs[h] = rs[h] + dss[h][:, j * LANES:(j + 1) * LANES]
                dkt[:, keys] = dk_t
                dvt[:, keys] = dv_t
                return (dq, *rs)

            z2 = jnp.zeros((TB, LANES), F32)
            carry = lax.fori_loop(0, i, lambda kb, cr: tile(kb, cr, False), (z2, z2, z2) if fox else (z2,))
            carry = tile(i, carry, True)
            if fox:
                dfc_ref[0, rows, :] = jnp.sum(carry[1], axis=1, keepdims=True)
                dfc_ref[1, rows, :] = jnp.sum(carry[2], axis=1, keepdims=True)
            dq = carry[0]
            if not fox:
                dq = _rope_t(dq, e0[rows, :], e1[rows, :], e2[...])
            dq_ref[rows, :] = dq.astype(BF16)
            return 0

        lax.fori_loop(0, nq, qloop, 0)

        def wloop(i, _):
            rows = _rows(i)
            dk = dkt[:, rows].T
            if not fox:
                dk = _rope_t(dk, e0[rows, :], e1[rows, :], e2[...])
            dk_ref[rows, :] = dk.astype(BF16)
            dv_ref[rows, :] = dvt[:, rows].T.astype(BF16)
            return 0

        lax.fori_loop(0, nq, wloop, 0)

    hp0 = head0_col
    out = jax.ShapeDtypeStruct((s, LANES * n_pairs), BF16)
    out_specs = [_head_spec(s, 0)] * 3
    out_shape = [out, out, out]
    if fox:
        e_specs = [_stat_spec(s), _rowstat_spec(s)]
        out_specs += [_rowstat_spec(s), _stat_spec(s)]
        out_shape += [jax.ShapeDtypeStruct((2 * n_pairs, 1, s), F32), jax.ShapeDtypeStruct((2 * n_pairs, s, 1), F32)]
    else:
        e_specs = [_full_spec((s, LANES)), _full_spec((s, LANES)), _full_spec((LANES, LANES))]
    return _call_pairs(
        body, name=name, n_pairs=n_pairs, comm=comm,
        in_specs=[_head_spec(s, hp0), _head_spec(s, K_COL + hp0), _head_spec(s, V_COL + hp0),
                  _head_spec(s, 0), _head_spec(s, do_col0), _stat_spec(s)] + e_specs,
        args=(qkv, qkv, qkv, o, do, lse, *extra),
        out_specs=out_specs, out_shape=out_shape,
        scratch=_bwd_scratch(s) + ([] if fox else [_dil_bias_scratch()]))


F_COL = 3 * D_ATTN // LANES


def _fgate_fwd(qkvf, brow, *, name):
    s = qkvf.shape[0]
    nb = s // BLK

    def body(f_ref, b_ref, fc_ref, fr_ref, fs):
        row, col = _iota2((BLK, BLK), 0), _iota2((BLK, BLK), 1)
        l_incl = (col <= row).astype(BF16)

        def step(i, carry):
            r0 = pl.multiple_of(i * BLK, BLK)
            lf, _ = _log_sig_pair(f_ref[pl.ds(r0, BLK), :] + b_ref[...])
            fblk = carry + _dot3_left(l_incl, lf)
            fs[pl.ds(r0, BLK), :] = fblk
            return fblk[BLK - 1:BLK, :]

        lax.fori_loop(0, nb, step, jnp.zeros((1, LANES), F32))
        ft = fs[...].T
        for h in range(N_HEADS):
            fc_ref[h, :, :] = fs[:, h:h + 1]
            fr_ref[h, :, :] = ft[h:h + 1, :]

    return pl.pallas_call(
        body, name=name, grid=(1,),
        in_specs=[pl.BlockSpec((s, LANES), lambda i: (0, F_COL)), pl.BlockSpec((1, LANES), lambda i: (0, 0))],
        out_specs=[pl.BlockSpec((N_HEADS, s, 1), lambda i: (0, 0, 0)),
                   pl.BlockSpec((N_HEADS, 1, s), lambda i: (0, 0, 0))],
        out_shape=[jax.ShapeDtypeStruct((N_HEADS, s, 1), F32), jax.ShapeDtypeStruct((N_HEADS, 1, s), F32)],
        scratch_shapes=[pltpu.VMEM((s, LANES), F32)],
        compiler_params=_params(1),
    )(qkvf, brow)


def _fgate_bwd(dfr, dfc, qkvf, brow, *, name):
    s = qkvf.shape[0]
    nb = s // BLK

    def body(dfr_ref, dfc_ref, f_ref, b_ref, dfl_ref, db_ref, ts, fs):
        ts[...] = jnp.zeros_like(ts)
        for h in range(N_HEADS):
            ts[h:h + 1, :] = dfr_ref[h]
        fs[...] = ts[...].T
        for h in range(N_HEADS):
            fs[:, h:h + 1] += dfc_ref[h]
        row, col = _iota2((BLK, BLK), 0), _iota2((BLK, BLK), 1)
        u_incl = (col >= row).astype(BF16)
        head_lane = _iota2((BLK, LANES), 1) < N_HEADS

        def step(ii, carry):
            tail, db = carry
            r0 = pl.multiple_of((nb - 1 - ii) * BLK, BLK)
            rblk = tail + _dot3_left(u_incl, fs[pl.ds(r0, BLK), :])
            _, lsn = _log_sig_pair(f_ref[pl.ds(r0, BLK), :] + b_ref[...])
            dfl = jnp.where(head_lane, rblk * jnp.exp(lsn), 0.0)
            dfl_ref[pl.ds(r0, BLK), :] = dfl.astype(BF16)
            return rblk[0:1, :], db + jnp.sum(dfl, axis=0, keepdims=True)

        z = jnp.zeros((1, LANES), F32)
        _, db = lax.fori_loop(0, nb, step, (z, z))
        db_ref[...] = db

    return pl.pallas_call(
        body, name=name, grid=(1,),
        in_specs=[pl.BlockSpec((N_HEADS, 1, s), lambda i: (0, 0, 0)), pl.BlockSpec((N_HEADS, s, 1), lambda i: (0, 0, 0)),
                  pl.BlockSpec((s, LANES), lambda i: (0, F_COL)), pl.BlockSpec((1, LANES), lambda i: (0, 0))],
        out_specs=[pl.BlockSpec((s, LANES), lambda i: (0, 0)), pl.BlockSpec((1, LANES), lambda i: (0, 0))],
        out_shape=[jax.ShapeDtypeStruct((s, LANES), BF16), jax.ShapeDtypeStruct((1, LANES), F32)],
        scratch_shapes=[pltpu.VMEM((LANES, s), F32), pltpu.VMEM((s, LANES), F32)],
        compiler_params=_params(1),
    )(dfr, dfc, qkvf, brow)


def _adamw_math(w, g, m, v):
    m2 = ADAM_B1 * m + (1.0 - ADAM_B1) * g
    v2 = ADAM_B2 * v + (1.0 - ADAM_B2) * (g * g)
    m_hat = m2 / (1.0 - ADAM_B1 ** ADAM_STEP)
    v_hat = v2 / (1.0 - ADAM_B2 ** ADAM_STEP)
    delta = -ADAM_LR * (m_hat / (jnp.sqrt(v_hat) + ADAM_EPS) + ADAM_WD * w)
    return delta, m2, v2


def _row_tile(r, cap=256, mult=16):
    best = None
    for t in range(mult, min(r, cap) + 1, mult):
        if r % t == 0:
            best = t
    assert best is not None, r
    return best


def _adamw_shard(w, m, v, lidx, g_all, r1, r2, sc, prev, *, name):
    nl, r, c = w.shape
    tr = _row_tile(r)

    def body(sc_ref, w_ref, m_ref, v_ref, g_ref, r1_ref, r2_ref, *rest):
        go_ref, d_ref, mo_ref, vo_ref = rest[-4:]
        g = g_ref[...] + r1_ref[...]
        g = g + r2_ref[0].astype(F32)
        g = g + r2_ref[1].astype(F32)
        g = g + r2_ref[2].astype(F32)
        delta, m2, v2 = _adamw_math(w_ref[...], g, m_ref[...], v_ref[...])
        go_ref[...] = g
        d_ref[...] = delta
        mo_ref[...] = m2
        vo_ref[...] = v2

    lay = pl.BlockSpec((None, tr, c), lambda i, s_: (lidx, i, 0))
    in_specs = [lay, lay, lay,
                pl.BlockSpec((None, tr, c), lambda i, s_: (s_[0], i, 0)),
                pl.BlockSpec((None, tr, c), lambda i, s_: (s_[1], i, 0)),
                pl.BlockSpec((3, tr, c), lambda i, s_: (0, i, 0))]
    args = [sc, w, m, v, g_all, r1, r2]
    aliases = {}
    if prev is not None:
        in_specs += [pl.BlockSpec(memory_space=pl.ANY)] * 4
        aliases = {7 + t: t for t in range(4)}
        args += list(prev)
    shp = jax.ShapeDtypeStruct((nl, r, c), F32)
    return pl.pallas_call(
        body, name=name,
        grid_spec=pltpu.PrefetchScalarGridSpec(
            num_scalar_prefetch=1, grid=(r // tr,), in_specs=in_specs, out_specs=[lay] * 4),
        out_shape=[shp] * 4, input_output_aliases=aliases,
        compiler_params=_params(1),
    )(*args)


def _adamw_small(w, g, m, v, *, name):
    def body(w_ref, g_ref, m_ref, v_ref, d_ref, mo_ref, vo_ref):
        delta, m2, v2 = _adamw_math(w_ref[...], g_ref[...], m_ref[...], v_ref[...])
        d_ref[...] = delta
        mo_ref[...] = m2
        vo_ref[...] = v2

    shp = jax.ShapeDtypeStruct(w.shape, F32)
    return pl.pallas_call(body, name=name, out_shape=[shp] * 3, compiler_params=_params())(w, g, m, v)


def _pos():
    return lax.axis_index("x"), lax.axis_index("y"), lax.axis_index("c")


def _other_chips(x, y):
    return [(1 - x, y), (x, 1 - y), (1 - x, 1 - y)]


def _dev_index(x, y, c):
    return 4 * x + 2 * y + c


HBM_SPEC = pl.BlockSpec(memory_space=pltpu.HBM)


class _Comm:
    def __init__(self, inputs, out_shape, scratch, start, mid, finish):
        self.inputs, self.out_shape, self.scratch = list(inputs), list(out_shape), list(scratch)
        self.start, self.mid, self.finish = start, mid, finish

    def run(self, name):
        n_in, n_out = len(self.inputs), len(self.out_shape)

        def body(*refs):
            parts = refs[:n_in], refs[n_in:n_in + n_out], refs[n_in + n_out:]
            self.start(*parts)
            self.mid(*parts)
            self.finish(*parts)

        return pl.pallas_call(
            body, name=name, in_specs=[HBM_SPEC] * n_in, out_specs=[HBM_SPEC] * n_out,
            out_shape=self.out_shape, scratch_shapes=self.scratch)(*self.inputs)


def _call_hosting(body, *, name, grid, in_specs, args, out_specs, out_shape, scratch, comm=None):
    if comm is None:
        res = pl.pallas_call(
            body, name=name, grid=grid, in_specs=in_specs, out_specs=out_specs, out_shape=out_shape,
            scratch_shapes=scratch, compiler_params=_params(len(grid)))(*args)
        return list(res), []
    sizes = (len(in_specs), len(comm.inputs), len(out_specs), len(comm.out_shape), len(scratch), len(comm.scratch))

    def fused(*refs):
        parts, o = [], 0
        for n in sizes:
            parts.append(refs[o:o + n])
            o += n
        h_in, c_in, h_out, c_out, h_scr, c_scr = parts
        first = last = None
        for d, n in enumerate(grid):
            p = pl.program_id(d)
            first = (p == 0) if first is None else jnp.logical_and(first, p == 0)
            last = (p == n - 1) if last is None else jnp.logical_and(last, p == n - 1)

        @pl.when(first)
        def _():
            comm.start(c_in, c_out, c_scr)

        @pl.when(last)
        def _():
            comm.mid(c_in, c_out, c_scr)

        body(*h_in, *h_out, *h_scr)

        @pl.when(last)
        def _():
            comm.finish(c_in, c_out, c_scr)

    res = pl.pallas_call(
        fused, name=name, grid=grid,
        in_specs=list(in_specs) + [HBM_SPEC] * sizes[1], out_specs=list(out_specs) + [HBM_SPEC] * sizes[3],
        out_shape=list(out_shape) + comm.out_shape, scratch_shapes=list(scratch) + comm.scratch,
        compiler_params=_params(len(grid)))(*args, *comm.inputs)
    return list(res[:sizes[2]]), list(res[sizes[2]:])


def _call_pairs(body, *, name, n_pairs, **kw):
    return _call_hosting(body, name=name, grid=(n_pairs,), **kw)


def _gather_comm(shards):
    n = len(shards)

    def plan(xs, outs, sems):
        send, recv, loc = sems
        x, y, c = _pos()
        me, sib = (x, y, c), (x, y, 1 - c)
        chips = _other_chips(x, y)

        def copy(a, k, block, to, src=None):
            dst = outs[a].at[_dev_index(*block)]
            return pltpu.make_async_remote_copy(
                src_ref=dst if src is None else src, dst_ref=dst,
                send_sem=send.at[a, k], recv_sem=recv.at[a, k], device_id=to, device_id_type=MESH)

        mine = [pltpu.make_async_copy(xs[a], outs[a].at[_dev_index(*me)], loc.at[a]) for a in range(n)]
        first = []
        for a in range(n):
            first.append(copy(a, 0, me, sib, src=xs[a]))
            first += [copy(a, 1 + j, me, (*chip, c), src=xs[a]) for j, chip in enumerate(chips)]
        passed = [(copy(a, 1 + j, (*chip, c), me), copy(a, 4 + j, (*chip, c), sib))
                  for j, chip in enumerate(chips) for a in range(n)]
        from_sib = [copy(a, 0, sib, me) for a in range(n)]
        from_sib += [copy(a, 4 + j, (*chip, 1 - c), me) for a in range(n) for j, chip in enumerate(chips)]
        return mine, first, passed, from_sib

    def start(xs, outs, sems):
        mine, first, _, _ = plan(xs, outs, sems)
        for cp in mine + first:
            cp.start()

    def mid(xs, outs, sems):
        for arrival, fwd in plan(xs, outs, sems)[2]:
            arrival.wait_recv()
            fwd.start()

    def finish(xs, outs, sems):
        mine, first, passed, from_sib = plan(xs, outs, sems)
        for cp in from_sib:
            cp.wait_recv()
        for cp in first + [fwd for _, fwd in passed]:
            cp.wait_send()
        for cp in mine:
            cp.wait()

    return _Comm(shards, [jax.ShapeDtypeStruct((N_DEV,) + a.shape, a.dtype) for a in shards],
                 [pltpu.SemaphoreType.DMA((n, 7)), pltpu.SemaphoreType.DMA((n, 7)), pltpu.SemaphoreType.DMA((n,))],
                 start, mid, finish)


def _sibling_comm(gs):
    n = len(gs)

    def plan(g_refs, r_refs, sems):
        send, recv = sems
        x, y, c = _pos()
        return [pltpu.make_async_remote_copy(
            src_ref=g_refs[a].at[_dev_index(k // 2, k % 2, 1 - c)], dst_ref=r_refs[a].at[k],
            send_sem=send.at[a, k], recv_sem=recv.at[a, k], device_id=(x, y, 1 - c), device_id_type=MESH)
            for a in range(n) for k in range(4)]

    def start(*parts):
        for cp in plan(*parts):
            cp.start()

    def mid(*parts):
        pass

    def finish(*parts):
        for cp in plan(*parts):
            cp.wait()

    return _Comm(gs, [jax.ShapeDtypeStruct((4,) + g.shape[1:], g.dtype) for g in gs],
                 [pltpu.SemaphoreType.DMA((n, 4)), pltpu.SemaphoreType.DMA((n, 4))], start, mid, finish)


def _rs_partial(g_all, r1, sc, *, name):
    _, r, c = g_all.shape
    tr = _row_tile(r)

    def body(sc_ref, g_ref, r_ref, o_ref):
        o_ref[...] = (g_ref[...] + r_ref[...]).astype(BF16)

    return pl.pallas_call(
        body, name=name,
        grid_spec=pltpu.PrefetchScalarGridSpec(
            num_scalar_prefetch=1, grid=(3, r // tr),
            in_specs=[pl.BlockSpec((None, tr, c), lambda j, i, s_: (s_[2 + j], i, 0)),
                      pl.BlockSpec((None, tr, c), lambda j, i, s_: (s_[5 + j], i, 0))],
            out_specs=pl.BlockSpec((None, tr, c), lambda j, i, s_: (j, i, 0))),
        out_shape=jax.ShapeDtypeStruct((3, r, c), BF16),
        compiler_params=_params(2),
    )(sc, g_all, r1)


def _cross_comm(ps):
    n = len(ps)

    def plan(p_refs, r_refs, sems):
        send, recv = sems
        x, y, c = _pos()
        return [pltpu.make_async_remote_copy(
            src_ref=p_refs[a].at[j], dst_ref=r_refs[a].at[j], send_sem=send.at[a, j], recv_sem=recv.at[a, j],
            device_id=(*chip, c), device_id_type=MESH)
            for j, chip in enumerate(_other_chips(x, y)) for a in range(n)]

    def start(*parts):
        for cp in plan(*parts):
            cp.start()

    def mid(*parts):
        pass

    def finish(*parts):
        for cp in plan(*parts):
            cp.wait()

    return _Comm(ps, [jax.ShapeDtypeStruct(p.shape, p.dtype) for p in ps],
                 [pltpu.SemaphoreType.DMA((n, 3)), pltpu.SemaphoreType.DMA((n, 3))], start, mid, finish)


SMALL_ROWS = 16


def _all_reduce_small(pack, *, name):
    def body(x_ref, o_ref, buf, send, recv):
        x, y, c = _pos()
        me = _dev_index(x, y, c)
        buf[me] = x_ref[...]
        copies = []
        for k in range(1, N_DEV):
            fx, fy, fc = (k >> 2) & 1, (k >> 1) & 1, k & 1
            peer = (1 - x if fx else x, 1 - y if fy else y, 1 - c if fc else c)
            copies.append(pltpu.make_async_remote_copy(
                src_ref=x_ref, dst_ref=buf.at[me], send_sem=send.at[k - 1], recv_sem=recv.at[k - 1],
                device_id=peer, device_id_type=MESH))
        for cp in copies:
            cp.start()
        for cp in copies:
            cp.wait()
        acc = buf[0]
        for d in range(1, N_DEV):
            acc = acc + buf[d]
        o_ref[...] = acc

    return pl.pallas_call(
        body, name=name,
        in_specs=[pl.BlockSpec(memory_space=pltpu.VMEM)], out_specs=pl.BlockSpec(memory_space=pltpu.VMEM),
        out_shape=jax.ShapeDtypeStruct(pack.shape, F32),
        scratch_shapes=[pltpu.VMEM((N_DEV,) + pack.shape, F32),
                        pltpu.SemaphoreType.DMA((N_DEV - 1,)), pltpu.SemaphoreType.DMA((N_DEV - 1,))],
    )(pack)


def _unshard_cols(g):
    return jnp.transpose(g, (1, 0, 2)).reshape(g.shape[1], N_DEV * g.shape[2])


def _shard_cols(w):
    k, n8 = w.shape
    return jnp.transpose(w.reshape(k, N_DEV, n8 // N_DEV), (1, 0, 2))


def _pad_row(v, width=D_MODEL):
    v = v.reshape(1, -1)
    return jnp.pad(v, ((0, 0), (0, width - v.shape[1])))


def _forward_mixer(l, xc, g_mix, wq, wo, rope, brow, comm_a=None, comm_b=None):
    even = l % 2 == 0
    h1 = _rms_fwd(xc, g_mix, name=f"norm_mix_fwd{l}")
    qkv = _mm(h1, wq, name=f"qkv_fwd{l}", tm=1024, tn=768 if even else 640)
    if even:
        (o_a, st_a), got_a = _sb_fwd(qkv, N_HEADS // 4, name=f"sb_fwd{l}", comm=comm_a)
        (o_b, st_b), got_b = _bias_fwd("dil", qkv, N_HEADS // 4, N_HEADS // 4, rope, name=f"dil_fwd{l}",
                                       comm=comm_b)
        o = jnp.concatenate([o_a, o_b], axis=1)
        att = (o_b, st_a, st_b)
    else:
        assert comm_b is None
        fcol, frow = _fgate_fwd(qkv, brow, name=f"fgate_fwd{l}")
        (o, lse), got_a = _bias_fwd("fox", qkv, 0, N_HEADS // 2, (fcol, frow), name=f"fox_fwd{l}", comm=comm_a)
        got_b = []
        att = (o, lse, fcol, frow)
    o_bf = o.astype(BF16)
    xm = _mm(o_bf, wo, add=xc, name=f"wo_fwd{l}", tm=512, tn=1024)
    return xm, (xc, h1, qkv, att, o_bf), got_a, got_b


def _forward_ffn(l, xm, g_ffn, win_t, wout):
    h2 = _rms_fwd(xm, g_ffn, name=f"norm_ffn_fwd{l}")
    gu = _mm(h2, win_t, tb=True, name=f"ffn_in_fwd{l}", tm=1024, tn=512)
    a = _swiglu_fwd(gu, name=f"swiglu_fwd{l}")
    xo = _mm(a, wout, add=xm, name=f"ffn_out_fwd{l}", tm=512, tn=1024)
    return xo, (xm, h2, gu, a)


def _backward_ffn(l, dx, dxb, saved, g_ffn, w, exchange=None):
    _, _, win_t, wout = w
    _, _, _, _, _, xm, h2, gu, a = saved
    da = _mm(dxb, wout, tb=True, name=f"ffn_out_dx{l}", tm=1024, tn=FF_BLK)
    d_wout = _mm(a, dxb, ta=True, name=f"ffn_out_dw{l}", tm=FF_BLK, tn=512)
    dgu = _swiglu_bwd(da, gu, name=f"swiglu_bwd{l}")
    d_win_t = _mm(dgu, h2, ta=True, name=f"ffn_in_dw{l}", tm=FF_BLK, tn=1024)
    comm = exchange(d_win_t, d_wout) if exchange is not None else None
    dh2 = _mm(dgu, win_t, name=f"ffn_in_dx{l}", tm=512, tn=1024, tk=FF_BLK, comm=comm)
    dh2, got = dh2 if comm is not None else (dh2, [])
    dxm, dxmb, dg_ffn = _rms_bwd(xm, g_ffn, dh2, dx, name=f"norm_ffn_bwd{l}")
    return dxm, dxmb, dg_ffn, d_win_t, d_wout, got


def _backward_attn(l, dxm, dxmb, saved, g_mix, w, rope, brow, comm_a=None, comm_b=None, exchange=None):
    wq, wo, _, _ = w
    xin, h1, qkv, att, o_bf, _, _, _, _ = saved
    even = l % 2 == 0
    d_wo = _mm(o_bf, dxmb, ta=True, name=f"wo_dw{l}", tm=512, tn=1024)
    do = _mm(dxmb, wo, tb=True, name=f"wo_dx{l}", tm=1024, tn=1024)
    db = None
    if even:
        o_b, st_a, st_b = att
        (dqa, dka, dva), got_a = _sb_bwd(qkv, do, st_a, N_HEADS // 4, 0, name=f"sb_bwd{l}", comm=comm_a)
        (dqb, dkb, dvb), got_b = _bias_bwd("dil", qkv, N_HEADS // 4, N_HEADS // 4, rope, o_b, do,
                                           N_HEADS // 4, st_b, name=f"dil_bwd{l}", comm=comm_b)
        dqkv = jnp.concatenate([dqa, dqb, dka, dkb, dva, dvb], axis=1)
    else:
        assert comm_b is None
        o, lse, fcol, frow = att
        (dq, dk, dv, dfr, dfc), got_a = _bias_bwd("fox", qkv, 0, N_HEADS // 2, (fcol, frow), o, do, 0, lse,
                                                  name=f"fox_bwd{l}", comm=comm_a)
        got_b = []
        dfl, db = _fgate_bwd(dfr, dfc, qkv, brow, name=f"fgate_bwd{l}")
        dqkv = jnp.concatenate([dq, dk, dv, dfl], axis=1)
    d_wq = _mm(h1, dqkv, ta=True, name=f"qkv_dw{l}", tm=1024, tn=768 if even else 640)
    comm = exchange(d_wq, d_wo) if exchange is not None else None
    dh1 = _mm(dqkv, wq, tb=True, name=f"qkv_dx{l}", tm=512, tn=1024, comm=comm)
    dh1, got_x = dh1 if comm is not None else (dh1, [])
    dx, dxb, dg_mix = _rms_bwd(xin, g_mix, dh1, dxm, name=f"norm_mix_bwd{l}")
    return dx, dxb, dg_mix, d_wq, d_wo, db, got_a, got_b, got_x


def kernel(x, norm_mix, w_qkv_even, w_o_even, w_qkvf_odd, b_forget, w_o_odd, norm_ffn, w_ffn_in, w_ffn_out, norm_final, loss_target, m_norm_mix, m_w_qkv_even, m_w_o_even, m_w_qkvf_odd, m_b_forget, m_w_o_odd, m_norm_ffn, m_w_ffn_in, m_w_ffn_out, m_norm_final, v_norm_mix, v_w_qkv_even, v_w_o_even, v_w_qkvf_odd, v_b_forget, v_w_o_odd, v_norm_ffn, v_w_ffn_in, v_w_ffn_out, v_norm_final):
    xi, yi, ci = _pos()
    others = _other_chips(xi, yi)
    sc = jnp.stack([_dev_index(xi, yi, ci), 2 * xi + yi]
                   + [_dev_index(px, py, ci) for px, py in others]
                   + [2 * px + py for px, py in others]).astype(jnp.int32)
    n_odd_cols = w_qkvf_odd.shape[2] * N_DEV

    xs, tgt = x[0], loss_target[0]
    rope = _rope_tables(xs.shape[0])
    brow = [_pad_row(b_forget[i], LANES) for i in range(DEPTH // 2)]
    w_in_t, m_in_t, v_in_t = (jnp.swapaxes(t, 1, 2) for t in (w_ffn_in, m_w_ffn_in, v_w_ffn_in))

    def shards(l):
        even = l % 2 == 0
        wq_s = (w_qkv_even if even else w_qkvf_odd)[l // 2]
        wo_s = (w_o_even if even else w_o_odd)[l // 2]
        return [wq_s.astype(BF16), wo_s.astype(BF16)], [w_in_t[l].astype(BF16), w_ffn_out[l].astype(BF16)]

    def full_mix(l, gq, go):
        wq = _unshard_cols(gq)
        if l % 2 == 1:
            wq = jnp.pad(wq, ((0, 0), (0, QKVF_PAD - n_odd_cols)))
        return wq, go.reshape(D_ATTN, D_MODEL)

    def full_ffn(gi, gout):
        return gi.reshape(2 * D_FF, D_MODEL), gout.reshape(D_FF, D_MODEL)

    mix0, ffn0 = shards(0)
    w_mix = {0: full_mix(0, *_gather_comm(mix0).run("gather_weights0"))}
    w_ffn = {}
    weights, saved = [], []
    xc = xs
    for l in range(DEPTH):
        comm_a = comm_b = None
        if l + 1 < DEPTH:
            mix_n, ffn_n = shards(l + 1)
            if l == 0:
                comm_a, comm_b = _gather_comm(ffn0 + mix_n), _gather_comm(ffn_n)
            elif l % 2 == 0:
                comm_a, comm_b = _gather_comm(ffn_n), _gather_comm(mix_n)
            else:
                comm_a = _gather_comm(mix_n + ffn_n)
        xm, sv_mix, got_a, got_b = _forward_mixer(l, xc, norm_mix[l:l + 1], *w_mix[l], rope, brow[l // 2],
                                                  comm_a, comm_b)
        if l + 1 < DEPTH:
            if l == 0:
                w_ffn[0] = full_ffn(*got_a[:2])
                w_mix[1], w_ffn[1] = full_mix(1, *got_a[2:]), full_ffn(*got_b)
            elif l % 2 == 0:
                w_mix[l + 1], w_ffn[l + 1] = full_mix(l + 1, *got_b), full_ffn(*got_a)
            else:
                w_mix[l + 1], w_ffn[l + 1] = full_mix(l + 1, *got_a[:2]), full_ffn(*got_a[2:])
        xc, sv_ffn = _forward_ffn(l, xm, norm_ffn[l:l + 1], *w_ffn[l])
        weights.append(w_mix[l] + w_ffn[l])
        saved.append(sv_mix + sv_ffn)

    loss_row, dx, dxb, dg_final = _final_loss(xc, norm_final.reshape(1, -1), tgt, name="final_loss")

    sharded = {
        "qkv_even": (w_qkv_even, m_w_qkv_even, v_w_qkv_even), "o_even": (w_o_even, m_w_o_even, v_w_o_even),
        "qkvf_odd": (w_qkvf_odd, m_w_qkvf_odd, v_w_qkvf_odd), "o_odd": (w_o_odd, m_w_o_odd, v_w_o_odd),
        "ffn_in": (w_in_t, m_in_t, v_in_t), "ffn_out": (w_ffn_out, m_w_ffn_out, v_w_ffn_out),
    }
    results = {k: None for k in sharded}

    def chip_sums(gs, r1s, keys, tag):
        ps = [_rs_partial(g, r1, sc, name=f"grads_chip_sum_{tag}_{a}") for a, (g, r1) in enumerate(zip(gs, r1s))]
        return gs, r1s, ps, keys

    held = {}

    def row_chunks(d):
        return d.reshape(N_DEV, d.shape[0] // N_DEV, D_MODEL)

    def to_sibling(tag, col_sharded, odd_qkv=False):
        def make(d_first, d_rows):
            if odd_qkv:
                d_first = d_first[:, :n_odd_cols]
            held[tag] = [_shard_cols(d_first) if col_sharded else row_chunks(d_first), row_chunks(d_rows)]
            return _sibling_comm(held[tag])
        return make

    def update(group, r2s, tag):
        gs, r1s, _, keys = group
        for a, (key, lidx) in enumerate(keys):
            w, m, v = sharded[key]
            results[key] = _adamw_shard(w, m, v, lidx, gs[a], r1s[a], r2s[a], sc, results[key],
                                        name=f"adamw_{key}_{tag}")

    dg_mix, dg_ffn, db_f = [None] * DEPTH, [None] * DEPTH, [None] * (DEPTH // 2)
    pending = None
    for l in reversed(range(DEPTH)):
        even = l % 2 == 0
        dxm, dxmb, dg_ffn[l], _, _, r1s = _backward_ffn(l, dx, dxb, saved[l], norm_ffn[l:l + 1], weights[l],
                                                        to_sibling(f"ffn{l}", col_sharded=False))
        ffn = chip_sums(held[f"ffn{l}"], r1s, [("ffn_in", l), ("ffn_out", l)], f"ffn{l}")
        if even:
            comm_a = _cross_comm(ffn[2])
            comm_b = _cross_comm(pending[2]) if pending is not None else None
        else:
            comm_a = _cross_comm(ffn[2] + (pending[2] if pending is not None else []))
            comm_b = None
        dx, dxb, dg_mix[l], _, _, db, got_a, got_b, r1s = _backward_attn(
            l, dxm, dxmb, saved[l], norm_mix[l:l + 1], weights[l], rope, brow[l // 2], comm_a, comm_b,
            to_sibling(f"mix{l}", col_sharded=True, odd_qkv=not even))
        update(ffn, got_a[:2], f"ffn{l}")
        if pending is not None:
            update(pending, got_b if even else got_a[2:], f"mix{l + 1}")
        if not even:
            db_f[l // 2] = db
        pending = chip_sums(held[f"mix{l}"], r1s,
                            [("qkv_even" if even else "qkvf_odd", l // 2), ("o_even" if even else "o_odd", l // 2)],
                            f"mix{l}")
    update(pending, _cross_comm(pending[2]).run("grads_to_chips_mix0"), "mix0")

    zeros = jnp.zeros((SMALL_ROWS - 11, D_MODEL), F32)
    db_row = _pad_row(jnp.concatenate([d[:, :N_HEADS] for d in db_f], axis=1))
    pack_g = jnp.concatenate(dg_mix + dg_ffn + [dg_final, db_row, _pad_row(loss_row[:, :1]), zeros], axis=0)
    tot = _all_reduce_small(pack_g, name="small_all_reduce")

    def pack(nm, nf, nfin, bf):
        return jnp.concatenate([nm, nf, nfin.reshape(1, -1), _pad_row(bf),
                                jnp.zeros((SMALL_ROWS - 10, D_MODEL), F32)], axis=0)

    d_s, m_s, v_s = _adamw_small(
        pack(norm_mix, norm_ffn, norm_final, b_forget), tot,
        pack(m_norm_mix, m_norm_ffn, m_norm_final, m_b_forget),
        pack(v_norm_mix, v_norm_ffn, v_norm_final, v_b_forget), name="adamw_small")

    def unpack(p):
        nb = b_forget.size
        return {"norm_mix": p[0:DEPTH], "norm_ffn": p[DEPTH:2 * DEPTH], "norm_final": p[2 * DEPTH],
                "b_forget": p[2 * DEPTH + 1, :nb].reshape(b_forget.shape)}

    small = [unpack(tot), unpack(d_s), unpack(m_s), unpack(v_s)]
    loss = tot[2 * DEPTH + 2, 0]

    order = ["norm_mix", "qkv_even", "o_even", "qkvf_odd", "b_forget", "o_odd", "norm_ffn", "ffn_in", "ffn_out",
             "norm_final"]
    outs = [loss, dx[None]]
    for t in range(4):
        for key in order:
            if key in small[t]:
                outs.append(small[t][key])
            elif key == "ffn_in":
                outs.append(jnp.swapaxes(results[key][t], 1, 2))
            else:
                outs.append(results[key][t])
    return tuple(outs)
```

```python
import jax
import jax.numpy as jnp
from jax import lax
from jax.experimental import pallas as pl
from jax.experimental.pallas import tpu as pltpu

F32 = jnp.float32
BF16 = jnp.bfloat16

D_MODEL = 1024
HEAD_DIM = 64
N_HEADS = 16
D_ATTN = N_HEADS * HEAD_DIM
D_FF = 2816
DEPTH = 4
ROPE_THETA = 500000.0
ROT_DIM = HEAD_DIM // 4
RMS_EPS = 1e-5
SCALE = HEAD_DIM ** -0.5
DIL_PATTERNS = ((128, 1), (512, 4), (2048, 16))
N_DEV = 8
QKVF_PAD = 3200

ADAM_LR = 0.001
ADAM_B1 = 0.9
ADAM_B2 = 0.999
ADAM_EPS = 1e-08
ADAM_WD = 0.01
ADAM_STEP = 10

LANES = 128
BLK = 128
TB = 256
NEG = -1e30
VMEM_LIMIT = 48 * 1024 * 1024

MESH = pl.DeviceIdType.MESH


def _params(n_grid=0, **kw):
    sem = ("arbitrary",) * n_grid if n_grid else None
    return pltpu.CompilerParams(dimension_semantics=sem, vmem_limit_bytes=VMEM_LIMIT, **kw)


def _mm(a, b, *, name, ta=False, tb=False, add=None, out_dtype=F32, tm=512, tn=512, tk=None, comm=None):
    a_planes, b_planes = a.ndim == 3, b.ndim == 3
    assert not (b_planes and tb)
    if a_planes and ta:
        m, k = a.shape[0] * a.shape[2], a.shape[1]
        tm = min(tm, a.shape[2])
        assert a.shape[2] % tm == 0
    elif a_planes:
        m, k = a.shape[1], a.shape[0] * a.shape[2]
        tk = a.shape[2] if tk is None else tk
        assert a.shape[2] % tk == 0
    else:
        m = a.shape[1] if ta else a.shape[0]
        k = a.shape[0] if ta else a.shape[1]
    if b_planes:
        n = b.shape[0] * b.shape[2]
        tn = min(tn, b.shape[2])
        assert b.shape[1] == k and b.shape[2] % tn == 0
    else:
        n = b.shape[0] if tb else b.shape[1]
        assert (b.shape[1] if tb else b.shape[0]) == k
    tm, tn = min(tm, m), min(tn, n)
    tk = k if tk is None else min(tk, k)
    assert m % tm == 0 and n % tn == 0 and k % tk == 0, (name, m, n, k, tm, tn, tk)
    nk = k // tk
    dn = (((0 if ta else 1,), (1 if tb else 0,)), ((), ()))

    def body(*refs):
        a_ref, b_ref = refs[0], refs[1]
        add_ref = refs[2] if add is not None else None
        o_ref = refs[3] if add is not None else refs[2]
        part = lax.dot_general(a_ref[...], b_ref[...], dn, preferred_element_type=F32)
        if nk == 1:
            if add_ref is not None:
                part = part + add_ref[...]
            o_ref[...] = part.astype(out_dtype)
            return
        acc_ref = refs[-1]
        kk = pl.program_id(2)

        @pl.when(kk == 0)
        def _():
            acc_ref[...] = part

        @pl.when(kk > 0)
        def _():
            acc_ref[...] += part

        @pl.when(kk == nk - 1)
        def _():
            res = acc_ref[...]
            if add_ref is not None:
                res = res + add_ref[...]
            o_ref[...] = res.astype(out_dtype)

    if a_planes and ta:
        a_per = a.shape[2] // tm
        a_spec = pl.BlockSpec((None, tk, tm), lambda i, j, kk: (i // a_per, kk, i % a_per))
    elif a_planes:
        a_per = a.shape[2] // tk
        a_spec = pl.BlockSpec((None, tm, tk), lambda i, j, kk: (kk // a_per, i, kk % a_per))
    elif ta:
        a_spec = pl.BlockSpec((tk, tm), lambda i, j, kk: (kk, i))
    else:
        a_spec = pl.BlockSpec((tm, tk), lambda i, j, kk: (i, kk))
    if b_planes:
        b_per = b.shape[2] // tn
        b_spec = pl.BlockSpec((None, tk, tn), lambda i, j, kk: (j // b_per, kk, j % b_per))
    elif tb:
        b_spec = pl.BlockSpec((tn, tk), lambda i, j, kk: (j, kk))
    else:
        b_spec = pl.BlockSpec((tk, tn), lambda i, j, kk: (kk, j))
    o_spec = pl.BlockSpec((tm, tn), lambda i, j, kk: (i, j))
    in_specs = [a_spec, b_spec] + ([o_spec] if add is not None else [])
    args = (a, b) + ((add,) if add is not None else ())
    (out,), got = _call_hosting(
        body, name=name, grid=(m // tm, n // tn, nk), in_specs=in_specs, args=args, out_specs=[o_spec],
        out_shape=[jax.ShapeDtypeStruct((m, n), out_dtype)],
        scratch=[pltpu.VMEM((tm, tn), F32)] if nk > 1 else [], comm=comm)
    return out if comm is None else (out, got)


def _rms_fwd(x, g, *, name, tr=256):
    s, d = x.shape

    def body(x_ref, g_ref, h_ref):
        xv = x_ref[...]
        r = lax.rsqrt(jnp.mean(xv * xv, axis=-1, keepdims=True) + RMS_EPS)
        h_ref[...] = (xv * r * g_ref[...]).astype(BF16)

    return pl.pallas_call(
        body, name=name, grid=(s // tr,),
        in_specs=[pl.BlockSpec((tr, d), lambda i: (i, 0)), pl.BlockSpec((1, d), lambda i: (0, 0))],
        out_specs=pl.BlockSpec((tr, d), lambda i: (i, 0)),
        out_shape=jax.ShapeDtypeStruct((s, d), BF16),
        compiler_params=_params(1),
    )(x, g)


def _rms_bwd(x, g, dh, dres, *, name, tr=256):
    s, d = x.shape

    def body(x_ref, g_ref, dh_ref, dres_ref, dx_ref, dxb_ref, dg_ref):
        xv = x_ref[...]
        r = lax.rsqrt(jnp.mean(xv * xv, axis=-1, keepdims=True) + RMS_EPS)
        y = xv * r
        dhv = dh_ref[...]
        dy = dhv * g_ref[...]
        dx = dres_ref[...] + r * (dy - y * jnp.mean(dy * y, axis=-1, keepdims=True))
        dx_ref[...] = dx
        dxb_ref[...] = dx.astype(BF16)
        part = jnp.sum(dhv * y, axis=0, keepdims=True)

        @pl.when(pl.program_id(0) == 0)
        def _():
            dg_ref[...] = part

        @pl.when(pl.program_id(0) > 0)
        def _():
            dg_ref[...] += part

    row = pl.BlockSpec((tr, d), lambda i: (i, 0))
    vec = pl.BlockSpec((1, d), lambda i: (0, 0))
    return pl.pallas_call(
        body, name=name, grid=(s // tr,),
        in_specs=[row, vec, row, row], out_specs=[row, row, vec],
        out_shape=[jax.ShapeDtypeStruct((s, d), F32), jax.ShapeDtypeStruct((s, d), BF16),
                   jax.ShapeDtypeStruct((1, d), F32)],
        compiler_params=_params(1),
    )(x, g, dh, dres)


def _final_loss(x, g, tgt, *, name, tr=256):
    s, d = x.shape

    def body(x_ref, g_ref, t_ref, loss_ref, dx_ref, dxb_ref, dg_ref):
        xv = x_ref[...]
        gv = g_ref[...]
        r = lax.rsqrt(jnp.mean(xv * xv, axis=-1, keepdims=True) + RMS_EPS)
        y = xv * r
        err = y * gv - t_ref[...]
        lpart = 0.5 * jnp.sum(jnp.mean(err * err, axis=-1, keepdims=True), axis=0, keepdims=True)
        dh = err * (1.0 / d)
        dy = dh * gv
        dx = r * (dy - y * jnp.mean(dy * y, axis=-1, keepdims=True))
        dx_ref[...] = dx
        dxb_ref[...] = dx.astype(BF16)
        gpart = jnp.sum(dh * y, axis=0, keepdims=True)
        lrow = jnp.broadcast_to(lpart, (1, LANES))

        @pl.when(pl.program_id(0) == 0)
        def _():
            dg_ref[...] = gpart
            loss_ref[...] = lrow

        @pl.when(pl.program_id(0) > 0)
        def _():
            dg_ref[...] += gpart
            loss_ref[...] += lrow

    row = pl.BlockSpec((tr, d), lambda i: (i, 0))
    vec = pl.BlockSpec((1, d), lambda i: (0, 0))
    lsp = pl.BlockSpec((1, LANES), lambda i: (0, 0))
    return pl.pallas_call(
        body, name=name, grid=(s // tr,),
        in_specs=[row, vec, row], out_specs=[lsp, row, row, vec],
        out_shape=[jax.ShapeDtypeStruct((1, LANES), F32), jax.ShapeDtypeStruct((s, d), F32),
                   jax.ShapeDtypeStruct((s, d), BF16), jax.ShapeDtypeStruct((1, d), F32)],
        compiler_params=_params(1),
    )(x, g, tgt)


FF_BLK = D_FF // 2


def _ffn_in_fwd(h, win_t, *, name, tm=512):
    s, d = h.shape

    def body(h_ref, wg_ref, wu_ref, g_ref, u_ref, a_ref):
        hv = h_ref[...]
        g = _nt(hv, wg_ref[...])
        u = _nt(hv, wu_ref[...])
        g_ref[...] = g
        u_ref[...] = u
        a_ref[...] = (g * jax.nn.sigmoid(g) * u).astype(BF16)

    blk = pl.BlockSpec((tm, FF_BLK), lambda i, j: (i, j))
    f32 = jax.ShapeDtypeStruct((s, D_FF), F32)
    return pl.pallas_call(
        body, name=name, grid=(s // tm, 2),
        in_specs=[pl.BlockSpec((tm, d), lambda i, j: (i, 0)),
                  pl.BlockSpec((FF_BLK, d), lambda i, j: (j, 0)),
                  pl.BlockSpec((FF_BLK, d), lambda i, j: (j + 2, 0))],
        out_specs=[blk, blk, blk],
        out_shape=[f32, f32, jax.ShapeDtypeStruct((s, D_FF), BF16)],
        compiler_params=_params(2),
    )(h, win_t, win_t)


def _swiglu_bwd(da, g, u, *, name, tr=256):
    s = g.shape[0]

    def body(da_ref, g_ref, u_ref, o_ref):
        gv = g_ref[...]
        dav = da_ref[...]
        sg = jax.nn.sigmoid(gv)
        o_ref[0] = (dav * u_ref[...] * (sg * (1.0 + gv * (1.0 - sg)))).astype(BF16)
        o_ref[1] = (dav * gv * sg).astype(BF16)

    return pl.pallas_call(
        body, name=name, grid=(s // tr, 2),
        in_specs=[pl.BlockSpec((tr, FF_BLK), lambda i, j: (i, j))] * 3,
        out_specs=pl.BlockSpec((2, tr, FF_BLK), lambda i, j: (0, i, j)),
        out_shape=jax.ShapeDtypeStruct((2, s, D_FF), BF16),
        compiler_params=_params(2),
    )(da, g, u)


def _split3(x):
    hi = x.astype(BF16)
    r1 = x - hi.astype(F32)
    mid = r1.astype(BF16)
    lo = (r1 - mid.astype(F32)).astype(BF16)
    return hi, mid, lo


def _dot3(x, m_bf):
    hi, mid, lo = _split3(x)
    return (jnp.dot(hi, m_bf, preferred_element_type=F32)
            + jnp.dot(mid, m_bf, preferred_element_type=F32)
            + jnp.dot(lo, m_bf, preferred_element_type=F32))


def _dot3_left(m_bf, x):
    hi, mid, lo = _split3(x)
    return (jnp.dot(m_bf, hi, preferred_element_type=F32)
            + jnp.dot(m_bf, mid, preferred_element_type=F32)
            + jnp.dot(m_bf, lo, preferred_element_type=F32))


def _dot2(x, m_bf):
    hi = x.astype(BF16)
    lo = (x - hi.astype(F32)).astype(BF16)
    return jnp.dot(hi, m_bf, preferred_element_type=F32) + jnp.dot(lo, m_bf, preferred_element_type=F32)


def _nt(a, b):
    return lax.dot_general(a, b, (((1,), (1,)), ((), ())), preferred_element_type=F32)


def _mm32(a, b):
    return jnp.dot(a, b, preferred_element_type=F32)


def _iota2(shape, dim):
    return lax.broadcasted_iota(jnp.int32, shape, dim)


def _rope_tables(s):
    half = ROT_DIM // 2
    pos = jnp.arange(s, dtype=F32)
    inv_freq = ROPE_THETA ** (-jnp.arange(half, dtype=F32) * 2.0 / ROT_DIM)
    ang = pos[:, None] * inv_freq[None, :]
    cos, sin = jnp.cos(ang), jnp.sin(ang)
    ones = jnp.ones((s, HEAD_DIM - ROT_DIM), F32)
    cos_t = jnp.concatenate([cos, cos, ones], axis=1)
    sin_t = jnp.concatenate([-sin, sin, 0.0 * ones], axis=1)
    idx = jnp.arange(HEAD_DIM)
    partner = jnp.where(idx < half, idx + half, idx - half)
    swap = ((idx[:, None] == partner[None, :]) & (idx[None, :] < ROT_DIM)).astype(F32)
    swap2 = jnp.kron(jnp.eye(2, dtype=F32), swap).astype(BF16)
    return jnp.tile(cos_t, (1, 2)), jnp.tile(sin_t, (1, 2)), swap2


def _rope(x, cos_t, sin_t, swap):
    return x * cos_t + _dot3(x, swap) * sin_t


def _rope_t(g, cos_t, sin_t, swap):
    return g * cos_t + _dot3(g * sin_t, swap)


def _dil_weight(dlt):
    nonneg = dlt >= 0
    w = jnp.zeros(dlt.shape, F32)
    for window, dil in DIL_PATTERNS:
        ok = nonneg & (dlt <= window) & ((dlt & (dil - 1)) == 0)
        w = w + ok.astype(F32)
    return w


FAR_TILES = 3
assert (FAR_TILES - 1) * TB + 1 > DIL_PATTERNS[1][0] and DIL_PATTERNS[2][0] >= 2048


def _dil_bias_scratch():
    return pltpu.VMEM((FAR_TILES + 1, TB, TB), F32)


def _dil_bias_tiles(bias_ref):
    rmc = _iota2((TB, TB), 0) - _iota2((TB, TB), 1)
    for d in range(FAR_TILES + 1):
        w = _dil_weight(d * TB + rmc)
        bias_ref[d] = jnp.where(w > 0.0, jnp.log(jnp.maximum(w, 1.0)), NEG)


def _log_sig_pair(z):
    sp = jnp.log(1.0 + jnp.exp(-jnp.abs(z)))
    return jnp.minimum(z, 0.0) - sp, -jnp.maximum(z, 0.0) - sp


def _log_one_minus_beta(z):
    return -(jnp.maximum(z, 0.0) + jnp.log(1.0 + jnp.exp(-jnp.abs(z))))


def _pair_masks(x, lane_lo):
    z = jnp.zeros_like(x)
    return jnp.where(lane_lo, x, z).astype(BF16), jnp.where(lane_lo, z, x).astype(BF16)


def _rows(i):
    return pl.ds(pl.multiple_of(i * TB, TB), TB)


def _head_spec(s, col0):
    return pl.BlockSpec((s, LANES), lambda p: (0, col0 + p))


def _stat_spec(s):
    return pl.BlockSpec((2, s, 1), lambda p: (p, 0, 0))


def _rowstat_spec(s):
    return pl.BlockSpec((2, 1, s), lambda p: (p, 0, 0))


def _full_spec(shape):
    nd = len(shape)
    return pl.BlockSpec(shape, lambda p: (0,) * nd)


K_COL, V_COL = D_ATTN // LANES, 2 * D_ATTN // LANES


def _bwd_scratch(s):
    return ([pltpu.VMEM((s, LANES), BF16)] * 8 + [pltpu.VMEM((LANES, s), BF16)] * 4
            + [pltpu.VMEM((LANES, s), F32)] * 2)


def _bwd_prep(i, q, k, v, dov, scr, lane_lo, sub_lo):
    qlo, qhi, klo, khi, kbf, vbf, dolo, dohi, qtlo, qthi, dotlo, dothi = scr[:12]
    rows = _rows(i)
    qs = q * SCALE
    qlo[rows, :], qhi[rows, :] = _pair_masks(qs, lane_lo)
    klo[rows, :], khi[rows, :] = _pair_masks(k * SCALE, lane_lo)
    kbf[rows, :] = k.astype(BF16)
    vbf[rows, :] = v.astype(BF16)
    dolo[rows, :], dohi[rows, :] = _pair_masks(dov, lane_lo)
    qtlo[:, rows], qthi[:, rows] = _pair_masks(qs.T, sub_lo)
    dotlo[:, rows], dothi[:, rows] = _pair_masks(dov.T, sub_lo)


def _sb_fwd(qkv, n_pairs, *, name, comm=None):
    s = qkv.shape[0]
    assert s % TB == 0
    nq = s // TB

    def body(q_ref, k_ref, v_ref, o_ref, ct_ref, qlo, qhi, kbf, vlo, vhi, sb0, sb1):
        sbuf = (sb0, sb1)
        lane_lo = _iota2((TB, LANES), 1) < HEAD_DIM

        def prep(i, _):
            rows = _rows(i)
            qlo[rows, :], qhi[rows, :] = _pair_masks(q_ref[rows, :] * SCALE, lane_lo)
            kbf[rows, :] = k_ref[rows, :].astype(BF16)
            vlo[rows, :], vhi[rows, :] = _pair_masks(v_ref[rows, :], lane_lo)
            return 0

        lax.fori_loop(0, nq, prep, 0)
        rmc = _iota2((TB, TB), 0) - _iota2((TB, TB), 1)
        strict = rmc > 0
        u_ge = (rmc >= 0).astype(BF16)
        qm, vm = (qlo, qhi), (vlo, vhi)

        def qloop(i, _):
            rows = _rows(i)

            def logits(kb, carry, diag):
                c = list(carry)
                keys = _rows(kb)
                k = kbf[keys, :]
                zs = [_nt(qm[h][rows, :], k) for h in range(2)]
                lms = [_log_one_minus_beta(z) for z in zs]
                if diag:
                    lms = [jnp.where(strict, lm, 0.0) for lm in lms]
                r_ins = [_dot2(lm, u_ge) for lm in lms]
                for h in range(2):
                    la = zs[h] + r_ins[h] + c[h]
                    sbuf[h][:, keys] = jnp.where(strict, la, NEG) if diag else la
                    c[h] = c[h] + r_ins[h][:, 0:1]
                return tuple(c)

            z1 = jnp.zeros((TB, 1), F32)
            c0, c1 = lax.fori_loop(0, i, lambda t, cr: logits(i - 1 - t, cr, False), logits(i, (z1, z1), True))

            def weigh(kb, acc):
                keys = _rows(kb)
                a_bf = [jnp.exp(sbuf[h][:, keys]).astype(BF16) for h in range(2)]
                return acc + _mm32(a_bf[0], vm[0][keys, :]) + _mm32(a_bf[1], vm[1][keys, :])

            acc = lax.fori_loop(0, i + 1, weigh, jnp.zeros((TB, LANES), F32))
            o_ref[rows, :] = acc
            ct_ref[0, rows, :] = c0
            ct_ref[1, rows, :] = c1
            return 0

        lax.fori_loop(0, nq, qloop, 0)

    return _call_pairs(
        body, name=name, n_pairs=n_pairs, comm=comm,
        in_specs=[_head_spec(s, 0), _head_spec(s, K_COL), _head_spec(s, V_COL)], args=(qkv, qkv, qkv),
        out_specs=[_head_spec(s, 0), _stat_spec(s)],
        out_shape=[jax.ShapeDtypeStruct((s, LANES * n_pairs), F32),
                   jax.ShapeDtypeStruct((2 * n_pairs, s, 1), F32)],
        scratch=[pltpu.VMEM((s, LANES), BF16)] * 5 + [pltpu.VMEM((TB, s), F32)] * 2)


def _sb_bwd(qkv, do, ctot, n_pairs, do_col0, *, name, comm=None):
    s = qkv.shape[0]
    assert s % TB == 0
    nq = s // TB

    def body(q_ref, k_ref, v_ref, do_ref, ct_ref, dq_ref, dk_ref, dv_ref, *scr):
        qlo, qhi, klo, khi, kbf, vbf, dolo, dohi, qtlo, qthi, dotlo, dothi, dkt, dvt = scr
        lane_lo = _iota2((TB, LANES), 1) < HEAD_DIM
        sub_lo = _iota2((LANES, TB), 0) < HEAD_DIM

        def prep(i, _):
            rows = _rows(i)
            _bwd_prep(i, q_ref[rows, :], k_ref[rows, :], v_ref[rows, :], do_ref[rows, :], scr, lane_lo, sub_lo)
            return 0

        lax.fori_loop(0, nq, prep, 0)
        dkt[...] = jnp.zeros_like(dkt)
        dvt[...] = jnp.zeros_like(dvt)
        rmc = _iota2((TB, TB), 0) - _iota2((TB, TB), 1)
        strict = rmc > 0
        u_le = (rmc <= 0).astype(BF16)
        qm, km, dom, qtm, dotm = (qlo, qhi), (klo, khi), (dolo, dohi), (qtlo, qthi), (dotlo, dothi)

        def qloop(i, _):
            rows = _rows(i)
            ct = (ct_ref[0, rows, :], ct_ref[1, rows, :])

            def tile(kb, carry, diag):
                pre, hl, dq = list(carry[0:2]), list(carry[2:4]), carry[4]
                keys = _rows(kb)
                k, v = kbf[keys, :], vbf[keys, :]
                zs = [_nt(qm[h][rows, :], k) for h in range(2)]
                das = [_nt(dom[h][rows, :], v) for h in range(2)]
                lms = [_log_one_minus_beta(z) for z in zs]
                if diag:
                    lms = [jnp.where(strict, lm, 0.0) for lm in lms]
                pins = [_dot2(lm, u_le) for lm in lms]
                gs, lbs = [], []
                a_bf = []
                for h in range(2):
                    lb = zs[h] + lms[h]
                    a = jnp.exp(lb + (ct[h] - pre[h]) - pins[h])
                    if diag:
                        a = jnp.where(strict, a, 0.0)
                    gs.append(a * das[h])
                    lbs.append(lb)
                    a_bf.append(a.astype(BF16))
                hins = [_dot2(g, u_le) for g in gs]
                dk_t, dv_t = dkt[:, keys], dvt[:, keys]
                for h in range(2):
                    g = gs[h]
                    dz = g - jnp.exp(lbs[h]) * (hl[h] + hins[h])
                    if diag:
                        dz = jnp.where(strict, dz, 0.0)
                    dzb = dz.astype(BF16)
                    dq = dq + _mm32(dzb, km[h][keys, :])
                    dk_t = dk_t + _mm32(qtm[h][:, rows], dzb)
                    dv_t = dv_t + _mm32(dotm[h][:, rows], a_bf[h])
                    pre[h] = pre[h] + pins[h][:, TB - 1:TB]
                    hl[h] = hl[h] + hins[h][:, TB - 1:TB]
                dkt[:, keys] = dk_t
                dvt[:, keys] = dv_t
                return pre[0], pre[1], hl[0], hl[1], dq

            z1 = jnp.zeros((TB, 1), F32)
            carry = lax.fori_loop(0, i, lambda kb, cr: tile(kb, cr, False),
                                  (z1, z1, z1, z1, jnp.zeros((TB, LANES), F32)))
            dq = tile(i, carry, True)[4]
            dq_ref[rows, :] = dq.astype(BF16)
            return 0

        lax.fori_loop(0, nq, qloop, 0)

        def wloop(i, _):
            rows = _rows(i)
            dk_ref[rows, :] = dkt[:, rows].T.astype(BF16)
            dv_ref[rows, :] = dvt[:, rows].T.astype(BF16)
            return 0

        lax.fori_loop(0, nq, wloop, 0)

    out = jax.ShapeDtypeStruct((s, LANES * n_pairs), BF16)
    return _call_pairs(
        body, name=name, n_pairs=n_pairs, comm=comm,
        in_specs=[_head_spec(s, 0), _head_spec(s, K_COL), _head_spec(s, V_COL),
                  _head_spec(s, do_col0), _stat_spec(s)], args=(qkv, qkv, qkv, do, ctot),
        out_specs=[_head_spec(s, 0)] * 3, out_shape=[out, out, out], scratch=_bwd_scratch(s))


def _bias_fwd(mode, qkv, head0_col, n_pairs, extra, *, name, comm=None):
    s = qkv.shape[0]
    assert s % TB == 0 and s <= DIL_PATTERNS[2][0]
    nq = s // TB
    fox = mode == "fox"

    def body(q_ref, k_ref, v_ref, e0, e1, *rest):
        if fox:
            o_ref, lse_ref, qlo, qhi, kbf, vx0, vx1, sb0, sb1 = rest
        else:
            e2, o_ref, lse_ref, qlo, qhi, kbf, vx0, vx1, sb0, sb1, bias = rest
            _dil_bias_tiles(bias)
        sbuf = (sb0, sb1)
        lane_lo = _iota2((TB, LANES), 1) < HEAD_DIM

        def prep(i, _):
            rows = _rows(i)
            q, k, v = q_ref[rows, :], k_ref[rows, :], v_ref[rows, :]
            if not fox:
                c, sn, sw = e0[rows, :], e1[rows, :], e2[...]
                q, k = _rope(q, c, sn, sw), _rope(k, c, sn, sw)
            qlo[rows, :], qhi[rows, :] = _pair_masks(q * SCALE, lane_lo)
            kbf[rows, :] = k.astype(BF16)
            one = jnp.ones_like(v)
            vx0[rows, :] = jnp.where(lane_lo, v, one).astype(BF16)
            vx1[rows, :] = jnp.where(lane_lo, one, v).astype(BF16)
            return 0

        lax.fori_loop(0, nq, prep, 0)
        rmc = _iota2((TB, TB), 0) - _iota2((TB, TB), 1)
        qm, vx = (qlo, qhi), (vx0, vx1)

        def qloop(i, _):
            rows = _rows(i)
            if fox:
                fq = (e0[0, rows, :], e0[1, rows, :])

            def scores(kb, carry, diag):
                keys = _rows(kb)
                k = kbf[keys, :]
                scs = [_nt(qm[h][rows, :], k) for h in range(2)]
                if not fox:
                    b = bias[jnp.minimum(i - kb, FAR_TILES)]
                out = []
                for h in range(2):
                    if fox:
                        sc = scs[h] + (fq[h] - e1[h, :, keys])
                        if diag:
                            sc = jnp.where(rmc >= 0, sc, NEG)
                    else:
                        sc = scs[h] + b
                    sbuf[h][:, keys] = sc
                    mx = carry[h]
                    for j in range(TB // LANES):
                        mx = jnp.maximum(mx, sc[:, j * LANES:(j + 1) * LANES])
                    out.append(mx)
                return tuple(out)

            mx0 = jnp.full((TB, LANES), NEG, F32)
            mxs = lax.fori_loop(0, i, lambda kb, cr: scores(kb, cr, False), (mx0, mx0))
            mxs = scores(i, mxs, True)
            m_0, m_1 = (jnp.max(mx, axis=1, keepdims=True) for mx in mxs)

            def weigh(kb, carry):
                keys = _rows(kb)
                ps = [jnp.exp(sbuf[h][:, keys] - m).astype(BF16) for h, m in enumerate((m_0, m_1))]
                return tuple(carry[h] + _mm32(ps[h], vx[h][keys, :]) for h in range(2))

            a0 = jnp.zeros((TB, LANES), F32)
            acc0, acc1 = lax.fori_loop(0, i + 1, weigh, (a0, a0))
            l0, l1 = acc0[:, HEAD_DIM:HEAD_DIM + 1], acc1[:, 0:1]
            o_ref[rows, :] = jnp.where(lane_lo, acc0 / l0, acc1 / l1)
            lse_ref[0, rows, :] = m_0 + jnp.log(l0)
            lse_ref[1, rows, :] = m_1 + jnp.log(l1)
            return 0

        lax.fori_loop(0, nq, qloop, 0)

    hp0 = head0_col
    if fox:
        e_specs = [_stat_spec(s), _rowstat_spec(s)]
    else:
        e_specs = [_full_spec((s, LANES)), _full_spec((s, LANES)), _full_spec((LANES, LANES))]
    return _call_pairs(
        body, name=name, n_pairs=n_pairs, comm=comm,
        in_specs=[_head_spec(s, hp0), _head_spec(s, K_COL + hp0), _head_spec(s, V_COL + hp0)] + e_specs,
        args=(qkv, qkv, qkv, *extra),
        out_specs=[_head_spec(s, 0), _stat_spec(s)],
        out_shape=[jax.ShapeDtypeStruct((s, LANES * n_pairs), F32),
                   jax.ShapeDtypeStruct((2 * n_pairs, s, 1), F32)],
        scratch=([pltpu.VMEM((s, LANES), BF16)] * 5 + [pltpu.VMEM((TB, s), F32)] * 2
                 + ([] if fox else [_dil_bias_scratch()])))


def _bias_bwd(mode, qkv, head0_col, n_pairs, extra, o, do, do_col0, lse, *, name, comm=None):
    s = qkv.shape[0]
    assert s % TB == 0 and s <= DIL_PATTERNS[2][0]
    nq = s // TB
    fox = mode == "fox"

    def body(q_ref, k_ref, v_ref, o_ref, do_ref, lse_ref, e0, e1, *rest):
        if fox:
            dq_ref, dk_ref, dv_ref, dfr_ref, dfc_ref = rest[:5]
            scr = rest[5:]
        else:
            e2, dq_ref, dk_ref, dv_ref = rest[:4]
            scr = rest[4:]
        qlo, qhi, klo, khi, kbf, vbf, dolo, dohi, qtlo, qthi, dotlo, dothi, dkt, dvt = scr[:14]
        if not fox:
            bias = scr[14]
            _dil_bias_tiles(bias)
        lane_lo = _iota2((TB, LANES), 1) < HEAD_DIM
        sub_lo = _iota2((LANES, TB), 0) < HEAD_DIM

        def prep(i, _):
            rows = _rows(i)
            q, k = q_ref[rows, :], k_ref[rows, :]
            if not fox:
                c, sn, sw = e0[rows, :], e1[rows, :], e2[...]
                q, k = _rope(q, c, sn, sw), _rope(k, c, sn, sw)
            _bwd_prep(i, q, k, v_ref[rows, :], do_ref[rows, :], scr, lane_lo, sub_lo)
            return 0

        lax.fori_loop(0, nq, prep, 0)
        dkt[...] = jnp.zeros_like(dkt)
        dvt[...] = jnp.zeros_like(dvt)
        if fox:
            dfr_ref[...] = jnp.zeros_like(dfr_ref)
        rmc = _iota2((TB, TB), 0) - _iota2((TB, TB), 1)
        qm, km, dom, qtm, dotm = (qlo, qhi), (klo, khi), (dolo, dohi), (qtlo, qthi), (dotlo, dothi)

        def qloop(i, _):
            rows = _rows(i)
            prod = do_ref[rows, :] * o_ref[rows, :]
            dsum = (jnp.sum(jnp.where(lane_lo, prod, 0.0), axis=1, keepdims=True),
                    jnp.sum(jnp.where(lane_lo, 0.0, prod), axis=1, keepdims=True))
            lse_i = (lse_ref[0, rows, :], lse_ref[1, rows, :])
            if fox:
                fql = (e0[0, rows, :] - lse_i[0], e0[1, rows, :] - lse_i[1])

            def tile(kb, carry, diag):
                dq, rs = carry[0], list(carry[1:])
                keys = _rows(kb)
                k, v = kbf[keys, :], vbf[keys, :]
                scs = [_nt(qm[h][rows, :], k) for h in range(2)]
                dps = [_nt(dom[h][rows, :], v) for h in range(2)]
                if not fox:
                    b = bias[jnp.minimum(i - kb, FAR_TILES)]
                ps, dss = [], []
                for h in range(2):
                    if fox:
                        sc = scs[h] + (fql[h] - e1[h, :, keys])
                        if diag:
                            sc = jnp.where(rmc >= 0, sc, NEG)
                    else:
                        sc = scs[h] + (b - lse_i[h])
                    p = jnp.exp(sc)
                    dss.append(p * (dps[h] - dsum[h]))
                    ps.append(p.astype(BF16))
                dk_t, dv_t = dkt[:, keys], dvt[:, keys]
                for h in range(2):
                    dsb = dss[h].astype(BF16)
                    dq = dq + _mm32(dsb, km[h][keys, :])
                    dk_t = dk_t + _mm32(qtm[h][:, rows], dsb)
                    dv_t = dv_t + _mm32(dotm[h][:, rows], ps[h])
                    if fox:
                        dfr_ref[h, :, keys] -= jnp.sum(dss[h], axis=0, keepdims=True)
                        for j in range(TB // LANES):
                            rs[h] = rs[h] + dss[h][:, j * LANES:(j + 1) * LANES]
                dkt[:, keys] = dk_t
                dvt[:, keys] = dv_t
                return (dq, *rs)

            z2 = jnp.zeros((TB, LANES), F32)
            carry = lax.fori_loop(0, i, lambda kb, cr: tile(kb, cr, False), (z2, z2, z2) if fox else (z2,))
            carry = tile(i, carry, True)
            if fox:
                dfc_ref[0, rows, :] = jnp.sum(carry[1], axis=1, keepdims=True)
                dfc_ref[1, rows, :] = jnp.sum(carry[2], axis=1, keepdims=True)
            dq = carry[0]
            if not fox:
                dq = _rope_t(dq, e0[rows, :], e1[rows, :], e2[...])
            dq_ref[rows, :] = dq.astype(BF16)
            return 0

        lax.fori_loop(0, nq, qloop, 0)

        def wloop(i, _):
            rows = _rows(i)
            dk = dkt[:, rows].T
            if not fox:
                dk = _rope_t(dk, e0[rows, :], e1[rows, :], e2[...])
            dk_ref[rows, :] = dk.astype(BF16)
            dv_ref[rows, :] = dvt[:, rows].T.astype(BF16)
            return 0

        lax.fori_loop(0, nq, wloop, 0)

    hp0 = head0_col
    out = jax.ShapeDtypeStruct((s, LANES * n_pairs), BF16)
    out_specs = [_head_spec(s, 0)] * 3
    out_shape = [out, out, out]
    if fox:
        e_specs = [_stat_spec(s), _rowstat_spec(s)]
        out_specs += [_rowstat_spec(s), _stat_spec(s)]
        out_shape += [jax.ShapeDtypeStruct((2 * n_pairs, 1, s), F32), jax.ShapeDtypeStruct((2 * n_pairs, s, 1), F32)]
    else:
        e_specs = [_full_spec((s, LANES)), _full_spec((s, LANES)), _full_spec((LANES, LANES))]
    return _call_pairs(
        body, name=name, n_pairs=n_pairs, comm=comm,
        in_specs=[_head_spec(s, hp0), _head_spec(s, K_COL + hp0), _head_spec(s, V_COL + hp0),
                  _head_spec(s, 0), _head_spec(s, do_col0), _stat_spec(s)] + e_specs,
        args=(qkv, qkv, qkv, o, do, lse, *extra),
        out_specs=out_specs, out_shape=out_shape,
        scratch=_bwd_scratch(s) + ([] if fox else [_dil_bias_scratch()]))


F_COL = 3 * D_ATTN // LANES


def _fgate_fwd(qkvf, brow, *, name):
    s = qkvf.shape[0]
    nb = s // BLK

    def body(f_ref, b_ref, fc_ref, fr_ref, fs):
        row, col = _iota2((BLK, BLK), 0), _iota2((BLK, BLK), 1)
        l_incl = (col <= row).astype(BF16)

        def step(i, carry):
            r0 = pl.multiple_of(i * BLK, BLK)
            lf, _ = _log_sig_pair(f_ref[pl.ds(r0, BLK), :] + b_ref[...])
            fblk = carry + _dot3_left(l_incl, lf)
            fs[pl.ds(r0, BLK), :] = fblk
            return fblk[BLK - 1:BLK, :]

        lax.fori_loop(0, nb, step, jnp.zeros((1, LANES), F32))
        ft = fs[...].T
        for h in range(N_HEADS):
            fc_ref[h, :, :] = fs[:, h:h + 1]
            fr_ref[h, :, :] = ft[h:h + 1, :]

    return pl.pallas_call(
        body, name=name, grid=(1,),
        in_specs=[pl.BlockSpec((s, LANES), lambda i: (0, F_COL)), pl.BlockSpec((1, LANES), lambda i: (0, 0))],
        out_specs=[pl.BlockSpec((N_HEADS, s, 1), lambda i: (0, 0, 0)),
                   pl.BlockSpec((N_HEADS, 1, s), lambda i: (0, 0, 0))],
        out_shape=[jax.ShapeDtypeStruct((N_HEADS, s, 1), F32), jax.ShapeDtypeStruct((N_HEADS, 1, s), F32)],
        scratch_shapes=[pltpu.VMEM((s, LANES), F32)],
        compiler_params=_params(1),
    )(qkvf, brow)


def _fgate_bwd(dfr, dfc, qkvf, brow, *, name):
    s = qkvf.shape[0]
    nb = s // BLK

    def body(dfr_ref, dfc_ref, f_ref, b_ref, dfl_ref, db_ref, ts, fs):
        ts[...] = jnp.zeros_like(ts)
        for h in range(N_HEADS):
            ts[h:h + 1, :] = dfr_ref[h]
        fs[...] = ts[...].T
        for h in range(N_HEADS):
            fs[:, h:h + 1] += dfc_ref[h]
        row, col = _iota2((BLK, BLK), 0), _iota2((BLK, BLK), 1)
        u_incl = (col >= row).astype(BF16)
        head_lane = _iota2((BLK, LANES), 1) < N_HEADS

        def step(ii, carry):
            tail, db = carry
            r0 = pl.multiple_of((nb - 1 - ii) * BLK, BLK)
            rblk = tail + _dot3_left(u_incl, fs[pl.ds(r0, BLK), :])
            _, lsn = _log_sig_pair(f_ref[pl.ds(r0, BLK), :] + b_ref[...])
            dfl = jnp.where(head_lane, rblk * jnp.exp(lsn), 0.0)
            dfl_ref[pl.ds(r0, BLK), :] = dfl.astype(BF16)
            return rblk[0:1, :], db + jnp.sum(dfl, axis=0, keepdims=True)

        z = jnp.zeros((1, LANES), F32)
        _, db = lax.fori_loop(0, nb, step, (z, z))
        db_ref[...] = db

    return pl.pallas_call(
        body, name=name, grid=(1,),
        in_specs=[pl.BlockSpec((N_HEADS, 1, s), lambda i: (0, 0, 0)), pl.BlockSpec((N_HEADS, s, 1), lambda i: (0, 0, 0)),
                  pl.BlockSpec((s, LANES), lambda i: (0, F_COL)), pl.BlockSpec((1, LANES), lambda i: (0, 0))],
        out_specs=[pl.BlockSpec((s, LANES), lambda i: (0, 0)), pl.BlockSpec((1, LANES), lambda i: (0, 0))],
        out_shape=[jax.ShapeDtypeStruct((s, LANES), BF16), jax.ShapeDtypeStruct((1, LANES), F32)],
        scratch_shapes=[pltpu.VMEM((LANES, s), F32), pltpu.VMEM((s, LANES), F32)],
        compiler_params=_params(1),
    )(dfr, dfc, qkvf, brow)


def _adamw_math(w, g, m, v):
    m2 = ADAM_B1 * m + (1.0 - ADAM_B1) * g
    v2 = ADAM_B2 * v + (1.0 - ADAM_B2) * (g * g)
    m_hat = m2 / (1.0 - ADAM_B1 ** ADAM_STEP)
    v_hat = v2 / (1.0 - ADAM_B2 ** ADAM_STEP)
    delta = -ADAM_LR * (m_hat / (jnp.sqrt(v_hat) + ADAM_EPS) + ADAM_WD * w)
    return delta, m2, v2


def _row_tile(r, cap=256, mult=16):
    best = None
    for t in range(mult, min(r, cap) + 1, mult):
        if r % t == 0:
            best = t
    assert best is not None, r
    return best


def _adamw_shard(w, m, v, lidx, g_all, r1, r2, sc, prev, *, name):
    nl, r, c = w.shape
    tr = _row_tile(r)

    def body(sc_ref, w_ref, m_ref, v_ref, g_ref, r1_ref, r2_ref, *rest):
        go_ref, d_ref, mo_ref, vo_ref = rest[-4:]
        g = g_ref[...] + r1_ref[...]
        g = g + r2_ref[0].astype(F32)
        g = g + r2_ref[1].astype(F32)
        g = g + r2_ref[2].astype(F32)
        delta, m2, v2 = _adamw_math(w_ref[...], g, m_ref[...], v_ref[...])
        go_ref[...] = g
        d_ref[...] = delta
        mo_ref[...] = m2
        vo_ref[...] = v2

    lay = pl.BlockSpec((None, tr, c), lambda i, s_: (lidx, i, 0))
    in_specs = [lay, lay, lay,
                pl.BlockSpec((None, tr, c), lambda i, s_: (s_[0], i, 0)),
                pl.BlockSpec((None, tr, c), lambda i, s_: (s_[1], i, 0)),
                pl.BlockSpec((3, tr, c), lambda i, s_: (0, i, 0))]
    args = [sc, w, m, v, g_all, r1, r2]
    aliases = {}
    if prev is not None:
        in_specs += [pl.BlockSpec(memory_space=pl.ANY)] * 4
        aliases = {7 + t: t for t in range(4)}
        args += list(prev)
    shp = jax.ShapeDtypeStruct((nl, r, c), F32)
    return pl.pallas_call(
        body, name=name,
        grid_spec=pltpu.PrefetchScalarGridSpec(
            num_scalar_prefetch=1, grid=(r // tr,), in_specs=in_specs, out_specs=[lay] * 4),
        out_shape=[shp] * 4, input_output_aliases=aliases,
        compiler_params=_params(1),
    )(*args)


def _adamw_small(w, g, m, v, *, name):
    def body(w_ref, g_ref, m_ref, v_ref, d_ref, mo_ref, vo_ref):
        delta, m2, v2 = _adamw_math(w_ref[...], g_ref[...], m_ref[...], v_ref[...])
        d_ref[...] = delta
        mo_ref[...] = m2
        vo_ref[...] = v2

    shp = jax.ShapeDtypeStruct(w.shape, F32)
    return pl.pallas_call(body, name=name, out_shape=[shp] * 3, compiler_params=_params())(w, g, m, v)


def _pos():
    return lax.axis_index("x"), lax.axis_index("y"), lax.axis_index("c")


def _other_chips(x, y):
    return [(1 - x, y), (x, 1 - y), (1 - x, 1 - y)]


def _dev_index(x, y, c):
    return 4 * x + 2 * y + c


HBM_SPEC = pl.BlockSpec(memory_space=pltpu.HBM)


class _Comm:
    def __init__(self, inputs, out_shape, scratch, start, mid, finish):
        self.inputs, self.out_shape, self.scratch = list(inputs), list(out_shape), list(scratch)
        self.start, self.mid, self.finish = start, mid, finish

    def run(self, name):
        n_in, n_out = len(self.inputs), len(self.out_shape)

        def body(*refs):
            parts = refs[:n_in], refs[n_in:n_in + n_out], refs[n_in + n_out:]
            self.start(*parts)
            self.mid(*parts)
            self.finish(*parts)

        return pl.pallas_call(
            body, name=name, in_specs=[HBM_SPEC] * n_in, out_specs=[HBM_SPEC] * n_out,
            out_shape=self.out_shape, scratch_shapes=self.scratch)(*self.inputs)


def _call_hosting(body, *, name, grid, in_specs, args, out_specs, out_shape, scratch, comm=None):
    if comm is None:
        res = pl.pallas_call(
            body, name=name, grid=grid, in_specs=in_specs, out_specs=out_specs, out_shape=out_shape,
            scratch_shapes=scratch, compiler_params=_params(len(grid)))(*args)
        return list(res), []
    sizes = (len(in_specs), len(comm.inputs), len(out_specs), len(comm.out_shape), len(scratch), len(comm.scratch))

    def fused(*refs):
        parts, o = [], 0
        for n in sizes:
            parts.append(refs[o:o + n])
            o += n
        h_in, c_in, h_out, c_out, h_scr, c_scr = parts
        first = last = None
        for d, n in enumerate(grid):
            p = pl.program_id(d)
            first = (p == 0) if first is None else jnp.logical_and(first, p == 0)
            last = (p == n - 1) if last is None else jnp.logical_and(last, p == n - 1)

        @pl.when(first)
        def _():
            comm.start(c_in, c_out, c_scr)

        @pl.when(last)
        def _():
            comm.mid(c_in, c_out, c_scr)

        body(*h_in, *h_out, *h_scr)

        @pl.when(last)
        def _():
            comm.finish(c_in, c_out, c_scr)

    res = pl.pallas_call(
        fused, name=name, grid=grid,
        in_specs=list(in_specs) + [HBM_SPEC] * sizes[1], out_specs=list(out_specs) + [HBM_SPEC] * sizes[3],
        out_shape=list(out_shape) + comm.out_shape, scratch_shapes=list(scratch) + comm.scratch,
        compiler_params=_params(len(grid)))(*args, *comm.inputs)
    return list(res[:sizes[2]]), list(res[sizes[2]:])


def _call_pairs(body, *, name, n_pairs, **kw):
    return _call_hosting(body, name=name, grid=(n_pairs,), **kw)


def _gather_comm(shards):
    n = len(shards)

    def plan(xs, outs, sems):
        send, recv, loc = sems
        x, y, c = _pos()
        me, sib = (x, y, c), (x, y, 1 - c)
        chips = _other_chips(x, y)

        def copy(a, k, block, to, src=None):
            dst = outs[a].at[_dev_index(*block)]
            return pltpu.make_async_remote_copy(
                src_ref=dst if src is None else src, dst_ref=dst,
                send_sem=send.at[a, k], recv_sem=recv.at[a, k], device_id=to, device_id_type=MESH)

        mine = [pltpu.make_async_copy(xs[a], outs[a].at[_dev_index(*me)], loc.at[a]) for a in range(n)]
        first = []
        for a in range(n):
            first.append(copy(a, 0, me, sib, src=xs[a]))
            first += [copy(a, 1 + j, me, (*chip, c), src=xs[a]) for j, chip in enumerate(chips)]
        passed = [(copy(a, 1 + j, (*chip, c), me), copy(a, 4 + j, (*chip, c), sib))
                  for j, chip in enumerate(chips) for a in range(n)]
        from_sib = [copy(a, 0, sib, me) for a in range(n)]
        from_sib += [copy(a, 4 + j, (*chip, 1 - c), me) for a in range(n) for j, chip in enumerate(chips)]
        return mine, first, passed, from_sib

    def start(xs, outs, sems):
        mine, first, _, _ = plan(xs, outs, sems)
        for cp in mine + first:
            cp.start()

    def mid(xs, outs, sems):
        for arrival, fwd in plan(xs, outs, sems)[2]:
            arrival.wait_recv()
            fwd.start()

    def finish(xs, outs, sems):
        mine, first, passed, from_sib = plan(xs, outs, sems)
        for cp in from_sib:
            cp.wait_recv()
        for cp in first + [fwd for _, fwd in passed]:
            cp.wait_send()
        for cp in mine:
            cp.wait()

    return _Comm(shards, [jax.ShapeDtypeStruct((N_DEV,) + a.shape, a.dtype) for a in shards],
                 [pltpu.SemaphoreType.DMA((n, 7)), pltpu.SemaphoreType.DMA((n, 7)), pltpu.SemaphoreType.DMA((n,))],
                 start, mid, finish)


def _sibling_comm(gs):
    n = len(gs)

    def plan(g_refs, r_refs, sems):
        send, recv = sems
        x, y, c = _pos()
        return [pltpu.make_async_remote_copy(
            src_ref=g_refs[a].at[_dev_index(k // 2, k % 2, 1 - c)], dst_ref=r_refs[a].at[k],
            send_sem=send.at[a, k], recv_sem=recv.at[a, k], device_id=(x, y, 1 - c), device_id_type=MESH)
            for a in range(n) for k in range(4)]

    def start(*parts):
        for cp in plan(*parts):
            cp.start()

    def mid(*parts):
        pass

    def finish(*parts):
        for cp in plan(*parts):
            cp.wait()

    return _Comm(gs, [jax.ShapeDtypeStruct((4,) + g.shape[1:], g.dtype) for g in gs],
                 [pltpu.SemaphoreType.DMA((n, 4)), pltpu.SemaphoreType.DMA((n, 4))], start, mid, finish)


def _rs_partial(g_all, r1, sc, *, name):
    _, r, c = g_all.shape
    tr = _row_tile(r)

    def body(sc_ref, g_ref, r_ref, o_ref):
        o_ref[...] = (g_ref[...] + r_ref[...]).astype(BF16)

    return pl.pallas_call(
        body, name=name,
        grid_spec=pltpu.PrefetchScalarGridSpec(
            num_scalar_prefetch=1, grid=(3, r // tr),
            in_specs=[pl.BlockSpec((None, tr, c), lambda j, i, s_: (s_[2 + j], i, 0)),
                      pl.BlockSpec((None, tr, c), lambda j, i, s_: (s_[5 + j], i, 0))],
            out_specs=pl.BlockSpec((None, tr, c), lambda j, i, s_: (j, i, 0))),
        out_shape=jax.ShapeDtypeStruct((3, r, c), BF16),
        compiler_params=_params(2),
    )(sc, g_all, r1)


def _cross_comm(ps):
    n = len(ps)

    def plan(p_refs, r_refs, sems):
        send, recv = sems
        x, y, c = _pos()
        return [pltpu.make_async_remote_copy(
            src_ref=p_refs[a].at[j], dst_ref=r_refs[a].at[j], send_sem=send.at[a, j], recv_sem=recv.at[a, j],
            device_id=(*chip, c), device_id_type=MESH)
            for j, chip in enumerate(_other_chips(x, y)) for a in range(n)]

    def start(*parts):
        for cp in plan(*parts):
            cp.start()

    def mid(*parts):
        pass

    def finish(*parts):
        for cp in plan(*parts):
            cp.wait()

    return _Comm(ps, [jax.ShapeDtypeStruct(p.shape, p.dtype) for p in ps],
                 [pltpu.SemaphoreType.DMA((n, 3)), pltpu.SemaphoreType.DMA((n, 3))], start, mid, finish)


SMALL_ROWS = 16


def _all_reduce_small(pack, *, name):
    def body(x_ref, o_ref, buf, send, recv):
        x, y, c = _pos()
        me = _dev_index(x, y, c)
        buf[me] = x_ref[...]
        copies = []
        for k in range(1, N_DEV):
            fx, fy, fc = (k >> 2) & 1, (k >> 1) & 1, k & 1
            peer = (1 - x if fx else x, 1 - y if fy else y, 1 - c if fc else c)
            copies.append(pltpu.make_async_remote_copy(
                src_ref=x_ref, dst_ref=buf.at[me], send_sem=send.at[k - 1], recv_sem=recv.at[k - 1],
                device_id=peer, device_id_type=MESH))
        for cp in copies:
            cp.start()
        for cp in copies:
            cp.wait()
        acc = buf[0]
        for d in range(1, N_DEV):
            acc = acc + buf[d]
        o_ref[...] = acc

    return pl.pallas_call(
        body, name=name,
        in_specs=[pl.BlockSpec(memory_space=pltpu.VMEM)], out_specs=pl.BlockSpec(memory_space=pltpu.VMEM),
        out_shape=jax.ShapeDtypeStruct(pack.shape, F32),
        scratch_shapes=[pltpu.VMEM((N_DEV,) + pack.shape, F32),
                        pltpu.SemaphoreType.DMA((N_DEV - 1,)), pltpu.SemaphoreType.DMA((N_DEV - 1,))],
    )(pack)


def _unshard_cols(g):
    return jnp.transpose(g, (1, 0, 2)).reshape(g.shape[1], N_DEV * g.shape[2])


def _shard_cols(w):
    k, n8 = w.shape
    return jnp.transpose(w.reshape(k, N_DEV, n8 // N_DEV), (1, 0, 2))


def _pad_row(v, width=D_MODEL):
    v = v.reshape(1, -1)
    return jnp.pad(v, ((0, 0), (0, width - v.shape[1])))


def _forward_mixer(l, xc, g_mix, wq, wo, rope, brow, comm_a=None, comm_b=None):
    even = l % 2 == 0
    h1 = _rms_fwd(xc, g_mix, name=f"norm_mix_fwd{l}")
    qkv = _mm(h1, wq, name=f"qkv_fwd{l}", tm=1024, tn=768 if even else 640)
    if even:
        (o_a, st_a), got_a = _sb_fwd(qkv, N_HEADS // 4, name=f"sb_fwd{l}", comm=comm_a)
        (o_b, st_b), got_b = _bias_fwd("dil", qkv, N_HEADS // 4, N_HEADS // 4, rope, name=f"dil_fwd{l}",
                                       comm=comm_b)
        o = jnp.concatenate([o_a, o_b], axis=1)
        att = (o_b, st_a, st_b)
    else:
        assert comm_b is None
        fcol, frow = _fgate_fwd(qkv, brow, name=f"fgate_fwd{l}")
        (o, lse), got_a = _bias_fwd("fox", qkv, 0, N_HEADS // 2, (fcol, frow), name=f"fox_fwd{l}", comm=comm_a)
        got_b = []
        att = (o, lse, fcol, frow)
    o_bf = o.astype(BF16)
    xm = _mm(o_bf, wo, add=xc, name=f"wo_fwd{l}", tm=512, tn=1024)
    return xm, (xc, h1, qkv, att, o_bf), got_a, got_b


def _forward_ffn(l, xm, g_ffn, win_t, wout):
    h2 = _rms_fwd(xm, g_ffn, name=f"norm_ffn_fwd{l}")
    g, u, a = _ffn_in_fwd(h2, win_t, name=f"ffn_in_fwd{l}")
    xo = _mm(a, wout, add=xm, name=f"ffn_out_fwd{l}", tm=512, tn=1024)
    return xo, (xm, h2, (g, u), a)


def _backward_ffn(l, dx, dxb, saved, g_ffn, w, exchange=None):
    _, _, win_t, wout = w
    _, _, _, _, _, xm, h2, gu, a = saved
    da = _mm(dxb, wout, tb=True, name=f"ffn_out_dx{l}", tm=1024, tn=FF_BLK)
    d_wout = _mm(a, dxb, ta=True, name=f"ffn_out_dw{l}", tm=FF_BLK, tn=512)
    dgu = _swiglu_bwd(da, *gu, name=f"swiglu_bwd{l}")
    d_win_t = _mm(dgu, h2, ta=True, name=f"ffn_in_dw{l}", tm=FF_BLK, tn=1024)
    comm = exchange(d_win_t, d_wout) if exchange is not None else None
    dh2 = _mm(dgu, win_t, name=f"ffn_in_dx{l}", tm=512, tn=1024, tk=FF_BLK, comm=comm)
    dh2, got = dh2 if comm is not None else (dh2, [])
    dxm, dxmb, dg_ffn = _rms_bwd(xm, g_ffn, dh2, dx, name=f"norm_ffn_bwd{l}")
    return dxm, dxmb, dg_ffn, d_win_t, d_wout, got


def _backward_attn(l, dxm, dxmb, saved, g_mix, w, rope, brow, comm_a=None, comm_b=None, exchange=None):
    wq, wo, _, _ = w
    xin, h1, qkv, att, o_bf, _, _, _, _ = saved
    even = l % 2 == 0
    d_wo = _mm(o_bf, dxmb, ta=True, name=f"wo_dw{l}", tm=512, tn=1024)
    do = _mm(dxmb, wo, tb=True, name=f"wo_dx{l}", tm=1024, tn=1024)
    db = None
    if even:
        o_b, st_a, st_b = att
        (dqa, dka, dva), got_a = _sb_bwd(qkv, do, st_a, N_HEADS // 4, 0, name=f"sb_bwd{l}", comm=comm_a)
        (dqb, dkb, dvb), got_b = _bias_bwd("dil", qkv, N_HEADS // 4, N_HEADS // 4, rope, o_b, do,
                                           N_HEADS // 4, st_b, name=f"dil_bwd{l}", comm=comm_b)
        dqkv = jnp.concatenate([dqa, dqb, dka, dkb, dva, dvb], axis=1)
    else:
        assert comm_b is None
        o, lse, fcol, frow = att
        (dq, dk, dv, dfr, dfc), got_a = _bias_bwd("fox", qkv, 0, N_HEADS // 2, (fcol, frow), o, do, 0, lse,
                                                  name=f"fox_bwd{l}", comm=comm_a)
        got_b = []
        dfl, db = _fgate_bwd(dfr, dfc, qkv, brow, name=f"fgate_bwd{l}")
        dqkv = jnp.concatenate([dq, dk, dv, dfl], axis=1)
    d_wq = _mm(h1, dqkv, ta=True, name=f"qkv_dw{l}", tm=1024, tn=768 if even else 640)
    comm = exchange(d_wq, d_wo) if exchange is not None else None
    dh1 = _mm(dqkv, wq, tb=True, name=f"qkv_dx{l}", tm=512, tn=1024, comm=comm)
    dh1, got_x = dh1 if comm is not None else (dh1, [])
    dx, dxb, dg_mix = _rms_bwd(xin, g_mix, dh1, dxm, name=f"norm_mix_bwd{l}")
    return dx, dxb, dg_mix, d_wq, d_wo, db, got_a, got_b, got_x


def kernel(x, norm_mix, w_qkv_even, w_o_even, w_qkvf_odd, b_forget, w_o_odd, norm_ffn, w_ffn_in, w_ffn_out, norm_final, loss_target, m_norm_mix, m_w_qkv_even, m_w_o_even, m_w_qkvf_odd, m_b_forget, m_w_o_odd, m_norm_ffn, m_w_ffn_in, m_w_ffn_out, m_norm_final, v_norm_mix, v_w_qkv_even, v_w_o_even, v_w_qkvf_odd, v_b_forget, v_w_o_odd, v_norm_ffn, v_w_ffn_in, v_w_ffn_out, v_norm_final):
    xi, yi, ci = _pos()
    others = _other_chips(xi, yi)
    sc = jnp.stack([_dev_index(xi, yi, ci), 2 * xi + yi]
                   + [_dev_index(px, py, ci) for px, py in others]
                   + [2 * px + py for px, py in others]).astype(jnp.int32)
    n_odd_cols = w_qkvf_odd.shape[2] * N_DEV

    xs, tgt = x[0], loss_target[0]
    rope = _rope_tables(xs.shape[0])
    brow = [_pad_row(b_forget[i], LANES) for i in range(DEPTH // 2)]
    w_in_t, m_in_t, v_in_t = (jnp.swapaxes(t, 1, 2) for t in (w_ffn_in, m_w_ffn_in, v_w_ffn_in))

    def shards(l):
        even = l % 2 == 0
        wq_s = (w_qkv_even if even else w_qkvf_odd)[l // 2]
        wo_s = (w_o_even if even else w_o_odd)[l // 2]
        return [wq_s.astype(BF16), wo_s.astype(BF16)], [w_in_t[l].astype(BF16), w_ffn_out[l].astype(BF16)]

    def full_mix(l, gq, go):
        wq = _unshard_cols(gq)
        if l % 2 == 1:
            wq = jnp.pad(wq, ((0, 0), (0, QKVF_PAD - n_odd_cols)))
        return wq, go.reshape(D_ATTN, D_MODEL)

    def full_ffn(gi, gout):
        return gi.reshape(2 * D_FF, D_MODEL), gout.reshape(D_FF, D_MODEL)

    mix0, ffn0 = shards(0)
    w_mix = {0: full_mix(0, *_gather_comm(mix0).run("gather_weights0"))}
    w_ffn = {}
    weights, saved = [], []
    xc = xs
    for l in range(DEPTH):
        comm_a = comm_b = None
        if l + 1 < DEPTH:
            mix_n, ffn_n = shards(l + 1)
            if l == 0:
                comm_a, comm_b = _gather_comm(ffn0 + mix_n), _gather_comm(ffn_n)
            elif l % 2 == 0:
                comm_a, comm_b = _gather_comm(ffn_n), _gather_comm(mix_n)
            else:
                comm_a = _gather_comm(mix_n + ffn_n)
        xm, sv_mix, got_a, got_b = _forward_mixer(l, xc, norm_mix[l:l + 1], *w_mix[l], rope, brow[l // 2],
                                                  comm_a, comm_b)
        if l + 1 < DEPTH:
            if l == 0:
                w_ffn[0] = full_ffn(*got_a[:2])
                w_mix[1], w_ffn[1] = full_mix(1, *got_a[2:]), full_ffn(*got_b)
            elif l % 2 == 0:
                w_mix[l + 1], w_ffn[l + 1] = full_mix(l + 1, *got_b), full_ffn(*got_a)
            else:
                w_mix[l + 1], w_ffn[l + 1] = full_mix(l + 1, *got_a[:2]), full_ffn(*got_a[2:])
        xc, sv_ffn = _forward_ffn(l, xm, norm_ffn[l:l + 1], *w_ffn[l])
        weights.append(w_mix[l] + w_ffn[l])
        saved.append(sv_mix + sv_ffn)

    loss_row, dx, dxb, dg_final = _final_loss(xc, norm_final.reshape(1, -1), tgt, name="final_loss")

    sharded = {
        "qkv_even": (w_qkv_even, m_w_qkv_even, v_w_qkv_even), "o_even": (w_o_even, m_w_o_even, v_w_o_even),
        "qkvf_odd": (w_qkvf_odd, m_w_qkvf_odd, v_w_qkvf_odd), "o_odd": (w_o_odd, m_w_o_odd, v_w_o_odd),
        "ffn_in": (w_in_t, m_in_t, v_in_t), "ffn_out": (w_ffn_out, m_w_ffn_out, v_w_ffn_out),
    }
    results = {k: None for k in sharded}

    def chip_sums(gs, r1s, keys, tag):
        ps = [_rs_partial(g, r1, sc, name=f"grads_chip_sum_{tag}_{a}") for a, (g, r1) in enumerate(zip(gs, r1s))]
        return gs, r1s, ps, keys

    held = {}

    def row_chunks(d):
        return d.reshape(N_DEV, d.shape[0] // N_DEV, D_MODEL)

    def to_sibling(tag, col_sharded, odd_qkv=False):
        def make(d_first, d_rows):
            if odd_qkv:
                d_first = d_first[:, :n_odd_cols]
            held[tag] = [_shard_cols(d_first) if col_sharded else row_chunks(d_first), row_chunks(d_rows)]
            return _sibling_comm(held[tag])
        return make

    def update(group, r2s, tag):
        gs, r1s, _, keys = group
        for a, (key, lidx) in enumerate(keys):
            w, m, v = sharded[key]
            results[key] = _adamw_shard(w, m, v, lidx, gs[a], r1s[a], r2s[a], sc, results[key],
                                        name=f"adamw_{key}_{tag}")

    dg_mix, dg_ffn, db_f = [None] * DEPTH, [None] * DEPTH, [None] * (DEPTH // 2)
    pending = None
    for l in reversed(range(DEPTH)):
        even = l % 2 == 0
        dxm, dxmb, dg_ffn[l], _, _, r1s = _backward_ffn(l, dx, dxb, saved[l], norm_ffn[l:l + 1], weights[l],
                                                        to_sibling(f"ffn{l}", col_sharded=False))
        ffn = chip_sums(held[f"ffn{l}"], r1s, [("ffn_in", l), ("ffn_out", l)], f"ffn{l}")
        if even:
            comm_a = _cross_comm(ffn[2])
            comm_b = _cross_comm(pending[2]) if pending is not None else None
        else:
            comm_a = _cross_comm(ffn[2] + (pending[2] if pending is not None else []))
            comm_b = None
        dx, dxb, dg_mix[l], _, _, db, got_a, got_b, r1s = _backward_attn(
            l, dxm, dxmb, saved[l], norm_mix[l:l + 1], weights[l], rope, brow[l // 2], comm_a, comm_b,
            to_sibling(f"mix{l}", col_sharded=True, odd_qkv=not even))
        update(ffn, got_a[:2], f"ffn{l}")
        if pending is not None:
            update(pending, got_b if even else got_a[2:], f"mix{l + 1}")
        if not even:
            db_f[l // 2] = db
        pending = chip_sums(held[f"mix{l}"], r1s,
                            [("qkv_even" if even else "qkvf_odd", l // 2), ("o_even" if even else "o_odd", l // 2)],
                            f"mix{l}")
    update(pending, _cross_comm(pending[2]).run("grads_to_chips_mix0"), "mix0")

    zeros = jnp.zeros((SMALL_ROWS - 11, D_MODEL), F32)
    db_row = _pad_row(jnp.concatenate([d[:, :N_HEADS] for d in db_f], axis=1))
    pack_g = jnp.concatenate(dg_mix + dg_ffn + [dg_final, db_row, _pad_row(loss_row[:, :1]), zeros], axis=0)
    tot = _all_reduce_small(pack_g, name="small_all_reduce")

    def pack(nm, nf, nfin, bf):
        return jnp.concatenate([nm, nf, nfin.reshape(1, -1), _pad_row(bf),
                                jnp.zeros((SMALL_ROWS - 10, D_MODEL), F32)], axis=0)

    d_s, m_s, v_s = _adamw_small(
        pack(norm_mix, norm_ffn, norm_final, b_forget), tot,
        pack(m_norm_mix, m_norm_ffn, m_norm_final, m_b_forget),
        pack(v_norm_mix, v_norm_ffn, v_norm_final, v_b_forget), name="adamw_small")

    def unpack(p):
        nb = b_forget.size
        return {"norm_mix": p[0:DEPTH], "norm_ffn": p[DEPTH:2 * DEPTH], "norm_final": p[2 * DEPTH],
                "b_forget": p[2 * DEPTH + 1, :nb].reshape(b_forget.shape)}

    small = [unpack(tot), unpack(d_s), unpack(m_s), unpack(v_s)]
    loss = tot[2 * DEPTH + 2, 0]

    order = ["norm_mix", "qkv_even", "o_even", "qkvf_odd", "b_forget", "o_odd", "norm_ffn", "ffn_in", "ffn_out",
             "norm_final"]
    outs = [loss, dx[None]]
    for t in range(4):
        for key in order:
            if key in small[t]:
                outs.append(small[t][key])
            elif key == "ffn_in":
                outs.append(jnp.swapaxes(results[key][t], 1, 2))
            else:
                outs.append(results[key][t])
    return tuple(outs)
```

```python
import jax
import jax.numpy as jnp
from jax import lax
from jax.experimental import pallas as pl
from jax.experimental.pallas import tpu as pltpu

F32 = jnp.float32
BF16 = jnp.bfloat16

D_MODEL = 1024
HEAD_DIM = 64
N_HEADS = 16
D_ATTN = N_HEADS * HEAD_DIM
D_FF = 2816
DEPTH = 4
ROPE_THETA = 500000.0
ROT_DIM = HEAD_DIM // 4
RMS_EPS = 1e-5
SCALE = HEAD_DIM ** -0.5
DIL_PATTERNS = ((128, 1), (512, 4), (2048, 16))
N_DEV = 8
QKVF_PAD = 3200

ADAM_LR = 0.001
ADAM_B1 = 0.9
ADAM_B2 = 0.999
ADAM_EPS = 1e-08
ADAM_WD = 0.01
ADAM_STEP = 10

LANES = 128
BLK = 128
TB = 256
NEG = -1e30
VMEM_LIMIT = 48 * 1024 * 1024

MESH = pl.DeviceIdType.MESH


def _params(n_grid=0, **kw):
    sem = ("arbitrary",) * n_grid if n_grid else None
    return pltpu.CompilerParams(dimension_semantics=sem, vmem_limit_bytes=VMEM_LIMIT, **kw)


def _mm(a, b, *, name, ta=False, tb=False, add=None, out_dtype=F32, tm=512, tn=512, tk=None, comm=None,
        out_planes=None):
    a_planes, b_planes = a.ndim == 3, b.ndim == 3
    assert not (b_planes and tb)
    if a_planes and ta:
        m, k = a.shape[0] * a.shape[2], a.shape[1]
        tm = min(tm, a.shape[2])
        assert a.shape[2] % tm == 0
    elif a_planes:
        m, k = a.shape[1], a.shape[0] * a.shape[2]
        tk = a.shape[2] if tk is None else tk
        assert a.shape[2] % tk == 0
    else:
        m = a.shape[1] if ta else a.shape[0]
        k = a.shape[0] if ta else a.shape[1]
    if b_planes:
        n = b.shape[0] * b.shape[2]
        tn = min(tn, b.shape[2])
        assert b.shape[1] == k and b.shape[2] % tn == 0
    else:
        n = b.shape[0] if tb else b.shape[1]
        assert (b.shape[1] if tb else b.shape[0]) == k
    tm, tn = min(tm, m), min(tn, n)
    tk = k if tk is None else min(tk, k)
    assert m % tm == 0 and n % tn == 0 and k % tk == 0, (name, m, n, k, tm, tn, tk)
    nk = k // tk
    dn = (((0 if ta else 1,), (1 if tb else 0,)), ((), ()))

    def body(*refs):
        a_ref, b_ref = refs[0], refs[1]
        add_ref = refs[2] if add is not None else None
        o_ref = refs[3] if add is not None else refs[2]
        part = lax.dot_general(a_ref[...], b_ref[...], dn, preferred_element_type=F32)
        if nk == 1:
            if add_ref is not None:
                part = part + add_ref[...]
            o_ref[...] = part.astype(out_dtype)
            return
        acc_ref = refs[-1]
        kk = pl.program_id(2)

        @pl.when(kk == 0)
        def _():
            acc_ref[...] = part

        @pl.when(kk > 0)
        def _():
            acc_ref[...] += part

        @pl.when(kk == nk - 1)
        def _():
            res = acc_ref[...]
            if add_ref is not None:
                res = res + add_ref[...]
            o_ref[...] = res.astype(out_dtype)

    if a_planes and ta:
        a_per = a.shape[2] // tm
        a_spec = pl.BlockSpec((None, tk, tm), lambda i, j, kk: (i // a_per, kk, i % a_per))
    elif a_planes:
        a_per = a.shape[2] // tk
        a_spec = pl.BlockSpec((None, tm, tk), lambda i, j, kk: (kk // a_per, i, kk % a_per))
    elif ta:
        a_spec = pl.BlockSpec((tk, tm), lambda i, j, kk: (kk, i))
    else:
        a_spec = pl.BlockSpec((tm, tk), lambda i, j, kk: (i, kk))
    if b_planes:
        b_per = b.shape[2] // tn
        b_spec = pl.BlockSpec((None, tk, tn), lambda i, j, kk: (j // b_per, kk, j % b_per))
    elif tb:
        b_spec = pl.BlockSpec((tn, tk), lambda i, j, kk: (j, kk))
    else:
        b_spec = pl.BlockSpec((tk, tn), lambda i, j, kk: (kk, j))
    if out_planes is None:
        o_spec = pl.BlockSpec((tm, tn), lambda i, j, kk: (i, j))
        o_shape = (m, n)
    else:
        o_per = n // out_planes // tn
        assert add is None and n == out_planes * o_per * tn
        o_spec = pl.BlockSpec((None, tm, tn), lambda i, j, kk: (j // o_per, i, j % o_per))
        o_shape = (out_planes, m, n // out_planes)
    in_specs = [a_spec, b_spec] + ([o_spec] if add is not None else [])
    args = (a, b) + ((add,) if add is not None else ())
    (out,), got = _call_hosting(
        body, name=name, grid=(m // tm, n // tn, nk), in_specs=in_specs, args=args, out_specs=[o_spec],
        out_shape=[jax.ShapeDtypeStruct(o_shape, out_dtype)],
        scratch=[pltpu.VMEM((tm, tn), F32)] if nk > 1 else [], comm=comm)
    return out if comm is None else (out, got)


def _rms_fwd(x, g, *, name, tr=256):
    s, d = x.shape

    def body(x_ref, g_ref, h_ref):
        xv = x_ref[...]
        r = lax.rsqrt(jnp.mean(xv * xv, axis=-1, keepdims=True) + RMS_EPS)
        h_ref[...] = (xv * r * g_ref[...]).astype(BF16)

    return pl.pallas_call(
        body, name=name, grid=(s // tr,),
        in_specs=[pl.BlockSpec((tr, d), lambda i: (i, 0)), pl.BlockSpec((1, d), lambda i: (0, 0))],
        out_specs=pl.BlockSpec((tr, d), lambda i: (i, 0)),
        out_shape=jax.ShapeDtypeStruct((s, d), BF16),
        compiler_params=_params(1),
    )(x, g)


def _rms_bwd(x, g, dh, dres, *, name, tr=256):
    s, d = x.shape

    def body(x_ref, g_ref, dh_ref, dres_ref, dx_ref, dxb_ref, dg_ref):
        xv = x_ref[...]
        r = lax.rsqrt(jnp.mean(xv * xv, axis=-1, keepdims=True) + RMS_EPS)
        y = xv * r
        dhv = dh_ref[...]
        dy = dhv * g_ref[...]
        dx = dres_ref[...] + r * (dy - y * jnp.mean(dy * y, axis=-1, keepdims=True))
        dx_ref[...] = dx
        dxb_ref[...] = dx.astype(BF16)
        part = jnp.sum(dhv * y, axis=0, keepdims=True)

        @pl.when(pl.program_id(0) == 0)
        def _():
            dg_ref[...] = part

        @pl.when(pl.program_id(0) > 0)
        def _():
            dg_ref[...] += part

    row = pl.BlockSpec((tr, d), lambda i: (i, 0))
    vec = pl.BlockSpec((1, d), lambda i: (0, 0))
    return pl.pallas_call(
        body, name=name, grid=(s // tr,),
        in_specs=[row, vec, row, row], out_specs=[row, row, vec],
        out_shape=[jax.ShapeDtypeStruct((s, d), F32), jax.ShapeDtypeStruct((s, d), BF16),
                   jax.ShapeDtypeStruct((1, d), F32)],
        compiler_params=_params(1),
    )(x, g, dh, dres)


def _final_loss(x, g, tgt, *, name, tr=256):
    s, d = x.shape

    def body(x_ref, g_ref, t_ref, loss_ref, dx_ref, dxb_ref, dg_ref):
        xv = x_ref[...]
        gv = g_ref[...]
        r = lax.rsqrt(jnp.mean(xv * xv, axis=-1, keepdims=True) + RMS_EPS)
        y = xv * r
        err = y * gv - t_ref[...]
        lpart = 0.5 * jnp.sum(jnp.mean(err * err, axis=-1, keepdims=True), axis=0, keepdims=True)
        dh = err * (1.0 / d)
        dy = dh * gv
        dx = r * (dy - y * jnp.mean(dy * y, axis=-1, keepdims=True))
        dx_ref[...] = dx
        dxb_ref[...] = dx.astype(BF16)
        gpart = jnp.sum(dh * y, axis=0, keepdims=True)
        lrow = jnp.broadcast_to(lpart, (1, LANES))

        @pl.when(pl.program_id(0) == 0)
        def _():
            dg_ref[...] = gpart
            loss_ref[...] = lrow

        @pl.when(pl.program_id(0) > 0)
        def _():
            dg_ref[...] += gpart
            loss_ref[...] += lrow

    row = pl.BlockSpec((tr, d), lambda i: (i, 0))
    vec = pl.BlockSpec((1, d), lambda i: (0, 0))
    lsp = pl.BlockSpec((1, LANES), lambda i: (0, 0))
    return pl.pallas_call(
        body, name=name, grid=(s // tr,),
        in_specs=[row, vec, row], out_specs=[lsp, row, row, vec],
        out_shape=[jax.ShapeDtypeStruct((1, LANES), F32), jax.ShapeDtypeStruct((s, d), F32),
                   jax.ShapeDtypeStruct((s, d), BF16), jax.ShapeDtypeStruct((1, d), F32)],
        compiler_params=_params(1),
    )(x, g, tgt)


FF_BLK = D_FF // 2


def _ffn_in_fwd(h, win_t, *, name, tm=512):
    s, d = h.shape

    def body(h_ref, wg_ref, wu_ref, g_ref, u_ref, a_ref):
        hv = h_ref[...]
        g = _nt(hv, wg_ref[...])
        u = _nt(hv, wu_ref[...])
        g_ref[...] = g
        u_ref[...] = u
        a_ref[...] = (g * jax.nn.sigmoid(g) * u).astype(BF16)

    blk = pl.BlockSpec((tm, FF_BLK), lambda i, j: (i, j))
    f32 = jax.ShapeDtypeStruct((s, D_FF), F32)
    return pl.pallas_call(
        body, name=name, grid=(s // tm, 2),
        in_specs=[pl.BlockSpec((tm, d), lambda i, j: (i, 0)),
                  pl.BlockSpec((FF_BLK, d), lambda i, j: (j, 0)),
                  pl.BlockSpec((FF_BLK, d), lambda i, j: (j + 2, 0))],
        out_specs=[blk, blk, blk],
        out_shape=[f32, f32, jax.ShapeDtypeStruct((s, D_FF), BF16)],
        compiler_params=_params(2),
    )(h, win_t, win_t)


def _ffn_out_dx(dxb, wout, g, u, *, name, tm=512):
    s, d = dxb.shape

    def body(dx_ref, w_ref, g_ref, u_ref, o_ref):
        dav = _nt(dx_ref[...], w_ref[...])
        gv = g_ref[...]
        sg = jax.nn.sigmoid(gv)
        o_ref[0] = (dav * u_ref[...] * (sg * (1.0 + gv * (1.0 - sg)))).astype(BF16)
        o_ref[1] = (dav * gv * sg).astype(BF16)

    blk = pl.BlockSpec((tm, FF_BLK), lambda i, j: (i, j))
    return pl.pallas_call(
        body, name=name, grid=(s // tm, 2),
        in_specs=[pl.BlockSpec((tm, d), lambda i, j: (i, 0)), pl.BlockSpec((FF_BLK, d), lambda i, j: (j, 0)),
                  blk, blk],
        out_specs=pl.BlockSpec((2, tm, FF_BLK), lambda i, j: (0, i, j)),
        out_shape=jax.ShapeDtypeStruct((2, s, D_FF), BF16),
        compiler_params=_params(2),
    )(dxb, wout, g, u)


def _split3(x):
    hi = x.astype(BF16)
    r1 = x - hi.astype(F32)
    mid = r1.astype(BF16)
    lo = (r1 - mid.astype(F32)).astype(BF16)
    return hi, mid, lo


def _dot3(x, m_bf):
    hi, mid, lo = _split3(x)
    return (jnp.dot(hi, m_bf, preferred_element_type=F32)
            + jnp.dot(mid, m_bf, preferred_element_type=F32)
            + jnp.dot(lo, m_bf, preferred_element_type=F32))


def _dot3_left(m_bf, x):
    hi, mid, lo = _split3(x)
    return (jnp.dot(m_bf, hi, preferred_element_type=F32)
            + jnp.dot(m_bf, mid, preferred_element_type=F32)
            + jnp.dot(m_bf, lo, preferred_element_type=F32))


def _dot2(x, m_bf):
    hi = x.astype(BF16)
    lo = (x - hi.astype(F32)).astype(BF16)
    return jnp.dot(hi, m_bf, preferred_element_type=F32) + jnp.dot(lo, m_bf, preferred_element_type=F32)


def _nt(a, b):
    return lax.dot_general(a, b, (((1,), (1,)), ((), ())), preferred_element_type=F32)


def _mm32(a, b):
    return jnp.dot(a, b, preferred_element_type=F32)


def _iota2(shape, dim):
    return lax.broadcasted_iota(jnp.int32, shape, dim)


def _rope_tables(s):
    half = ROT_DIM // 2
    pos = jnp.arange(s, dtype=F32)
    inv_freq = ROPE_THETA ** (-jnp.arange(half, dtype=F32) * 2.0 / ROT_DIM)
    ang = pos[:, None] * inv_freq[None, :]
    cos, sin = jnp.cos(ang), jnp.sin(ang)
    ones = jnp.ones((s, HEAD_DIM - ROT_DIM), F32)
    cos_t = jnp.concatenate([cos, cos, ones], axis=1)
    sin_t = jnp.concatenate([-sin, sin, 0.0 * ones], axis=1)
    idx = jnp.arange(HEAD_DIM)
    partner = jnp.where(idx < half, idx + half, idx - half)
    swap = ((idx[:, None] == partner[None, :]) & (idx[None, :] < ROT_DIM)).astype(F32)
    swap2 = jnp.kron(jnp.eye(2, dtype=F32), swap).astype(BF16)
    return jnp.tile(cos_t, (1, 2)), jnp.tile(sin_t, (1, 2)), swap2


def _rope(x, cos_t, sin_t, swap):
    return x * cos_t + _dot3(x, swap) * sin_t


def _rope_t(g, cos_t, sin_t, swap):
    return g * cos_t + _dot3(g * sin_t, swap)


def _dil_weight(dlt):
    nonneg = dlt >= 0
    w = jnp.zeros(dlt.shape, F32)
    for window, dil in DIL_PATTERNS:
        ok = nonneg & (dlt <= window) & ((dlt & (dil - 1)) == 0)
        w = w + ok.astype(F32)
    return w


FAR_TILES = 3
assert (FAR_TILES - 1) * TB + 1 > DIL_PATTERNS[1][0] and DIL_PATTERNS[2][0] >= 2048


def _dil_bias_scratch():
    return pltpu.VMEM((FAR_TILES + 1, TB, TB), F32)


def _dil_bias_tiles(bias_ref):
    rmc = _iota2((TB, TB), 0) - _iota2((TB, TB), 1)
    for d in range(FAR_TILES + 1):
        w = _dil_weight(d * TB + rmc)
        bias_ref[d] = jnp.where(w > 0.0, jnp.log(jnp.maximum(w, 1.0)), NEG)


def _log_sig_pair(z):
    sp = jnp.log(1.0 + jnp.exp(-jnp.abs(z)))
    return jnp.minimum(z, 0.0) - sp, -jnp.maximum(z, 0.0) - sp


def _log_one_minus_beta(z):
    return -(jnp.maximum(z, 0.0) + jnp.log(1.0 + jnp.exp(-jnp.abs(z))))


def _pair_masks(x, lane_lo):
    z = jnp.zeros_like(x)
    return jnp.where(lane_lo, x, z).astype(BF16), jnp.where(lane_lo, z, x).astype(BF16)


def _rows(i):
    return pl.ds(pl.multiple_of(i * TB, TB), TB)


def _head_spec(s, col0):
    return pl.BlockSpec((s, LANES), lambda p: (0, col0 + p))


def _stat_spec(s):
    return pl.BlockSpec((2, s, 1), lambda p: (p, 0, 0))


def _rowstat_spec(s):
    return pl.BlockSpec((2, 1, s), lambda p: (p, 0, 0))


def _full_spec(shape):
    nd = len(shape)
    return pl.BlockSpec(shape, lambda p: (0,) * nd)


K_COL, V_COL = D_ATTN // LANES, 2 * D_ATTN // LANES


def _bwd_scratch(s):
    return ([pltpu.VMEM((s, LANES), BF16)] * 8 + [pltpu.VMEM((LANES, s), BF16)] * 4
            + [pltpu.VMEM((LANES, s), F32)] * 2)


def _bwd_prep(i, q, k, v, dov, scr, lane_lo, sub_lo):
    qlo, qhi, klo, khi, kbf, vbf, dolo, dohi, qtlo, qthi, dotlo, dothi = scr[:12]
    rows = _rows(i)
    qs = q * SCALE
    qlo[rows, :], qhi[rows, :] = _pair_masks(qs, lane_lo)
    klo[rows, :], khi[rows, :] = _pair_masks(k * SCALE, lane_lo)
    kbf[rows, :] = k.astype(BF16)
    vbf[rows, :] = v.astype(BF16)
    dolo[rows, :], dohi[rows, :] = _pair_masks(dov, lane_lo)
    qtlo[:, rows], qthi[:, rows] = _pair_masks(qs.T, sub_lo)
    dotlo[:, rows], dothi[:, rows] = _pair_masks(dov.T, sub_lo)


def _sb_fwd(qkv, n_pairs, *, name, comm=None):
    s = qkv.shape[0]
    assert s % TB == 0
    nq = s // TB

    def body(q_ref, k_ref, v_ref, o_ref, ct_ref, qlo, qhi, kbf, vlo, vhi, sb0, sb1):
        sbuf = (sb0, sb1)
        lane_lo = _iota2((TB, LANES), 1) < HEAD_DIM

        def prep(i, _):
            rows = _rows(i)
            qlo[rows, :], qhi[rows, :] = _pair_masks(q_ref[rows, :] * SCALE, lane_lo)
            kbf[rows, :] = k_ref[rows, :].astype(BF16)
            vlo[rows, :], vhi[rows, :] = _pair_masks(v_ref[rows, :], lane_lo)
            return 0

        lax.fori_loop(0, nq, prep, 0)
        rmc = _iota2((TB, TB), 0) - _iota2((TB, TB), 1)
        strict = rmc > 0
        u_ge = (rmc >= 0).astype(BF16)
        qm, vm = (qlo, qhi), (vlo, vhi)

        def qloop(i, _):
            rows = _rows(i)

            def logits(kb, carry, diag):
                c = list(carry)
                keys = _rows(kb)
                k = kbf[keys, :]
                zs = [_nt(qm[h][rows, :], k) for h in range(2)]
                lms = [_log_one_minus_beta(z) for z in zs]
                if diag:
                    lms = [jnp.where(strict, lm, 0.0) for lm in lms]
                r_ins = [_dot2(lm, u_ge) for lm in lms]
                for h in range(2):
                    la = zs[h] + r_ins[h] + c[h]
                    sbuf[h][:, keys] = jnp.where(strict, la, NEG) if diag else la
                    c[h] = c[h] + r_ins[h][:, 0:1]
                return tuple(c)

            z1 = jnp.zeros((TB, 1), F32)
            c0, c1 = lax.fori_loop(0, i, lambda t, cr: logits(i - 1 - t, cr, False), logits(i, (z1, z1), True))

            def weigh(kb, acc):
                keys = _rows(kb)
                a_bf = [jnp.exp(sbuf[h][:, keys]).astype(BF16) for h in range(2)]
                return acc + _mm32(a_bf[0], vm[0][keys, :]) + _mm32(a_bf[1], vm[1][keys, :])

            acc = lax.fori_loop(0, i + 1, weigh, jnp.zeros((TB, LANES), F32))
            o_ref[rows, :] = acc
            ct_ref[0, rows, :] = c0
            ct_ref[1, rows, :] = c1
            return 0

        lax.fori_loop(0, nq, qloop, 0)

    return _call_pairs(
        body, name=name, n_pairs=n_pairs, comm=comm,
        in_specs=[_head_spec(s, 0), _head_spec(s, K_COL), _head_spec(s, V_COL)], args=(qkv, qkv, qkv),
        out_specs=[_head_spec(s, 0), _stat_spec(s)],
        out_shape=[jax.ShapeDtypeStruct((s, LANES * n_pairs), F32),
                   jax.ShapeDtypeStruct((2 * n_pairs, s, 1), F32)],
        scratch=[pltpu.VMEM((s, LANES), BF16)] * 5 + [pltpu.VMEM((TB, s), F32)] * 2)


def _sb_bwd(qkv, do, ctot, n_pairs, do_col0, *, name, comm=None):
    s = qkv.shape[0]
    assert s % TB == 0
    nq = s // TB

    def body(q_ref, k_ref, v_ref, do_ref, ct_ref, dq_ref, dk_ref, dv_ref, *scr):
        qlo, qhi, klo, khi, kbf, vbf, dolo, dohi, qtlo, qthi, dotlo, dothi, dkt, dvt = scr
        lane_lo = _iota2((TB, LANES), 1) < HEAD_DIM
        sub_lo = _iota2((LANES, TB), 0) < HEAD_DIM

        def prep(i, _):
            rows = _rows(i)
            _bwd_prep(i, q_ref[rows, :], k_ref[rows, :], v_ref[rows, :], do_ref[rows, :], scr, lane_lo, sub_lo)
            return 0

        lax.fori_loop(0, nq, prep, 0)
        dkt[...] = jnp.zeros_like(dkt)
        dvt[...] = jnp.zeros_like(dvt)
        rmc = _iota2((TB, TB), 0) - _iota2((TB, TB), 1)
        strict = rmc > 0
        u_le = (rmc <= 0).astype(BF16)
        qm, km, dom, qtm, dotm = (qlo, qhi), (klo, khi), (dolo, dohi), (qtlo, qthi), (dotlo, dothi)

        def qloop(i, _):
            rows = _rows(i)
            ct = (ct_ref[0, rows, :], ct_ref[1, rows, :])

            def tile(kb, carry, diag):
                pre, hl, dq = list(carry[0:2]), list(carry[2:4]), carry[4]
                keys = _rows(kb)
                k, v = kbf[keys, :], vbf[keys, :]
                zs = [_nt(qm[h][rows, :], k) for h in range(2)]
                das = [_nt(dom[h][rows, :], v) for h in range(2)]
                lms = [_log_one_minus_beta(z) for z in zs]
                if diag:
                    lms = [jnp.where(strict, lm, 0.0) for lm in lms]
                pins = [_dot2(lm, u_le) for lm in lms]
                gs, lbs = [], []
                a_bf = []
                for h in range(2):
                    lb = zs[h] + lms[h]
                    a = jnp.exp(lb + (ct[h] - pre[h]) - pins[h])
                    if diag:
                        a = jnp.where(strict, a, 0.0)
                    gs.append(a * das[h])
                    lbs.append(lb)
                    a_bf.append(a.astype(BF16))
                hins = [_dot2(g, u_le) for g in gs]
                dk_t, dv_t = dkt[:, keys], dvt[:, keys]
                for h in range(2):
                    g = gs[h]
                    dz = g - jnp.exp(lbs[h]) * (hl[h] + hins[h])
                    if diag:
                        dz = jnp.where(strict, dz, 0.0)
                    dzb = dz.astype(BF16)
                    dq = dq + _mm32(dzb, km[h][keys, :])
                    dk_t = dk_t + _mm32(qtm[h][:, rows], dzb)
                    dv_t = dv_t + _mm32(dotm[h][:, rows], a_bf[h])
                    pre[h] = pre[h] + pins[h][:, TB - 1:TB]
                    hl[h] = hl[h] + hins[h][:, TB - 1:TB]
                dkt[:, keys] = dk_t
                dvt[:, keys] = dv_t
                return pre[0], pre[1], hl[0], hl[1], dq

            z1 = jnp.zeros((TB, 1), F32)
            carry = lax.fori_loop(0, i, lambda kb, cr: tile(kb, cr, False),
                                  (z1, z1, z1, z1, jnp.zeros((TB, LANES), F32)))
            dq = tile(i, carry, True)[4]
            dq_ref[rows, :] = dq.astype(BF16)
            return 0

        lax.fori_loop(0, nq, qloop, 0)

        def wloop(i, _):
            rows = _rows(i)
            dk_ref[rows, :] = dkt[:, rows].T.astype(BF16)
            dv_ref[rows, :] = dvt[:, rows].T.astype(BF16)
            return 0

        lax.fori_loop(0, nq, wloop, 0)

    out = jax.ShapeDtypeStruct((s, LANES * n_pairs), BF16)
    return _call_pairs(
        body, name=name, n_pairs=n_pairs, comm=comm,
        in_specs=[_head_spec(s, 0), _head_spec(s, K_COL), _head_spec(s, V_COL),
                  _head_spec(s, do_col0), _stat_spec(s)], args=(qkv, qkv, qkv, do, ctot),
        out_specs=[_head_spec(s, 0)] * 3, out_shape=[out, out, out], scratch=_bwd_scratch(s))


def _bias_fwd(mode, qkv, head0_col, n_pairs, extra, *, name, comm=None):
    s = qkv.shape[0]
    assert s % TB == 0 and s <= DIL_PATTERNS[2][0]
    nq = s // TB
    fox = mode == "fox"

    def body(q_ref, k_ref, v_ref, e0, e1, *rest):
        if fox:
            o_ref, lse_ref, qlo, qhi, kbf, vx0, vx1, sb0, sb1 = rest
        else:
            e2, o_ref, lse_ref, qlo, qhi, kbf, vx0, vx1, sb0, sb1, bias = rest
            _dil_bias_tiles(bias)
        sbuf = (sb0, sb1)
        lane_lo = _iota2((TB, LANES), 1) < HEAD_DIM

        def prep(i, _):
            rows = _rows(i)
            q, k, v = q_ref[rows, :], k_ref[rows, :], v_ref[rows, :]
            if not fox:
                c, sn, sw = e0[rows, :], e1[rows, :], e2[...]
                q, k = _rope(q, c, sn, sw), _rope(k, c, sn, sw)
            qlo[rows, :], qhi[rows, :] = _pair_masks(q * SCALE, lane_lo)
            kbf[rows, :] = k.astype(BF16)
            one = jnp.ones_like(v)
            vx0[rows, :] = jnp.where(lane_lo, v, one).astype(BF16)
            vx1[rows, :] = jnp.where(lane_lo, one, v).astype(BF16)
            return 0

        lax.fori_loop(0, nq, prep, 0)
        rmc = _iota2((TB, TB), 0) - _iota2((TB, TB), 1)
        qm, vx = (qlo, qhi), (vx0, vx1)

        def qloop(i, _):
            rows = _rows(i)
            if fox:
                fq = (e0[0, rows, :], e0[1, rows, :])

            def scores(kb, carry, diag):
                keys = _rows(kb)
                k = kbf[keys, :]
                scs = [_nt(qm[h][rows, :], k) for h in range(2)]
                if not fox:
                    b = bias[jnp.minimum(i - kb, FAR_TILES)]
                out = []
                for h in range(2):
                    if fox:
                        sc = scs[h] + (fq[h] - e1[h, :, keys])
                        if diag:
                            sc = jnp.where(rmc >= 0, sc, NEG)
                    else:
                        sc = scs[h] + b
                    sbuf[h][:, keys] = sc
                    mx = carry[h]
                    for j in range(TB // LANES):
                        mx = jnp.maximum(mx, sc[:, j * LANES:(j + 1) * LANES])
                    out.append(mx)
                return tuple(out)

            mx0 = jnp.full((TB, LANES), NEG, F32)
            mxs = lax.fori_loop(0, i, lambda kb, cr: scores(kb, cr, False), (mx0, mx0))
            mxs = scores(i, mxs, True)
            m_0, m_1 = (jnp.max(mx, axis=1, keepdims=True) for mx in mxs)

            def weigh(kb, carry):
                keys = _rows(kb)
                ps = [jnp.exp(sbuf[h][:, keys] - m).astype(BF16) for h, m in enumerate((m_0, m_1))]
                return tuple(carry[h] + _mm32(ps[h], vx[h][keys, :]) for h in range(2))

            a0 = jnp.zeros((TB, LANES), F32)
            acc0, acc1 = lax.fori_loop(0, i + 1, weigh, (a0, a0))
            l0, l1 = acc0[:, HEAD_DIM:HEAD_DIM + 1], acc1[:, 0:1]
            o_ref[rows, :] = jnp.where(lane_lo, acc0 / l0, acc1 / l1)
            lse_ref[0, rows, :] = m_0 + jnp.log(l0)
            lse_ref[1, rows, :] = m_1 + jnp.log(l1)
            return 0

        lax.fori_loop(0, nq, qloop, 0)

    hp0 = head0_col
    if fox:
        e_specs = [_stat_spec(s), _rowstat_spec(s)]
    else:
        e_specs = [_full_spec((s, LANES)), _full_spec((s, LANES)), _full_spec((LANES, LANES))]
    return _call_pairs(
        body, name=name, n_pairs=n_pairs, comm=comm,
        in_specs=[_head_spec(s, hp0), _head_spec(s, K_COL + hp0), _head_spec(s, V_COL + hp0)] + e_specs,
        args=(qkv, qkv, qkv, *extra),
        out_specs=[_head_spec(s, 0), _stat_spec(s)],
        out_shape=[jax.ShapeDtypeStruct((s, LANES * n_pairs), F32),
                   jax.ShapeDtypeStruct((2 * n_pairs, s, 1), F32)],
        scratch=([pltpu.VMEM((s, LANES), BF16)] * 5 + [pltpu.VMEM((TB, s), F32)] * 2
                 + ([] if fox else [_dil_bias_scratch()])))


def _bias_bwd(mode, qkv, head0_col, n_pairs, extra, o, do, do_col0, lse, *, name, comm=None):
    s = qkv.shape[0]
    assert s % TB == 0 and s <= DIL_PATTERNS[2][0]
    nq = s // TB
    fox = mode == "fox"

    def body(q_ref, k_ref, v_ref, o_ref, do_ref, lse_ref, e0, e1, *rest):
        if fox:
            dq_ref, dk_ref, dv_ref, dfr_ref, dfc_ref = rest[:5]
            scr = rest[5:]
        else:
            e2, dq_ref, dk_ref, dv_ref = rest[:4]
            scr = rest[4:]
        qlo, qhi, klo, khi, kbf, vbf, dolo, dohi, qtlo, qthi, dotlo, dothi, dkt, dvt = scr[:14]
        if not fox:
            bias = scr[14]
            _dil_bias_tiles(bias)
        lane_lo = _iota2((TB, LANES), 1) < HEAD_DIM
        sub_lo = _iota2((LANES, TB), 0) < HEAD_DIM

        def prep(i, _):
            rows = _rows(i)
            q, k = q_ref[rows, :], k_ref[rows, :]
            if not fox:
                c, sn, sw = e0[rows, :], e1[rows, :], e2[...]
                q, k = _rope(q, c, sn, sw), _rope(k, c, sn, sw)
            _bwd_prep(i, q, k, v_ref[rows, :], do_ref[rows, :], scr, lane_lo, sub_lo)
            return 0

        lax.fori_loop(0, nq, prep, 0)
        dkt[...] = jnp.zeros_like(dkt)
        dvt[...] = jnp.zeros_like(dvt)
        if fox:
            dfr_ref[...] = jnp.zeros_like(dfr_ref)
        rmc = _iota2((TB, TB), 0) - _iota2((TB, TB), 1)
        qm, km, dom, qtm, dotm = (qlo, qhi), (klo, khi), (dolo, dohi), (qtlo, qthi), (dotlo, dothi)

        def qloop(i, _):
            rows = _rows(i)
            prod = do_ref[rows, :] * o_ref[rows, :]
            dsum = (jnp.sum(jnp.where(lane_lo, prod, 0.0), axis=1, keepdims=True),
                    jnp.sum(jnp.where(lane_lo, 0.0, prod), axis=1, keepdims=True))
            lse_i = (lse_ref[0, rows, :], lse_ref[1, rows, :])
            if fox:
                fql = (e0[0, rows, :] - lse_i[0], e0[1, rows, :] - lse_i[1])

            def tile(kb, carry, diag):
                dq, rs = carry[0], list(carry[1:])
                keys = _rows(kb)
                k, v = kbf[keys, :], vbf[keys, :]
                scs = [_nt(qm[h][rows, :], k) for h in range(2)]
                dps = [_nt(dom[h][rows, :], v) for h in range(2)]
                if not fox:
                    b = bias[jnp.minimum(i - kb, FAR_TILES)]
                ps, dss = [], []
                for h in range(2):
                    if fox:
                        sc = scs[h] + (fql[h] - e1[h, :, keys])
                        if diag:
                            sc = jnp.where(rmc >= 0, sc, NEG)
                    else:
                        sc = scs[h] + (b - lse_i[h])
                    p = jnp.exp(sc)
                    dss.append(p * (dps[h] - dsum[h]))
                    ps.append(p.astype(BF16))
                dk_t, dv_t = dkt[:, keys], dvt[:, keys]
                for h in range(2):
                    dsb = dss[h].astype(BF16)
                    dq = dq + _mm32(dsb, km[h][keys, :])
                    dk_t = dk_t + _mm32(qtm[h][:, rows], dsb)
                    dv_t = dv_t + _mm32(dotm[h][:, rows], ps[h])
                    if fox:
                        dfr_ref[h, :, keys] -= jnp.sum(dss[h], axis=0, keepdims=True)
                        for j in range(TB // LANES):
                            rs[h] = rs[h] + dss[h][:, j * LANES:(j + 1) * LANES]
                dkt[:, keys] = dk_t
                dvt[:, keys] = dv_t
                return (dq, *rs)

            z2 = jnp.zeros((TB, LANES), F32)
            carry = lax.fori_loop(0, i, lambda kb, cr: tile(kb, cr, False), (z2, z2, z2) if fox else (z2,))
            carry = tile(i, carry, True)
            if fox:
                dfc_ref[0, rows, :] = jnp.sum(carry[1], axis=1, keepdims=True)
                dfc_ref[1, rows, :] = jnp.sum(carry[2], axis=1, keepdims=True)
            dq = carry[0]
            if not fox:
                dq = _rope_t(dq, e0[rows, :], e1[rows, :], e2[...])
            dq_ref[rows, :] = dq.astype(BF16)
            return 0

        lax.fori_loop(0, nq, qloop, 0)

        def wloop(i, _):
            rows = _rows(i)
            dk = dkt[:, rows].T
            if not fox:
                dk = _rope_t(dk, e0[rows, :], e1[rows, :], e2[...])
            dk_ref[rows, :] = dk.astype(BF16)
            dv_ref[rows, :] = dvt[:, rows].T.astype(BF16)
            return 0

        lax.fori_loop(0, nq, wloop, 0)

    hp0 = head0_col
    out = jax.ShapeDtypeStruct((s, LANES * n_pairs), BF16)
    out_specs = [_head_spec(s, 0)] * 3
    out_shape = [out, out, out]
    if fox:
        e_specs = [_stat_spec(s), _rowstat_spec(s)]
        out_specs += [_rowstat_spec(s), _stat_spec(s)]
        out_shape += [jax.ShapeDtypeStruct((2 * n_pairs, 1, s), F32), jax.ShapeDtypeStruct((2 * n_pairs, s, 1), F32)]
    else:
        e_specs = [_full_spec((s, LANES)), _full_spec((s, LANES)), _full_spec((LANES, LANES))]
    return _call_pairs(
        body, name=name, n_pairs=n_pairs, comm=comm,
        in_specs=[_head_spec(s, hp0), _head_spec(s, K_COL + hp0), _head_spec(s, V_COL + hp0),
                  _head_spec(s, 0), _head_spec(s, do_col0), _stat_spec(s)] + e_specs,
        args=(qkv, qkv, qkv, o, do, lse, *extra),
        out_specs=out_specs, out_shape=out_shape,
        scratch=_bwd_scratch(s) + ([] if fox else [_dil_bias_scratch()]))


F_COL = 3 * D_ATTN // LANES


def _fgate_fwd(qkvf, brow, *, name):
    s = qkvf.shape[0]
    nb = s // BLK

    def body(f_ref, b_ref, fc_ref, fr_ref, fs):
        row, col = _iota2((BLK, BLK), 0), _iota2((BLK, BLK), 1)
        l_incl = (col <= row).astype(BF16)

        def step(i, carry):
            r0 = pl.multiple_of(i * BLK, BLK)
            lf, _ = _log_sig_pair(f_ref[pl.ds(r0, BLK), :] + b_ref[...])
            fblk = carry + _dot3_left(l_incl, lf)
            fs[pl.ds(r0, BLK), :] = fblk
            return fblk[BLK - 1:BLK, :]

        lax.fori_loop(0, nb, step, jnp.zeros((1, LANES), F32))
        ft = fs[...].T
        for h in range(N_HEADS):
            fc_ref[h, :, :] = fs[:, h:h + 1]
            fr_ref[h, :, :] = ft[h:h + 1, :]

    return pl.pallas_call(
        body, name=name, grid=(1,),
        in_specs=[pl.BlockSpec((s, LANES), lambda i: (0, F_COL)), pl.BlockSpec((1, LANES), lambda i: (0, 0))],
        out_specs=[pl.BlockSpec((N_HEADS, s, 1), lambda i: (0, 0, 0)),
                   pl.BlockSpec((N_HEADS, 1, s), lambda i: (0, 0, 0))],
        out_shape=[jax.ShapeDtypeStruct((N_HEADS, s, 1), F32), jax.ShapeDtypeStruct((N_HEADS, 1, s), F32)],
        scratch_shapes=[pltpu.VMEM((s, LANES), F32)],
        compiler_params=_params(1),
    )(qkvf, brow)


def _fgate_bwd(dfr, dfc, qkvf, brow, *, name):
    s = qkvf.shape[0]
    nb = s // BLK

    def body(dfr_ref, dfc_ref, f_ref, b_ref, dfl_ref, db_ref, ts, fs):
        ts[...] = jnp.zeros_like(ts)
        for h in range(N_HEADS):
            ts[h:h + 1, :] = dfr_ref[h]
        fs[...] = ts[...].T
        for h in range(N_HEADS):
            fs[:, h:h + 1] += dfc_ref[h]
        row, col = _iota2((BLK, BLK), 0), _iota2((BLK, BLK), 1)
        u_incl = (col >= row).astype(BF16)
        head_lane = _iota2((BLK, LANES), 1) < N_HEADS

        def step(ii, carry):
            tail, db = carry
            r0 = pl.multiple_of((nb - 1 - ii) * BLK, BLK)
            rblk = tail + _dot3_left(u_incl, fs[pl.ds(r0, BLK), :])
            _, lsn = _log_sig_pair(f_ref[pl.ds(r0, BLK), :] + b_ref[...])
            dfl = jnp.where(head_lane, rblk * jnp.exp(lsn), 0.0)
            dfl_ref[pl.ds(r0, BLK), :] = dfl.astype(BF16)
            return rblk[0:1, :], db + jnp.sum(dfl, axis=0, keepdims=True)

        z = jnp.zeros((1, LANES), F32)
        _, db = lax.fori_loop(0, nb, step, (z, z))
        db_ref[...] = db

    return pl.pallas_call(
        body, name=name, grid=(1,),
        in_specs=[pl.BlockSpec((N_HEADS, 1, s), lambda i: (0, 0, 0)), pl.BlockSpec((N_HEADS, s, 1), lambda i: (0, 0, 0)),
                  pl.BlockSpec((s, LANES), lambda i: (0, F_COL)), pl.BlockSpec((1, LANES), lambda i: (0, 0))],
        out_specs=[pl.BlockSpec((s, LANES), lambda i: (0, 0)), pl.BlockSpec((1, LANES), lambda i: (0, 0))],
        out_shape=[jax.ShapeDtypeStruct((s, LANES), BF16), jax.ShapeDtypeStruct((1, LANES), F32)],
        scratch_shapes=[pltpu.VMEM((LANES, s), F32), pltpu.VMEM((s, LANES), F32)],
        compiler_params=_params(1),
    )(dfr, dfc, qkvf, brow)


def _adamw_math(w, g, m, v):
    m2 = ADAM_B1 * m + (1.0 - ADAM_B1) * g
    v2 = ADAM_B2 * v + (1.0 - ADAM_B2) * (g * g)
    m_hat = m2 / (1.0 - ADAM_B1 ** ADAM_STEP)
    v_hat = v2 / (1.0 - ADAM_B2 ** ADAM_STEP)
    delta = -ADAM_LR * (m_hat / (jnp.sqrt(v_hat) + ADAM_EPS) + ADAM_WD * w)
    return delta, m2, v2


def _row_tile(r, cap=256, mult=16):
    best = None
    for t in range(mult, min(r, cap) + 1, mult):
        if r % t == 0:
            best = t
    assert best is not None, r
    return best


def _adamw_shard(w, m, v, lidx, g_all, r1, r2, sc, prev, *, name):
    nl, r, c = w.shape
    tr = _row_tile(r)

    def body(sc_ref, w_ref, m_ref, v_ref, g_ref, r1_ref, r2_ref, *rest):
        go_ref, d_ref, mo_ref, vo_ref = rest[-4:]
        g = g_ref[...] + r1_ref[...]
        g = g + r2_ref[0].astype(F32)
        g = g + r2_ref[1].astype(F32)
        g = g + r2_ref[2].astype(F32)
        delta, m2, v2 = _adamw_math(w_ref[...], g, m_ref[...], v_ref[...])
        go_ref[...] = g
        d_ref[...] = delta
        mo_ref[...] = m2
        vo_ref[...] = v2

    lay = pl.BlockSpec((None, tr, c), lambda i, s_: (lidx, i, 0))
    in_specs = [lay, lay, lay,
                pl.BlockSpec((None, tr, c), lambda i, s_: (s_[0], i, 0)),
                pl.BlockSpec((None, tr, c), lambda i, s_: (s_[1], i, 0)),
                pl.BlockSpec((3, tr, c), lambda i, s_: (0, i, 0))]
    args = [sc, w, m, v, g_all, r1, r2]
    aliases = {}
    if prev is not None:
        in_specs += [pl.BlockSpec(memory_space=pl.ANY)] * 4
        aliases = {7 + t: t for t in range(4)}
        args += list(prev)
    shp = jax.ShapeDtypeStruct((nl, r, c), F32)
    return pl.pallas_call(
        body, name=name,
        grid_spec=pltpu.PrefetchScalarGridSpec(
            num_scalar_prefetch=1, grid=(r // tr,), in_specs=in_specs, out_specs=[lay] * 4),
        out_shape=[shp] * 4, input_output_aliases=aliases,
        compiler_params=_params(1),
    )(*args)


def _adamw_small(w, g, m, v, *, name):
    def body(w_ref, g_ref, m_ref, v_ref, d_ref, mo_ref, vo_ref):
        delta, m2, v2 = _adamw_math(w_ref[...], g_ref[...], m_ref[...], v_ref[...])
        d_ref[...] = delta
        mo_ref[...] = m2
        vo_ref[...] = v2

    shp = jax.ShapeDtypeStruct(w.shape, F32)
    return pl.pallas_call(body, name=name, out_shape=[shp] * 3, compiler_params=_params())(w, g, m, v)


def _pos():
    return lax.axis_index("x"), lax.axis_index("y"), lax.axis_index("c")


def _other_chips(x, y):
    return [(1 - x, y), (x, 1 - y), (1 - x, 1 - y)]


def _dev_index(x, y, c):
    return 4 * x + 2 * y + c


HBM_SPEC = pl.BlockSpec(memory_space=pltpu.HBM)


class _Comm:
    def __init__(self, inputs, out_shape, scratch, start, mid, finish):
        self.inputs, self.out_shape, self.scratch = list(inputs), list(out_shape), list(scratch)
        self.start, self.mid, self.finish = start, mid, finish

    def run(self, name):
        n_in, n_out = len(self.inputs), len(self.out_shape)

        def body(*refs):
            parts = refs[:n_in], refs[n_in:n_in + n_out], refs[n_in + n_out:]
            self.start(*parts)
            self.mid(*parts)
            self.finish(*parts)

        return pl.pallas_call(
            body, name=name, in_specs=[HBM_SPEC] * n_in, out_specs=[HBM_SPEC] * n_out,
            out_shape=self.out_shape, scratch_shapes=self.scratch)(*self.inputs)


def _call_hosting(body, *, name, grid, in_specs, args, out_specs, out_shape, scratch, comm=None):
    if comm is None:
        res = pl.pallas_call(
            body, name=name, grid=grid, in_specs=in_specs, out_specs=out_specs, out_shape=out_shape,
            scratch_shapes=scratch, compiler_params=_params(len(grid)))(*args)
        return list(res), []
    sizes = (len(in_specs), len(comm.inputs), len(out_specs), len(comm.out_shape), len(scratch), len(comm.scratch))

    def fused(*refs):
        parts, o = [], 0
        for n in sizes:
            parts.append(refs[o:o + n])
            o += n
        h_in, c_in, h_out, c_out, h_scr, c_scr = parts
        first = last = None
        for d, n in enumerate(grid):
            p = pl.program_id(d)
            first = (p == 0) if first is None else jnp.logical_and(first, p == 0)
            last = (p == n - 1) if last is None else jnp.logical_and(last, p == n - 1)

        @pl.when(first)
        def _():
            comm.start(c_in, c_out, c_scr)

        @pl.when(last)
        def _():
            comm.mid(c_in, c_out, c_scr)

        body(*h_in, *h_out, *h_scr)

        @pl.when(last)
        def _():
            comm.finish(c_in, c_out, c_scr)

    res = pl.pallas_call(
        fused, name=name, grid=grid,
        in_specs=list(in_specs) + [HBM_SPEC] * sizes[1], out_specs=list(out_specs) + [HBM_SPEC] * sizes[3],
        out_shape=list(out_shape) + comm.out_shape, scratch_shapes=list(scratch) + comm.scratch,
        compiler_params=_params(len(grid)))(*args, *comm.inputs)
    return list(res[:sizes[2]]), list(res[sizes[2]:])


def _call_pairs(body, *, name, n_pairs, **kw):
    return _call_hosting(body, name=name, grid=(n_pairs,), **kw)


def _gather_comm(shards):
    n = len(shards)

    def plan(xs, outs, sems):
        send, recv, loc = sems
        x, y, c = _pos()
        me, sib = (x, y, c), (x, y, 1 - c)
        chips = _other_chips(x, y)

        def copy(a, k, block, to, src=None):
            dst = outs[a].at[_dev_index(*block)]
            return pltpu.make_async_remote_copy(
                src_ref=dst if src is None else src, dst_ref=dst,
                send_sem=send.at[a, k], recv_sem=recv.at[a, k], device_id=to, device_id_type=MESH)

        mine = [pltpu.make_async_copy(xs[a], outs[a].at[_dev_index(*me)], loc.at[a]) for a in range(n)]
        first = []
        for a in range(n):
            first.append(copy(a, 0, me, sib, src=xs[a]))
            first += [copy(a, 1 + j, me, (*chip, c), src=xs[a]) for j, chip in enumerate(chips)]
        passed = [(copy(a, 1 + j, (*chip, c), me), copy(a, 4 + j, (*chip, c), sib))
                  for j, chip in enumerate(chips) for a in range(n)]
        from_sib = [copy(a, 0, sib, me) for a in range(n)]
        from_sib += [copy(a, 4 + j, (*chip, 1 - c), me) for a in range(n) for j, chip in enumerate(chips)]
        return mine, first, passed, from_sib

    def start(xs, outs, sems):
        mine, first, _, _ = plan(xs, outs, sems)
        for cp in mine + first:
            cp.start()

    def mid(xs, outs, sems):
        for arrival, fwd in plan(xs, outs, sems)[2]:
            arrival.wait_recv()
            fwd.start()

    def finish(xs, outs, sems):
        mine, first, passed, from_sib = plan(xs, outs, sems)
        for cp in from_sib:
            cp.wait_recv()
        for cp in first + [fwd for _, fwd in passed]:
            cp.wait_send()
        for cp in mine:
            cp.wait()

    return _Comm(shards, [jax.ShapeDtypeStruct((N_DEV,) + a.shape, a.dtype) for a in shards],
                 [pltpu.SemaphoreType.DMA((n, 7)), pltpu.SemaphoreType.DMA((n, 7)), pltpu.SemaphoreType.DMA((n,))],
                 start, mid, finish)


def _sibling_comm(gs):
    n = len(gs)

    def plan(g_refs, r_refs, sems):
        send, recv = sems
        x, y, c = _pos()
        return [pltpu.make_async_remote_copy(
            src_ref=g_refs[a].at[_dev_index(k // 2, k % 2, 1 - c)], dst_ref=r_refs[a].at[k],
            send_sem=send.at[a, k], recv_sem=recv.at[a, k], device_id=(x, y, 1 - c), device_id_type=MESH)
            for a in range(n) for k in range(4)]

    def start(*parts):
        for cp in plan(*parts):
            cp.start()

    def mid(*parts):
        pass

    def finish(*parts):
        for cp in plan(*parts):
            cp.wait()

    return _Comm(gs, [jax.ShapeDtypeStruct((4,) + g.shape[1:], g.dtype) for g in gs],
                 [pltpu.SemaphoreType.DMA((n, 4)), pltpu.SemaphoreType.DMA((n, 4))], start, mid, finish)


def _rs_partial(g_all, r1, sc, *, name):
    _, r, c = g_all.shape
    tr = _row_tile(r)

    def body(sc_ref, g_ref, r_ref, o_ref):
        o_ref[...] = (g_ref[...] + r_ref[...]).astype(BF16)

    return pl.pallas_call(
        body, name=name,
        grid_spec=pltpu.PrefetchScalarGridSpec(
            num_scalar_prefetch=1, grid=(3, r // tr),
            in_specs=[pl.BlockSpec((None, tr, c), lambda j, i, s_: (s_[2 + j], i, 0)),
                      pl.BlockSpec((None, tr, c), lambda j, i, s_: (s_[5 + j], i, 0))],
            out_specs=pl.BlockSpec((None, tr, c), lambda j, i, s_: (j, i, 0))),
        out_shape=jax.ShapeDtypeStruct((3, r, c), BF16),
        compiler_params=_params(2),
    )(sc, g_all, r1)


def _cross_comm(ps):
    n = len(ps)

    def plan(p_refs, r_refs, sems):
        send, recv = sems
        x, y, c = _pos()
        return [pltpu.make_async_remote_copy(
            src_ref=p_refs[a].at[j], dst_ref=r_refs[a].at[j], send_sem=send.at[a, j], recv_sem=recv.at[a, j],
            device_id=(*chip, c), device_id_type=MESH)
            for j, chip in enumerate(_other_chips(x, y)) for a in range(n)]

    def start(*parts):
        for cp in plan(*parts):
            cp.start()

    def mid(*parts):
        pass

    def finish(*parts):
        for cp in plan(*parts):
            cp.wait()

    return _Comm(ps, [jax.ShapeDtypeStruct(p.shape, p.dtype) for p in ps],
                 [pltpu.SemaphoreType.DMA((n, 3)), pltpu.SemaphoreType.DMA((n, 3))], start, mid, finish)


SMALL_ROWS = 16


def _all_reduce_small(pack, *, name):
    def body(x_ref, o_ref, buf, send, recv):
        x, y, c = _pos()
        me = _dev_index(x, y, c)
        buf[me] = x_ref[...]
        copies = []
        for k in range(1, N_DEV):
            fx, fy, fc = (k >> 2) & 1, (k >> 1) & 1, k & 1
            peer = (1 - x if fx else x, 1 - y if fy else y, 1 - c if fc else c)
            copies.append(pltpu.make_async_remote_copy(
                src_ref=x_ref, dst_ref=buf.at[me], send_sem=send.at[k - 1], recv_sem=recv.at[k - 1],
                device_id=peer, device_id_type=MESH))
        for cp in copies:
            cp.start()
        for cp in copies:
            cp.wait()
        acc = buf[0]
        for d in range(1, N_DEV):
            acc = acc + buf[d]
        o_ref[...] = acc

    return pl.pallas_call(
        body, name=name,
        in_specs=[pl.BlockSpec(memory_space=pltpu.VMEM)], out_specs=pl.BlockSpec(memory_space=pltpu.VMEM),
        out_shape=jax.ShapeDtypeStruct(pack.shape, F32),
        scratch_shapes=[pltpu.VMEM((N_DEV,) + pack.shape, F32),
                        pltpu.SemaphoreType.DMA((N_DEV - 1,)), pltpu.SemaphoreType.DMA((N_DEV - 1,))],
    )(pack)


def _unshard_cols(g):
    return jnp.transpose(g, (1, 0, 2)).reshape(g.shape[1], N_DEV * g.shape[2])


def _shard_cols(w):
    k, n8 = w.shape
    return jnp.transpose(w.reshape(k, N_DEV, n8 // N_DEV), (1, 0, 2))


def _pad_row(v, width=D_MODEL):
    v = v.reshape(1, -1)
    return jnp.pad(v, ((0, 0), (0, width - v.shape[1])))


def _forward_mixer(l, xc, g_mix, wq, wo, rope, brow, comm_a=None, comm_b=None):
    even = l % 2 == 0
    h1 = _rms_fwd(xc, g_mix, name=f"norm_mix_fwd{l}")
    qkv = _mm(h1, wq, name=f"qkv_fwd{l}", tm=1024, tn=768 if even else 640)
    if even:
        (o_a, st_a), got_a = _sb_fwd(qkv, N_HEADS // 4, name=f"sb_fwd{l}", comm=comm_a)
        (o_b, st_b), got_b = _bias_fwd("dil", qkv, N_HEADS // 4, N_HEADS // 4, rope, name=f"dil_fwd{l}",
                                       comm=comm_b)
        o = jnp.concatenate([o_a, o_b], axis=1)
        att = (o_b, st_a, st_b)
    else:
        assert comm_b is None
        fcol, frow = _fgate_fwd(qkv, brow, name=f"fgate_fwd{l}")
        (o, lse), got_a = _bias_fwd("fox", qkv, 0, N_HEADS // 2, (fcol, frow), name=f"fox_fwd{l}", comm=comm_a)
        got_b = []
        att = (o, lse, fcol, frow)
    o_bf = o.astype(BF16)
    xm = _mm(o_bf, wo, add=xc, name=f"wo_fwd{l}", tm=512, tn=1024)
    return xm, (xc, h1, qkv, att, o_bf), got_a, got_b


def _forward_ffn(l, xm, g_ffn, win_t, wout):
    h2 = _rms_fwd(xm, g_ffn, name=f"norm_ffn_fwd{l}")
    g, u, a = _ffn_in_fwd(h2, win_t, name=f"ffn_in_fwd{l}")
    xo = _mm(a, wout, add=xm, name=f"ffn_out_fwd{l}", tm=512, tn=1024)
    return xo, (xm, h2, (g, u), a)


def _backward_ffn(l, dx, dxb, saved, g_ffn, w, exchange=None):
    _, _, win_t, wout = w
    _, _, _, _, _, xm, h2, gu, a = saved
    dgu = _ffn_out_dx(dxb, wout, *gu, name=f"ffn_out_dx{l}")
    d_wout = _mm(a, dxb, ta=True, name=f"ffn_out_dw{l}", tm=FF_BLK, tn=512)
    d_win_t = _mm(dgu, h2, ta=True, name=f"ffn_in_dw{l}", tm=FF_BLK, tn=1024)
    comm = exchange(d_win_t, d_wout) if exchange is not None else None
    dh2 = _mm(dgu, win_t, name=f"ffn_in_dx{l}", tm=512, tn=1024, tk=FF_BLK, comm=comm)
    dh2, got = dh2 if comm is not None else (dh2, [])
    dxm, dxmb, dg_ffn = _rms_bwd(xm, g_ffn, dh2, dx, name=f"norm_ffn_bwd{l}")
    return dxm, dxmb, dg_ffn, d_win_t, d_wout, got


def _backward_attn(l, dxm, dxmb, saved, g_mix, w, rope, brow, comm_a=None, comm_b=None, exchange=None):
    wq, wo, _, _ = w
    xin, h1, qkv, att, o_bf, _, _, _, _ = saved
    even = l % 2 == 0
    d_wo = _mm(o_bf, dxmb, ta=True, name=f"wo_dw{l}", tm=512, tn=1024)
    do = _mm(dxmb, wo, tb=True, name=f"wo_dx{l}", tm=1024, tn=1024)
    db = None
    if even:
        o_b, st_a, st_b = att
        (dqa, dka, dva), got_a = _sb_bwd(qkv, do, st_a, N_HEADS // 4, 0, name=f"sb_bwd{l}", comm=comm_a)
        (dqb, dkb, dvb), got_b = _bias_bwd("dil", qkv, N_HEADS // 4, N_HEADS // 4, rope, o_b, do,
                                           N_HEADS // 4, st_b, name=f"dil_bwd{l}", comm=comm_b)
        dqkv = jnp.concatenate([dqa, dqb, dka, dkb, dva, dvb], axis=1)
    else:
        assert comm_b is None
        o, lse, fcol, frow = att
        (dq, dk, dv, dfr, dfc), got_a = _bias_bwd("fox", qkv, 0, N_HEADS // 2, (fcol, frow), o, do, 0, lse,
                                                  name=f"fox_bwd{l}", comm=comm_a)
        got_b = []
        dfl, db = _fgate_bwd(dfr, dfc, qkv, brow, name=f"fgate_bwd{l}")
        dqkv = jnp.concatenate([dq, dk, dv, dfl], axis=1)
    if even:
        d_wq = _mm(h1, dqkv, ta=True, name=f"qkv_dw{l}", tm=1024, tn=dqkv.shape[1] // N_DEV, out_planes=N_DEV)
    else:
        d_wq = _mm(h1, dqkv, ta=True, name=f"qkv_dw{l}", tm=1024, tn=640)
    comm = exchange(d_wq, d_wo) if exchange is not None else None
    dh1 = _mm(dqkv, wq, tb=True, name=f"qkv_dx{l}", tm=512, tn=1024, comm=comm)
    dh1, got_x = dh1 if comm is not None else (dh1, [])
    dx, dxb, dg_mix = _rms_bwd(xin, g_mix, dh1, dxm, name=f"norm_mix_bwd{l}")
    return dx, dxb, dg_mix, d_wq, d_wo, db, got_a, got_b, got_x


def kernel(x, norm_mix, w_qkv_even, w_o_even, w_qkvf_odd, b_forget, w_o_odd, norm_ffn, w_ffn_in, w_ffn_out, norm_final, loss_target, m_norm_mix, m_w_qkv_even, m_w_o_even, m_w_qkvf_odd, m_b_forget, m_w_o_odd, m_norm_ffn, m_w_ffn_in, m_w_ffn_out, m_norm_final, v_norm_mix, v_w_qkv_even, v_w_o_even, v_w_qkvf_odd, v_b_forget, v_w_o_odd, v_norm_ffn, v_w_ffn_in, v_w_ffn_out, v_norm_final):
    xi, yi, ci = _pos()
    others = _other_chips(xi, yi)
    sc = jnp.stack([_dev_index(xi, yi, ci), 2 * xi + yi]
                   + [_dev_index(px, py, ci) for px, py in others]
                   + [2 * px + py for px, py in others]).astype(jnp.int32)
    n_odd_cols = w_qkvf_odd.shape[2] * N_DEV

    xs, tgt = x[0], loss_target[0]
    rope = _rope_tables(xs.shape[0])
    brow = [_pad_row(b_forget[i], LANES) for i in range(DEPTH // 2)]
    w_in_t, m_in_t, v_in_t = (jnp.swapaxes(t, 1, 2) for t in (w_ffn_in, m_w_ffn_in, v_w_ffn_in))

    def shards(l):
        even = l % 2 == 0
        wq_s = (w_qkv_even if even else w_qkvf_odd)[l // 2]
        wo_s = (w_o_even if even else w_o_odd)[l // 2]
        return [wq_s.astype(BF16), wo_s.astype(BF16)], [w_in_t[l].astype(BF16), w_ffn_out[l].astype(BF16)]

    def full_mix(l, gq, go):
        wq = _unshard_cols(gq)
        if l % 2 == 1:
            wq = jnp.pad(wq, ((0, 0), (0, QKVF_PAD - n_odd_cols)))
        return wq, go.reshape(D_ATTN, D_MODEL)

    def full_ffn(gi, gout):
        return gi.reshape(2 * D_FF, D_MODEL), gout.reshape(D_FF, D_MODEL)

    mix0, ffn0 = shards(0)
    w_mix = {0: full_mix(0, *_gather_comm(mix0).run("gather_weights0"))}
    w_ffn = {}
    weights, saved = [], []
    xc = xs
    for l in range(DEPTH):
        comm_a = comm_b = None
        if l + 1 < DEPTH:
            mix_n, ffn_n = shards(l + 1)
            if l == 0:
                comm_a, comm_b = _gather_comm(ffn0 + mix_n), _gather_comm(ffn_n)
            elif l % 2 == 0:
                comm_a, comm_b = _gather_comm(ffn_n), _gather_comm(mix_n)
            else:
                comm_a = _gather_comm(mix_n + ffn_n)
        xm, sv_mix, got_a, got_b = _forward_mixer(l, xc, norm_mix[l:l + 1], *w_mix[l], rope, brow[l // 2],
                                                  comm_a, comm_b)
        if l + 1 < DEPTH:
            if l == 0:
                w_ffn[0] = full_ffn(*got_a[:2])
                w_mix[1], w_ffn[1] = full_mix(1, *got_a[2:]), full_ffn(*got_b)
            elif l % 2 == 0:
                w_mix[l + 1], w_ffn[l + 1] = full_mix(l + 1, *got_b), full_ffn(*got_a)
            else:
                w_mix[l + 1], w_ffn[l + 1] = full_mix(l + 1, *got_a[:2]), full_ffn(*got_a[2:])
        xc, sv_ffn = _forward_ffn(l, xm, norm_ffn[l:l + 1], *w_ffn[l])
        weights.append(w_mix[l] + w_ffn[l])
        saved.append(sv_mix + sv_ffn)

    loss_row, dx, dxb, dg_final = _final_loss(xc, norm_final.reshape(1, -1), tgt, name="final_loss")

    sharded = {
        "qkv_even": (w_qkv_even, m_w_qkv_even, v_w_qkv_even), "o_even": (w_o_even, m_w_o_even, v_w_o_even),
        "qkvf_odd": (w_qkvf_odd, m_w_qkvf_odd, v_w_qkvf_odd), "o_odd": (w_o_odd, m_w_o_odd, v_w_o_odd),
        "ffn_in": (w_in_t, m_in_t, v_in_t), "ffn_out": (w_ffn_out, m_w_ffn_out, v_w_ffn_out),
    }
    results = {k: None for k in sharded}

    def chip_sums(gs, r1s, keys, tag):
        ps = [_rs_partial(g, r1, sc, name=f"grads_chip_sum_{tag}_{a}") for a, (g, r1) in enumerate(zip(gs, r1s))]
        return gs, r1s, ps, keys

    held = {}

    def row_chunks(d):
        return d.reshape(N_DEV, d.shape[0] // N_DEV, D_MODEL)

    def to_sibling(tag, col_sharded, odd_qkv=False):
        def make(d_first, d_rows):
            if odd_qkv:
                d_first = d_first[:, :n_odd_cols]
            if d_first.ndim == 2:
                d_first = _shard_cols(d_first) if col_sharded else row_chunks(d_first)
            held[tag] = [d_first, row_chunks(d_rows)]
            return _sibling_comm(held[tag])
        return make

    def update(group, r2s, tag):
        gs, r1s, _, keys = group
        for a, (key, lidx) in enumerate(keys):
            w, m, v = sharded[key]
            results[key] = _adamw_shard(w, m, v, lidx, gs[a], r1s[a], r2s[a], sc, results[key],
                                        name=f"adamw_{key}_{tag}")

    dg_mix, dg_ffn, db_f = [None] * DEPTH, [None] * DEPTH, [None] * (DEPTH // 2)
    pending = None
    for l in reversed(range(DEPTH)):
        even = l % 2 == 0
        dxm, dxmb, dg_ffn[l], _, _, r1s = _backward_ffn(l, dx, dxb, saved[l], norm_ffn[l:l + 1], weights[l],
                                                        to_sibling(f"ffn{l}", col_sharded=False))
        ffn = chip_sums(held[f"ffn{l}"], r1s, [("ffn_in", l), ("ffn_out", l)], f"ffn{l}")
        if even:
            comm_a = _cross_comm(ffn[2])
            comm_b = _cross_comm(pending[2]) if pending is not None else None
        else:
            comm_a = _cross_comm(ffn[2] + (pending[2] if pending is not None else []))
            comm_b = None
        dx, dxb, dg_mix[l], _, _, db, got_a, got_b, r1s = _backward_attn(
            l, dxm, dxmb, saved[l], norm_mix[l:l + 1], weights[l], rope, brow[l // 2], comm_a, comm_b,
            to_sibling(f"mix{l}", col_sharded=True, odd_qkv=not even))
        update(ffn, got_a[:2], f"ffn{l}")
        if pending is not None:
            update(pending, got_b if even else got_a[2:], f"mix{l + 1}")
        if not even:
            db_f[l // 2] = db
        pending = chip_sums(held[f"mix{l}"], r1s,
                            [("qkv_even" if even else "qkvf_odd", l // 2), ("o_even" if even else "o_odd", l // 2)],
                            f"mix{l}")
    update(pending, _cross_comm(pending[2]).run("grads_to_chips_mix0"), "mix0")

    zeros = jnp.zeros((SMALL_ROWS - 11, D_MODEL), F32)
    db_row = _pad_row(jnp.concatenate([d[:, :N_HEADS] for d in db_f], axis=1))
    pack_g = jnp.concatenate(dg_mix + dg_ffn + [dg_final, db_row, _pad_row(loss_row[:, :1]), zeros], axis=0)
    tot = _all_reduce_small(pack_g, name="small_all_reduce")

    def pack(nm, nf, nfin, bf):
        return jnp.concatenate([nm, nf, nfin.reshape(1, -1), _pad_row(bf),
                                jnp.zeros((SMALL_ROWS - 10, D_MODEL), F32)], axis=0)

    d_s, m_s, v_s = _adamw_small(
        pack(norm_mix, norm_ffn, norm_final, b_forget), tot,
        pack(m_norm_mix, m_norm_ffn, m_norm_final, m_b_forget),
        pack(v_norm_mix, v_norm_ffn, v_norm_final, v_b_forget), name="adamw_small")

    def unpack(p):
        nb = b_forget.size
        return {"norm_mix": p[0:DEPTH], "norm_ffn": p[DEPTH:2 * DEPTH], "norm_final": p[2 * DEPTH],
                "b_forget": p[2 * DEPTH + 1, :nb].reshape(b_forget.shape)}

    small = [unpack(tot), unpack(d_s), unpack(m_s), unpack(v_s)]
    loss = tot[2 * DEPTH + 2, 0]

    order = ["norm_mix", "qkv_even", "o_even", "qkvf_odd", "b_forget", "o_odd", "norm_ffn", "ffn_in", "ffn_out",
             "norm_final"]
    outs = [loss, dx[None]]
    for t in range(4):
        for key in order:
            if key in small[t]:
                outs.append(small[t][key])
            elif key == "ffn_in":
                outs.append(jnp.swapaxes(results[key][t], 1, 2))
            else:
                outs.append(results[key][t])
    return tuple(outs)
```

```python
import jax
import jax.numpy as jnp
from jax import lax
from jax.experimental import pallas as pl
from jax.experimental.pallas import tpu as pltpu

F32 = jnp.float32
BF16 = jnp.bfloat16

D_MODEL = 1024
HEAD_DIM = 64
N_HEADS = 16
D_ATTN = N_HEADS * HEAD_DIM
D_FF = 2816
DEPTH = 4
ROPE_THETA = 500000.0
ROT_DIM = HEAD_DIM // 4
RMS_EPS = 1e-5
SCALE = HEAD_DIM ** -0.5
DIL_PATTERNS = ((128, 1), (512, 4), (2048, 16))
N_DEV = 8
QKVF_PAD = 3200

ADAM_LR = 0.001
ADAM_B1 = 0.9
ADAM_B2 = 0.999
ADAM_EPS = 1e-08
ADAM_WD = 0.01
ADAM_STEP = 10

LANES = 128
BLK = 128
TB = 256
NEG = -1e30
VMEM_LIMIT = 48 * 1024 * 1024

MESH = pl.DeviceIdType.MESH


def _params(n_grid=0, **kw):
    sem = ("arbitrary",) * n_grid if n_grid else None
    return pltpu.CompilerParams(dimension_semantics=sem, vmem_limit_bytes=VMEM_LIMIT, **kw)


def _mm(a, b, *, name, ta=False, tb=False, add=None, out_dtype=F32, tm=512, tn=512, tk=None, comm=None,
        out_planes=None):
    a_planes, b_planes = a.ndim == 3, b.ndim == 3
    assert not (b_planes and tb)
    if a_planes and ta:
        m, k = a.shape[0] * a.shape[2], a.shape[1]
        tm = min(tm, a.shape[2])
        assert a.shape[2] % tm == 0
    elif a_planes:
        m, k = a.shape[1], a.shape[0] * a.shape[2]
        tk = a.shape[2] if tk is None else tk
        assert a.shape[2] % tk == 0
    else:
        m = a.shape[1] if ta else a.shape[0]
        k = a.shape[0] if ta else a.shape[1]
    if b_planes:
        n = b.shape[0] * b.shape[2]
        tn = min(tn, b.shape[2])
        assert b.shape[1] == k and b.shape[2] % tn == 0
    else:
        n = b.shape[0] if tb else b.shape[1]
        assert (b.shape[1] if tb else b.shape[0]) == k
    tm, tn = min(tm, m), min(tn, n)
    tk = k if tk is None else min(tk, k)
    assert m % tm == 0 and n % tn == 0 and k % tk == 0, (name, m, n, k, tm, tn, tk)
    nk = k // tk
    dn = (((0 if ta else 1,), (1 if tb else 0,)), ((), ()))

    def body(*refs):
        a_ref, b_ref = refs[0], refs[1]
        add_ref = refs[2] if add is not None else None
        o_ref = refs[3] if add is not None else refs[2]
        part = lax.dot_general(a_ref[...], b_ref[...], dn, preferred_element_type=F32)
        if nk == 1:
            if add_ref is not None:
                part = part + add_ref[...]
            o_ref[...] = part.astype(out_dtype)
            return
        acc_ref = refs[-1]
        kk = pl.program_id(2)

        @pl.when(kk == 0)
        def _():
            acc_ref[...] = part

        @pl.when(kk > 0)
        def _():
            acc_ref[...] += part

        @pl.when(kk == nk - 1)
        def _():
            res = acc_ref[...]
            if add_ref is not None:
                res = res + add_ref[...]
            o_ref[...] = res.astype(out_dtype)

    if a_planes and ta:
        a_per = a.shape[2] // tm
        a_spec = pl.BlockSpec((None, tk, tm), lambda i, j, kk: (i // a_per, kk, i % a_per))
    elif a_planes:
        a_per = a.shape[2] // tk
        a_spec = pl.BlockSpec((None, tm, tk), lambda i, j, kk: (kk // a_per, i, kk % a_per))
    elif ta:
        a_spec = pl.BlockSpec((tk, tm), lambda i, j, kk: (kk, i))
    else:
        a_spec = pl.BlockSpec((tm, tk), lambda i, j, kk: (i, kk))
    if b_planes:
        b_per = b.shape[2] // tn
        b_spec = pl.BlockSpec((None, tk, tn), lambda i, j, kk: (j // b_per, kk, j % b_per))
    elif tb:
        b_spec = pl.BlockSpec((tn, tk), lambda i, j, kk: (j, kk))
    else:
        b_spec = pl.BlockSpec((tk, tn), lambda i, j, kk: (kk, j))
    if out_planes is None:
        o_spec = pl.BlockSpec((tm, tn), lambda i, j, kk: (i, j))
        o_shape = (m, n)
    else:
        o_per = n // out_planes // tn
        assert add is None and n == out_planes * o_per * tn
        o_spec = pl.BlockSpec((None, tm, tn), lambda i, j, kk: (j // o_per, i, j % o_per))
        o_shape = (out_planes, m, n // out_planes)
    in_specs = [a_spec, b_spec] + ([o_spec] if add is not None else [])
    args = (a, b) + ((add,) if add is not None else ())
    (out,), got = _call_hosting(
        body, name=name, grid=(m // tm, n // tn, nk), in_specs=in_specs, args=args, out_specs=[o_spec],
        out_shape=[jax.ShapeDtypeStruct(o_shape, out_dtype)],
        scratch=[pltpu.VMEM((tm, tn), F32)] if nk > 1 else [], comm=comm)
    return out if comm is None else (out, got)


def _rms_fwd(x, g, *, name, tr=256):
    s, d = x.shape

    def body(x_ref, g_ref, h_ref):
        xv = x_ref[...]
        r = lax.rsqrt(jnp.mean(xv * xv, axis=-1, keepdims=True) + RMS_EPS)
        h_ref[...] = (xv * r * g_ref[...]).astype(BF16)

    return pl.pallas_call(
        body, name=name, grid=(s // tr,),
        in_specs=[pl.BlockSpec((tr, d), lambda i: (i, 0)), pl.BlockSpec((1, d), lambda i: (0, 0))],
        out_specs=pl.BlockSpec((tr, d), lambda i: (i, 0)),
        out_shape=jax.ShapeDtypeStruct((s, d), BF16),
        compiler_params=_params(1),
    )(x, g)


def _rms_bwd(x, g, dh, dres, *, name, tr=256):
    s, d = x.shape

    def body(x_ref, g_ref, dh_ref, dres_ref, dx_ref, dxb_ref, dg_ref):
        xv = x_ref[...]
        r = lax.rsqrt(jnp.mean(xv * xv, axis=-1, keepdims=True) + RMS_EPS)
        y = xv * r
        dhv = dh_ref[...]
        dy = dhv * g_ref[...]
        dx = dres_ref[...] + r * (dy - y * jnp.mean(dy * y, axis=-1, keepdims=True))
        dx_ref[...] = dx
        dxb_ref[...] = dx.astype(BF16)
        part = jnp.sum(dhv * y, axis=0, keepdims=True)

        @pl.when(pl.program_id(0) == 0)
        def _():
            dg_ref[...] = part

        @pl.when(pl.program_id(0) > 0)
        def _():
            dg_ref[...] += part

    row = pl.BlockSpec((tr, d), lambda i: (i, 0))
    vec = pl.BlockSpec((1, d), lambda i: (0, 0))
    return pl.pallas_call(
        body, name=name, grid=(s // tr,),
        in_specs=[row, vec, row, row], out_specs=[row, row, vec],
        out_shape=[jax.ShapeDtypeStruct((s, d), F32), jax.ShapeDtypeStruct((s, d), BF16),
                   jax.ShapeDtypeStruct((1, d), F32)],
        compiler_params=_params(1),
    )(x, g, dh, dres)


def _final_loss(x, g, tgt, *, name, tr=256):
    s, d = x.shape

    def body(x_ref, g_ref, t_ref, loss_ref, dx_ref, dxb_ref, dg_ref):
        xv = x_ref[...]
        gv = g_ref[...]
        r = lax.rsqrt(jnp.mean(xv * xv, axis=-1, keepdims=True) + RMS_EPS)
        y = xv * r
        err = y * gv - t_ref[...]
        lpart = 0.5 * jnp.sum(jnp.mean(err * err, axis=-1, keepdims=True), axis=0, keepdims=True)
        dh = err * (1.0 / d)
        dy = dh * gv
        dx = r * (dy - y * jnp.mean(dy * y, axis=-1, keepdims=True))
        dx_ref[...] = dx
        dxb_ref[...] = dx.astype(BF16)
        gpart = jnp.sum(dh * y, axis=0, keepdims=True)
        lrow = jnp.broadcast_to(lpart, (1, LANES))

        @pl.when(pl.program_id(0) == 0)
        def _():
            dg_ref[...] = gpart
            loss_ref[...] = lrow

        @pl.when(pl.program_id(0) > 0)
        def _():
            dg_ref[...] += gpart
            loss_ref[...] += lrow

    row = pl.BlockSpec((tr, d), lambda i: (i, 0))
    vec = pl.BlockSpec((1, d), lambda i: (0, 0))
    lsp = pl.BlockSpec((1, LANES), lambda i: (0, 0))
    return pl.pallas_call(
        body, name=name, grid=(s // tr,),
        in_specs=[row, vec, row], out_specs=[lsp, row, row, vec],
        out_shape=[jax.ShapeDtypeStruct((1, LANES), F32), jax.ShapeDtypeStruct((s, d), F32),
                   jax.ShapeDtypeStruct((s, d), BF16), jax.ShapeDtypeStruct((1, d), F32)],
        compiler_params=_params(1),
    )(x, g, tgt)


FF_BLK = D_FF // 2


def _ffn_in_fwd(h, win_t, *, name, tm=512):
    s, d = h.shape

    def body(h_ref, wg_ref, wu_ref, g_ref, u_ref, a_ref):
        hv = h_ref[...]
        g = _nt(hv, wg_ref[...])
        u = _nt(hv, wu_ref[...])
        g_ref[...] = g
        u_ref[...] = u
        a_ref[...] = (g * jax.nn.sigmoid(g) * u).astype(BF16)

    blk = pl.BlockSpec((tm, FF_BLK), lambda i, j: (i, j))
    f32 = jax.ShapeDtypeStruct((s, D_FF), F32)
    return pl.pallas_call(
        body, name=name, grid=(s // tm, 2),
        in_specs=[pl.BlockSpec((tm, d), lambda i, j: (i, 0)),
                  pl.BlockSpec((FF_BLK, d), lambda i, j: (j, 0)),
                  pl.BlockSpec((FF_BLK, d), lambda i, j: (j + 2, 0))],
        out_specs=[blk, blk, blk],
        out_shape=[f32, f32, jax.ShapeDtypeStruct((s, D_FF), BF16)],
        compiler_params=_params(2),
    )(h, win_t, win_t)


def _ffn_out_dx(dxb, wout, g, u, *, name, tm=512):
    s, d = dxb.shape

    def body(dx_ref, w_ref, g_ref, u_ref, o_ref):
        dav = _nt(dx_ref[...], w_ref[...])
        gv = g_ref[...]
        sg = jax.nn.sigmoid(gv)
        o_ref[0] = (dav * u_ref[...] * (sg * (1.0 + gv * (1.0 - sg)))).astype(BF16)
        o_ref[1] = (dav * gv * sg).astype(BF16)

    blk = pl.BlockSpec((tm, FF_BLK), lambda i, j: (i, j))
    return pl.pallas_call(
        body, name=name, grid=(s // tm, 2),
        in_specs=[pl.BlockSpec((tm, d), lambda i, j: (i, 0)), pl.BlockSpec((FF_BLK, d), lambda i, j: (j, 0)),
                  blk, blk],
        out_specs=pl.BlockSpec((2, tm, FF_BLK), lambda i, j: (0, i, j)),
        out_shape=jax.ShapeDtypeStruct((2, s, D_FF), BF16),
        compiler_params=_params(2),
    )(dxb, wout, g, u)


def _split3(x):
    hi = x.astype(BF16)
    r1 = x - hi.astype(F32)
    mid = r1.astype(BF16)
    lo = (r1 - mid.astype(F32)).astype(BF16)
    return hi, mid, lo


def _dot3(x, m_bf):
    hi, mid, lo = _split3(x)
    return (jnp.dot(hi, m_bf, preferred_element_type=F32)
            + jnp.dot(mid, m_bf, preferred_element_type=F32)
            + jnp.dot(lo, m_bf, preferred_element_type=F32))


def _dot3_left(m_bf, x):
    hi, mid, lo = _split3(x)
    return (jnp.dot(m_bf, hi, preferred_element_type=F32)
            + jnp.dot(m_bf, mid, preferred_element_type=F32)
            + jnp.dot(m_bf, lo, preferred_element_type=F32))


def _dot2(x, m_bf):
    hi = x.astype(BF16)
    lo = (x - hi.astype(F32)).astype(BF16)
    return jnp.dot(hi, m_bf, preferred_element_type=F32) + jnp.dot(lo, m_bf, preferred_element_type=F32)


def _nt(a, b):
    return lax.dot_general(a, b, (((1,), (1,)), ((), ())), preferred_element_type=F32)


def _mm32(a, b):
    return jnp.dot(a, b, preferred_element_type=F32)


def _iota2(shape, dim):
    return lax.broadcasted_iota(jnp.int32, shape, dim)


def _rope_tables(s):
    half = ROT_DIM // 2
    pos = jnp.arange(s, dtype=F32)
    inv_freq = ROPE_THETA ** (-jnp.arange(half, dtype=F32) * 2.0 / ROT_DIM)
    ang = pos[:, None] * inv_freq[None, :]
    cos, sin = jnp.cos(ang), jnp.sin(ang)
    ones = jnp.ones((s, HEAD_DIM - ROT_DIM), F32)
    cos_t = jnp.concatenate([cos, cos, ones], axis=1)
    sin_t = jnp.concatenate([-sin, sin, 0.0 * ones], axis=1)
    idx = jnp.arange(HEAD_DIM)
    partner = jnp.where(idx < half, idx + half, idx - half)
    swap = ((idx[:, None] == partner[None, :]) & (idx[None, :] < ROT_DIM)).astype(F32)
    swap2 = jnp.kron(jnp.eye(2, dtype=F32), swap).astype(BF16)
    return jnp.tile(cos_t, (1, 2)), jnp.tile(sin_t, (1, 2)), swap2


def _rope(x, cos_t, sin_t, swap):
    return x * cos_t + _dot3(x, swap) * sin_t


def _rope_t(g, cos_t, sin_t, swap):
    return g * cos_t + _dot3(g * sin_t, swap)


def _dil_weight(dlt):
    nonneg = dlt >= 0
    w = jnp.zeros(dlt.shape, F32)
    for window, dil in DIL_PATTERNS:
        ok = nonneg & (dlt <= window) & ((dlt & (dil - 1)) == 0)
        w = w + ok.astype(F32)
    return w


FAR_TILES = 3
assert (FAR_TILES - 1) * TB + 1 > DIL_PATTERNS[1][0] and DIL_PATTERNS[2][0] >= 2048


def _dil_bias_scratch():
    return pltpu.VMEM((FAR_TILES + 1, TB, TB), F32)


def _dil_bias_tiles(bias_ref):
    rmc = _iota2((TB, TB), 0) - _iota2((TB, TB), 1)
    for d in range(FAR_TILES + 1):
        w = _dil_weight(d * TB + rmc)
        bias_ref[d] = jnp.where(w > 0.0, jnp.log(jnp.maximum(w, 1.0)), NEG)


def _log_sig_pair(z):
    sp = jnp.log(1.0 + jnp.exp(-jnp.abs(z)))
    return jnp.minimum(z, 0.0) - sp, -jnp.maximum(z, 0.0) - sp


def _log_one_minus_beta(z):
    return -(jnp.maximum(z, 0.0) + jnp.log(1.0 + jnp.exp(-jnp.abs(z))))


def _pair_masks(x, lane_lo):
    z = jnp.zeros_like(x)
    return jnp.where(lane_lo, x, z).astype(BF16), jnp.where(lane_lo, z, x).astype(BF16)


def _rows(i):
    return pl.ds(pl.multiple_of(i * TB, TB), TB)


def _head_spec(s, col0):
    return pl.BlockSpec((s, LANES), lambda p: (0, col0 + p))


def _stat_spec(s):
    return pl.BlockSpec((2, s, 1), lambda p: (p, 0, 0))


def _rowstat_spec(s):
    return pl.BlockSpec((2, 1, s), lambda p: (p, 0, 0))


def _full_spec(shape):
    nd = len(shape)
    return pl.BlockSpec(shape, lambda p: (0,) * nd)


K_COL, V_COL = D_ATTN // LANES, 2 * D_ATTN // LANES


def _bwd_scratch(s):
    return ([pltpu.VMEM((s, LANES), BF16)] * 8 + [pltpu.VMEM((LANES, s), BF16)] * 4
            + [pltpu.VMEM((LANES, s), F32)] * 2)


def _bwd_prep(i, q, k, v, dov, scr, lane_lo, sub_lo):
    qlo, qhi, klo, khi, kbf, vbf, dolo, dohi, qtlo, qthi, dotlo, dothi = scr[:12]
    rows = _rows(i)
    qs = q * SCALE
    qlo[rows, :], qhi[rows, :] = _pair_masks(qs, lane_lo)
    klo[rows, :], khi[rows, :] = _pair_masks(k * SCALE, lane_lo)
    kbf[rows, :] = k.astype(BF16)
    vbf[rows, :] = v.astype(BF16)
    dolo[rows, :], dohi[rows, :] = _pair_masks(dov, lane_lo)
    qtlo[:, rows], qthi[:, rows] = _pair_masks(qs.T, sub_lo)
    dotlo[:, rows], dothi[:, rows] = _pair_masks(dov.T, sub_lo)


def _sb_fwd(qkv, n_pairs, *, name, comm=None):
    s = qkv.shape[0]
    assert s % TB == 0
    nq = s // TB

    def body(q_ref, k_ref, v_ref, o_ref, ct_ref, qlo, qhi, kbf, vlo, vhi, sb0, sb1):
        sbuf = (sb0, sb1)
        lane_lo = _iota2((TB, LANES), 1) < HEAD_DIM

        def prep(i, _):
            rows = _rows(i)
            qlo[rows, :], qhi[rows, :] = _pair_masks(q_ref[rows, :] * SCALE, lane_lo)
            kbf[rows, :] = k_ref[rows, :].astype(BF16)
            vlo[rows, :], vhi[rows, :] = _pair_masks(v_ref[rows, :], lane_lo)
            return 0

        lax.fori_loop(0, nq, prep, 0)
        rmc = _iota2((TB, TB), 0) - _iota2((TB, TB), 1)
        strict = rmc > 0
        u_ge = (rmc >= 0).astype(BF16)
        qm, vm = (qlo, qhi), (vlo, vhi)

        def qloop(i, _):
            rows = _rows(i)

            def logits(kb, carry, diag):
                c = list(carry)
                keys = _rows(kb)
                k = kbf[keys, :]
                zs = [_nt(qm[h][rows, :], k) for h in range(2)]
                lms = [_log_one_minus_beta(z) for z in zs]
                if diag:
                    lms = [jnp.where(strict, lm, 0.0) for lm in lms]
                r_ins = [_dot2(lm, u_ge) for lm in lms]
                for h in range(2):
                    la = zs[h] + r_ins[h] + c[h]
                    sbuf[h][:, keys] = jnp.where(strict, la, NEG) if diag else la
                    c[h] = c[h] + r_ins[h][:, 0:1]
                return tuple(c)

            z1 = jnp.zeros((TB, 1), F32)
            c0, c1 = lax.fori_loop(0, i, lambda t, cr: logits(i - 1 - t, cr, False), logits(i, (z1, z1), True))

            def weigh(kb, acc):
                keys = _rows(kb)
                a_bf = [jnp.exp(sbuf[h][:, keys]).astype(BF16) for h in range(2)]
                return acc + _mm32(a_bf[0], vm[0][keys, :]) + _mm32(a_bf[1], vm[1][keys, :])

            acc = lax.fori_loop(0, i + 1, weigh, jnp.zeros((TB, LANES), F32))
            o_ref[rows, :] = acc
            ct_ref[0, rows, :] = c0
            ct_ref[1, rows, :] = c1
            return 0

        lax.fori_loop(0, nq, qloop, 0)

    return _call_pairs(
        body, name=name, n_pairs=n_pairs, comm=comm,
        in_specs=[_head_spec(s, 0), _head_spec(s, K_COL), _head_spec(s, V_COL)], args=(qkv, qkv, qkv),
        out_specs=[_head_spec(s, 0), _stat_spec(s)],
        out_shape=[jax.ShapeDtypeStruct((s, LANES * n_pairs), F32),
                   jax.ShapeDtypeStruct((2 * n_pairs, s, 1), F32)],
        scratch=[pltpu.VMEM((s, LANES), BF16)] * 5 + [pltpu.VMEM((TB, s), F32)] * 2)


def _sb_bwd(qkv, do, ctot, n_pairs, do_col0, *, name, comm=None):
    s = qkv.shape[0]
    assert s % TB == 0
    nq = s // TB

    def body(q_ref, k_ref, v_ref, do_ref, ct_ref, dq_ref, dk_ref, dv_ref, *scr):
        qlo, qhi, klo, khi, kbf, vbf, dolo, dohi, qtlo, qthi, dotlo, dothi, dkt, dvt = scr
        lane_lo = _iota2((TB, LANES), 1) < HEAD_DIM
        sub_lo = _iota2((LANES, TB), 0) < HEAD_DIM

        def prep(i, _):
            rows = _rows(i)
            _bwd_prep(i, q_ref[rows, :], k_ref[rows, :], v_ref[rows, :], do_ref[rows, :], scr, lane_lo, sub_lo)
            return 0

        lax.fori_loop(0, nq, prep, 0)
        dkt[...] = jnp.zeros_like(dkt)
        dvt[...] = jnp.zeros_like(dvt)
        rmc = _iota2((TB, TB), 0) - _iota2((TB, TB), 1)
        strict = rmc > 0
        u_le = (rmc <= 0).astype(BF16)
        qm, km, dom, qtm, dotm = (qlo, qhi), (klo, khi), (dolo, dohi), (qtlo, qthi), (dotlo, dothi)

        def qloop(i, _):
            rows = _rows(i)
            ct = (ct_ref[0, rows, :], ct_ref[1, rows, :])

            def tile(kb, carry, diag):
                pre, hl, dq = list(carry[0:2]), list(carry[2:4]), carry[4]
                keys = _rows(kb)
                k, v = kbf[keys, :], vbf[keys, :]
                zs = [_nt(qm[h][rows, :], k) for h in range(2)]
                das = [_nt(dom[h][rows, :], v) for h in range(2)]
                lms = [_log_one_minus_beta(z) for z in zs]
                if diag:
                    lms = [jnp.where(strict, lm, 0.0) for lm in lms]
                pins = [_dot2(lm, u_le) for lm in lms]
                gs, lbs = [], []
                a_bf = []
                for h in range(2):
                    lb = zs[h] + lms[h]
                    a = jnp.exp(lb + (ct[h] - pre[h]) - pins[h])
                    if diag:
                        a = jnp.where(strict, a, 0.0)
                    gs.append(a * das[h])
                    lbs.append(lb)
                    a_bf.append(a.astype(BF16))
                hins = [_dot2(g, u_le) for g in gs]
                dk_t, dv_t = dkt[:, keys], dvt[:, keys]
                for h in range(2):
                    g = gs[h]
                    dz = g - jnp.exp(lbs[h]) * (hl[h] + hins[h])
                    if diag:
                        dz = jnp.where(strict, dz, 0.0)
                    dzb = dz.astype(BF16)
                    dq = dq + _mm32(dzb, km[h][keys, :])
                    dk_t = dk_t + _mm32(qtm[h][:, rows], dzb)
                    dv_t = dv_t + _mm32(dotm[h][:, rows], a_bf[h])
                    pre[h] = pre[h] + pins[h][:, TB - 1:TB]
                    hl[h] = hl[h] + hins[h][:, TB - 1:TB]
                dkt[:, keys] = dk_t
                dvt[:, keys] = dv_t
                return pre[0], pre[1], hl[0], hl[1], dq

            z1 = jnp.zeros((TB, 1), F32)
            carry = lax.fori_loop(0, i, lambda kb, cr: tile(kb, cr, False),
                                  (z1, z1, z1, z1, jnp.zeros((TB, LANES), F32)))
            dq = tile(i, carry, True)[4]
            dq_ref[rows, :] = dq.astype(BF16)
            return 0

        lax.fori_loop(0, nq, qloop, 0)

        def wloop(i, _):
            rows = _rows(i)
            dk_ref[rows, :] = dkt[:, rows].T.astype(BF16)
            dv_ref[rows, :] = dvt[:, rows].T.astype(BF16)
            return 0

        lax.fori_loop(0, nq, wloop, 0)

    out = jax.ShapeDtypeStruct((s, LANES * n_pairs), BF16)
    return _call_pairs(
        body, name=name, n_pairs=n_pairs, comm=comm,
        in_specs=[_head_spec(s, 0), _head_spec(s, K_COL), _head_spec(s, V_COL),
                  _head_spec(s, do_col0), _stat_spec(s)], args=(qkv, qkv, qkv, do, ctot),
        out_specs=[_head_spec(s, 0)] * 3, out_shape=[out, out, out], scratch=_bwd_scratch(s))


def _bias_fwd(mode, qkv, head0_col, n_pairs, extra, *, name, comm=None):
    s = qkv.shape[0]
    assert s % TB == 0 and s <= DIL_PATTERNS[2][0]
    nq = s // TB
    fox = mode == "fox"

    def body(q_ref, k_ref, v_ref, e0, e1, *rest):
        if fox:
            o_ref, lse_ref, qlo, qhi, kbf, vx0, vx1, sb0, sb1 = rest
        else:
            e2, o_ref, lse_ref, qlo, qhi, kbf, vx0, vx1, sb0, sb1, bias = rest
            _dil_bias_tiles(bias)
        sbuf = (sb0, sb1)
        lane_lo = _iota2((TB, LANES), 1) < HEAD_DIM

        def prep(i, _):
            rows = _rows(i)
            q, k, v = q_ref[rows, :], k_ref[rows, :], v_ref[rows, :]
            if not fox:
                c, sn, sw = e0[rows, :], e1[rows, :], e2[...]
                q, k = _rope(q, c, sn, sw), _rope(k, c, sn, sw)
            qlo[rows, :], qhi[rows, :] = _pair_masks(q * SCALE, lane_lo)
            kbf[rows, :] = k.astype(BF16)
            one = jnp.ones_like(v)
            vx0[rows, :] = jnp.where(lane_lo, v, one).astype(BF16)
            vx1[rows, :] = jnp.where(lane_lo, one, v).astype(BF16)
            return 0

        lax.fori_loop(0, nq, prep, 0)
        rmc = _iota2((TB, TB), 0) - _iota2((TB, TB), 1)
        qm, vx = (qlo, qhi), (vx0, vx1)

        def qloop(i, _):
            rows = _rows(i)
            if fox:
                fq = (e0[0, rows, :], e0[1, rows, :])

            def scores(kb, carry, diag):
                keys = _rows(kb)
                k = kbf[keys, :]
                scs = [_nt(qm[h][rows, :], k) for h in range(2)]
                if not fox:
                    b = bias[jnp.minimum(i - kb, FAR_TILES)]
                out = []
                for h in range(2):
                    if fox:
                        sc = scs[h] + (fq[h] - e1[h, :, keys])
                        if diag:
                            sc = jnp.where(rmc >= 0, sc, NEG)
                    else:
                        sc = scs[h] + b
                    sbuf[h][:, keys] = sc
                    mx = carry[h]
                    for j in range(TB // LANES):
                        mx = jnp.maximum(mx, sc[:, j * LANES:(j + 1) * LANES])
                    out.append(mx)
                return tuple(out)

            mx0 = jnp.full((TB, LANES), NEG, F32)
            mxs = lax.fori_loop(0, i, lambda kb, cr: scores(kb, cr, False), (mx0, mx0))
            mxs = scores(i, mxs, True)
            m_0, m_1 = (jnp.max(mx, axis=1, keepdims=True) for mx in mxs)

            def weigh(kb, carry):
                keys = _rows(kb)
                ps = [jnp.exp(sbuf[h][:, keys] - m).astype(BF16) for h, m in enumerate((m_0, m_1))]
                return tuple(carry[h] + _mm32(ps[h], vx[h][keys, :]) for h in range(2))

            a0 = jnp.zeros((TB, LANES), F32)
            acc0, acc1 = lax.fori_loop(0, i + 1, weigh, (a0, a0))
            l0, l1 = acc0[:, HEAD_DIM:HEAD_DIM + 1], acc1[:, 0:1]
            o_ref[rows, :] = jnp.where(lane_lo, acc0 / l0, acc1 / l1)
            lse_ref[0, rows, :] = m_0 + jnp.log(l0)
            lse_ref[1, rows, :] = m_1 + jnp.log(l1)
            return 0

        lax.fori_loop(0, nq, qloop, 0)

    hp0 = head0_col
    if fox:
        e_specs = [_stat_spec(s), _rowstat_spec(s)]
    else:
        e_specs = [_full_spec((s, LANES)), _full_spec((s, LANES)), _full_spec((LANES, LANES))]
    return _call_pairs(
        body, name=name, n_pairs=n_pairs, comm=comm,
        in_specs=[_head_spec(s, hp0), _head_spec(s, K_COL + hp0), _head_spec(s, V_COL + hp0)] + e_specs,
        args=(qkv, qkv, qkv, *extra),
        out_specs=[_head_spec(s, 0), _stat_spec(s)],
        out_shape=[jax.ShapeDtypeStruct((s, LANES * n_pairs), F32),
                   jax.ShapeDtypeStruct((2 * n_pairs, s, 1), F32)],
        scratch=([pltpu.VMEM((s, LANES), BF16)] * 5 + [pltpu.VMEM((TB, s), F32)] * 2
                 + ([] if fox else [_dil_bias_scratch()])))


def _bias_bwd(mode, qkv, head0_col, n_pairs, extra, o, do, do_col0, lse, *, name, comm=None):
    s = qkv.shape[0]
    assert s % TB == 0 and s <= DIL_PATTERNS[2][0]
    nq = s // TB
    fox = mode == "fox"

    def body(q_ref, k_ref, v_ref, o_ref, do_ref, lse_ref, e0, e1, *rest):
        if fox:
            dq_ref, dk_ref, dv_ref, dfr_ref, dfc_ref = rest[:5]
            scr = rest[5:]
        else:
            e2, dq_ref, dk_ref, dv_ref = rest[:4]
            scr = rest[4:]
        qlo, qhi, klo, khi, kbf, vbf, dolo, dohi, qtlo, qthi, dotlo, dothi, dkt, dvt = scr[:14]
        if not fox:
            bias = scr[14]
            _dil_bias_tiles(bias)
        lane_lo = _iota2((TB, LANES), 1) < HEAD_DIM
        sub_lo = _iota2((LANES, TB), 0) < HEAD_DIM

        def prep(i, _):
            rows = _rows(i)
            q, k = q_ref[rows, :], k_ref[rows, :]
            if not fox:
                c, sn, sw = e0[rows, :], e1[rows, :], e2[...]
                q, k = _rope(q, c, sn, sw), _rope(k, c, sn, sw)
            _bwd_prep(i, q, k, v_ref[rows, :], do_ref[rows, :], scr, lane_lo, sub_lo)
            return 0

        lax.fori_loop(0, nq, prep, 0)
        dkt[...] = jnp.zeros_like(dkt)
        dvt[...] = jnp.zeros_like(dvt)
        if fox:
            dfr_ref[...] = jnp.zeros_like(dfr_ref)
        rmc = _iota2((TB, TB), 0) - _iota2((TB, TB), 1)
        qm, km, dom, qtm, dotm = (qlo, qhi), (klo, khi), (dolo, dohi), (qtlo, qthi), (dotlo, dothi)

        def qloop(i, _):
            rows = _rows(i)
            prod = do_ref[rows, :] * o_ref[rows, :]
            dsum = (jnp.sum(jnp.where(lane_lo, prod, 0.0), axis=1, keepdims=True),
                    jnp.sum(jnp.where(lane_lo, 0.0, prod), axis=1, keepdims=True))
            lse_i = (lse_ref[0, rows, :], lse_ref[1, rows, :])
            if fox:
                fql = (e0[0, rows, :] - lse_i[0], e0[1, rows, :] - lse_i[1])

            def tile(kb, carry, diag):
                dq, rs = carry[0], list(carry[1:])
                keys = _rows(kb)
                k, v = kbf[keys, :], vbf[keys, :]
                scs = [_nt(qm[h][rows, :], k) for h in range(2)]
                dps = [_nt(dom[h][rows, :], v) for h in range(2)]
                if not fox:
                    b = bias[jnp.minimum(i - kb, FAR_TILES)]
                ps, dss = [], []
                for h in range(2):
                    if fox:
                        sc = scs[h] + (fql[h] - e1[h, :, keys])
                        if diag:
                            sc = jnp.where(rmc >= 0, sc, NEG)
                    else:
                        sc = scs[h] + (b - lse_i[h])
                    p = jnp.exp(sc)
                    dss.append(p * (dps[h] - dsum[h]))
                    ps.append(p.astype(BF16))
                dk_t, dv_t = dkt[:, keys], dvt[:, keys]
                for h in range(2):
                    dsb = dss[h].astype(BF16)
                    dq = dq + _mm32(dsb, km[h][keys, :])
                    dk_t = dk_t + _mm32(qtm[h][:, rows], dsb)
                    dv_t = dv_t + _mm32(dotm[h][:, rows], ps[h])
                    if fox:
                        dfr_ref[h, :, keys] -= jnp.sum(dss[h], axis=0, keepdims=True)
                        for j in range(TB // LANES):
                            rs[h] = rs[h] + dss[h][:, j * LANES:(j + 1) * LANES]
                dkt[:, keys] = dk_t
                dvt[:, keys] = dv_t
                return (dq, *rs)

            z2 = jnp.zeros((TB, LANES), F32)
            carry = lax.fori_loop(0, i, lambda kb, cr: tile(kb, cr, False), (z2, z2, z2) if fox else (z2,))
            carry = tile(i, carry, True)
            if fox:
                dfc_ref[0, rows, :] = jnp.sum(carry[1], axis=1, keepdims=True)
                dfc_ref[1, rows, :] = jnp.sum(carry[2], axis=1, keepdims=True)
            dq = carry[0]
            if not fox:
                dq = _rope_t(dq, e0[rows, :], e1[rows, :], e2[...])
            dq_ref[rows, :] = dq.astype(BF16)
            return 0

        lax.fori_loop(0, nq, qloop, 0)

        def wloop(i, _):
            rows = _rows(i)
            dk = dkt[:, rows].T
            if not fox:
                dk = _rope_t(dk, e0[rows, :], e1[rows, :], e2[...])
            dk_ref[rows, :] = dk.astype(BF16)
            dv_ref[rows, :] = dvt[:, rows].T.astype(BF16)
            return 0

        lax.fori_loop(0, nq, wloop, 0)

    hp0 = head0_col
    out = jax.ShapeDtypeStruct((s, LANES * n_pairs), BF16)
    out_specs = [_head_spec(s, 0)] * 3
    out_shape = [out, out, out]
    if fox:
        e_specs = [_stat_spec(s), _rowstat_spec(s)]
        out_specs += [_rowstat_spec(s), _stat_spec(s)]
        out_shape += [jax.ShapeDtypeStruct((2 * n_pairs, 1, s), F32), jax.ShapeDtypeStruct((2 * n_pairs, s, 1), F32)]
    else:
        e_specs = [_full_spec((s, LANES)), _full_spec((s, LANES)), _full_spec((LANES, LANES))]
    return _call_pairs(
        body, name=name, n_pairs=n_pairs, comm=comm,
        in_specs=[_head_spec(s, hp0), _head_spec(s, K_COL + hp0), _head_spec(s, V_COL + hp0),
                  _head_spec(s, 0), _head_spec(s, do_col0), _stat_spec(s)] + e_specs,
        args=(qkv, qkv, qkv, o, do, lse, *extra),
        out_specs=out_specs, out_shape=out_shape,
        scratch=_bwd_scratch(s) + ([] if fox else [_dil_bias_scratch()]))


F_COL = 3 * D_ATTN // LANES


def _fgate_fwd(qkvf, brow, *, name):
    s = qkvf.shape[0]
    nb = s // BLK

    def body(f_ref, b_ref, fc_ref, fr_ref, fs):
        row, col = _iota2((BLK, BLK), 0), _iota2((BLK, BLK), 1)
        l_incl = (col <= row).astype(BF16)

        def step(i, carry):
            r0 = pl.multiple_of(i * BLK, BLK)
            lf, _ = _log_sig_pair(f_ref[pl.ds(r0, BLK), :] + b_ref[...])
            fblk = carry + _dot3_left(l_incl, lf)
            fs[pl.ds(r0, BLK), :] = fblk
            return fblk[BLK - 1:BLK, :]

        lax.fori_loop(0, nb, step, jnp.zeros((1, LANES), F32))
        ft = fs[...].T
        for h in range(N_HEADS):
            fc_ref[h, :, :] = fs[:, h:h + 1]
            fr_ref[h, :, :] = ft[h:h + 1, :]

    return pl.pallas_call(
        body, name=name, grid=(1,),
        in_specs=[pl.BlockSpec((s, LANES), lambda i: (0, F_COL)), pl.BlockSpec((1, LANES), lambda i: (0, 0))],
        out_specs=[pl.BlockSpec((N_HEADS, s, 1), lambda i: (0, 0, 0)),
                   pl.BlockSpec((N_HEADS, 1, s), lambda i: (0, 0, 0))],
        out_shape=[jax.ShapeDtypeStruct((N_HEADS, s, 1), F32), jax.ShapeDtypeStruct((N_HEADS, 1, s), F32)],
        scratch_shapes=[pltpu.VMEM((s, LANES), F32)],
        compiler_params=_params(1),
    )(qkvf, brow)


def _fgate_bwd(dfr, dfc, qkvf, brow, *, name):
    s = qkvf.shape[0]
    nb = s // BLK

    def body(dfr_ref, dfc_ref, f_ref, b_ref, dfl_ref, db_ref, ts, fs):
        ts[...] = jnp.zeros_like(ts)
        for h in range(N_HEADS):
            ts[h:h + 1, :] = dfr_ref[h]
        fs[...] = ts[...].T
        for h in range(N_HEADS):
            fs[:, h:h + 1] += dfc_ref[h]
        row, col = _iota2((BLK, BLK), 0), _iota2((BLK, BLK), 1)
        u_incl = (col >= row).astype(BF16)
        head_lane = _iota2((BLK, LANES), 1) < N_HEADS

        def step(ii, carry):
            tail, db = carry
            r0 = pl.multiple_of((nb - 1 - ii) * BLK, BLK)
            rblk = tail + _dot3_left(u_incl, fs[pl.ds(r0, BLK), :])
            _, lsn = _log_sig_pair(f_ref[pl.ds(r0, BLK), :] + b_ref[...])
            dfl = jnp.where(head_lane, rblk * jnp.exp(lsn), 0.0)
            dfl_ref[pl.ds(r0, BLK), :] = dfl.astype(BF16)
            return rblk[0:1, :], db + jnp.sum(dfl, axis=0, keepdims=True)

        z = jnp.zeros((1, LANES), F32)
        _, db = lax.fori_loop(0, nb, step, (z, z))
        db_ref[...] = db

    return pl.pallas_call(
        body, name=name, grid=(1,),
        in_specs=[pl.BlockSpec((N_HEADS, 1, s), lambda i: (0, 0, 0)), pl.BlockSpec((N_HEADS, s, 1), lambda i: (0, 0, 0)),
                  pl.BlockSpec((s, LANES), lambda i: (0, F_COL)), pl.BlockSpec((1, LANES), lambda i: (0, 0))],
        out_specs=[pl.BlockSpec((s, LANES), lambda i: (0, 0)), pl.BlockSpec((1, LANES), lambda i: (0, 0))],
        out_shape=[jax.ShapeDtypeStruct((s, LANES), BF16), jax.ShapeDtypeStruct((1, LANES), F32)],
        scratch_shapes=[pltpu.VMEM((LANES, s), F32), pltpu.VMEM((s, LANES), F32)],
        compiler_params=_params(1),
    )(dfr, dfc, qkvf, brow)


def _adamw_math(w, g, m, v):
    m2 = ADAM_B1 * m + (1.0 - ADAM_B1) * g
    v2 = ADAM_B2 * v + (1.0 - ADAM_B2) * (g * g)
    m_hat = m2 / (1.0 - ADAM_B1 ** ADAM_STEP)
    v_hat = v2 / (1.0 - ADAM_B2 ** ADAM_STEP)
    delta = -ADAM_LR * (m_hat / (jnp.sqrt(v_hat) + ADAM_EPS) + ADAM_WD * w)
    return delta, m2, v2


def _row_tile(r, cap=256, mult=16):
    best = None
    for t in range(mult, min(r, cap) + 1, mult):
        if r % t == 0:
            best = t
    assert best is not None, r
    return best


def _adamw_shard(w, m, v, lidx, g_all, r1, r2, sc, prev, *, name):
    nl, r, c = w.shape
    tr = _row_tile(r)

    def body(sc_ref, w_ref, m_ref, v_ref, g_ref, r1_ref, r2_ref, *rest):
        go_ref, d_ref, mo_ref, vo_ref = rest[-4:]
        g = g_ref[...] + r1_ref[...]
        g = g + r2_ref[0].astype(F32)
        g = g + r2_ref[1].astype(F32)
        g = g + r2_ref[2].astype(F32)
        delta, m2, v2 = _adamw_math(w_ref[...], g, m_ref[...], v_ref[...])
        go_ref[...] = g
        d_ref[...] = delta
        mo_ref[...] = m2
        vo_ref[...] = v2

    lay = pl.BlockSpec((None, tr, c), lambda i, s_: (lidx, i, 0))
    in_specs = [lay, lay, lay,
                pl.BlockSpec((None, tr, c), lambda i, s_: (s_[0], i, 0)),
                pl.BlockSpec((None, tr, c), lambda i, s_: (s_[1], i, 0)),
                pl.BlockSpec((3, tr, c), lambda i, s_: (0, i, 0))]
    args = [sc, w, m, v, g_all, r1, r2]
    aliases = {}
    if prev is not None:
        in_specs += [pl.BlockSpec(memory_space=pl.ANY)] * 4
        aliases = {7 + t: t for t in range(4)}
        args += list(prev)
    shp = jax.ShapeDtypeStruct((nl, r, c), F32)
    return pl.pallas_call(
        body, name=name,
        grid_spec=pltpu.PrefetchScalarGridSpec(
            num_scalar_prefetch=1, grid=(r // tr,), in_specs=in_specs, out_specs=[lay] * 4),
        out_shape=[shp] * 4, input_output_aliases=aliases,
        compiler_params=_params(1),
    )(*args)


def _adamw_small(w, g, m, v, *, name):
    def body(w_ref, g_ref, m_ref, v_ref, d_ref, mo_ref, vo_ref):
        delta, m2, v2 = _adamw_math(w_ref[...], g_ref[...], m_ref[...], v_ref[...])
        d_ref[...] = delta
        mo_ref[...] = m2
        vo_ref[...] = v2

    shp = jax.ShapeDtypeStruct(w.shape, F32)
    return pl.pallas_call(body, name=name, out_shape=[shp] * 3, compiler_params=_params())(w, g, m, v)


def _pos():
    return lax.axis_index("x"), lax.axis_index("y"), lax.axis_index("c")


def _other_chips(x, y):
    return [(1 - x, y), (x, 1 - y), (1 - x, 1 - y)]


def _dev_index(x, y, c):
    return 4 * x + 2 * y + c


HBM_SPEC = pl.BlockSpec(memory_space=pltpu.HBM)


class _Comm:
    def __init__(self, inputs, out_shape, scratch, start, mid, finish):
        self.inputs, self.out_shape, self.scratch = list(inputs), list(out_shape), list(scratch)
        self.start, self.mid, self.finish = start, mid, finish

    def run(self, name):
        n_in, n_out = len(self.inputs), len(self.out_shape)

        def body(*refs):
            parts = refs[:n_in], refs[n_in:n_in + n_out], refs[n_in + n_out:]
            self.start(*parts)
            self.mid(*parts)
            self.finish(*parts)

        return pl.pallas_call(
            body, name=name, in_specs=[HBM_SPEC] * n_in, out_specs=[HBM_SPEC] * n_out,
            out_shape=self.out_shape, scratch_shapes=self.scratch)(*self.inputs)


def _call_hosting(body, *, name, grid, in_specs, args, out_specs, out_shape, scratch, comm=None):
    if comm is None:
        res = pl.pallas_call(
            body, name=name, grid=grid, in_specs=in_specs, out_specs=out_specs, out_shape=out_shape,
            scratch_shapes=scratch, compiler_params=_params(len(grid)))(*args)
        return list(res), []
    sizes = (len(in_specs), len(comm.inputs), len(out_specs), len(comm.out_shape), len(scratch), len(comm.scratch))

    def fused(*refs):
        parts, o = [], 0
        for n in sizes:
            parts.append(refs[o:o + n])
            o += n
        h_in, c_in, h_out, c_out, h_scr, c_scr = parts
        first = last = None
        for d, n in enumerate(grid):
            p = pl.program_id(d)
            first = (p == 0) if first is None else jnp.logical_and(first, p == 0)
            last = (p == n - 1) if last is None else jnp.logical_and(last, p == n - 1)

        @pl.when(first)
        def _():
            comm.start(c_in, c_out, c_scr)

        @pl.when(last)
        def _():
            comm.mid(c_in, c_out, c_scr)

        body(*h_in, *h_out, *h_scr)

        @pl.when(last)
        def _():
            comm.finish(c_in, c_out, c_scr)

    res = pl.pallas_call(
        fused, name=name, grid=grid,
        in_specs=list(in_specs) + [HBM_SPEC] * sizes[1], out_specs=list(out_specs) + [HBM_SPEC] * sizes[3],
        out_shape=list(out_shape) + comm.out_shape, scratch_shapes=list(scratch) + comm.scratch,
        compiler_params=_params(len(grid)))(*args, *comm.inputs)
    return list(res[:sizes[2]]), list(res[sizes[2]:])


def _call_pairs(body, *, name, n_pairs, **kw):
    return _call_hosting(body, name=name, grid=(n_pairs,), **kw)


def _gather_comm(shards):
    n = len(shards)

    def plan(xs, outs, sems):
        send, recv, loc = sems
        x, y, c = _pos()
        me, sib = (x, y, c), (x, y, 1 - c)
        chips = _other_chips(x, y)

        def copy(a, k, block, to, src=None):
            dst = outs[a].at[_dev_index(*block)]
            return pltpu.make_async_remote_copy(
                src_ref=dst if src is None else src, dst_ref=dst,
                send_sem=send.at[a, k], recv_sem=recv.at[a, k], device_id=to, device_id_type=MESH)

        mine = [pltpu.make_async_copy(xs[a], outs[a].at[_dev_index(*me)], loc.at[a]) for a in range(n)]
        first = []
        for a in range(n):
            first.append(copy(a, 0, me, sib, src=xs[a]))
            first += [copy(a, 1 + j, me, (*chip, c), src=xs[a]) for j, chip in enumerate(chips)]
        passed = [(copy(a, 1 + j, (*chip, c), me), copy(a, 4 + j, (*chip, c), sib))
                  for j, chip in enumerate(chips) for a in range(n)]
        from_sib = [copy(a, 0, sib, me) for a in range(n)]
        from_sib += [copy(a, 4 + j, (*chip, 1 - c), me) for a in range(n) for j, chip in enumerate(chips)]
        return mine, first, passed, from_sib

    def start(xs, outs, sems):
        mine, first, _, _ = plan(xs, outs, sems)
        for cp in mine + first:
            cp.start()

    def mid(xs, outs, sems):
        for arrival, fwd in plan(xs, outs, sems)[2]:
            arrival.wait_recv()
            fwd.start()

    def finish(xs, outs, sems):
        mine, first, passed, from_sib = plan(xs, outs, sems)
        for cp in from_sib:
            cp.wait_recv()
        for cp in first + [fwd for _, fwd in passed]:
            cp.wait_send()
        for cp in mine:
            cp.wait()

    return _Comm(shards, [jax.ShapeDtypeStruct((N_DEV,) + a.shape, a.dtype) for a in shards],
                 [pltpu.SemaphoreType.DMA((n, 7)), pltpu.SemaphoreType.DMA((n, 7)), pltpu.SemaphoreType.DMA((n,))],
                 start, mid, finish)


def _sibling_comm(gs):
    n = len(gs)

    def plan(g_refs, r_refs, sems):
        send, recv = sems
        x, y, c = _pos()
        return [pltpu.make_async_remote_copy(
            src_ref=g_refs[a].at[_dev_index(k // 2, k % 2, 1 - c)], dst_ref=r_refs[a].at[k],
            send_sem=send.at[a, k], recv_sem=recv.at[a, k], device_id=(x, y, 1 - c), device_id_type=MESH)
            for a in range(n) for k in range(4)]

    def start(*parts):
        for cp in plan(*parts):
            cp.start()

    def mid(*parts):
        pass

    def finish(*parts):
        for cp in plan(*parts):
            cp.wait()

    return _Comm(gs, [jax.ShapeDtypeStruct((4,) + g.shape[1:], g.dtype) for g in gs],
                 [pltpu.SemaphoreType.DMA((n, 4)), pltpu.SemaphoreType.DMA((n, 4))], start, mid, finish)


def _rs_partial(g_all, r1, sc, *, name):
    _, r, c = g_all.shape
    tr = _row_tile(r)

    def body(sc_ref, g_ref, r_ref, o_ref):
        o_ref[...] = (g_ref[...] + r_ref[...]).astype(BF16)

    return pl.pallas_call(
        body, name=name,
        grid_spec=pltpu.PrefetchScalarGridSpec(
            num_scalar_prefetch=1, grid=(3, r // tr),
            in_specs=[pl.BlockSpec((None, tr, c), lambda j, i, s_: (s_[2 + j], i, 0)),
                      pl.BlockSpec((None, tr, c), lambda j, i, s_: (s_[5 + j], i, 0))],
            out_specs=pl.BlockSpec((None, tr, c), lambda j, i, s_: (j, i, 0))),
        out_shape=jax.ShapeDtypeStruct((3, r, c), BF16),
        compiler_params=_params(2),
    )(sc, g_all, r1)


def _cross_comm(ps):
    n = len(ps)

    def plan(p_refs, r_refs, sems):
        send, recv = sems
        x, y, c = _pos()
        return [pltpu.make_async_remote_copy(
            src_ref=p_refs[a].at[j], dst_ref=r_refs[a].at[j], send_sem=send.at[a, j], recv_sem=recv.at[a, j],
            device_id=(*chip, c), device_id_type=MESH)
            for j, chip in enumerate(_other_chips(x, y)) for a in range(n)]

    def start(*parts):
        for cp in plan(*parts):
            cp.start()

    def mid(*parts):
        pass

    def finish(*parts):
        for cp in plan(*parts):
            cp.wait()

    return _Comm(ps, [jax.ShapeDtypeStruct(p.shape, p.dtype) for p in ps],
                 [pltpu.SemaphoreType.DMA((n, 3)), pltpu.SemaphoreType.DMA((n, 3))], start, mid, finish)


SMALL_ROWS = 16


def _all_reduce_small(pack, *, name):
    def body(x_ref, o_ref, buf, send, recv):
        x, y, c = _pos()
        me = _dev_index(x, y, c)
        buf[me] = x_ref[...]
        copies = []
        for k in range(1, N_DEV):
            fx, fy, fc = (k >> 2) & 1, (k >> 1) & 1, k & 1
            peer = (1 - x if fx else x, 1 - y if fy else y, 1 - c if fc else c)
            copies.append(pltpu.make_async_remote_copy(
                src_ref=x_ref, dst_ref=buf.at[me], send_sem=send.at[k - 1], recv_sem=recv.at[k - 1],
                device_id=peer, device_id_type=MESH))
        for cp in copies:
            cp.start()
        for cp in copies:
            cp.wait()
        acc = buf[0]
        for d in range(1, N_DEV):
            acc = acc + buf[d]
        o_ref[...] = acc

    return pl.pallas_call(
        body, name=name,
        in_specs=[pl.BlockSpec(memory_space=pltpu.VMEM)], out_specs=pl.BlockSpec(memory_space=pltpu.VMEM),
        out_shape=jax.ShapeDtypeStruct(pack.shape, F32),
        scratch_shapes=[pltpu.VMEM((N_DEV,) + pack.shape, F32),
                        pltpu.SemaphoreType.DMA((N_DEV - 1,)), pltpu.SemaphoreType.DMA((N_DEV - 1,))],
    )(pack)


def _unshard_cols(g):
    return jnp.transpose(g, (1, 0, 2)).reshape(g.shape[1], N_DEV * g.shape[2])


def _shard_cols(w):
    k, n8 = w.shape
    return jnp.transpose(w.reshape(k, N_DEV, n8 // N_DEV), (1, 0, 2))


def _pad_row(v, width=D_MODEL):
    v = v.reshape(1, -1)
    return jnp.pad(v, ((0, 0), (0, width - v.shape[1])))


def _forward_mixer(l, xc, g_mix, wq, wo, rope, brow, comm_a=None, comm_b=None):
    even = l % 2 == 0
    h1 = _rms_fwd(xc, g_mix, name=f"norm_mix_fwd{l}")
    qkv = _mm(h1, wq, name=f"qkv_fwd{l}", tm=1024, tn=768 if even else 640)
    if even:
        (o_a, st_a), got_a = _sb_fwd(qkv, N_HEADS // 4, name=f"sb_fwd{l}", comm=comm_a)
        (o_b, st_b), got_b = _bias_fwd("dil", qkv, N_HEADS // 4, N_HEADS // 4, rope, name=f"dil_fwd{l}",
                                       comm=comm_b)
        o = jnp.concatenate([o_a, o_b], axis=1)
        att = (o_b, st_a, st_b)
    else:
        assert comm_b is None
        fcol, frow = _fgate_fwd(qkv, brow, name=f"fgate_fwd{l}")
        (o, lse), got_a = _bias_fwd("fox", qkv, 0, N_HEADS // 2, (fcol, frow), name=f"fox_fwd{l}", comm=comm_a)
        got_b = []
        att = (o, lse, fcol, frow)
    o_bf = o.astype(BF16)
    if callable(wo):
        wo = wo(got_a, got_b)
    xm = _mm(o_bf, wo, add=xc, name=f"wo_fwd{l}", tm=512, tn=1024)
    return xm, (xc, h1, qkv, att, o_bf), got_a, got_b


def _forward_ffn(l, xm, g_ffn, win_t, wout):
    h2 = _rms_fwd(xm, g_ffn, name=f"norm_ffn_fwd{l}")
    g, u, a = _ffn_in_fwd(h2, win_t, name=f"ffn_in_fwd{l}")
    xo = _mm(a, wout, add=xm, name=f"ffn_out_fwd{l}", tm=512, tn=1024)
    return xo, (xm, h2, (g, u), a)


def _backward_ffn(l, dx, dxb, saved, g_ffn, w, exchange=None):
    _, _, win_t, wout = w
    _, _, _, _, _, xm, h2, gu, a = saved
    dgu = _ffn_out_dx(dxb, wout, *gu, name=f"ffn_out_dx{l}")
    d_wout = _mm(a, dxb, ta=True, name=f"ffn_out_dw{l}", tm=FF_BLK, tn=512)
    d_win_t = _mm(dgu, h2, ta=True, name=f"ffn_in_dw{l}", tm=FF_BLK, tn=1024)
    comm = exchange(d_win_t, d_wout) if exchange is not None else None
    dh2 = _mm(dgu, win_t, name=f"ffn_in_dx{l}", tm=512, tn=1024, tk=FF_BLK, comm=comm)
    dh2, got = dh2 if comm is not None else (dh2, [])
    dxm, dxmb, dg_ffn = _rms_bwd(xm, g_ffn, dh2, dx, name=f"norm_ffn_bwd{l}")
    return dxm, dxmb, dg_ffn, d_win_t, d_wout, got


def _backward_attn(l, dxm, dxmb, saved, g_mix, w, rope, brow, comm_a=None, comm_b=None, exchange=None):
    wq, wo, _, _ = w
    xin, h1, qkv, att, o_bf, _, _, _, _ = saved
    even = l % 2 == 0
    d_wo = _mm(o_bf, dxmb, ta=True, name=f"wo_dw{l}", tm=512, tn=1024)
    do = _mm(dxmb, wo, tb=True, name=f"wo_dx{l}", tm=1024, tn=1024)
    db = None
    if even:
        o_b, st_a, st_b = att
        (dqa, dka, dva), got_a = _sb_bwd(qkv, do, st_a, N_HEADS // 4, 0, name=f"sb_bwd{l}", comm=comm_a)
        (dqb, dkb, dvb), got_b = _bias_bwd("dil", qkv, N_HEADS // 4, N_HEADS // 4, rope, o_b, do,
                                           N_HEADS // 4, st_b, name=f"dil_bwd{l}", comm=comm_b)
        dqkv = jnp.concatenate([dqa, dqb, dka, dkb, dva, dvb], axis=1)
    else:
        assert comm_b is None
        o, lse, fcol, frow = att
        (dq, dk, dv, dfr, dfc), got_a = _bias_bwd("fox", qkv, 0, N_HEADS // 2, (fcol, frow), o, do, 0, lse,
                                                  name=f"fox_bwd{l}", comm=comm_a)
        got_b = []
        dfl, db = _fgate_bwd(dfr, dfc, qkv, brow, name=f"fgate_bwd{l}")
        dqkv = jnp.concatenate([dq, dk, dv, dfl], axis=1)
    if even:
        d_wq = _mm(h1, dqkv, ta=True, name=f"qkv_dw{l}", tm=1024, tn=dqkv.shape[1] // N_DEV, out_planes=N_DEV)
    else:
        d_wq = _mm(h1, dqkv, ta=True, name=f"qkv_dw{l}", tm=1024, tn=640)
    comm = exchange(d_wq, d_wo) if exchange is not None else None
    dh1 = _mm(dqkv, wq, tb=True, name=f"qkv_dx{l}", tm=512, tn=1024, comm=comm)
    dh1, got_x = dh1 if comm is not None else (dh1, [])
    dx, dxb, dg_mix = _rms_bwd(xin, g_mix, dh1, dxm, name=f"norm_mix_bwd{l}")
    return dx, dxb, dg_mix, d_wq, d_wo, db, got_a, got_b, got_x


def kernel(x, norm_mix, w_qkv_even, w_o_even, w_qkvf_odd, b_forget, w_o_odd, norm_ffn, w_ffn_in, w_ffn_out, norm_final, loss_target, m_norm_mix, m_w_qkv_even, m_w_o_even, m_w_qkvf_odd, m_b_forget, m_w_o_odd, m_norm_ffn, m_w_ffn_in, m_w_ffn_out, m_norm_final, v_norm_mix, v_w_qkv_even, v_w_o_even, v_w_qkvf_odd, v_b_forget, v_w_o_odd, v_norm_ffn, v_w_ffn_in, v_w_ffn_out, v_norm_final):
    xi, yi, ci = _pos()
    others = _other_chips(xi, yi)
    sc = jnp.stack([_dev_index(xi, yi, ci), 2 * xi + yi]
                   + [_dev_index(px, py, ci) for px, py in others]
                   + [2 * px + py for px, py in others]).astype(jnp.int32)
    n_odd_cols = w_qkvf_odd.shape[2] * N_DEV

    xs, tgt = x[0], loss_target[0]
    rope = _rope_tables(xs.shape[0])
    brow = [_pad_row(b_forget[i], LANES) for i in range(DEPTH // 2)]
    w_in_t, m_in_t, v_in_t = (jnp.swapaxes(t, 1, 2) for t in (w_ffn_in, m_w_ffn_in, v_w_ffn_in))

    def shards(l):
        even = l % 2 == 0
        return {"wq": (w_qkv_even if even else w_qkvf_odd)[l // 2].astype(BF16),
                "wo": (w_o_even if even else w_o_odd)[l // 2].astype(BF16),
                "win": w_in_t[l].astype(BF16), "wout": w_ffn_out[l].astype(BF16)}

    def full_wq(l, gq):
        wq = _unshard_cols(gq)
        if l % 2 == 1:
            wq = jnp.pad(wq, ((0, 0), (0, QKVF_PAD - n_odd_cols)))
        return wq

    def full_wo(go):
        return go.reshape(D_ATTN, D_MODEL)

    sh = [shards(l) for l in range(DEPTH)]
    wq = {0: full_wq(0, _gather_comm([sh[0]["wq"]]).run("gather_weights0")[0])}
    weights, saved = [], []
    xc = xs
    for l in range(DEPTH):
        even = l % 2 == 0
        nxt = [sh[l + 1]["wq"]] if l + 1 < DEPTH else []
        if even:
            comm_a = _gather_comm([sh[l]["win"], sh[l]["wout"]])
            comm_b = _gather_comm([sh[l]["wo"]] + nxt)
        else:
            comm_a, comm_b = _gather_comm([sh[l]["wo"], sh[l]["win"], sh[l]["wout"]] + nxt), None
        xm, sv_mix, got_a, got_b = _forward_mixer(
            l, xc, norm_mix[l:l + 1], wq[l], (lambda ga, gb: full_wo(gb[0] if even else ga[0])), rope,
            brow[l // 2], comm_a, comm_b)
        if even:
            (gi, gout), go, gq_next = got_a, got_b[0], got_b[1:]
        else:
            go, gi, gout, gq_next = got_a[0], got_a[1], got_a[2], got_a[3:]
        if gq_next:
            wq[l + 1] = full_wq(l + 1, gq_next[0])
        win_t, wout = gi.reshape(2 * D_FF, D_MODEL), gout.reshape(D_FF, D_MODEL)
        xc, sv_ffn = _forward_ffn(l, xm, norm_ffn[l:l + 1], win_t, wout)
        weights.append((wq[l], full_wo(go), win_t, wout))
        saved.append(sv_mix + sv_ffn)

    loss_row, dx, dxb, dg_final = _final_loss(xc, norm_final.reshape(1, -1), tgt, name="final_loss")

    sharded = {
        "qkv_even": (w_qkv_even, m_w_qkv_even, v_w_qkv_even), "o_even": (w_o_even, m_w_o_even, v_w_o_even),
        "qkvf_odd": (w_qkvf_odd, m_w_qkvf_odd, v_w_qkvf_odd), "o_odd": (w_o_odd, m_w_o_odd, v_w_o_odd),
        "ffn_in": (w_in_t, m_in_t, v_in_t), "ffn_out": (w_ffn_out, m_w_ffn_out, v_w_ffn_out),
    }
    results = {k: None for k in sharded}

    def chip_sums(gs, r1s, keys, tag):
        ps = [_rs_partial(g, r1, sc, name=f"grads_chip_sum_{tag}_{a}") for a, (g, r1) in enumerate(zip(gs, r1s))]
        return gs, r1s, ps, keys

    held = {}

    def row_chunks(d):
        return d.reshape(N_DEV, d.shape[0] // N_DEV, D_MODEL)

    def to_sibling(tag, col_sharded, odd_qkv=False):
        def make(d_first, d_rows):
            if odd_qkv:
                d_first = d_first[:, :n_odd_cols]
            if d_first.ndim == 2:
                d_first = _shard_cols(d_first) if col_sharded else row_chunks(d_first)
            held[tag] = [d_first, row_chunks(d_rows)]
            return _sibling_comm(held[tag])
        return make

    def update(group, r2s, tag):
        gs, r1s, _, keys = group
        for a, (key, lidx) in enumerate(keys):
            w, m, v = sharded[key]
            results[key] = _adamw_shard(w, m, v, lidx, gs[a], r1s[a], r2s[a], sc, results[key],
                                        name=f"adamw_{key}_{tag}")

    dg_mix, dg_ffn, db_f = [None] * DEPTH, [None] * DEPTH, [None] * (DEPTH // 2)
    pending = None
    for l in reversed(range(DEPTH)):
        even = l % 2 == 0
        dxm, dxmb, dg_ffn[l], _, _, r1s = _backward_ffn(l, dx, dxb, saved[l], norm_ffn[l:l + 1], weights[l],
                                                        to_sibling(f"ffn{l}", col_sharded=False))
        ffn = chip_sums(held[f"ffn{l}"], r1s, [("ffn_in", l), ("ffn_out", l)], f"ffn{l}")
        if even:
            comm_a = _cross_comm(ffn[2])
            comm_b = _cross_comm(pending[2]) if pending is not None else None
        else:
            comm_a = _cross_comm(ffn[2] + (pending[2] if pending is not None else []))
            comm_b = None
        dx, dxb, dg_mix[l], _, _, db, got_a, got_b, r1s = _backward_attn(
            l, dxm, dxmb, saved[l], norm_mix[l:l + 1], weights[l], rope, brow[l // 2], comm_a, comm_b,
            to_sibling(f"mix{l}", col_sharded=True, odd_qkv=not even))
        update(ffn, got_a[:2], f"ffn{l}")
        if pending is not None:
            update(pending, got_b if even else got_a[2:], f"mix{l + 1}")
        if not even:
            db_f[l // 2] = db
        pending = chip_sums(held[f"mix{l}"], r1s,
                            [("qkv_even" if even else "qkvf_odd", l // 2), ("o_even" if even else "o_odd", l // 2)],
                            f"mix{l}")
    update(pending, _cross_comm(pending[2]).run("grads_to_chips_mix0"), "mix0")

    zeros = jnp.zeros((SMALL_ROWS - 11, D_MODEL), F32)
    db_row = _pad_row(jnp.concatenate([d[:, :N_HEADS] for d in db_f], axis=1))
    pack_g = jnp.concatenate(dg_mix + dg_ffn + [dg_final, db_row, _pad_row(loss_row[:, :1]), zeros], axis=0)
    tot = _all_reduce_small(pack_g, name="small_all_reduce")

    def pack(nm, nf, nfin, bf):
        return jnp.concatenate([nm, nf, nfin.reshape(1, -1), _pad_row(bf),
                                jnp.zeros((SMALL_ROWS - 10, D_MODEL), F32)], axis=0)

    d_s, m_s, v_s = _adamw_small(
        pack(norm_mix, norm_ffn, norm_final, b_forget), tot,
        pack(m_norm_mix, m_norm_ffn, m_norm_final, m_b_forget),
        pack(v_norm_mix, v_norm_ffn, v_norm_final, v_b_forget), name="adamw_small")

    def unpack(p):
        nb = b_forget.size
        return {"norm_mix": p[0:DEPTH], "norm_ffn": p[DEPTH:2 * DEPTH], "norm_final": p[2 * DEPTH],
                "b_forget": p[2 * DEPTH + 1, :nb].reshape(b_forget.shape)}

    small = [unpack(tot), unpack(d_s), unpack(m_s), unpack(v_s)]
    loss = tot[2 * DEPTH + 2, 0]

    order = ["norm_mix", "qkv_even", "o_even", "qkvf_odd", "b_forget", "o_odd", "norm_ffn", "ffn_in", "ffn_out",
             "norm_final"]
    outs = [loss, dx[None]]
    for t in range(4):
        for key in order:
            if key in small[t]:
                outs.append(small[t][key])
            elif key == "ffn_in":
                outs.append(jnp.swapaxes(results[key][t], 1, 2))
            else:
                outs.append(results[key][t])
    return tuple(outs)
```

```python
import jax
import jax.numpy as jnp
from jax import lax
from jax.experimental import pallas as pl
from jax.experimental.pallas import tpu as pltpu

F32 = jnp.float32
BF16 = jnp.bfloat16

D_MODEL = 1024
HEAD_DIM = 64
N_HEADS = 16
D_ATTN = N_HEADS * HEAD_DIM
D_FF = 2816
DEPTH = 4
ROPE_THETA = 500000.0
ROT_DIM = HEAD_DIM // 4
RMS_EPS = 1e-5
SCALE = HEAD_DIM ** -0.5
DIL_PATTERNS = ((128, 1), (512, 4), (2048, 16))
N_DEV = 8
QKVF_PAD = 3200

ADAM_LR = 0.001
ADAM_B1 = 0.9
ADAM_B2 = 0.999
ADAM_EPS = 1e-08
ADAM_WD = 0.01
ADAM_STEP = 10

LANES = 128
BLK = 128
TB = 256
NEG = -1e30
VMEM_LIMIT = 48 * 1024 * 1024

MESH = pl.DeviceIdType.MESH


def _params(n_grid=0, **kw):
    sem = ("arbitrary",) * n_grid if n_grid else None
    return pltpu.CompilerParams(dimension_semantics=sem, vmem_limit_bytes=VMEM_LIMIT, **kw)


def _mm(a, b, *, name, ta=False, tb=False, add=None, out_dtype=F32, tm=512, tn=512, tk=None, comm=None,
        out_planes=None, norm_bwd=None):
    a_planes, b_planes = a.ndim == 3, b.ndim == 3
    assert not (b_planes and tb)
    if a_planes and ta:
        m, k = a.shape[0] * a.shape[2], a.shape[1]
        tm = min(tm, a.shape[2])
        assert a.shape[2] % tm == 0
    elif a_planes:
        m, k = a.shape[1], a.shape[0] * a.shape[2]
        tk = a.shape[2] if tk is None else tk
        assert a.shape[2] % tk == 0
    else:
        m = a.shape[1] if ta else a.shape[0]
        k = a.shape[0] if ta else a.shape[1]
    if b_planes:
        n = b.shape[0] * b.shape[2]
        tn = min(tn, b.shape[2])
        assert b.shape[1] == k and b.shape[2] % tn == 0
    else:
        n = b.shape[0] if tb else b.shape[1]
        assert (b.shape[1] if tb else b.shape[0]) == k
    tm, tn = min(tm, m), min(tn, n)
    tk = k if tk is None else min(tk, k)
    assert m % tm == 0 and n % tn == 0 and k % tk == 0, (name, m, n, k, tm, tn, tk)
    nk = k // tk
    dn = (((0 if ta else 1,), (1 if tb else 0,)), ((), ()))

    extras = list(norm_bwd) if norm_bwd is not None else ([add] if add is not None else [])
    n_out = 3 if norm_bwd is not None else 1
    assert norm_bwd is None or (add is None and out_planes is None and tn == n)

    def body(*refs):
        a_ref, b_ref = refs[0], refs[1]
        extra = refs[2:2 + len(extras)]
        outs = refs[2 + len(extras):2 + len(extras) + n_out]
        part = lax.dot_general(a_ref[...], b_ref[...], dn, preferred_element_type=F32)
        first_rows = pl.program_id(0) == 0

        def store(res):
            if norm_bwd is None:
                if add is not None:
                    res = res + extra[0][...]
                outs[0][...] = res.astype(out_dtype)
                return
            x_ref, g_ref, dres_ref = extra
            dx_ref, dxb_ref, dg_ref = outs
            xv = x_ref[...]
            r = lax.rsqrt(jnp.mean(xv * xv, axis=-1, keepdims=True) + RMS_EPS)
            y = xv * r
            dy = res * g_ref[...]
            dx = dres_ref[...] + r * (dy - y * jnp.mean(dy * y, axis=-1, keepdims=True))
            dx_ref[...] = dx
            dxb_ref[...] = dx.astype(BF16)
            gpart = jnp.sum(res * y, axis=0, keepdims=True)

            @pl.when(first_rows)
            def _():
                dg_ref[...] = gpart

            @pl.when(jnp.logical_not(first_rows))
            def _():
                dg_ref[...] += gpart

        if nk == 1:
            store(part)
            return
        acc_ref = refs[-1]
        kk = pl.program_id(2)

        @pl.when(kk == 0)
        def _():
            acc_ref[...] = part

        @pl.when(kk > 0)
        def _():
            acc_ref[...] += part

        @pl.when(kk == nk - 1)
        def _():
            store(acc_ref[...])

    if a_planes and ta:
        a_per = a.shape[2] // tm
        a_spec = pl.BlockSpec((None, tk, tm), lambda i, j, kk: (i // a_per, kk, i % a_per))
    elif a_planes:
        a_per = a.shape[2] // tk
        a_spec = pl.BlockSpec((None, tm, tk), lambda i, j, kk: (kk // a_per, i, kk % a_per))
    elif ta:
        a_spec = pl.BlockSpec((tk, tm), lambda i, j, kk: (kk, i))
    else:
        a_spec = pl.BlockSpec((tm, tk), lambda i, j, kk: (i, kk))
    if b_planes:
        b_per = b.shape[2] // tn
        b_spec = pl.BlockSpec((None, tk, tn), lambda i, j, kk: (j // b_per, kk, j % b_per))
    elif tb:
        b_spec = pl.BlockSpec((tn, tk), lambda i, j, kk: (j, kk))
    else:
        b_spec = pl.BlockSpec((tk, tn), lambda i, j, kk: (kk, j))
    if out_planes is None:
        o_spec = pl.BlockSpec((tm, tn), lambda i, j, kk: (i, j))
        o_shape = (m, n)
    else:
        o_per = n // out_planes // tn
        assert add is None and n == out_planes * o_per * tn
        o_spec = pl.BlockSpec((None, tm, tn), lambda i, j, kk: (j // o_per, i, j % o_per))
        o_shape = (out_planes, m, n // out_planes)
    if norm_bwd is not None:
        vec = pl.BlockSpec((1, n), lambda i, j, kk: (0, 0))
        in_specs = [a_spec, b_spec, o_spec, vec, o_spec]
        out_specs = [o_spec, o_spec, vec]
        out_shape = [jax.ShapeDtypeStruct((m, n), F32), jax.ShapeDtypeStruct((m, n), BF16),
                     jax.ShapeDtypeStruct((1, n), F32)]
    else:
        in_specs = [a_spec, b_spec] + ([o_spec] if add is not None else [])
        out_specs = [o_spec]
        out_shape = [jax.ShapeDtypeStruct(o_shape, out_dtype)]
    outs, got = _call_hosting(
        body, name=name, grid=(m // tm, n // tn, nk), in_specs=in_specs, args=(a, b, *extras),
        out_specs=out_specs, out_shape=out_shape,
        scratch=[pltpu.VMEM((tm, tn), F32)] if nk > 1 else [], comm=comm)
    out = tuple(outs) if norm_bwd is not None else outs[0]
    return out if comm is None else (out, got)


def _rms_fwd(x, g, *, name, tr=256):
    s, d = x.shape

    def body(x_ref, g_ref, h_ref):
        xv = x_ref[...]
        r = lax.rsqrt(jnp.mean(xv * xv, axis=-1, keepdims=True) + RMS_EPS)
        h_ref[...] = (xv * r * g_ref[...]).astype(BF16)

    return pl.pallas_call(
        body, name=name, grid=(s // tr,),
        in_specs=[pl.BlockSpec((tr, d), lambda i: (i, 0)), pl.BlockSpec((1, d), lambda i: (0, 0))],
        out_specs=pl.BlockSpec((tr, d), lambda i: (i, 0)),
        out_shape=jax.ShapeDtypeStruct((s, d), BF16),
        compiler_params=_params(1),
    )(x, g)


def _final_loss(x, g, tgt, *, name, tr=256):
    s, d = x.shape

    def body(x_ref, g_ref, t_ref, loss_ref, dx_ref, dxb_ref, dg_ref):
        xv = x_ref[...]
        gv = g_ref[...]
        r = lax.rsqrt(jnp.mean(xv * xv, axis=-1, keepdims=True) + RMS_EPS)
        y = xv * r
        err = y * gv - t_ref[...]
        lpart = 0.5 * jnp.sum(jnp.mean(err * err, axis=-1, keepdims=True), axis=0, keepdims=True)
        dh = err * (1.0 / d)
        dy = dh * gv
        dx = r * (dy - y * jnp.mean(dy * y, axis=-1, keepdims=True))
        dx_ref[...] = dx
        dxb_ref[...] = dx.astype(BF16)
        gpart = jnp.sum(dh * y, axis=0, keepdims=True)
        lrow = jnp.broadcast_to(lpart, (1, LANES))

        @pl.when(pl.program_id(0) == 0)
        def _():
            dg_ref[...] = gpart
            loss_ref[...] = lrow

        @pl.when(pl.program_id(0) > 0)
        def _():
            dg_ref[...] += gpart
            loss_ref[...] += lrow

    row = pl.BlockSpec((tr, d), lambda i: (i, 0))
    vec = pl.BlockSpec((1, d), lambda i: (0, 0))
    lsp = pl.BlockSpec((1, LANES), lambda i: (0, 0))
    return pl.pallas_call(
        body, name=name, grid=(s // tr,),
        in_specs=[row, vec, row], out_specs=[lsp, row, row, vec],
        out_shape=[jax.ShapeDtypeStruct((1, LANES), F32), jax.ShapeDtypeStruct((s, d), F32),
                   jax.ShapeDtypeStruct((s, d), BF16), jax.ShapeDtypeStruct((1, d), F32)],
        compiler_params=_params(1),
    )(x, g, tgt)


FF_BLK = D_FF // 2


def _ffn_in_fwd(h, win_t, *, name, tm=512):
    s, d = h.shape

    def body(h_ref, wg_ref, wu_ref, g_ref, u_ref, a_ref):
        hv = h_ref[...]
        g = _nt(hv, wg_ref[...])
        u = _nt(hv, wu_ref[...])
        g_ref[...] = g
        u_ref[...] = u
        a_ref[...] = (g * jax.nn.sigmoid(g) * u).astype(BF16)

    blk = pl.BlockSpec((tm, FF_BLK), lambda i, j: (i, j))
    f32 = jax.ShapeDtypeStruct((s, D_FF), F32)
    return pl.pallas_call(
        body, name=name, grid=(s // tm, 2),
        in_specs=[pl.BlockSpec((tm, d), lambda i, j: (i, 0)),
                  pl.BlockSpec((FF_BLK, d), lambda i, j: (j, 0)),
                  pl.BlockSpec((FF_BLK, d), lambda i, j: (j + 2, 0))],
        out_specs=[blk, blk, blk],
        out_shape=[f32, f32, jax.ShapeDtypeStruct((s, D_FF), BF16)],
        compiler_params=_params(2),
    )(h, win_t, win_t)


def _ffn_out_dx(dxb, wout, g, u, *, name, tm=512):
    s, d = dxb.shape

    def body(dx_ref, w_ref, g_ref, u_ref, o_ref):
        dav = _nt(dx_ref[...], w_ref[...])
        gv = g_ref[...]
        sg = jax.nn.sigmoid(gv)
        o_ref[0] = (dav * u_ref[...] * (sg * (1.0 + gv * (1.0 - sg)))).astype(BF16)
        o_ref[1] = (dav * gv * sg).astype(BF16)

    blk = pl.BlockSpec((tm, FF_BLK), lambda i, j: (i, j))
    return pl.pallas_call(
        body, name=name, grid=(s // tm, 2),
        in_specs=[pl.BlockSpec((tm, d), lambda i, j: (i, 0)), pl.BlockSpec((FF_BLK, d), lambda i, j: (j, 0)),
                  blk, blk],
        out_specs=pl.BlockSpec((2, tm, FF_BLK), lambda i, j: (0, i, j)),
        out_shape=jax.ShapeDtypeStruct((2, s, D_FF), BF16),
        compiler_params=_params(2),
    )(dxb, wout, g, u)


def _split3(x):
    hi = x.astype(BF16)
    r1 = x - hi.astype(F32)
    mid = r1.astype(BF16)
    lo = (r1 - mid.astype(F32)).astype(BF16)
    return hi, mid, lo


def _dot3(x, m_bf):
    hi, mid, lo = _split3(x)
    return (jnp.dot(hi, m_bf, preferred_element_type=F32)
            + jnp.dot(mid, m_bf, preferred_element_type=F32)
            + jnp.dot(lo, m_bf, preferred_element_type=F32))


def _dot3_left(m_bf, x):
    hi, mid, lo = _split3(x)
    return (jnp.dot(m_bf, hi, preferred_element_type=F32)
            + jnp.dot(m_bf, mid, preferred_element_type=F32)
            + jnp.dot(m_bf, lo, preferred_element_type=F32))


def _dot2(x, m_bf):
    hi = x.astype(BF16)
    lo = (x - hi.astype(F32)).astype(BF16)
    return jnp.dot(hi, m_bf, preferred_element_type=F32) + jnp.dot(lo, m_bf, preferred_element_type=F32)


def _nt(a, b):
    return lax.dot_general(a, b, (((1,), (1,)), ((), ())), preferred_element_type=F32)


def _mm32(a, b):
    return jnp.dot(a, b, preferred_element_type=F32)


def _iota2(shape, dim):
    return lax.broadcasted_iota(jnp.int32, shape, dim)


def _rope_tables(s):
    half = ROT_DIM // 2
    pos = jnp.arange(s, dtype=F32)
    inv_freq = ROPE_THETA ** (-jnp.arange(half, dtype=F32) * 2.0 / ROT_DIM)
    ang = pos[:, None] * inv_freq[None, :]
    cos, sin = jnp.cos(ang), jnp.sin(ang)
    ones = jnp.ones((s, HEAD_DIM - ROT_DIM), F32)
    cos_t = jnp.concatenate([cos, cos, ones], axis=1)
    sin_t = jnp.concatenate([-sin, sin, 0.0 * ones], axis=1)
    idx = jnp.arange(HEAD_DIM)
    partner = jnp.where(idx < half, idx + half, idx - half)
    swap = ((idx[:, None] == partner[None, :]) & (idx[None, :] < ROT_DIM)).astype(F32)
    swap2 = jnp.kron(jnp.eye(2, dtype=F32), swap).astype(BF16)
    return jnp.tile(cos_t, (1, 2)), jnp.tile(sin_t, (1, 2)), swap2


def _rope(x, cos_t, sin_t, swap):
    return x * cos_t + _dot3(x, swap) * sin_t


def _rope_t(g, cos_t, sin_t, swap):
    return g * cos_t + _dot3(g * sin_t, swap)


def _dil_weight(dlt):
    nonneg = dlt >= 0
    w = jnp.zeros(dlt.shape, F32)
    for window, dil in DIL_PATTERNS:
        ok = nonneg & (dlt <= window) & ((dlt & (dil - 1)) == 0)
        w = w + ok.astype(F32)
    return w


FAR_TILES = 3
assert (FAR_TILES - 1) * TB + 1 > DIL_PATTERNS[1][0] and DIL_PATTERNS[2][0] >= 2048


def _dil_bias_scratch():
    return pltpu.VMEM((FAR_TILES + 1, TB, TB), F32)


def _dil_bias_tiles(bias_ref):
    rmc = _iota2((TB, TB), 0) - _iota2((TB, TB), 1)
    for d in range(FAR_TILES + 1):
        w = _dil_weight(d * TB + rmc)
        bias_ref[d] = jnp.where(w > 0.0, jnp.log(jnp.maximum(w, 1.0)), NEG)


def _log_sig_pair(z):
    sp = jnp.log(1.0 + jnp.exp(-jnp.abs(z)))
    return jnp.minimum(z, 0.0) - sp, -jnp.maximum(z, 0.0) - sp


def _log_one_minus_beta(z):
    return -(jnp.maximum(z, 0.0) + jnp.log(1.0 + jnp.exp(-jnp.abs(z))))


def _pair_masks(x, lane_lo):
    z = jnp.zeros_like(x)
    return jnp.where(lane_lo, x, z).astype(BF16), jnp.where(lane_lo, z, x).astype(BF16)


def _rows(i):
    return pl.ds(pl.multiple_of(i * TB, TB), TB)


def _head_spec(s, col0):
    return pl.BlockSpec((s, LANES), lambda p: (0, col0 + p))


def _stat_spec(s):
    return pl.BlockSpec((2, s, 1), lambda p: (p, 0, 0))


def _rowstat_spec(s):
    return pl.BlockSpec((2, 1, s), lambda p: (p, 0, 0))


def _full_spec(shape):
    nd = len(shape)
    return pl.BlockSpec(shape, lambda p: (0,) * nd)


K_COL, V_COL = D_ATTN // LANES, 2 * D_ATTN // LANES


def _bwd_scratch(s):
    return ([pltpu.VMEM((s, LANES), BF16)] * 8 + [pltpu.VMEM((LANES, s), BF16)] * 4
            + [pltpu.VMEM((LANES, s), F32)] * 2)


def _bwd_prep(i, q, k, v, dov, scr, lane_lo, sub_lo):
    qlo, qhi, klo, khi, kbf, vbf, dolo, dohi, qtlo, qthi, dotlo, dothi = scr[:12]
    rows = _rows(i)
    qs = q * SCALE
    qlo[rows, :], qhi[rows, :] = _pair_masks(qs, lane_lo)
    klo[rows, :], khi[rows, :] = _pair_masks(k * SCALE, lane_lo)
    kbf[rows, :] = k.astype(BF16)
    vbf[rows, :] = v.astype(BF16)
    dolo[rows, :], dohi[rows, :] = _pair_masks(dov, lane_lo)
    qtlo[:, rows], qthi[:, rows] = _pair_masks(qs.T, sub_lo)
    dotlo[:, rows], dothi[:, rows] = _pair_masks(dov.T, sub_lo)


def _sb_fwd(qkv, n_pairs, *, name, comm=None):
    s = qkv.shape[0]
    assert s % TB == 0
    nq = s // TB

    def body(q_ref, k_ref, v_ref, o_ref, ct_ref, qlo, qhi, kbf, vlo, vhi, sb0, sb1):
        sbuf = (sb0, sb1)
        lane_lo = _iota2((TB, LANES), 1) < HEAD_DIM

        def prep(i, _):
            rows = _rows(i)
            qlo[rows, :], qhi[rows, :] = _pair_masks(q_ref[rows, :] * SCALE, lane_lo)
            kbf[rows, :] = k_ref[rows, :].astype(BF16)
            vlo[rows, :], vhi[rows, :] = _pair_masks(v_ref[rows, :], lane_lo)
            return 0

        lax.fori_loop(0, nq, prep, 0)
        rmc = _iota2((TB, TB), 0) - _iota2((TB, TB), 1)
        strict = rmc > 0
        u_ge = (rmc >= 0).astype(BF16)
        qm, vm = (qlo, qhi), (vlo, vhi)

        def qloop(i, _):
            rows = _rows(i)

            def logits(kb, carry, diag):
                c = list(carry)
                keys = _rows(kb)
                k = kbf[keys, :]
                zs = [_nt(qm[h][rows, :], k) for h in range(2)]
                lms = [_log_one_minus_beta(z) for z in zs]
                if diag:
                    lms = [jnp.where(strict, lm, 0.0) for lm in lms]
                r_ins = [_dot2(lm, u_ge) for lm in lms]
                for h in range(2):
                    la = zs[h] + r_ins[h] + c[h]
                    sbuf[h][:, keys] = jnp.where(strict, la, NEG) if diag else la
                    c[h] = c[h] + r_ins[h][:, 0:1]
                return tuple(c)

            z1 = jnp.zeros((TB, 1), F32)
            c0, c1 = lax.fori_loop(0, i, lambda t, cr: logits(i - 1 - t, cr, False), logits(i, (z1, z1), True))

            def weigh(kb, acc):
                keys = _rows(kb)
                a_bf = [jnp.exp(sbuf[h][:, keys]).astype(BF16) for h in range(2)]
                return acc + _mm32(a_bf[0], vm[0][keys, :]) + _mm32(a_bf[1], vm[1][keys, :])

            acc = lax.fori_loop(0, i + 1, weigh, jnp.zeros((TB, LANES), F32))
            o_ref[rows, :] = acc
            ct_ref[0, rows, :] = c0
            ct_ref[1, rows, :] = c1
            return 0

        lax.fori_loop(0, nq, qloop, 0)

    return _call_pairs(
        body, name=name, n_pairs=n_pairs, comm=comm,
        in_specs=[_head_spec(s, 0), _head_spec(s, K_COL), _head_spec(s, V_COL)], args=(qkv, qkv, qkv),
        out_specs=[_head_spec(s, 0), _stat_spec(s)],
        out_shape=[jax.ShapeDtypeStruct((s, LANES * n_pairs), F32),
                   jax.ShapeDtypeStruct((2 * n_pairs, s, 1), F32)],
        scratch=[pltpu.VMEM((s, LANES), BF16)] * 5 + [pltpu.VMEM((TB, s), F32)] * 2)


def _sb_bwd(qkv, do, ctot, n_pairs, do_col0, *, name, comm=None):
    s = qkv.shape[0]
    assert s % TB == 0
    nq = s // TB

    def body(q_ref, k_ref, v_ref, do_ref, ct_ref, dq_ref, dk_ref, dv_ref, *scr):
        qlo, qhi, klo, khi, kbf, vbf, dolo, dohi, qtlo, qthi, dotlo, dothi, dkt, dvt = scr
        lane_lo = _iota2((TB, LANES), 1) < HEAD_DIM
        sub_lo = _iota2((LANES, TB), 0) < HEAD_DIM

        def prep(i, _):
            rows = _rows(i)
            _bwd_prep(i, q_ref[rows, :], k_ref[rows, :], v_ref[rows, :], do_ref[rows, :], scr, lane_lo, sub_lo)
            return 0

        lax.fori_loop(0, nq, prep, 0)
        dkt[...] = jnp.zeros_like(dkt)
        dvt[...] = jnp.zeros_like(dvt)
        rmc = _iota2((TB, TB), 0) - _iota2((TB, TB), 1)
        strict = rmc > 0
        u_le = (rmc <= 0).astype(BF16)
        qm, km, dom, qtm, dotm = (qlo, qhi), (klo, khi), (dolo, dohi), (qtlo, qthi), (dotlo, dothi)

        def qloop(i, _):
            rows = _rows(i)
            ct = (ct_ref[0, rows, :], ct_ref[1, rows, :])

            def tile(kb, carry, diag):
                pre, hl, dq = list(carry[0:2]), list(carry[2:4]), carry[4]
                keys = _rows(kb)
                k, v = kbf[keys, :], vbf[keys, :]
                zs = [_nt(qm[h][rows, :], k) for h in range(2)]
                das = [_nt(dom[h][rows, :], v) for h in range(2)]
                lms = [_log_one_minus_beta(z) for z in zs]
                if diag:
                    lms = [jnp.where(strict, lm, 0.0) for lm in lms]
                pins = [_dot2(lm, u_le) for lm in lms]
                gs, lbs = [], []
                a_bf = []
                for h in range(2):
                    lb = zs[h] + lms[h]
                    a = jnp.exp(lb + (ct[h] - pre[h]) - pins[h])
                    if diag:
                        a = jnp.where(strict, a, 0.0)
                    gs.append(a * das[h])
                    lbs.append(lb)
                    a_bf.append(a.astype(BF16))
                hins = [_dot2(g, u_le) for g in gs]
                dk_t, dv_t = dkt[:, keys], dvt[:, keys]
                for h in range(2):
                    g = gs[h]
                    dz = g - jnp.exp(lbs[h]) * (hl[h] + hins[h])
                    if diag:
                        dz = jnp.where(strict, dz, 0.0)
                    dzb = dz.astype(BF16)
                    dq = dq + _mm32(dzb, km[h][keys, :])
                    dk_t = dk_t + _mm32(qtm[h][:, rows], dzb)
                    dv_t = dv_t + _mm32(dotm[h][:, rows], a_bf[h])
                    pre[h] = pre[h] + pins[h][:, TB - 1:TB]
                    hl[h] = hl[h] + hins[h][:, TB - 1:TB]
                dkt[:, keys] = dk_t
                dvt[:, keys] = dv_t
                return pre[0], pre[1], hl[0], hl[1], dq

            z1 = jnp.zeros((TB, 1), F32)
            carry = lax.fori_loop(0, i, lambda kb, cr: tile(kb, cr, False),
                                  (z1, z1, z1, z1, jnp.zeros((TB, LANES), F32)))
            dq = tile(i, carry, True)[4]
            dq_ref[rows, :] = dq.astype(BF16)
            return 0

        lax.fori_loop(0, nq, qloop, 0)

        def wloop(i, _):
            rows = _rows(i)
            dk_ref[rows, :] = dkt[:, rows].T.astype(BF16)
            dv_ref[rows, :] = dvt[:, rows].T.astype(BF16)
            return 0

        lax.fori_loop(0, nq, wloop, 0)

    out = jax.ShapeDtypeStruct((s, LANES * n_pairs), BF16)
    return _call_pairs(
        body, name=name, n_pairs=n_pairs, comm=comm,
        in_specs=[_head_spec(s, 0), _head_spec(s, K_COL), _head_spec(s, V_COL),
                  _head_spec(s, do_col0), _stat_spec(s)], args=(qkv, qkv, qkv, do, ctot),
        out_specs=[_head_spec(s, 0)] * 3, out_shape=[out, out, out], scratch=_bwd_scratch(s))


def _bias_fwd(mode, qkv, head0_col, n_pairs, extra, *, name, comm=None):
    s = qkv.shape[0]
    assert s % TB == 0 and s <= DIL_PATTERNS[2][0]
    nq = s // TB
    fox = mode == "fox"

    def body(q_ref, k_ref, v_ref, e0, e1, *rest):
        if fox:
            o_ref, lse_ref, qlo, qhi, kbf, vx0, vx1, sb0, sb1 = rest
        else:
            e2, o_ref, lse_ref, qlo, qhi, kbf, vx0, vx1, sb0, sb1, bias = rest
            _dil_bias_tiles(bias)
        sbuf = (sb0, sb1)
        lane_lo = _iota2((TB, LANES), 1) < HEAD_DIM

        def prep(i, _):
            rows = _rows(i)
            q, k, v = q_ref[rows, :], k_ref[rows, :], v_ref[rows, :]
            if not fox:
                c, sn, sw = e0[rows, :], e1[rows, :], e2[...]
                q, k = _rope(q, c, sn, sw), _rope(k, c, sn, sw)
            qlo[rows, :], qhi[rows, :] = _pair_masks(q * SCALE, lane_lo)
            kbf[rows, :] = k.astype(BF16)
            one = jnp.ones_like(v)
            vx0[rows, :] = jnp.where(lane_lo, v, one).astype(BF16)
            vx1[rows, :] = jnp.where(lane_lo, one, v).astype(BF16)
            return 0

        lax.fori_loop(0, nq, prep, 0)
        rmc = _iota2((TB, TB), 0) - _iota2((TB, TB), 1)
        qm, vx = (qlo, qhi), (vx0, vx1)

        def qloop(i, _):
            rows = _rows(i)
            if fox:
                fq = (e0[0, rows, :], e0[1, rows, :])

            def scores(kb, carry, diag):
                keys = _rows(kb)
                k = kbf[keys, :]
                scs = [_nt(qm[h][rows, :], k) for h in range(2)]
                if not fox:
                    b = bias[jnp.minimum(i - kb, FAR_TILES)]
                out = []
                for h in range(2):
                    if fox:
                        sc = scs[h] + (fq[h] - e1[h, :, keys])
                        if diag:
                            sc = jnp.where(rmc >= 0, sc, NEG)
                    else:
                        sc = scs[h] + b
                    sbuf[h][:, keys] = sc
                    mx = carry[h]
                    for j in range(TB // LANES):
                        mx = jnp.maximum(mx, sc[:, j * LANES:(j + 1) * LANES])
                    out.append(mx)
                return tuple(out)

            mx0 = jnp.full((TB, LANES), NEG, F32)
            mxs = lax.fori_loop(0, i, lambda kb, cr: scores(kb, cr, False), (mx0, mx0))
            mxs = scores(i, mxs, True)
            m_0, m_1 = (jnp.max(mx, axis=1, keepdims=True) for mx in mxs)

            def weigh(kb, carry):
                keys = _rows(kb)
                ps = [jnp.exp(sbuf[h][:, keys] - m).astype(BF16) for h, m in enumerate((m_0, m_1))]
                return tuple(carry[h] + _mm32(ps[h], vx[h][keys, :]) for h in range(2))

            a0 = jnp.zeros((TB, LANES), F32)
            acc0, acc1 = lax.fori_loop(0, i + 1, weigh, (a0, a0))
            l0, l1 = acc0[:, HEAD_DIM:HEAD_DIM + 1], acc1[:, 0:1]
            o_ref[rows, :] = jnp.where(lane_lo, acc0 / l0, acc1 / l1)
            lse_ref[0, rows, :] = m_0 + jnp.log(l0)
            lse_ref[1, rows, :] = m_1 + jnp.log(l1)
            return 0

        lax.fori_loop(0, nq, qloop, 0)

    hp0 = head0_col
    if fox:
        e_specs = [_stat_spec(s), _rowstat_spec(s)]
    else:
        e_specs = [_full_spec((s, LANES)), _full_spec((s, LANES)), _full_spec((LANES, LANES))]
    return _call_pairs(
        body, name=name, n_pairs=n_pairs, comm=comm,
        in_specs=[_head_spec(s, hp0), _head_spec(s, K_COL + hp0), _head_spec(s, V_COL + hp0)] + e_specs,
        args=(qkv, qkv, qkv, *extra),
        out_specs=[_head_spec(s, 0), _stat_spec(s)],
        out_shape=[jax.ShapeDtypeStruct((s, LANES * n_pairs), F32),
                   jax.ShapeDtypeStruct((2 * n_pairs, s, 1), F32)],
        scratch=([pltpu.VMEM((s, LANES), BF16)] * 5 + [pltpu.VMEM((TB, s), F32)] * 2
                 + ([] if fox else [_dil_bias_scratch()])))


def _bias_bwd(mode, qkv, head0_col, n_pairs, extra, o, do, do_col0, lse, *, name, comm=None):
    s = qkv.shape[0]
    assert s % TB == 0 and s <= DIL_PATTERNS[2][0]
    nq = s // TB
    fox = mode == "fox"

    def body(q_ref, k_ref, v_ref, o_ref, do_ref, lse_ref, e0, e1, *rest):
        if fox:
            dq_ref, dk_ref, dv_ref, dfr_ref, dfc_ref = rest[:5]
            scr = rest[5:]
        else:
            e2, dq_ref, dk_ref, dv_ref = rest[:4]
            scr = rest[4:]
        qlo, qhi, klo, khi, kbf, vbf, dolo, dohi, qtlo, qthi, dotlo, dothi, dkt, dvt = scr[:14]
        if not fox:
            bias = scr[14]
            _dil_bias_tiles(bias)
        lane_lo = _iota2((TB, LANES), 1) < HEAD_DIM
        sub_lo = _iota2((LANES, TB), 0) < HEAD_DIM

        def prep(i, _):
            rows = _rows(i)
            q, k = q_ref[rows, :], k_ref[rows, :]
            if not fox:
                c, sn, sw = e0[rows, :], e1[rows, :], e2[...]
                q, k = _rope(q, c, sn, sw), _rope(k, c, sn, sw)
            _bwd_prep(i, q, k, v_ref[rows, :], do_ref[rows, :], scr, lane_lo, sub_lo)
            return 0

        lax.fori_loop(0, nq, prep, 0)
        dkt[...] = jnp.zeros_like(dkt)
        dvt[...] = jnp.zeros_like(dvt)
        if fox:
            dfr_ref[...] = jnp.zeros_like(dfr_ref)
        rmc = _iota2((TB, TB), 0) - _iota2((TB, TB), 1)
        qm, km, dom, qtm, dotm = (qlo, qhi), (klo, khi), (dolo, dohi), (qtlo, qthi), (dotlo, dothi)

        def qloop(i, _):
            rows = _rows(i)
            prod = do_ref[rows, :] * o_ref[rows, :]
            dsum = (jnp.sum(jnp.where(lane_lo, prod, 0.0), axis=1, keepdims=True),
                    jnp.sum(jnp.where(lane_lo, 0.0, prod), axis=1, keepdims=True))
            lse_i = (lse_ref[0, rows, :], lse_ref[1, rows, :])
            if fox:
                fql = (e0[0, rows, :] - lse_i[0], e0[1, rows, :] - lse_i[1])

            def tile(kb, carry, diag):
                dq, rs = carry[0], list(carry[1:])
                keys = _rows(kb)
                k, v = kbf[keys, :], vbf[keys, :]
                scs = [_nt(qm[h][rows, :], k) for h in range(2)]
                dps = [_nt(dom[h][rows, :], v) for h in range(2)]
                if not fox:
                    b = bias[jnp.minimum(i - kb, FAR_TILES)]
                ps, dss = [], []
                for h in range(2):
                    if fox:
                        sc = scs[h] + (fql[h] - e1[h, :, keys])
                        if diag:
                            sc = jnp.where(rmc >= 0, sc, NEG)
                    else:
                        sc = scs[h] + (b - lse_i[h])
                    p = jnp.exp(sc)
                    dss.append(p * (dps[h] - dsum[h]))
                    ps.append(p.astype(BF16))
                dk_t, dv_t = dkt[:, keys], dvt[:, keys]
                for h in range(2):
                    dsb = dss[h].astype(BF16)
                    dq = dq + _mm32(dsb, km[h][keys, :])
                    dk_t = dk_t + _mm32(qtm[h][:, rows], dsb)
                    dv_t = dv_t + _mm32(dotm[h][:, rows], ps[h])
                    if fox:
                        dfr_ref[h, :, keys] -= jnp.sum(dss[h], axis=0, keepdims=True)
                        for j in range(TB // LANES):
                            rs[h] = rs[h] + dss[h][:, j * LANES:(j + 1) * LANES]
                dkt[:, keys] = dk_t
                dvt[:, keys] = dv_t
                return (dq, *rs)

            z2 = jnp.zeros((TB, LANES), F32)
            carry = lax.fori_loop(0, i, lambda kb, cr: tile(kb, cr, False), (z2, z2, z2) if fox else (z2,))
            carry = tile(i, carry, True)
            if fox:
                dfc_ref[0, rows, :] = jnp.sum(carry[1], axis=1, keepdims=True)
                dfc_ref[1, rows, :] = jnp.sum(carry[2], axis=1, keepdims=True)
            dq = carry[0]
            if not fox:
                dq = _rope_t(dq, e0[rows, :], e1[rows, :], e2[...])
            dq_ref[rows, :] = dq.astype(BF16)
            return 0

        lax.fori_loop(0, nq, qloop, 0)

        def wloop(i, _):
            rows = _rows(i)
            dk = dkt[:, rows].T
            if not fox:
                dk = _rope_t(dk, e0[rows, :], e1[rows, :], e2[...])
            dk_ref[rows, :] = dk.astype(BF16)
            dv_ref[rows, :] = dvt[:, rows].T.astype(BF16)
            return 0

        lax.fori_loop(0, nq, wloop, 0)

    hp0 = head0_col
    out = jax.ShapeDtypeStruct((s, LANES * n_pairs), BF16)
    out_specs = [_head_spec(s, 0)] * 3
    out_shape = [out, out, out]
    if fox:
        e_specs = [_stat_spec(s), _rowstat_spec(s)]
        out_specs += [_rowstat_spec(s), _stat_spec(s)]
        out_shape += [jax.ShapeDtypeStruct((2 * n_pairs, 1, s), F32), jax.ShapeDtypeStruct((2 * n_pairs, s, 1), F32)]
    else:
        e_specs = [_full_spec((s, LANES)), _full_spec((s, LANES)), _full_spec((LANES, LANES))]
    return _call_pairs(
        body, name=name, n_pairs=n_pairs, comm=comm,
        in_specs=[_head_spec(s, hp0), _head_spec(s, K_COL + hp0), _head_spec(s, V_COL + hp0),
                  _head_spec(s, 0), _head_spec(s, do_col0), _stat_spec(s)] + e_specs,
        args=(qkv, qkv, qkv, o, do, lse, *extra),
        out_specs=out_specs, out_shape=out_shape,
        scratch=_bwd_scratch(s) + ([] if fox else [_dil_bias_scratch()]))


F_COL = 3 * D_ATTN // LANES


def _fgate_fwd(qkvf, brow, *, name):
    s = qkvf.shape[0]
    nb = s // BLK

    def body(f_ref, b_ref, fc_ref, fr_ref, fs):
        row, col = _iota2((BLK, BLK), 0), _iota2((BLK, BLK), 1)
        l_incl = (col <= row).astype(BF16)

        def step(i, carry):
            r0 = pl.multiple_of(i * BLK, BLK)
            lf, _ = _log_sig_pair(f_ref[pl.ds(r0, BLK), :] + b_ref[...])
            fblk = carry + _dot3_left(l_incl, lf)
            fs[pl.ds(r0, BLK), :] = fblk
            return fblk[BLK - 1:BLK, :]

        lax.fori_loop(0, nb, step, jnp.zeros((1, LANES), F32))
        ft = fs[...].T
        for h in range(N_HEADS):
            fc_ref[h, :, :] = fs[:, h:h + 1]
            fr_ref[h, :, :] = ft[h:h + 1, :]

    return pl.pallas_call(
        body, name=name, grid=(1,),
        in_specs=[pl.BlockSpec((s, LANES), lambda i: (0, F_COL)), pl.BlockSpec((1, LANES), lambda i: (0, 0))],
        out_specs=[pl.BlockSpec((N_HEADS, s, 1), lambda i: (0, 0, 0)),
                   pl.BlockSpec((N_HEADS, 1, s), lambda i: (0, 0, 0))],
        out_shape=[jax.ShapeDtypeStruct((N_HEADS, s, 1), F32), jax.ShapeDtypeStruct((N_HEADS, 1, s), F32)],
        scratch_shapes=[pltpu.VMEM((s, LANES), F32)],
        compiler_params=_params(1),
    )(qkvf, brow)


def _fgate_bwd(dfr, dfc, qkvf, brow, *, name):
    s = qkvf.shape[0]
    nb = s // BLK

    def body(dfr_ref, dfc_ref, f_ref, b_ref, dfl_ref, db_ref, ts, fs):
        ts[...] = jnp.zeros_like(ts)
        for h in range(N_HEADS):
            ts[h:h + 1, :] = dfr_ref[h]
        fs[...] = ts[...].T
        for h in range(N_HEADS):
            fs[:, h:h + 1] += dfc_ref[h]
        row, col = _iota2((BLK, BLK), 0), _iota2((BLK, BLK), 1)
        u_incl = (col >= row).astype(BF16)
        head_lane = _iota2((BLK, LANES), 1) < N_HEADS

        def step(ii, carry):
            tail, db = carry
            r0 = pl.multiple_of((nb - 1 - ii) * BLK, BLK)
            rblk = tail + _dot3_left(u_incl, fs[pl.ds(r0, BLK), :])
            _, lsn = _log_sig_pair(f_ref[pl.ds(r0, BLK), :] + b_ref[...])
            dfl = jnp.where(head_lane, rblk * jnp.exp(lsn), 0.0)
            dfl_ref[pl.ds(r0, BLK), :] = dfl.astype(BF16)
            return rblk[0:1, :], db + jnp.sum(dfl, axis=0, keepdims=True)

        z = jnp.zeros((1, LANES), F32)
        _, db = lax.fori_loop(0, nb, step, (z, z))
        db_ref[...] = db

    return pl.pallas_call(
        body, name=name, grid=(1,),
        in_specs=[pl.BlockSpec((N_HEADS, 1, s), lambda i: (0, 0, 0)), pl.BlockSpec((N_HEADS, s, 1), lambda i: (0, 0, 0)),
                  pl.BlockSpec((s, LANES), lambda i: (0, F_COL)), pl.BlockSpec((1, LANES), lambda i: (0, 0))],
        out_specs=[pl.BlockSpec((s, LANES), lambda i: (0, 0)), pl.BlockSpec((1, LANES), lambda i: (0, 0))],
        out_shape=[jax.ShapeDtypeStruct((s, LANES), BF16), jax.ShapeDtypeStruct((1, LANES), F32)],
        scratch_shapes=[pltpu.VMEM((LANES, s), F32), pltpu.VMEM((s, LANES), F32)],
        compiler_params=_params(1),
    )(dfr, dfc, qkvf, brow)


def _adamw_math(w, g, m, v):
    m2 = ADAM_B1 * m + (1.0 - ADAM_B1) * g
    v2 = ADAM_B2 * v + (1.0 - ADAM_B2) * (g * g)
    m_hat = m2 / (1.0 - ADAM_B1 ** ADAM_STEP)
    v_hat = v2 / (1.0 - ADAM_B2 ** ADAM_STEP)
    delta = -ADAM_LR * (m_hat / (jnp.sqrt(v_hat) + ADAM_EPS) + ADAM_WD * w)
    return delta, m2, v2


def _row_tile(r, cap=256, mult=16):
    best = None
    for t in range(mult, min(r, cap) + 1, mult):
        if r % t == 0:
            best = t
    assert best is not None, r
    return best


def _adamw_shard(w, m, v, lidx, g_all, r1, r2, sc, prev, *, name):
    nl, r, c = w.shape
    tr = _row_tile(r)

    def body(sc_ref, w_ref, m_ref, v_ref, g_ref, r1_ref, r2_ref, *rest):
        go_ref, d_ref, mo_ref, vo_ref = rest[-4:]
        g = g_ref[...] + r1_ref[...]
        g = g + r2_ref[0].astype(F32)
        g = g + r2_ref[1].astype(F32)
        g = g + r2_ref[2].astype(F32)
        delta, m2, v2 = _adamw_math(w_ref[...], g, m_ref[...], v_ref[...])
        go_ref[...] = g
        d_ref[...] = delta
        mo_ref[...] = m2
        vo_ref[...] = v2

    lay = pl.BlockSpec((None, tr, c), lambda i, s_: (lidx, i, 0))
    in_specs = [lay, lay, lay,
                pl.BlockSpec((None, tr, c), lambda i, s_: (s_[0], i, 0)),
                pl.BlockSpec((None, tr, c), lambda i, s_: (s_[1], i, 0)),
                pl.BlockSpec((3, tr, c), lambda i, s_: (0, i, 0))]
    args = [sc, w, m, v, g_all, r1, r2]
    aliases = {}
    if prev is not None:
        in_specs += [pl.BlockSpec(memory_space=pl.ANY)] * 4
        aliases = {7 + t: t for t in range(4)}
        args += list(prev)
    shp = jax.ShapeDtypeStruct((nl, r, c), F32)
    return pl.pallas_call(
        body, name=name,
        grid_spec=pltpu.PrefetchScalarGridSpec(
            num_scalar_prefetch=1, grid=(r // tr,), in_specs=in_specs, out_specs=[lay] * 4),
        out_shape=[shp] * 4, input_output_aliases=aliases,
        compiler_params=_params(1),
    )(*args)


def _adamw_small(w, g, m, v, *, name):
    def body(w_ref, g_ref, m_ref, v_ref, d_ref, mo_ref, vo_ref):
        delta, m2, v2 = _adamw_math(w_ref[...], g_ref[...], m_ref[...], v_ref[...])
        d_ref[...] = delta
        mo_ref[...] = m2
        vo_ref[...] = v2

    shp = jax.ShapeDtypeStruct(w.shape, F32)
    return pl.pallas_call(body, name=name, out_shape=[shp] * 3, compiler_params=_params())(w, g, m, v)


def _pos():
    return lax.axis_index("x"), lax.axis_index("y"), lax.axis_index("c")


def _other_chips(x, y):
    return [(1 - x, y), (x, 1 - y), (1 - x, 1 - y)]


def _dev_index(x, y, c):
    return 4 * x + 2 * y + c


HBM_SPEC = pl.BlockSpec(memory_space=pltpu.HBM)


class _Comm:
    def __init__(self, inputs, out_shape, scratch, start, mid, finish):
        self.inputs, self.out_shape, self.scratch = list(inputs), list(out_shape), list(scratch)
        self.start, self.mid, self.finish = start, mid, finish

    def run(self, name):
        n_in, n_out = len(self.inputs), len(self.out_shape)

        def body(*refs):
            parts = refs[:n_in], refs[n_in:n_in + n_out], refs[n_in + n_out:]
            self.start(*parts)
            self.mid(*parts)
            self.finish(*parts)

        return pl.pallas_call(
            body, name=name, in_specs=[HBM_SPEC] * n_in, out_specs=[HBM_SPEC] * n_out,
            out_shape=self.out_shape, scratch_shapes=self.scratch)(*self.inputs)


def _call_hosting(body, *, name, grid, in_specs, args, out_specs, out_shape, scratch, comm=None):
    if comm is None:
        res = pl.pallas_call(
            body, name=name, grid=grid, in_specs=in_specs, out_specs=out_specs, out_shape=out_shape,
            scratch_shapes=scratch, compiler_params=_params(len(grid)))(*args)
        return list(res), []
    sizes = (len(in_specs), len(comm.inputs), len(out_specs), len(comm.out_shape), len(scratch), len(comm.scratch))

    def fused(*refs):
        parts, o = [], 0
        for n in sizes:
            parts.append(refs[o:o + n])
            o += n
        h_in, c_in, h_out, c_out, h_scr, c_scr = parts
        first = last = None
        for d, n in enumerate(grid):
            p = pl.program_id(d)
            first = (p == 0) if first is None else jnp.logical_and(first, p == 0)
            last = (p == n - 1) if last is None else jnp.logical_and(last, p == n - 1)

        @pl.when(first)
        def _():
            comm.start(c_in, c_out, c_scr)

        @pl.when(last)
        def _():
            comm.mid(c_in, c_out, c_scr)

        body(*h_in, *h_out, *h_scr)

        @pl.when(last)
        def _():
            comm.finish(c_in, c_out, c_scr)

    res = pl.pallas_call(
        fused, name=name, grid=grid,
        in_specs=list(in_specs) + [HBM_SPEC] * sizes[1], out_specs=list(out_specs) + [HBM_SPEC] * sizes[3],
        out_shape=list(out_shape) + comm.out_shape, scratch_shapes=list(scratch) + comm.scratch,
        compiler_params=_params(len(grid)))(*args, *comm.inputs)
    return list(res[:sizes[2]]), list(res[sizes[2]:])


def _call_pairs(body, *, name, n_pairs, **kw):
    return _call_hosting(body, name=name, grid=(n_pairs,), **kw)


def _gather_comm(shards):
    n = len(shards)

    def plan(xs, outs, sems):
        send, recv, loc = sems
        x, y, c = _pos()
        me, sib = (x, y, c), (x, y, 1 - c)
        chips = _other_chips(x, y)

        def copy(a, k, block, to, src=None):
            dst = outs[a].at[_dev_index(*block)]
            return pltpu.make_async_remote_copy(
                src_ref=dst if src is None else src, dst_ref=dst,
                send_sem=send.at[a, k], recv_sem=recv.at[a, k], device_id=to, device_id_type=MESH)

        mine = [pltpu.make_async_copy(xs[a], outs[a].at[_dev_index(*me)], loc.at[a]) for a in range(n)]
        first = []
        for a in range(n):
            first.append(copy(a, 0, me, sib, src=xs[a]))
            first += [copy(a, 1 + j, me, (*chip, c), src=xs[a]) for j, chip in enumerate(chips)]
        passed = [(copy(a, 1 + j, (*chip, c), me), copy(a, 4 + j, (*chip, c), sib))
                  for j, chip in enumerate(chips) for a in range(n)]
        from_sib = [copy(a, 0, sib, me) for a in range(n)]
        from_sib += [copy(a, 4 + j, (*chip, 1 - c), me) for a in range(n) for j, chip in enumerate(chips)]
        return mine, first, passed, from_sib

    def start(xs, outs, sems):
        mine, first, _, _ = plan(xs, outs, sems)
        for cp in mine + first:
            cp.start()

    def mid(xs, outs, sems):
        for arrival, fwd in plan(xs, outs, sems)[2]:
            arrival.wait_recv()
            fwd.start()

    def finish(xs, outs, sems):
        mine, first, passed, from_sib = plan(xs, outs, sems)
        for cp in from_sib:
            cp.wait_recv()
        for cp in first + [fwd for _, fwd in passed]:
            cp.wait_send()
        for cp in mine:
            cp.wait()

    return _Comm(shards, [jax.ShapeDtypeStruct((N_DEV,) + a.shape, a.dtype) for a in shards],
                 [pltpu.SemaphoreType.DMA((n, 7)), pltpu.SemaphoreType.DMA((n, 7)), pltpu.SemaphoreType.DMA((n,))],
                 start, mid, finish)


def _sibling_comm(gs):
    n = len(gs)

    def plan(g_refs, r_refs, sems):
        send, recv = sems
        x, y, c = _pos()
        return [pltpu.make_async_remote_copy(
            src_ref=g_refs[a].at[_dev_index(k // 2, k % 2, 1 - c)], dst_ref=r_refs[a].at[k],
            send_sem=send.at[a, k], recv_sem=recv.at[a, k], device_id=(x, y, 1 - c), device_id_type=MESH)
            for a in range(n) for k in range(4)]

    def start(*parts):
        for cp in plan(*parts):
            cp.start()

    def mid(*parts):
        pass

    def finish(*parts):
        for cp in plan(*parts):
            cp.wait()

    return _Comm(gs, [jax.ShapeDtypeStruct((4,) + g.shape[1:], g.dtype) for g in gs],
                 [pltpu.SemaphoreType.DMA((n, 4)), pltpu.SemaphoreType.DMA((n, 4))], start, mid, finish)


def _rs_partial(g_all, r1, sc, *, name):
    _, r, c = g_all.shape
    tr = _row_tile(r)

    def body(sc_ref, g_ref, r_ref, o_ref):
        o_ref[...] = (g_ref[...] + r_ref[...]).astype(BF16)

    return pl.pallas_call(
        body, name=name,
        grid_spec=pltpu.PrefetchScalarGridSpec(
            num_scalar_prefetch=1, grid=(3, r // tr),
            in_specs=[pl.BlockSpec((None, tr, c), lambda j, i, s_: (s_[2 + j], i, 0)),
                      pl.BlockSpec((None, tr, c), lambda j, i, s_: (s_[5 + j], i, 0))],
            out_specs=pl.BlockSpec((None, tr, c), lambda j, i, s_: (j, i, 0))),
        out_shape=jax.ShapeDtypeStruct((3, r, c), BF16),
        compiler_params=_params(2),
    )(sc, g_all, r1)


def _cross_comm(ps):
    n = len(ps)

    def plan(p_refs, r_refs, sems):
        send, recv = sems
        x, y, c = _pos()
        return [pltpu.make_async_remote_copy(
            src_ref=p_refs[a].at[j], dst_ref=r_refs[a].at[j], send_sem=send.at[a, j], recv_sem=recv.at[a, j],
            device_id=(*chip, c), device_id_type=MESH)
            for j, chip in enumerate(_other_chips(x, y)) for a in range(n)]

    def start(*parts):
        for cp in plan(*parts):
            cp.start()

    def mid(*parts):
        pass

    def finish(*parts):
        for cp in plan(*parts):
            cp.wait()

    return _Comm(ps, [jax.ShapeDtypeStruct(p.shape, p.dtype) for p in ps],
                 [pltpu.SemaphoreType.DMA((n, 3)), pltpu.SemaphoreType.DMA((n, 3))], start, mid, finish)


SMALL_ROWS = 16


def _all_reduce_small(pack, *, name):
    def body(x_ref, o_ref, buf, send, recv):
        x, y, c = _pos()
        me = _dev_index(x, y, c)
        buf[me] = x_ref[...]
        copies = []
        for k in range(1, N_DEV):
            fx, fy, fc = (k >> 2) & 1, (k >> 1) & 1, k & 1
            peer = (1 - x if fx else x, 1 - y if fy else y, 1 - c if fc else c)
            copies.append(pltpu.make_async_remote_copy(
                src_ref=x_ref, dst_ref=buf.at[me], send_sem=send.at[k - 1], recv_sem=recv.at[k - 1],
                device_id=peer, device_id_type=MESH))
        for cp in copies:
            cp.start()
        for cp in copies:
            cp.wait()
        acc = buf[0]
        for d in range(1, N_DEV):
            acc = acc + buf[d]
        o_ref[...] = acc

    return pl.pallas_call(
        body, name=name,
        in_specs=[pl.BlockSpec(memory_space=pltpu.VMEM)], out_specs=pl.BlockSpec(memory_space=pltpu.VMEM),
        out_shape=jax.ShapeDtypeStruct(pack.shape, F32),
        scratch_shapes=[pltpu.VMEM((N_DEV,) + pack.shape, F32),
                        pltpu.SemaphoreType.DMA((N_DEV - 1,)), pltpu.SemaphoreType.DMA((N_DEV - 1,))],
    )(pack)


def _unshard_cols(g):
    return jnp.transpose(g, (1, 0, 2)).reshape(g.shape[1], N_DEV * g.shape[2])


def _shard_cols(w):
    k, n8 = w.shape
    return jnp.transpose(w.reshape(k, N_DEV, n8 // N_DEV), (1, 0, 2))


def _pad_row(v, width=D_MODEL):
    v = v.reshape(1, -1)
    return jnp.pad(v, ((0, 0), (0, width - v.shape[1])))


def _forward_mixer(l, xc, g_mix, wq, wo, rope, brow, comm_a=None, comm_b=None):
    even = l % 2 == 0
    h1 = _rms_fwd(xc, g_mix, name=f"norm_mix_fwd{l}")
    qkv = _mm(h1, wq, name=f"qkv_fwd{l}", tm=1024, tn=768 if even else 640)
    if even:
        (o_a, st_a), got_a = _sb_fwd(qkv, N_HEADS // 4, name=f"sb_fwd{l}", comm=comm_a)
        (o_b, st_b), got_b = _bias_fwd("dil", qkv, N_HEADS // 4, N_HEADS // 4, rope, name=f"dil_fwd{l}",
                                       comm=comm_b)
        o = jnp.concatenate([o_a, o_b], axis=1)
        att = (o_b, st_a, st_b)
    else:
        assert comm_b is None
        fcol, frow = _fgate_fwd(qkv, brow, name=f"fgate_fwd{l}")
        (o, lse), got_a = _bias_fwd("fox", qkv, 0, N_HEADS // 2, (fcol, frow), name=f"fox_fwd{l}", comm=comm_a)
        got_b = []
        att = (o, lse, fcol, frow)
    o_bf = o.astype(BF16)
    if callable(wo):
        wo = wo(got_a, got_b)
    xm = _mm(o_bf, wo, add=xc, name=f"wo_fwd{l}", tm=512, tn=1024)
    return xm, (xc, h1, qkv, att, o_bf), got_a, got_b


def _forward_ffn(l, xm, g_ffn, win_t, wout):
    h2 = _rms_fwd(xm, g_ffn, name=f"norm_ffn_fwd{l}")
    g, u, a = _ffn_in_fwd(h2, win_t, name=f"ffn_in_fwd{l}")
    xo = _mm(a, wout, add=xm, name=f"ffn_out_fwd{l}", tm=512, tn=1024)
    return xo, (xm, h2, (g, u), a)


def _backward_ffn(l, dx, dxb, saved, g_ffn, w, exchange=None):
    _, _, win_t, wout = w
    _, _, _, _, _, xm, h2, gu, a = saved
    dgu = _ffn_out_dx(dxb, wout, *gu, name=f"ffn_out_dx{l}")
    d_wout = _mm(a, dxb, ta=True, name=f"ffn_out_dw{l}", tm=FF_BLK, tn=512)
    d_win_t = _mm(dgu, h2, ta=True, name=f"ffn_in_dw{l}", tm=FF_BLK, tn=1024)
    comm = exchange(d_win_t, d_wout) if exchange is not None else None
    res = _mm(dgu, win_t, name=f"ffn_in_dx{l}", tm=512, tn=1024, tk=FF_BLK, comm=comm, norm_bwd=(xm, g_ffn, dx))
    (dxm, dxmb, dg_ffn), got = res if comm is not None else (res, [])
    return dxm, dxmb, dg_ffn, d_win_t, d_wout, got


def _backward_attn(l, dxm, dxmb, saved, g_mix, w, rope, brow, comm_a=None, comm_b=None, exchange=None):
    wq, wo, _, _ = w
    xin, h1, qkv, att, o_bf, _, _, _, _ = saved
    even = l % 2 == 0
    d_wo = _mm(o_bf, dxmb, ta=True, name=f"wo_dw{l}", tm=512, tn=1024)
    do = _mm(dxmb, wo, tb=True, name=f"wo_dx{l}", tm=1024, tn=1024)
    db = None
    if even:
        o_b, st_a, st_b = att
        (dqa, dka, dva), got_a = _sb_bwd(qkv, do, st_a, N_HEADS // 4, 0, name=f"sb_bwd{l}", comm=comm_a)
        (dqb, dkb, dvb), got_b = _bias_bwd("dil", qkv, N_HEADS // 4, N_HEADS // 4, rope, o_b, do,
                                           N_HEADS // 4, st_b, name=f"dil_bwd{l}", comm=comm_b)
        dqkv = jnp.concatenate([dqa, dqb, dka, dkb, dva, dvb], axis=1)
    else:
        assert comm_b is None
        o, lse, fcol, frow = att
        (dq, dk, dv, dfr, dfc), got_a = _bias_bwd("fox", qkv, 0, N_HEADS // 2, (fcol, frow), o, do, 0, lse,
                                                  name=f"fox_bwd{l}", comm=comm_a)
        got_b = []
        dfl, db = _fgate_bwd(dfr, dfc, qkv, brow, name=f"fgate_bwd{l}")
        dqkv = jnp.concatenate([dq, dk, dv, dfl], axis=1)
    if even:
        d_wq = _mm(h1, dqkv, ta=True, name=f"qkv_dw{l}", tm=1024, tn=dqkv.shape[1] // N_DEV, out_planes=N_DEV)
    else:
        d_wq = _mm(h1, dqkv, ta=True, name=f"qkv_dw{l}", tm=1024, tn=640)
    comm = exchange(d_wq, d_wo) if exchange is not None else None
    res = _mm(dqkv, wq, tb=True, name=f"qkv_dx{l}", tm=512, tn=1024, comm=comm, norm_bwd=(xin, g_mix, dxm))
    (dx, dxb, dg_mix), got_x = res if comm is not None else (res, [])
    return dx, dxb, dg_mix, d_wq, d_wo, db, got_a, got_b, got_x


def kernel(x, norm_mix, w_qkv_even, w_o_even, w_qkvf_odd, b_forget, w_o_odd, norm_ffn, w_ffn_in, w_ffn_out, norm_final, loss_target, m_norm_mix, m_w_qkv_even, m_w_o_even, m_w_qkvf_odd, m_b_forget, m_w_o_odd, m_norm_ffn, m_w_ffn_in, m_w_ffn_out, m_norm_final, v_norm_mix, v_w_qkv_even, v_w_o_even, v_w_qkvf_odd, v_b_forget, v_w_o_odd, v_norm_ffn, v_w_ffn_in, v_w_ffn_out, v_norm_final):
    xi, yi, ci = _pos()
    others = _other_chips(xi, yi)
    sc = jnp.stack([_dev_index(xi, yi, ci), 2 * xi + yi]
                   + [_dev_index(px, py, ci) for px, py in others]
                   + [2 * px + py for px, py in others]).astype(jnp.int32)
    n_odd_cols = w_qkvf_odd.shape[2] * N_DEV

    xs, tgt = x[0], loss_target[0]
    rope = _rope_tables(xs.shape[0])
    brow = [_pad_row(b_forget[i], LANES) for i in range(DEPTH // 2)]
    w_in_t, m_in_t, v_in_t = (jnp.swapaxes(t, 1, 2) for t in (w_ffn_in, m_w_ffn_in, v_w_ffn_in))

    def shards(l):
        even = l % 2 == 0
        return {"wq": (w_qkv_even if even else w_qkvf_odd)[l // 2].astype(BF16),
                "wo": (w_o_even if even else w_o_odd)[l // 2].astype(BF16),
                "win": w_in_t[l].astype(BF16), "wout": w_ffn_out[l].astype(BF16)}

    def full_wq(l, gq):
        wq = _unshard_cols(gq)
        if l % 2 == 1:
            wq = jnp.pad(wq, ((0, 0), (0, QKVF_PAD - n_odd_cols)))
        return wq

    def full_wo(go):
        return go.reshape(D_ATTN, D_MODEL)

    sh = [shards(l) for l in range(DEPTH)]
    wq = {0: full_wq(0, _gather_comm([sh[0]["wq"]]).run("gather_weights0")[0])}
    weights, saved = [], []
    xc = xs
    for l in range(DEPTH):
        even = l % 2 == 0
        nxt = [sh[l + 1]["wq"]] if l + 1 < DEPTH else []
        if even:
            comm_a = _gather_comm([sh[l]["win"], sh[l]["wout"]])
            comm_b = _gather_comm([sh[l]["wo"]] + nxt)
        else:
            comm_a, comm_b = _gather_comm([sh[l]["wo"], sh[l]["win"], sh[l]["wout"]] + nxt), None
        xm, sv_mix, got_a, got_b = _forward_mixer(
            l, xc, norm_mix[l:l + 1], wq[l], (lambda ga, gb: full_wo(gb[0] if even else ga[0])), rope,
            brow[l // 2], comm_a, comm_b)
        if even:
            (gi, gout), go, gq_next = got_a, got_b[0], got_b[1:]
        else:
            go, gi, gout, gq_next = got_a[0], got_a[1], got_a[2], got_a[3:]
        if gq_next:
            wq[l + 1] = full_wq(l + 1, gq_next[0])
        win_t, wout = gi.reshape(2 * D_FF, D_MODEL), gout.reshape(D_FF, D_MODEL)
        xc, sv_ffn = _forward_ffn(l, xm, norm_ffn[l:l + 1], win_t, wout)
        weights.append((wq[l], full_wo(go), win_t, wout))
        saved.append(sv_mix + sv_ffn)

    loss_row, dx, dxb, dg_final = _final_loss(xc, norm_final.reshape(1, -1), tgt, name="final_loss")

    sharded = {
        "qkv_even": (w_qkv_even, m_w_qkv_even, v_w_qkv_even), "o_even": (w_o_even, m_w_o_even, v_w_o_even),
        "qkvf_odd": (w_qkvf_odd, m_w_qkvf_odd, v_w_qkvf_odd), "o_odd": (w_o_odd, m_w_o_odd, v_w_o_odd),
        "ffn_in": (w_in_t, m_in_t, v_in_t), "ffn_out": (w_ffn_out, m_w_ffn_out, v_w_ffn_out),
    }
    results = {k: None for k in sharded}

    def chip_sums(gs, r1s, keys, tag):
        ps = [_rs_partial(g, r1, sc, name=f"grads_chip_sum_{tag}_{a}") for a, (g, r1) in enumerate(zip(gs, r1s))]
        return gs, r1s, ps, keys

    held = {}

    def row_chunks(d):
        return d.reshape(N_DEV, d.shape[0] // N_DEV, D_MODEL)

    def to_sibling(tag, col_sharded, odd_qkv=False):
        def make(d_first, d_rows):
            if odd_qkv:
                d_first = d_first[:, :n_odd_cols]
            if d_first.ndim == 2:
                d_first = _shard_cols(d_first) if col_sharded else row_chunks(d_first)
            held[tag] = [d_first, row_chunks(d_rows)]
            return _sibling_comm(held[tag])
        return make

    def update(group, r2s, tag):
        gs, r1s, _, keys = group
        for a, (key, lidx) in enumerate(keys):
            w, m, v = sharded[key]
            results[key] = _adamw_shard(w, m, v, lidx, gs[a], r1s[a], r2s[a], sc, results[key],
                                        name=f"adamw_{key}_{tag}")

    dg_mix, dg_ffn, db_f = [None] * DEPTH, [None] * DEPTH, [None] * (DEPTH // 2)
    pending = None
    for l in reversed(range(DEPTH)):
        even = l % 2 == 0
        dxm, dxmb, dg_ffn[l], _, _, r1s = _backward_ffn(l, dx, dxb, saved[l], norm_ffn[l:l + 1], weights[l],
                                                        to_sibling(f"ffn{l}", col_sharded=False))
        ffn = chip_sums(held[f"ffn{l}"], r1s, [("ffn_in", l), ("ffn_out", l)], f"ffn{l}")
        if even:
            comm_a = _cross_comm(ffn[2])
            comm_b = _cross_comm(pending[2]) if pending is not None else None
        else:
            comm_a = _cross_comm(ffn[2] + (pending[2] if pending is not None else []))
            comm_b = None
        dx, dxb, dg_mix[l], _, _, db, got_a, got_b, r1s = _backward_attn(
            l, dxm, dxmb, saved[l], norm_mix[l:l + 1], weights[l], rope, brow[l // 2], comm_a, comm_b,
            to_sibling(f"mix{l}", col_sharded=True, odd_qkv=not even))
        update(ffn, got_a[:2], f"ffn{l}")
        if pending is not None:
            update(pending, got_b if even else got_a[2:], f"mix{l + 1}")
        if not even:
            db_f[l // 2] = db
        pending = chip_sums(held[f"mix{l}"], r1s,
                            [("qkv_even" if even else "qkvf_odd", l // 2), ("o_even" if even else "o_odd", l // 2)],
                            f"mix{l}")
    update(pending, _cross_comm(pending[2]).run("grads_to_chips_mix0"), "mix0")

    zeros = jnp.zeros((SMALL_ROWS - 11, D_MODEL), F32)
    db_row = _pad_row(jnp.concatenate([d[:, :N_HEADS] for d in db_f], axis=1))
    pack_g = jnp.concatenate(dg_mix + dg_ffn + [dg_final, db_row, _pad_row(loss_row[:, :1]), zeros], axis=0)
    tot = _all_reduce_small(pack_g, name="small_all_reduce")

    def pack(nm, nf, nfin, bf):
        return jnp.concatenate([nm, nf, nfin.reshape(1, -1), _pad_row(bf),
                                jnp.zeros((SMALL_ROWS - 10, D_MODEL), F32)], axis=0)

    d_s, m_s, v_s = _adamw_small(
        pack(norm_mix, norm_ffn, norm_final, b_forget), tot,
        pack(m_norm_mix, m_norm_ffn, m_norm_final, m_b_forget),
        pack(v_norm_mix, v_norm_ffn, v_norm_final, v_b_forget), name="adamw_small")

    def unpack(p):
        nb = b_forget.size
        return {"norm_mix": p[0:DEPTH], "norm_ffn": p[DEPTH:2 * DEPTH], "norm_final": p[2 * DEPTH],
                "b_forget": p[2 * DEPTH + 1, :nb].reshape(b_forget.shape)}

    small = [unpack(tot), unpack(d_s), unpack(m_s), unpack(v_s)]
    loss = tot[2 * DEPTH + 2, 0]

    order = ["norm_mix", "qkv_even", "o_even", "qkvf_odd", "b_forget", "o_odd", "norm_ffn", "ffn_in", "ffn_out",
             "norm_final"]
    outs = [loss, dx[None]]
    for t in range(4):
        for key in order:
            if key in small[t]:
                outs.append(small[t][key])
            elif key == "ffn_in":
                outs.append(jnp.swapaxes(results[key][t], 1, 2))
            else:
                outs.append(results[key][t])
    return tuple(outs)
```

```python
import jax
import jax.numpy as jnp
from jax import lax
from jax.experimental import pallas as pl
from jax.experimental.pallas import tpu as pltpu

F32 = jnp.float32
BF16 = jnp.bfloat16

D_MODEL = 1024
HEAD_DIM = 64
N_HEADS = 16
D_ATTN = N_HEADS * HEAD_DIM
D_FF = 2816
DEPTH = 4
ROPE_THETA = 500000.0
ROT_DIM = HEAD_DIM // 4
RMS_EPS = 1e-5
SCALE = HEAD_DIM ** -0.5
DIL_PATTERNS = ((128, 1), (512, 4), (2048, 16))
N_DEV = 8
QKVF_PAD = 3200

ADAM_LR = 0.001
ADAM_B1 = 0.9
ADAM_B2 = 0.999
ADAM_EPS = 1e-08
ADAM_WD = 0.01
ADAM_STEP = 10

LANES = 128
BLK = 256
TB = 256
NEG = -1e30
VMEM_LIMIT = 48 * 1024 * 1024

MESH = pl.DeviceIdType.MESH


def _params(n_grid=0, **kw):
    sem = ("arbitrary",) * n_grid if n_grid else None
    return pltpu.CompilerParams(dimension_semantics=sem, vmem_limit_bytes=VMEM_LIMIT, **kw)


def _mm(a, b, *, name, ta=False, tb=False, add=None, out_dtype=F32, tm=512, tn=512, tk=None, comm=None,
        out_planes=None, norm_bwd=None):
    a_planes, b_planes = a.ndim == 3, b.ndim == 3
    assert not (b_planes and tb)
    if a_planes and ta:
        m, k = a.shape[0] * a.shape[2], a.shape[1]
        tm = min(tm, a.shape[2])
        assert a.shape[2] % tm == 0
    elif a_planes:
        m, k = a.shape[1], a.shape[0] * a.shape[2]
        tk = a.shape[2] if tk is None else tk
        assert a.shape[2] % tk == 0
    else:
        m = a.shape[1] if ta else a.shape[0]
        k = a.shape[0] if ta else a.shape[1]
    if b_planes:
        n = b.shape[0] * b.shape[2]
        tn = min(tn, b.shape[2])
        assert b.shape[1] == k and b.shape[2] % tn == 0
    else:
        n = b.shape[0] if tb else b.shape[1]
        assert (b.shape[1] if tb else b.shape[0]) == k
    tm, tn = min(tm, m), min(tn, n)
    tk = k if tk is None else min(tk, k)
    assert m % tm == 0 and n % tn == 0 and k % tk == 0, (name, m, n, k, tm, tn, tk)
    nk = k // tk
    dn = (((0 if ta else 1,), (1 if tb else 0,)), ((), ()))

    extras = list(norm_bwd) if norm_bwd is not None else ([add] if add is not None else [])
    n_out = 3 if norm_bwd is not None else 1
    assert norm_bwd is None or (add is None and out_planes is None and tn == n)

    def body(*refs):
        a_ref, b_ref = refs[0], refs[1]
        extra = refs[2:2 + len(extras)]
        outs = refs[2 + len(extras):2 + len(extras) + n_out]
        part = lax.dot_general(a_ref[...], b_ref[...], dn, preferred_element_type=F32)
        first_rows = pl.program_id(0) == 0

        def store(res):
            if norm_bwd is None:
                if add is not None:
                    res = res + extra[0][...]
                outs[0][...] = res.astype(out_dtype)
                return
            x_ref, g_ref, dres_ref = extra
            dx_ref, dxb_ref, dg_ref = outs
            xv = x_ref[...]
            r = lax.rsqrt(jnp.mean(xv * xv, axis=-1, keepdims=True) + RMS_EPS)
            y = xv * r
            dy = res * g_ref[...]
            dx = dres_ref[...] + r * (dy - y * jnp.mean(dy * y, axis=-1, keepdims=True))
            dx_ref[...] = dx
            dxb_ref[...] = dx.astype(BF16)
            gpart = jnp.sum(res * y, axis=0, keepdims=True)

            @pl.when(first_rows)
            def _():
                dg_ref[...] = gpart

            @pl.when(jnp.logical_not(first_rows))
            def _():
                dg_ref[...] += gpart

        if nk == 1:
            store(part)
            return
        acc_ref = refs[-1]
        kk = pl.program_id(2)

        @pl.when(kk == 0)
        def _():
            acc_ref[...] = part

        @pl.when(kk > 0)
        def _():
            acc_ref[...] += part

        @pl.when(kk == nk - 1)
        def _():
            store(acc_ref[...])

    if a_planes and ta:
        a_per = a.shape[2] // tm
        a_spec = pl.BlockSpec((None, tk, tm), lambda i, j, kk: (i // a_per, kk, i % a_per))
    elif a_planes:
        a_per = a.shape[2] // tk
        a_spec = pl.BlockSpec((None, tm, tk), lambda i, j, kk: (kk // a_per, i, kk % a_per))
    elif ta:
        a_spec = pl.BlockSpec((tk, tm), lambda i, j, kk: (kk, i))
    else:
        a_spec = pl.BlockSpec((tm, tk), lambda i, j, kk: (i, kk))
    if b_planes:
        b_per = b.shape[2] // tn
        b_spec = pl.BlockSpec((None, tk, tn), lambda i, j, kk: (j // b_per, kk, j % b_per))
    elif tb:
        b_spec = pl.BlockSpec((tn, tk), lambda i, j, kk: (j, kk))
    else:
        b_spec = pl.BlockSpec((tk, tn), lambda i, j, kk: (kk, j))
    if out_planes is None:
        o_spec = pl.BlockSpec((tm, tn), lambda i, j, kk: (i, j))
        o_shape = (m, n)
    else:
        o_per = n // out_planes // tn
        assert add is None and n == out_planes * o_per * tn
        o_spec = pl.BlockSpec((None, tm, tn), lambda i, j, kk: (j // o_per, i, j % o_per))
        o_shape = (out_planes, m, n // out_planes)
    if norm_bwd is not None:
        vec = pl.BlockSpec((1, n), lambda i, j, kk: (0, 0))
        in_specs = [a_spec, b_spec, o_spec, vec, o_spec]
        out_specs = [o_spec, o_spec, vec]
        out_shape = [jax.ShapeDtypeStruct((m, n), F32), jax.ShapeDtypeStruct((m, n), BF16),
                     jax.ShapeDtypeStruct((1, n), F32)]
    else:
        in_specs = [a_spec, b_spec] + ([o_spec] if add is not None else [])
        out_specs = [o_spec]
        out_shape = [jax.ShapeDtypeStruct(o_shape, out_dtype)]
    outs, got = _call_hosting(
        body, name=name, grid=(m // tm, n // tn, nk), in_specs=in_specs, args=(a, b, *extras),
        out_specs=out_specs, out_shape=out_shape,
        scratch=[pltpu.VMEM((tm, tn), F32)] if nk > 1 else [], comm=comm)
    out = tuple(outs) if norm_bwd is not None else outs[0]
    return out if comm is None else (out, got)


def _rms_fwd(x, g, *, name, tr=256):
    s, d = x.shape

    def body(x_ref, g_ref, h_ref):
        xv = x_ref[...]
        r = lax.rsqrt(jnp.mean(xv * xv, axis=-1, keepdims=True) + RMS_EPS)
        h_ref[...] = (xv * r * g_ref[...]).astype(BF16)

    return pl.pallas_call(
        body, name=name, grid=(s // tr,),
        in_specs=[pl.BlockSpec((tr, d), lambda i: (i, 0)), pl.BlockSpec((1, d), lambda i: (0, 0))],
        out_specs=pl.BlockSpec((tr, d), lambda i: (i, 0)),
        out_shape=jax.ShapeDtypeStruct((s, d), BF16),
        compiler_params=_params(1),
    )(x, g)


def _final_loss(x, g, tgt, *, name, tr=256):
    s, d = x.shape

    def body(x_ref, g_ref, t_ref, loss_ref, dx_ref, dxb_ref, dg_ref):
        xv = x_ref[...]
        gv = g_ref[...]
        r = lax.rsqrt(jnp.mean(xv * xv, axis=-1, keepdims=True) + RMS_EPS)
        y = xv * r
        err = y * gv - t_ref[...]
        lpart = 0.5 * jnp.sum(jnp.mean(err * err, axis=-1, keepdims=True), axis=0, keepdims=True)
        dh = err * (1.0 / d)
        dy = dh * gv
        dx = r * (dy - y * jnp.mean(dy * y, axis=-1, keepdims=True))
        dx_ref[...] = dx
        dxb_ref[...] = dx.astype(BF16)
        gpart = jnp.sum(dh * y, axis=0, keepdims=True)
        lrow = jnp.broadcast_to(lpart, (1, LANES))

        @pl.when(pl.program_id(0) == 0)
        def _():
            dg_ref[...] = gpart
            loss_ref[...] = lrow

        @pl.when(pl.program_id(0) > 0)
        def _():
            dg_ref[...] += gpart
            loss_ref[...] += lrow

    row = pl.BlockSpec((tr, d), lambda i: (i, 0))
    vec = pl.BlockSpec((1, d), lambda i: (0, 0))
    lsp = pl.BlockSpec((1, LANES), lambda i: (0, 0))
    return pl.pallas_call(
        body, name=name, grid=(s // tr,),
        in_specs=[row, vec, row], out_specs=[lsp, row, row, vec],
        out_shape=[jax.ShapeDtypeStruct((1, LANES), F32), jax.ShapeDtypeStruct((s, d), F32),
                   jax.ShapeDtypeStruct((s, d), BF16), jax.ShapeDtypeStruct((1, d), F32)],
        compiler_params=_params(1),
    )(x, g, tgt)


FF_BLK = D_FF // 2


def _ffn_in_fwd(h, win_t, *, name, tm=512):
    s, d = h.shape

    def body(h_ref, wg_ref, wu_ref, g_ref, u_ref, a_ref):
        hv = h_ref[...]
        g = _nt(hv, wg_ref[...])
        u = _nt(hv, wu_ref[...])
        g_ref[...] = g
        u_ref[...] = u
        a_ref[...] = (g * jax.nn.sigmoid(g) * u).astype(BF16)

    blk = pl.BlockSpec((tm, FF_BLK), lambda i, j: (i, j))
    f32 = jax.ShapeDtypeStruct((s, D_FF), F32)
    return pl.pallas_call(
        body, name=name, grid=(s // tm, 2),
        in_specs=[pl.BlockSpec((tm, d), lambda i, j: (i, 0)),
                  pl.BlockSpec((FF_BLK, d), lambda i, j: (j, 0)),
                  pl.BlockSpec((FF_BLK, d), lambda i, j: (j + 2, 0))],
        out_specs=[blk, blk, blk],
        out_shape=[f32, f32, jax.ShapeDtypeStruct((s, D_FF), BF16)],
        compiler_params=_params(2),
    )(h, win_t, win_t)


def _ffn_out_dx(dxb, wout, g, u, *, name, tm=512):
    s, d = dxb.shape

    def body(dx_ref, w_ref, g_ref, u_ref, o_ref):
        dav = _nt(dx_ref[...], w_ref[...])
        gv = g_ref[...]
        sg = jax.nn.sigmoid(gv)
        o_ref[0] = (dav * u_ref[...] * (sg * (1.0 + gv * (1.0 - sg)))).astype(BF16)
        o_ref[1] = (dav * gv * sg).astype(BF16)

    blk = pl.BlockSpec((tm, FF_BLK), lambda i, j: (i, j))
    return pl.pallas_call(
        body, name=name, grid=(s // tm, 2),
        in_specs=[pl.BlockSpec((tm, d), lambda i, j: (i, 0)), pl.BlockSpec((FF_BLK, d), lambda i, j: (j, 0)),
                  blk, blk],
        out_specs=pl.BlockSpec((2, tm, FF_BLK), lambda i, j: (0, i, j)),
        out_shape=jax.ShapeDtypeStruct((2, s, D_FF), BF16),
        compiler_params=_params(2),
    )(dxb, wout, g, u)


def _split3(x):
    hi = x.astype(BF16)
    r1 = x - hi.astype(F32)
    mid = r1.astype(BF16)
    lo = (r1 - mid.astype(F32)).astype(BF16)
    return hi, mid, lo


def _dot3(x, m_bf):
    hi, mid, lo = _split3(x)
    return (jnp.dot(hi, m_bf, preferred_element_type=F32)
            + jnp.dot(mid, m_bf, preferred_element_type=F32)
            + jnp.dot(lo, m_bf, preferred_element_type=F32))


def _dot3_left(m_bf, x):
    hi, mid, lo = _split3(x)
    return (jnp.dot(m_bf, hi, preferred_element_type=F32)
            + jnp.dot(m_bf, mid, preferred_element_type=F32)
            + jnp.dot(m_bf, lo, preferred_element_type=F32))


def _dot2(x, m_bf):
    hi = x.astype(BF16)
    lo = (x - hi.astype(F32)).astype(BF16)
    return jnp.dot(hi, m_bf, preferred_element_type=F32) + jnp.dot(lo, m_bf, preferred_element_type=F32)


def _nt(a, b):
    return lax.dot_general(a, b, (((1,), (1,)), ((), ())), preferred_element_type=F32)


def _mm32(a, b):
    return jnp.dot(a, b, preferred_element_type=F32)


def _iota2(shape, dim):
    return lax.broadcasted_iota(jnp.int32, shape, dim)


def _rope_tables(s):
    half = ROT_DIM // 2
    pos = jnp.arange(s, dtype=F32)
    inv_freq = ROPE_THETA ** (-jnp.arange(half, dtype=F32) * 2.0 / ROT_DIM)
    ang = pos[:, None] * inv_freq[None, :]
    cos, sin = jnp.cos(ang), jnp.sin(ang)
    ones = jnp.ones((s, HEAD_DIM - ROT_DIM), F32)
    cos_t = jnp.concatenate([cos, cos, ones], axis=1)
    sin_t = jnp.concatenate([-sin, sin, 0.0 * ones], axis=1)
    idx = jnp.arange(HEAD_DIM)
    partner = jnp.where(idx < half, idx + half, idx - half)
    swap = ((idx[:, None] == partner[None, :]) & (idx[None, :] < ROT_DIM)).astype(F32)
    swap2 = jnp.kron(jnp.eye(2, dtype=F32), swap).astype(BF16)
    return jnp.tile(cos_t, (1, 2)), jnp.tile(sin_t, (1, 2)), swap2


def _rope(x, cos_t, sin_t, swap):
    return x * cos_t + _dot3(x, swap) * sin_t


def _rope_t(g, cos_t, sin_t, swap):
    return g * cos_t + _dot3(g * sin_t, swap)


def _dil_weight(dlt):
    nonneg = dlt >= 0
    w = jnp.zeros(dlt.shape, F32)
    for window, dil in DIL_PATTERNS:
        ok = nonneg & (dlt <= window) & ((dlt & (dil - 1)) == 0)
        w = w + ok.astype(F32)
    return w


FAR_TILES = 3
assert (FAR_TILES - 1) * TB + 1 > DIL_PATTERNS[1][0] and DIL_PATTERNS[2][0] >= 2048


def _dil_bias_scratch():
    return pltpu.VMEM((FAR_TILES + 1, TB, TB), F32)


def _dil_bias_tiles(bias_ref):
    rmc = _iota2((TB, TB), 0) - _iota2((TB, TB), 1)
    for d in range(FAR_TILES + 1):
        w = _dil_weight(d * TB + rmc)
        bias_ref[d] = jnp.where(w > 0.0, jnp.log(jnp.maximum(w, 1.0)), NEG)


def _log_sig_pair(z):
    sp = jnp.log(1.0 + jnp.exp(-jnp.abs(z)))
    return jnp.minimum(z, 0.0) - sp, -jnp.maximum(z, 0.0) - sp


def _log_one_minus_beta(z):
    return -(jnp.maximum(z, 0.0) + jnp.log(1.0 + jnp.exp(-jnp.abs(z))))


def _pair_masks(x, lane_lo):
    z = jnp.zeros_like(x)
    return jnp.where(lane_lo, x, z).astype(BF16), jnp.where(lane_lo, z, x).astype(BF16)


def _rows(i):
    return pl.ds(pl.multiple_of(i * TB, TB), TB)


def _head_spec(s, col0):
    return pl.BlockSpec((s, LANES), lambda p: (0, col0 + p))


def _stat_spec(s):
    return pl.BlockSpec((2, s, 1), lambda p: (p, 0, 0))


def _rowstat_spec(s):
    return pl.BlockSpec((2, 1, s), lambda p: (p, 0, 0))


def _full_spec(shape):
    nd = len(shape)
    return pl.BlockSpec(shape, lambda p: (0,) * nd)


K_COL, V_COL = D_ATTN // LANES, 2 * D_ATTN // LANES


def _bwd_scratch(s):
    return ([pltpu.VMEM((s, LANES), BF16)] * 8 + [pltpu.VMEM((LANES, s), BF16)] * 4
            + [pltpu.VMEM((LANES, s), F32)] * 2)


def _bwd_prep(i, q, k, v, dov, scr, lane_lo, sub_lo):
    qlo, qhi, klo, khi, kbf, vbf, dolo, dohi, qtlo, qthi, dotlo, dothi = scr[:12]
    rows = _rows(i)
    qs = q * SCALE
    qlo[rows, :], qhi[rows, :] = _pair_masks(qs, lane_lo)
    klo[rows, :], khi[rows, :] = _pair_masks(k * SCALE, lane_lo)
    kbf[rows, :] = k.astype(BF16)
    vbf[rows, :] = v.astype(BF16)
    dolo[rows, :], dohi[rows, :] = _pair_masks(dov, lane_lo)
    qtlo[:, rows], qthi[:, rows] = _pair_masks(qs.T, sub_lo)
    dotlo[:, rows], dothi[:, rows] = _pair_masks(dov.T, sub_lo)


def _sb_fwd(qkv, n_pairs, *, name, comm=None):
    s = qkv.shape[0]
    assert s % TB == 0
    nq = s // TB

    def body(q_ref, k_ref, v_ref, o_ref, ct_ref, qlo, qhi, kbf, vlo, vhi, sb0, sb1):
        sbuf = (sb0, sb1)
        lane_lo = _iota2((TB, LANES), 1) < HEAD_DIM

        def prep(i, _):
            rows = _rows(i)
            qlo[rows, :], qhi[rows, :] = _pair_masks(q_ref[rows, :] * SCALE, lane_lo)
            kbf[rows, :] = k_ref[rows, :].astype(BF16)
            vlo[rows, :], vhi[rows, :] = _pair_masks(v_ref[rows, :], lane_lo)
            return 0

        lax.fori_loop(0, nq, prep, 0)
        rmc = _iota2((TB, TB), 0) - _iota2((TB, TB), 1)
        strict = rmc > 0
        u_ge = (rmc >= 0).astype(BF16)
        qm, vm = (qlo, qhi), (vlo, vhi)

        def qloop(i, _):
            rows = _rows(i)

            def logits(kb, carry, diag):
                c = list(carry)
                keys = _rows(kb)
                k = kbf[keys, :]
                zs = [_nt(qm[h][rows, :], k) for h in range(2)]
                lms = [_log_one_minus_beta(z) for z in zs]
                if diag:
                    lms = [jnp.where(strict, lm, 0.0) for lm in lms]
                r_ins = [_dot2(lm, u_ge) for lm in lms]
                for h in range(2):
                    la = zs[h] + r_ins[h] + c[h]
                    sbuf[h][:, keys] = jnp.where(strict, la, NEG) if diag else la
                    c[h] = c[h] + r_ins[h][:, 0:1]
                return tuple(c)

            z1 = jnp.zeros((TB, 1), F32)
            c0, c1 = lax.fori_loop(0, i, lambda t, cr: logits(i - 1 - t, cr, False), logits(i, (z1, z1), True))

            def weigh(kb, acc):
                keys = _rows(kb)
                a_bf = [jnp.exp(sbuf[h][:, keys]).astype(BF16) for h in range(2)]
                return acc + _mm32(a_bf[0], vm[0][keys, :]) + _mm32(a_bf[1], vm[1][keys, :])

            acc = lax.fori_loop(0, i + 1, weigh, jnp.zeros((TB, LANES), F32))
            o_ref[rows, :] = acc
            ct_ref[0, rows, :] = c0
            ct_ref[1, rows, :] = c1
            return 0

        lax.fori_loop(0, nq, qloop, 0)

    return _call_pairs(
        body, name=name, n_pairs=n_pairs, comm=comm,
        in_specs=[_head_spec(s, 0), _head_spec(s, K_COL), _head_spec(s, V_COL)], args=(qkv, qkv, qkv),
        out_specs=[_head_spec(s, 0), _stat_spec(s)],
        out_shape=[jax.ShapeDtypeStruct((s, LANES * n_pairs), F32),
                   jax.ShapeDtypeStruct((2 * n_pairs, s, 1), F32)],
        scratch=[pltpu.VMEM((s, LANES), BF16)] * 5 + [pltpu.VMEM((TB, s), F32)] * 2)


def _sb_bwd(qkv, do, ctot, n_pairs, do_col0, *, name, comm=None):
    s = qkv.shape[0]
    assert s % TB == 0
    nq = s // TB

    def body(q_ref, k_ref, v_ref, do_ref, ct_ref, dq_ref, dk_ref, dv_ref, *scr):
        qlo, qhi, klo, khi, kbf, vbf, dolo, dohi, qtlo, qthi, dotlo, dothi, dkt, dvt = scr
        lane_lo = _iota2((TB, LANES), 1) < HEAD_DIM
        sub_lo = _iota2((LANES, TB), 0) < HEAD_DIM

        def prep(i, _):
            rows = _rows(i)
            _bwd_prep(i, q_ref[rows, :], k_ref[rows, :], v_ref[rows, :], do_ref[rows, :], scr, lane_lo, sub_lo)
            return 0

        lax.fori_loop(0, nq, prep, 0)
        dkt[...] = jnp.zeros_like(dkt)
        dvt[...] = jnp.zeros_like(dvt)
        rmc = _iota2((TB, TB), 0) - _iota2((TB, TB), 1)
        strict = rmc > 0
        u_le = (rmc <= 0).astype(BF16)
        qm, km, dom, qtm, dotm = (qlo, qhi), (klo, khi), (dolo, dohi), (qtlo, qthi), (dotlo, dothi)

        def qloop(i, _):
            rows = _rows(i)
            ct = (ct_ref[0, rows, :], ct_ref[1, rows, :])

            def tile(kb, carry, diag):
                pre, hl, dq = list(carry[0:2]), list(carry[2:4]), carry[4]
                keys = _rows(kb)
                k, v = kbf[keys, :], vbf[keys, :]
                zs = [_nt(qm[h][rows, :], k) for h in range(2)]
                das = [_nt(dom[h][rows, :], v) for h in range(2)]
                lms = [_log_one_minus_beta(z) for z in zs]
                if diag:
                    lms = [jnp.where(strict, lm, 0.0) for lm in lms]
                pins = [_dot2(lm, u_le) for lm in lms]
                gs, lbs = [], []
                a_bf = []
                for h in range(2):
                    lb = zs[h] + lms[h]
                    a = jnp.exp(lb + (ct[h] - pre[h]) - pins[h])
                    if diag:
                        a = jnp.where(strict, a, 0.0)
                    gs.append(a * das[h])
                    lbs.append(lb)
                    a_bf.append(a.astype(BF16))
                hins = [_dot2(g, u_le) for g in gs]
                dk_t, dv_t = dkt[:, keys], dvt[:, keys]
                for h in range(2):
                    g = gs[h]
                    dz = g - jnp.exp(lbs[h]) * (hl[h] + hins[h])
                    if diag:
                        dz = jnp.where(strict, dz, 0.0)
                    dzb = dz.astype(BF16)
                    dq = dq + _mm32(dzb, km[h][keys, :])
                    dk_t = dk_t + _mm32(qtm[h][:, rows], dzb)
                    dv_t = dv_t + _mm32(dotm[h][:, rows], a_bf[h])
                    pre[h] = pre[h] + pins[h][:, TB - 1:TB]
                    hl[h] = hl[h] + hins[h][:, TB - 1:TB]
                dkt[:, keys] = dk_t
                dvt[:, keys] = dv_t
                return pre[0], pre[1], hl[0], hl[1], dq

            z1 = jnp.zeros((TB, 1), F32)
            carry = lax.fori_loop(0, i, lambda kb, cr: tile(kb, cr, False),
                                  (z1, z1, z1, z1, jnp.zeros((TB, LANES), F32)))
            dq = tile(i, carry, True)[4]
            dq_ref[rows, :] = dq.astype(BF16)
            return 0

        lax.fori_loop(0, nq, qloop, 0)

        def wloop(i, _):
            rows = _rows(i)
            dk_ref[rows, :] = dkt[:, rows].T.astype(BF16)
            dv_ref[rows, :] = dvt[:, rows].T.astype(BF16)
            return 0

        lax.fori_loop(0, nq, wloop, 0)

    out = jax.ShapeDtypeStruct((s, LANES * n_pairs), BF16)
    return _call_pairs(
        body, name=name, n_pairs=n_pairs, comm=comm,
        in_specs=[_head_spec(s, 0), _head_spec(s, K_COL), _head_spec(s, V_COL),
                  _head_spec(s, do_col0), _stat_spec(s)], args=(qkv, qkv, qkv, do, ctot),
        out_specs=[_head_spec(s, 0)] * 3, out_shape=[out, out, out], scratch=_bwd_scratch(s))


def _bias_fwd(mode, qkv, head0_col, n_pairs, extra, *, name, comm=None):
    s = qkv.shape[0]
    assert s % TB == 0 and s <= DIL_PATTERNS[2][0]
    nq = s // TB
    fox = mode == "fox"

    def body(q_ref, k_ref, v_ref, e0, e1, *rest):
        if fox:
            o_ref, lse_ref, qlo, qhi, kbf, vx0, vx1, sb0, sb1 = rest
        else:
            e2, o_ref, lse_ref, qlo, qhi, kbf, vx0, vx1, sb0, sb1, bias = rest
            _dil_bias_tiles(bias)
        sbuf = (sb0, sb1)
        lane_lo = _iota2((TB, LANES), 1) < HEAD_DIM

        def prep(i, _):
            rows = _rows(i)
            q, k, v = q_ref[rows, :], k_ref[rows, :], v_ref[rows, :]
            if not fox:
                c, sn, sw = e0[rows, :], e1[rows, :], e2[...]
                q, k = _rope(q, c, sn, sw), _rope(k, c, sn, sw)
            qlo[rows, :], qhi[rows, :] = _pair_masks(q * SCALE, lane_lo)
            kbf[rows, :] = k.astype(BF16)
            one = jnp.ones_like(v)
            vx0[rows, :] = jnp.where(lane_lo, v, one).astype(BF16)
            vx1[rows, :] = jnp.where(lane_lo, one, v).astype(BF16)
            return 0

        lax.fori_loop(0, nq, prep, 0)
        rmc = _iota2((TB, TB), 0) - _iota2((TB, TB), 1)
        qm, vx = (qlo, qhi), (vx0, vx1)

        def qloop(i, _):
            rows = _rows(i)
            if fox:
                fq = (e0[0, rows, :], e0[1, rows, :])

            def scores(kb, carry, diag):
                keys = _rows(kb)
                k = kbf[keys, :]
                scs = [_nt(qm[h][rows, :], k) for h in range(2)]
                if not fox:
                    b = bias[jnp.minimum(i - kb, FAR_TILES)]
                out = []
                for h in range(2):
                    if fox:
                        sc = scs[h] + (fq[h] - e1[h, :, keys])
                        if diag:
                            sc = jnp.where(rmc >= 0, sc, NEG)
                    else:
                        sc = scs[h] + b
                    sbuf[h][:, keys] = sc
                    mx = carry[h]
                    for j in range(TB // LANES):
                        mx = jnp.maximum(mx, sc[:, j * LANES:(j + 1) * LANES])
                    out.append(mx)
                return tuple(out)

            mx0 = jnp.full((TB, LANES), NEG, F32)
            mxs = lax.fori_loop(0, i, lambda kb, cr: scores(kb, cr, False), (mx0, mx0))
            mxs = scores(i, mxs, True)
            m_0, m_1 = (jnp.max(mx, axis=1, keepdims=True) for mx in mxs)

            def weigh(kb, carry):
                keys = _rows(kb)
                ps = [jnp.exp(sbuf[h][:, keys] - m).astype(BF16) for h, m in enumerate((m_0, m_1))]
                return tuple(carry[h] + _mm32(ps[h], vx[h][keys, :]) for h in range(2))

            a0 = jnp.zeros((TB, LANES), F32)
            acc0, acc1 = lax.fori_loop(0, i + 1, weigh, (a0, a0))
            l0, l1 = acc0[:, HEAD_DIM:HEAD_DIM + 1], acc1[:, 0:1]
            o_ref[rows, :] = jnp.where(lane_lo, acc0 / l0, acc1 / l1)
            lse_ref[0, rows, :] = m_0 + jnp.log(l0)
            lse_ref[1, rows, :] = m_1 + jnp.log(l1)
            return 0

        lax.fori_loop(0, nq, qloop, 0)

    hp0 = head0_col
    if fox:
        e_specs = [_stat_spec(s), _rowstat_spec(s)]
    else:
        e_specs = [_full_spec((s, LANES)), _full_spec((s, LANES)), _full_spec((LANES, LANES))]
    return _call_pairs(
        body, name=name, n_pairs=n_pairs, comm=comm,
        in_specs=[_head_spec(s, hp0), _head_spec(s, K_COL + hp0), _head_spec(s, V_COL + hp0)] + e_specs,
        args=(qkv, qkv, qkv, *extra),
        out_specs=[_head_spec(s, 0), _stat_spec(s)],
        out_shape=[jax.ShapeDtypeStruct((s, LANES * n_pairs), F32),
                   jax.ShapeDtypeStruct((2 * n_pairs, s, 1), F32)],
        scratch=([pltpu.VMEM((s, LANES), BF16)] * 5 + [pltpu.VMEM((TB, s), F32)] * 2
                 + ([] if fox else [_dil_bias_scratch()])))


def _bias_bwd(mode, qkv, head0_col, n_pairs, extra, o, do, do_col0, lse, *, name, comm=None):
    s = qkv.shape[0]
    assert s % TB == 0 and s <= DIL_PATTERNS[2][0]
    nq = s // TB
    fox = mode == "fox"

    def body(q_ref, k_ref, v_ref, o_ref, do_ref, lse_ref, e0, e1, *rest):
        if fox:
            dq_ref, dk_ref, dv_ref, dfr_ref, dfc_ref = rest[:5]
            scr = rest[5:]
        else:
            e2, dq_ref, dk_ref, dv_ref = rest[:4]
            scr = rest[4:]
        qlo, qhi, klo, khi, kbf, vbf, dolo, dohi, qtlo, qthi, dotlo, dothi, dkt, dvt = scr[:14]
        if not fox:
            bias = scr[14]
            _dil_bias_tiles(bias)
        lane_lo = _iota2((TB, LANES), 1) < HEAD_DIM
        sub_lo = _iota2((LANES, TB), 0) < HEAD_DIM

        def prep(i, _):
            rows = _rows(i)
            q, k = q_ref[rows, :], k_ref[rows, :]
            if not fox:
                c, sn, sw = e0[rows, :], e1[rows, :], e2[...]
                q, k = _rope(q, c, sn, sw), _rope(k, c, sn, sw)
            _bwd_prep(i, q, k, v_ref[rows, :], do_ref[rows, :], scr, lane_lo, sub_lo)
            return 0

        lax.fori_loop(0, nq, prep, 0)
        dkt[...] = jnp.zeros_like(dkt)
        dvt[...] = jnp.zeros_like(dvt)
        if fox:
            dfr_ref[...] = jnp.zeros_like(dfr_ref)
        rmc = _iota2((TB, TB), 0) - _iota2((TB, TB), 1)
        qm, km, dom, qtm, dotm = (qlo, qhi), (klo, khi), (dolo, dohi), (qtlo, qthi), (dotlo, dothi)

        def qloop(i, _):
            rows = _rows(i)
            prod = do_ref[rows, :] * o_ref[rows, :]
            dsum = (jnp.sum(jnp.where(lane_lo, prod, 0.0), axis=1, keepdims=True),
                    jnp.sum(jnp.where(lane_lo, 0.0, prod), axis=1, keepdims=True))
            lse_i = (lse_ref[0, rows, :], lse_ref[1, rows, :])
            if fox:
                fql = (e0[0, rows, :] - lse_i[0], e0[1, rows, :] - lse_i[1])

            def tile(kb, carry, diag):
                dq, rs = carry[0], list(carry[1:])
                keys = _rows(kb)
                k, v = kbf[keys, :], vbf[keys, :]
                scs = [_nt(qm[h][rows, :], k) for h in range(2)]
                dps = [_nt(dom[h][rows, :], v) for h in range(2)]
                if not fox:
                    b = bias[jnp.minimum(i - kb, FAR_TILES)]
                ps, dss = [], []
                for h in range(2):
                    if fox:
                        sc = scs[h] + (fql[h] - e1[h, :, keys])
                        if diag:
                            sc = jnp.where(rmc >= 0, sc, NEG)
                    else:
                        sc = scs[h] + (b - lse_i[h])
                    p = jnp.exp(sc)
                    dss.append(p * (dps[h] - dsum[h]))
                    ps.append(p.astype(BF16))
                dk_t, dv_t = dkt[:, keys], dvt[:, keys]
                for h in range(2):
                    dsb = dss[h].astype(BF16)
                    dq = dq + _mm32(dsb, km[h][keys, :])
                    dk_t = dk_t + _mm32(qtm[h][:, rows], dsb)
                    dv_t = dv_t + _mm32(dotm[h][:, rows], ps[h])
                    if fox:
                        dfr_ref[h, :, keys] -= jnp.sum(dss[h], axis=0, keepdims=True)
                        for j in range(TB // LANES):
                            rs[h] = rs[h] + dss[h][:, j * LANES:(j + 1) * LANES]
                dkt[:, keys] = dk_t
                dvt[:, keys] = dv_t
                return (dq, *rs)

            z2 = jnp.zeros((TB, LANES), F32)
            carry = lax.fori_loop(0, i, lambda kb, cr: tile(kb, cr, False), (z2, z2, z2) if fox else (z2,))
            carry = tile(i, carry, True)
            if fox:
                dfc_ref[0, rows, :] = jnp.sum(carry[1], axis=1, keepdims=True)
                dfc_ref[1, rows, :] = jnp.sum(carry[2], axis=1, keepdims=True)
            dq = carry[0]
            if not fox:
                dq = _rope_t(dq, e0[rows, :], e1[rows, :], e2[...])
            dq_ref[rows, :] = dq.astype(BF16)
            return 0

        lax.fori_loop(0, nq, qloop, 0)

        def wloop(i, _):
            rows = _rows(i)
            dk = dkt[:, rows].T
            if not fox:
                dk = _rope_t(dk, e0[rows, :], e1[rows, :], e2[...])
            dk_ref[rows, :] = dk.astype(BF16)
            dv_ref[rows, :] = dvt[:, rows].T.astype(BF16)
            return 0

        lax.fori_loop(0, nq, wloop, 0)

    hp0 = head0_col
    out = jax.ShapeDtypeStruct((s, LANES * n_pairs), BF16)
    out_specs = [_head_spec(s, 0)] * 3
    out_shape = [out, out, out]
    if fox:
        e_specs = [_stat_spec(s), _rowstat_spec(s)]
        out_specs += [_rowstat_spec(s), _stat_spec(s)]
        out_shape += [jax.ShapeDtypeStruct((2 * n_pairs, 1, s), F32), jax.ShapeDtypeStruct((2 * n_pairs, s, 1), F32)]
    else:
        e_specs = [_full_spec((s, LANES)), _full_spec((s, LANES)), _full_spec((LANES, LANES))]
    return _call_pairs(
        body, name=name, n_pairs=n_pairs, comm=comm,
        in_specs=[_head_spec(s, hp0), _head_spec(s, K_COL + hp0), _head_spec(s, V_COL + hp0),
                  _head_spec(s, 0), _head_spec(s, do_col0), _stat_spec(s)] + e_specs,
        args=(qkv, qkv, qkv, o, do, lse, *extra),
        out_specs=out_specs, out_shape=out_shape,
        scratch=_bwd_scratch(s) + ([] if fox else [_dil_bias_scratch()]))


F_COL = 3 * D_ATTN // LANES


def _fgate_fwd(qkvf, brow, *, name):
    s = qkvf.shape[0]
    nb = s // BLK

    def body(f_ref, b_ref, fc_ref, fr_ref, fs):
        row, col = _iota2((BLK, BLK), 0), _iota2((BLK, BLK), 1)
        l_incl = (col <= row).astype(BF16)

        def step(i, carry):
            r0 = pl.multiple_of(i * BLK, BLK)
            lf, _ = _log_sig_pair(f_ref[pl.ds(r0, BLK), :] + b_ref[...])
            fblk = carry + _dot3_left(l_incl, lf)
            fs[pl.ds(r0, BLK), :] = fblk
            return fblk[BLK - 1:BLK, :]

        lax.fori_loop(0, nb, step, jnp.zeros((1, LANES), F32))
        ft = fs[...].T
        for h in range(N_HEADS):
            fc_ref[h, :, :] = fs[:, h:h + 1]
            fr_ref[h, :, :] = ft[h:h + 1, :]

    return pl.pallas_call(
        body, name=name, grid=(1,),
        in_specs=[pl.BlockSpec((s, LANES), lambda i: (0, F_COL)), pl.BlockSpec((1, LANES), lambda i: (0, 0))],
        out_specs=[pl.BlockSpec((N_HEADS, s, 1), lambda i: (0, 0, 0)),
                   pl.BlockSpec((N_HEADS, 1, s), lambda i: (0, 0, 0))],
        out_shape=[jax.ShapeDtypeStruct((N_HEADS, s, 1), F32), jax.ShapeDtypeStruct((N_HEADS, 1, s), F32)],
        scratch_shapes=[pltpu.VMEM((s, LANES), F32)],
        compiler_params=_params(1),
    )(qkvf, brow)


def _fgate_bwd(dfr, dfc, qkvf, brow, *, name):
    s = qkvf.shape[0]
    nb = s // BLK

    def body(dfr_ref, dfc_ref, f_ref, b_ref, dfl_ref, db_ref, ts, fs):
        ts[...] = jnp.zeros_like(ts)
        for h in range(N_HEADS):
            ts[h:h + 1, :] = dfr_ref[h]
        fs[...] = ts[...].T
        for h in range(N_HEADS):
            fs[:, h:h + 1] += dfc_ref[h]
        row, col = _iota2((BLK, BLK), 0), _iota2((BLK, BLK), 1)
        u_incl = (col >= row).astype(BF16)
        head_lane = _iota2((BLK, LANES), 1) < N_HEADS

        def step(ii, carry):
            tail, db = carry
            r0 = pl.multiple_of((nb - 1 - ii) * BLK, BLK)
            rblk = tail + _dot3_left(u_incl, fs[pl.ds(r0, BLK), :])
            _, lsn = _log_sig_pair(f_ref[pl.ds(r0, BLK), :] + b_ref[...])
            dfl = jnp.where(head_lane, rblk * jnp.exp(lsn), 0.0)
            dfl_ref[pl.ds(r0, BLK), :] = dfl.astype(BF16)
            return rblk[0:1, :], db + jnp.sum(dfl, axis=0, keepdims=True)

        z = jnp.zeros((1, LANES), F32)
        _, db = lax.fori_loop(0, nb, step, (z, z))
        db_ref[...] = db

    return pl.pallas_call(
        body, name=name, grid=(1,),
        in_specs=[pl.BlockSpec((N_HEADS, 1, s), lambda i: (0, 0, 0)), pl.BlockSpec((N_HEADS, s, 1), lambda i: (0, 0, 0)),
                  pl.BlockSpec((s, LANES), lambda i: (0, F_COL)), pl.BlockSpec((1, LANES), lambda i: (0, 0))],
        out_specs=[pl.BlockSpec((s, LANES), lambda i: (0, 0)), pl.BlockSpec((1, LANES), lambda i: (0, 0))],
        out_shape=[jax.ShapeDtypeStruct((s, LANES), BF16), jax.ShapeDtypeStruct((1, LANES), F32)],
        scratch_shapes=[pltpu.VMEM((LANES, s), F32), pltpu.VMEM((s, LANES), F32)],
        compiler_params=_params(1),
    )(dfr, dfc, qkvf, brow)


def _adamw_math(w, g, m, v):
    m2 = ADAM_B1 * m + (1.0 - ADAM_B1) * g
    v2 = ADAM_B2 * v + (1.0 - ADAM_B2) * (g * g)
    m_hat = m2 / (1.0 - ADAM_B1 ** ADAM_STEP)
    v_hat = v2 / (1.0 - ADAM_B2 ** ADAM_STEP)
    delta = -ADAM_LR * (m_hat / (jnp.sqrt(v_hat) + ADAM_EPS) + ADAM_WD * w)
    return delta, m2, v2


def _row_tile(r, cap=256, mult=16):
    best = None
    for t in range(mult, min(r, cap) + 1, mult):
        if r % t == 0:
            best = t
    assert best is not None, r
    return best


def _adamw_shard(w, m, v, lidx, g_all, r1, r2, sc, prev, *, name):
    nl, r, c = w.shape
    tr = _row_tile(r)

    def body(sc_ref, w_ref, m_ref, v_ref, g_ref, r1_ref, r2_ref, *rest):
        go_ref, d_ref, mo_ref, vo_ref = rest[-4:]
        g = g_ref[...] + r1_ref[...]
        g = g + r2_ref[0].astype(F32)
        g = g + r2_ref[1].astype(F32)
        g = g + r2_ref[2].astype(F32)
        delta, m2, v2 = _adamw_math(w_ref[...], g, m_ref[...], v_ref[...])
        go_ref[...] = g
        d_ref[...] = delta
        mo_ref[...] = m2
        vo_ref[...] = v2

    lay = pl.BlockSpec((None, tr, c), lambda i, s_: (lidx, i, 0))
    in_specs = [lay, lay, lay,
                pl.BlockSpec((None, tr, c), lambda i, s_: (s_[0], i, 0)),
                pl.BlockSpec((None, tr, c), lambda i, s_: (s_[1], i, 0)),
                pl.BlockSpec((3, tr, c), lambda i, s_: (0, i, 0))]
    args = [sc, w, m, v, g_all, r1, r2]
    aliases = {}
    if prev is not None:
        in_specs += [pl.BlockSpec(memory_space=pl.ANY)] * 4
        aliases = {7 + t: t for t in range(4)}
        args += list(prev)
    shp = jax.ShapeDtypeStruct((nl, r, c), F32)
    return pl.pallas_call(
        body, name=name,
        grid_spec=pltpu.PrefetchScalarGridSpec(
            num_scalar_prefetch=1, grid=(r // tr,), in_specs=in_specs, out_specs=[lay] * 4),
        out_shape=[shp] * 4, input_output_aliases=aliases,
        compiler_params=_params(1),
    )(*args)


def _adamw_small(w, g, m, v, *, name):
    def body(w_ref, g_ref, m_ref, v_ref, d_ref, mo_ref, vo_ref):
        delta, m2, v2 = _adamw_math(w_ref[...], g_ref[...], m_ref[...], v_ref[...])
        d_ref[...] = delta
        mo_ref[...] = m2
        vo_ref[...] = v2

    shp = jax.ShapeDtypeStruct(w.shape, F32)
    return pl.pallas_call(body, name=name, out_shape=[shp] * 3, compiler_params=_params())(w, g, m, v)


def _pos():
    return lax.axis_index("x"), lax.axis_index("y"), lax.axis_index("c")


def _other_chips(x, y):
    return [(1 - x, y), (x, 1 - y), (1 - x, 1 - y)]


def _dev_index(x, y, c):
    return 4 * x + 2 * y + c


HBM_SPEC = pl.BlockSpec(memory_space=pltpu.HBM)


class _Comm:
    def __init__(self, inputs, out_shape, scratch, start, mid, finish):
        self.inputs, self.out_shape, self.scratch = list(inputs), list(out_shape), list(scratch)
        self.start, self.mid, self.finish = start, mid, finish

    def run(self, name):
        n_in, n_out = len(self.inputs), len(self.out_shape)

        def body(*refs):
            parts = refs[:n_in], refs[n_in:n_in + n_out], refs[n_in + n_out:]
            self.start(*parts)
            self.mid(*parts)
            self.finish(*parts)

        return pl.pallas_call(
            body, name=name, in_specs=[HBM_SPEC] * n_in, out_specs=[HBM_SPEC] * n_out,
            out_shape=self.out_shape, scratch_shapes=self.scratch)(*self.inputs)


def _call_hosting(body, *, name, grid, in_specs, args, out_specs, out_shape, scratch, comm=None):
    if comm is None:
        res = pl.pallas_call(
            body, name=name, grid=grid, in_specs=in_specs, out_specs=out_specs, out_shape=out_shape,
            scratch_shapes=scratch, compiler_params=_params(len(grid)))(*args)
        return list(res), []
    sizes = (len(in_specs), len(comm.inputs), len(out_specs), len(comm.out_shape), len(scratch), len(comm.scratch))

    def fused(*refs):
        parts, o = [], 0
        for n in sizes:
            parts.append(refs[o:o + n])
            o += n
        h_in, c_in, h_out, c_out, h_scr, c_scr = parts
        first = last = None
        for d, n in enumerate(grid):
            p = pl.program_id(d)
            first = (p == 0) if first is None else jnp.logical_and(first, p == 0)
            last = (p == n - 1) if last is None else jnp.logical_and(last, p == n - 1)

        @pl.when(first)
        def _():
            comm.start(c_in, c_out, c_scr)

        @pl.when(last)
        def _():
            comm.mid(c_in, c_out, c_scr)

        body(*h_in, *h_out, *h_scr)

        @pl.when(last)
        def _():
            comm.finish(c_in, c_out, c_scr)

    res = pl.pallas_call(
        fused, name=name, grid=grid,
        in_specs=list(in_specs) + [HBM_SPEC] * sizes[1], out_specs=list(out_specs) + [HBM_SPEC] * sizes[3],
        out_shape=list(out_shape) + comm.out_shape, scratch_shapes=list(scratch) + comm.scratch,
        compiler_params=_params(len(grid)))(*args, *comm.inputs)
    return list(res[:sizes[2]]), list(res[sizes[2]:])


def _call_pairs(body, *, name, n_pairs, **kw):
    return _call_hosting(body, name=name, grid=(n_pairs,), **kw)


def _gather_comm(shards):
    n = len(shards)

    def plan(xs, outs, sems):
        send, recv, loc = sems
        x, y, c = _pos()
        me, sib = (x, y, c), (x, y, 1 - c)
        chips = _other_chips(x, y)

        def copy(a, k, block, to, src=None):
            dst = outs[a].at[_dev_index(*block)]
            return pltpu.make_async_remote_copy(
                src_ref=dst if src is None else src, dst_ref=dst,
                send_sem=send.at[a, k], recv_sem=recv.at[a, k], device_id=to, device_id_type=MESH)

        mine = [pltpu.make_async_copy(xs[a], outs[a].at[_dev_index(*me)], loc.at[a]) for a in range(n)]
        first = []
        for a in range(n):
            first.append(copy(a, 0, me, sib, src=xs[a]))
            first += [copy(a, 1 + j, me, (*chip, c), src=xs[a]) for j, chip in enumerate(chips)]
        passed = [(copy(a, 1 + j, (*chip, c), me), copy(a, 4 + j, (*chip, c), sib))
                  for j, chip in enumerate(chips) for a in range(n)]
        from_sib = [copy(a, 0, sib, me) for a in range(n)]
        from_sib += [copy(a, 4 + j, (*chip, 1 - c), me) for a in range(n) for j, chip in enumerate(chips)]
        return mine, first, passed, from_sib

    def start(xs, outs, sems):
        mine, first, _, _ = plan(xs, outs, sems)
        for cp in mine + first:
            cp.start()

    def mid(xs, outs, sems):
        for arrival, fwd in plan(xs, outs, sems)[2]:
            arrival.wait_recv()
            fwd.start()

    def finish(xs, outs, sems):
        mine, first, passed, from_sib = plan(xs, outs, sems)
        for cp in from_sib:
            cp.wait_recv()
        for cp in first + [fwd for _, fwd in passed]:
            cp.wait_send()
        for cp in mine:
            cp.wait()

    return _Comm(shards, [jax.ShapeDtypeStruct((N_DEV,) + a.shape, a.dtype) for a in shards],
                 [pltpu.SemaphoreType.DMA((n, 7)), pltpu.SemaphoreType.DMA((n, 7)), pltpu.SemaphoreType.DMA((n,))],
                 start, mid, finish)


def _sibling_comm(gs):
    n = len(gs)

    def plan(g_refs, r_refs, sems):
        send, recv = sems
        x, y, c = _pos()
        return [pltpu.make_async_remote_copy(
            src_ref=g_refs[a].at[_dev_index(k // 2, k % 2, 1 - c)], dst_ref=r_refs[a].at[k],
            send_sem=send.at[a, k], recv_sem=recv.at[a, k], device_id=(x, y, 1 - c), device_id_type=MESH)
            for a in range(n) for k in range(4)]

    def start(*parts):
        for cp in plan(*parts):
            cp.start()

    def mid(*parts):
        pass

    def finish(*parts):
        for cp in plan(*parts):
            cp.wait()

    return _Comm(gs, [jax.ShapeDtypeStruct((4,) + g.shape[1:], g.dtype) for g in gs],
                 [pltpu.SemaphoreType.DMA((n, 4)), pltpu.SemaphoreType.DMA((n, 4))], start, mid, finish)


def _rs_partial(g_all, r1, sc, *, name):
    _, r, c = g_all.shape
    tr = _row_tile(r)

    def body(sc_ref, g_ref, r_ref, o_ref):
        o_ref[...] = (g_ref[...] + r_ref[...]).astype(BF16)

    return pl.pallas_call(
        body, name=name,
        grid_spec=pltpu.PrefetchScalarGridSpec(
            num_scalar_prefetch=1, grid=(3, r // tr),
            in_specs=[pl.BlockSpec((None, tr, c), lambda j, i, s_: (s_[2 + j], i, 0)),
                      pl.BlockSpec((None, tr, c), lambda j, i, s_: (s_[5 + j], i, 0))],
            out_specs=pl.BlockSpec((None, tr, c), lambda j, i, s_: (j, i, 0))),
        out_shape=jax.ShapeDtypeStruct((3, r, c), BF16),
        compiler_params=_params(2),
    )(sc, g_all, r1)


def _cross_comm(ps):
    n = len(ps)

    def plan(p_refs, r_refs, sems):
        send, recv = sems
        x, y, c = _pos()
        return [pltpu.make_async_remote_copy(
            src_ref=p_refs[a].at[j], dst_ref=r_refs[a].at[j], send_sem=send.at[a, j], recv_sem=recv.at[a, j],
            device_id=(*chip, c), device_id_type=MESH)
            for j, chip in enumerate(_other_chips(x, y)) for a in range(n)]

    def start(*parts):
        for cp in plan(*parts):
            cp.start()

    def mid(*parts):
        pass

    def finish(*parts):
        for cp in plan(*parts):
            cp.wait()

    return _Comm(ps, [jax.ShapeDtypeStruct(p.shape, p.dtype) for p in ps],
                 [pltpu.SemaphoreType.DMA((n, 3)), pltpu.SemaphoreType.DMA((n, 3))], start, mid, finish)


SMALL_ROWS = 16


def _all_reduce_small(pack, *, name):
    def body(x_ref, o_ref, buf, send, recv):
        x, y, c = _pos()
        me = _dev_index(x, y, c)
        buf[me] = x_ref[...]
        copies = []
        for k in range(1, N_DEV):
            fx, fy, fc = (k >> 2) & 1, (k >> 1) & 1, k & 1
            peer = (1 - x if fx else x, 1 - y if fy else y, 1 - c if fc else c)
            copies.append(pltpu.make_async_remote_copy(
                src_ref=x_ref, dst_ref=buf.at[me], send_sem=send.at[k - 1], recv_sem=recv.at[k - 1],
                device_id=peer, device_id_type=MESH))
        for cp in copies:
            cp.start()
        for cp in copies:
            cp.wait()
        acc = buf[0]
        for d in range(1, N_DEV):
            acc = acc + buf[d]
        o_ref[...] = acc

    return pl.pallas_call(
        body, name=name,
        in_specs=[pl.BlockSpec(memory_space=pltpu.VMEM)], out_specs=pl.BlockSpec(memory_space=pltpu.VMEM),
        out_shape=jax.ShapeDtypeStruct(pack.shape, F32),
        scratch_shapes=[pltpu.VMEM((N_DEV,) + pack.shape, F32),
                        pltpu.SemaphoreType.DMA((N_DEV - 1,)), pltpu.SemaphoreType.DMA((N_DEV - 1,))],
    )(pack)


def _unshard_cols(g):
    return jnp.transpose(g, (1, 0, 2)).reshape(g.shape[1], N_DEV * g.shape[2])


def _shard_cols(w):
    k, n8 = w.shape
    return jnp.transpose(w.reshape(k, N_DEV, n8 // N_DEV), (1, 0, 2))


def _pad_row(v, width=D_MODEL):
    v = v.reshape(1, -1)
    return jnp.pad(v, ((0, 0), (0, width - v.shape[1])))


def _forward_mixer(l, xc, g_mix, wq, wo, rope, brow, comm_a=None, comm_b=None):
    even = l % 2 == 0
    h1 = _rms_fwd(xc, g_mix, name=f"norm_mix_fwd{l}")
    qkv = _mm(h1, wq, name=f"qkv_fwd{l}", tm=1024, tn=768 if even else 640)
    if even:
        (o_a, st_a), got_a = _sb_fwd(qkv, N_HEADS // 4, name=f"sb_fwd{l}", comm=comm_a)
        (o_b, st_b), got_b = _bias_fwd("dil", qkv, N_HEADS // 4, N_HEADS // 4, rope, name=f"dil_fwd{l}",
                                       comm=comm_b)
        o = jnp.concatenate([o_a, o_b], axis=1)
        att = (o_b, st_a, st_b)
    else:
        assert comm_b is None
        fcol, frow = _fgate_fwd(qkv, brow, name=f"fgate_fwd{l}")
        (o, lse), got_a = _bias_fwd("fox", qkv, 0, N_HEADS // 2, (fcol, frow), name=f"fox_fwd{l}", comm=comm_a)
        got_b = []
        att = (o, lse, fcol, frow)
    o_bf = o.astype(BF16)
    if callable(wo):
        wo = wo(got_a, got_b)
    xm = _mm(o_bf, wo, add=xc, name=f"wo_fwd{l}", tm=512, tn=1024)
    return xm, (xc, h1, qkv, att, o_bf), got_a, got_b


def _forward_ffn(l, xm, g_ffn, win_t, wout):
    h2 = _rms_fwd(xm, g_ffn, name=f"norm_ffn_fwd{l}")
    g, u, a = _ffn_in_fwd(h2, win_t, name=f"ffn_in_fwd{l}")
    xo = _mm(a, wout, add=xm, name=f"ffn_out_fwd{l}", tm=512, tn=1024)
    return xo, (xm, h2, (g, u), a)


def _backward_ffn(l, dx, dxb, saved, g_ffn, w, exchange=None):
    _, _, win_t, wout = w
    _, _, _, _, _, xm, h2, gu, a = saved
    dgu = _ffn_out_dx(dxb, wout, *gu, name=f"ffn_out_dx{l}")
    d_wout = _mm(a, dxb, ta=True, name=f"ffn_out_dw{l}", tm=FF_BLK, tn=512)
    d_win_t = _mm(dgu, h2, ta=True, name=f"ffn_in_dw{l}", tm=FF_BLK, tn=1024)
    comm = exchange(d_win_t, d_wout) if exchange is not None else None
    res = _mm(dgu, win_t, name=f"ffn_in_dx{l}", tm=512, tn=1024, tk=D_FF, comm=comm, norm_bwd=(xm, g_ffn, dx))
    (dxm, dxmb, dg_ffn), got = res if comm is not None else (res, [])
    return dxm, dxmb, dg_ffn, d_win_t, d_wout, got


def _backward_attn(l, dxm, dxmb, saved, g_mix, w, rope, brow, comm_a=None, comm_b=None, exchange=None):
    wq, wo, _, _ = w
    xin, h1, qkv, att, o_bf, _, _, _, _ = saved
    even = l % 2 == 0
    d_wo = _mm(o_bf, dxmb, ta=True, name=f"wo_dw{l}", tm=512, tn=1024)
    do = _mm(dxmb, wo, tb=True, name=f"wo_dx{l}", tm=1024, tn=1024)
    db = None
    if even:
        o_b, st_a, st_b = att
        (dqa, dka, dva), got_a = _sb_bwd(qkv, do, st_a, N_HEADS // 4, 0, name=f"sb_bwd{l}", comm=comm_a)
        (dqb, dkb, dvb), got_b = _bias_bwd("dil", qkv, N_HEADS // 4, N_HEADS // 4, rope, o_b, do,
                                           N_HEADS // 4, st_b, name=f"dil_bwd{l}", comm=comm_b)
        dqkv = jnp.concatenate([dqa, dqb, dka, dkb, dva, dvb], axis=1)
    else:
        assert comm_b is None
        o, lse, fcol, frow = att
        (dq, dk, dv, dfr, dfc), got_a = _bias_bwd("fox", qkv, 0, N_HEADS // 2, (fcol, frow), o, do, 0, lse,
                                                  name=f"fox_bwd{l}", comm=comm_a)
        got_b = []
        dfl, db = _fgate_bwd(dfr, dfc, qkv, brow, name=f"fgate_bwd{l}")
        dqkv = jnp.concatenate([dq, dk, dv, dfl], axis=1)
    if even:
        d_wq = _mm(h1, dqkv, ta=True, name=f"qkv_dw{l}", tm=1024, tn=dqkv.shape[1] // N_DEV, out_planes=N_DEV)
    else:
        d_wq = _mm(h1, dqkv, ta=True, name=f"qkv_dw{l}", tm=1024, tn=640)
    comm = exchange(d_wq, d_wo) if exchange is not None else None
    res = _mm(dqkv, wq, tb=True, name=f"qkv_dx{l}", tm=512, tn=1024, comm=comm, norm_bwd=(xin, g_mix, dxm))
    (dx, dxb, dg_mix), got_x = res if comm is not None else (res, [])
    return dx, dxb, dg_mix, d_wq, d_wo, db, got_a, got_b, got_x


def kernel(x, norm_mix, w_qkv_even, w_o_even, w_qkvf_odd, b_forget, w_o_odd, norm_ffn, w_ffn_in, w_ffn_out, norm_final, loss_target, m_norm_mix, m_w_qkv_even, m_w_o_even, m_w_qkvf_odd, m_b_forget, m_w_o_odd, m_norm_ffn, m_w_ffn_in, m_w_ffn_out, m_norm_final, v_norm_mix, v_w_qkv_even, v_w_o_even, v_w_qkvf_odd, v_b_forget, v_w_o_odd, v_norm_ffn, v_w_ffn_in, v_w_ffn_out, v_norm_final):
    xi, yi, ci = _pos()
    others = _other_chips(xi, yi)
    sc = jnp.stack([_dev_index(xi, yi, ci), 2 * xi + yi]
                   + [_dev_index(px, py, ci) for px, py in others]
                   + [2 * px + py for px, py in others]).astype(jnp.int32)
    n_odd_cols = w_qkvf_odd.shape[2] * N_DEV

    xs, tgt = x[0], loss_target[0]
    rope = _rope_tables(xs.shape[0])
    brow = [_pad_row(b_forget[i], LANES) for i in range(DEPTH // 2)]
    w_in_t, m_in_t, v_in_t = (jnp.swapaxes(t, 1, 2) for t in (w_ffn_in, m_w_ffn_in, v_w_ffn_in))

    def shards(l):
        even = l % 2 == 0
        return {"wq": (w_qkv_even if even else w_qkvf_odd)[l // 2].astype(BF16),
                "wo": (w_o_even if even else w_o_odd)[l // 2].astype(BF16),
                "win": w_in_t[l].astype(BF16), "wout": w_ffn_out[l].astype(BF16)}

    def full_wq(l, gq):
        wq = _unshard_cols(gq)
        if l % 2 == 1:
            wq = jnp.pad(wq, ((0, 0), (0, QKVF_PAD - n_odd_cols)))
        return wq

    def full_wo(go):
        return go.reshape(D_ATTN, D_MODEL)

    sh = [shards(l) for l in range(DEPTH)]
    wq = {0: full_wq(0, _gather_comm([sh[0]["wq"]]).run("gather_weights0")[0])}
    weights, saved = [], []
    xc = xs
    for l in range(DEPTH):
        even = l % 2 == 0
        nxt = [sh[l + 1]["wq"]] if l + 1 < DEPTH else []
        if even:
            comm_a = _gather_comm([sh[l]["win"], sh[l]["wout"]])
            comm_b = _gather_comm([sh[l]["wo"]] + nxt)
        else:
            comm_a, comm_b = _gather_comm([sh[l]["wo"], sh[l]["win"], sh[l]["wout"]] + nxt), None
        xm, sv_mix, got_a, got_b = _forward_mixer(
            l, xc, norm_mix[l:l + 1], wq[l], (lambda ga, gb: full_wo(gb[0] if even else ga[0])), rope,
            brow[l // 2], comm_a, comm_b)
        if even:
            (gi, gout), go, gq_next = got_a, got_b[0], got_b[1:]
        else:
            go, gi, gout, gq_next = got_a[0], got_a[1], got_a[2], got_a[3:]
        if gq_next:
            wq[l + 1] = full_wq(l + 1, gq_next[0])
        win_t, wout = gi.reshape(2 * D_FF, D_MODEL), gout.reshape(D_FF, D_MODEL)
        xc, sv_ffn = _forward_ffn(l, xm, norm_ffn[l:l + 1], win_t, wout)
        weights.append((wq[l], full_wo(go), win_t, wout))
        saved.append(sv_mix + sv_ffn)

    loss_row, dx, dxb, dg_final = _final_loss(xc, norm_final.reshape(1, -1), tgt, name="final_loss")

    sharded = {
        "qkv_even": (w_qkv_even, m_w_qkv_even, v_w_qkv_even), "o_even": (w_o_even, m_w_o_even, v_w_o_even),
        "qkvf_odd": (w_qkvf_odd, m_w_qkvf_odd, v_w_qkvf_odd), "o_odd": (w_o_odd, m_w_o_odd, v_w_o_odd),
        "ffn_in": (w_in_t, m_in_t, v_in_t), "ffn_out": (w_ffn_out, m_w_ffn_out, v_w_ffn_out),
    }
    results = {k: None for k in sharded}

    def chip_sums(gs, r1s, keys, tag):
        ps = [_rs_partial(g, r1, sc, name=f"grads_chip_sum_{tag}_{a}") for a, (g, r1) in enumerate(zip(gs, r1s))]
        return gs, r1s, ps, keys

    held = {}

    def row_chunks(d):
        return d.reshape(N_DEV, d.shape[0] // N_DEV, D_MODEL)

    def to_sibling(tag, col_sharded, odd_qkv=False):
        def make(d_first, d_rows):
            if odd_qkv:
                d_first = d_first[:, :n_odd_cols]
            if d_first.ndim == 2:
                d_first = _shard_cols(d_first) if col_sharded else row_chunks(d_first)
            held[tag] = [d_first, row_chunks(d_rows)]
            return _sibling_comm(held[tag])
        return make

    def update(group, r2s, tag):
        gs, r1s, _, keys = group
        for a, (key, lidx) in enumerate(keys):
            w, m, v = sharded[key]
            results[key] = _adamw_shard(w, m, v, lidx, gs[a], r1s[a], r2s[a], sc, results[key],
                                        name=f"adamw_{key}_{tag}")

    dg_mix, dg_ffn, db_f = [None] * DEPTH, [None] * DEPTH, [None] * (DEPTH // 2)
    pending = None
    for l in reversed(range(DEPTH)):
        even = l % 2 == 0
        dxm, dxmb, dg_ffn[l], _, _, r1s = _backward_ffn(l, dx, dxb, saved[l], norm_ffn[l:l + 1], weights[l],
                                                        to_sibling(f"ffn{l}", col_sharded=False))
        ffn = chip_sums(held[f"ffn{l}"], r1s, [("ffn_in", l), ("ffn_out", l)], f"ffn{l}")
        if even:
            comm_a = _cross_comm(ffn[2])
            comm_b = _cross_comm(pending[2]) if pending is not None else None
        else:
            comm_a = _cross_comm(ffn[2] + (pending[2] if pending is not None else []))
            comm_b = None
        dx, dxb, dg_mix[l], _, _, db, got_a, got_b, r1s = _backward_attn(
            l, dxm, dxmb, saved[l], norm_mix[l:l + 1], weights[l], rope, brow[l // 2], comm_a, comm_b,
            to_sibling(f"mix{l}", col_sharded=True, odd_qkv=not even))
        update(ffn, got_a[:2], f"ffn{l}")
        if pending is not None:
            update(pending, got_b if even else got_a[2:], f"mix{l + 1}")
        if not even:
            db_f[l // 2] = db
        pending = chip_sums(held[f"mix{l}"], r1s,
                            [("qkv_even" if even else "qkvf_odd", l // 2), ("o_even" if even else "o_odd", l // 2)],
                            f"mix{l}")
    update(pending, _cross_comm(pending[2]).run("grads_to_chips_mix0"), "mix0")

    zeros = jnp.zeros((SMALL_ROWS - 11, D_MODEL), F32)
    db_row = _pad_row(jnp.concatenate([d[:, :N_HEADS] for d in db_f], axis=1))
    pack_g = jnp.concatenate(dg_mix + dg_ffn + [dg_final, db_row, _pad_row(loss_row[:, :1]), zeros], axis=0)
    tot = _all_reduce_small(pack_g, name="small_all_reduce")

    def pack(nm, nf, nfin, bf):
        return jnp.concatenate([nm, nf, nfin.reshape(1, -1), _pad_row(bf),
                                jnp.zeros((SMALL_ROWS - 10, D_MODEL), F32)], axis=0)

    d_s, m_s, v_s = _adamw_small(
        pack(norm_mix, norm_ffn, norm_final, b_forget), tot,
        pack(m_norm_mix, m_norm_ffn, m_norm_final, m_b_forget),
        pack(v_norm_mix, v_norm_ffn, v_norm_final, v_b_forget), name="adamw_small")

    def unpack(p):
        nb = b_forget.size
        return {"norm_mix": p[0:DEPTH], "norm_ffn": p[DEPTH:2 * DEPTH], "norm_final": p[2 * DEPTH],
                "b_forget": p[2 * DEPTH + 1, :nb].reshape(b_forget.shape)}

    small = [unpack(tot), unpack(d_s), unpack(m_s), unpack(v_s)]
    loss = tot[2 * DEPTH + 2, 0]

    order = ["norm_mix", "qkv_even", "o_even", "qkvf_odd", "b_forget", "o_odd", "norm_ffn", "ffn_in", "ffn_out",
             "norm_final"]
    outs = [loss, dx[None]]
    for t in range(4):
        for key in order:
            if key in small[t]:
                outs.append(small[t][key])
            elif key == "ffn_in":
                outs.append(jnp.swapaxes(results[key][t], 1, 2))
            else:
                outs.append(results[key][t])
    return tuple(outs)
```
